```python
import math
import jax
import jax.numpy as jnp
from jax import lax
import numpy as np

D_MODEL = 1024
BATCH = 16
SEQ = 2048
DEPTH = 2
DEC_BATCH = 32
DEC_SEQ = 32
PAST_LEN = 1024

CHUNK = 64
N_META = 16
QBLOCK = 128
MLA_HEADS = 4
MLA_Q_LORA = 256
MLA_KV_LORA = 128
MLA_NOPE = 64
MLA_ROPE = 32
MLA_V = 64
ROPE_THETA = 10000.0
SB_HEADS = 4
SB_DH = 64
DIFF_HEADS = 4
DIFF_DQK = 32
DIFF_DV = 64
DSA_HEADS = 4
DSA_DH = 64
IDX_HEADS = 8
IDX_DIM = 32
DSA_TOPK = 256
N_BRANCH = 4
BR_W = 256
D_FF = 4 * D_MODEL
T5_BUCKETS = 32
T5_MAX_DIST = 128
LN_EPS = 1e-5
RMS_EPS = 1e-6
NEG_INF = -1e30
DN_ALPHA = (2 * DEPTH) ** 0.25
DN_BETA = (8 * DEPTH) ** -0.25
IN_SIZES = (MLA_Q_LORA, MLA_KV_LORA, MLA_ROPE,
            SB_HEADS * SB_DH, SB_HEADS * SB_DH, SB_HEADS * SB_DH,
            DIFF_HEADS * 2 * DIFF_DQK, DIFF_HEADS * 2 * DIFF_DQK, DIFF_HEADS * DIFF_DV,
            DSA_HEADS * DSA_DH, DSA_HEADS * DSA_DH, DSA_HEADS * DSA_DH,
            IDX_HEADS * IDX_DIM, IDX_DIM, IDX_HEADS,
            N_BRANCH * D_MODEL)
IN_SPLITS = tuple(int(s) for s in np.cumsum(IN_SIZES)[:-1])
D_IN = int(sum(IN_SIZES))

kernel_name = 'hybrid_streaming_encoder_step'


def _layer_norm(x, g, b):
    xf = x.astype(jnp.float32)
    mu = jnp.mean(xf, axis=-1, keepdims=True)
    var = jnp.mean(jnp.square(xf - mu), axis=-1, keepdims=True)
    y = (xf - mu) * lax.rsqrt(var + LN_EPS) * g.astype(jnp.float32) + b.astype(jnp.float32)
    return y.astype(x.dtype)


def _rms_norm(x, g):
    xf = x.astype(jnp.float32)
    y = xf * lax.rsqrt(jnp.mean(jnp.square(xf), axis=-1, keepdims=True) + RMS_EPS)
    return (y * g.astype(jnp.float32)).astype(x.dtype)


def _rope(x, pos):
    half = x.shape[-1] // 2
    inv_freq = ROPE_THETA ** (-jnp.arange(half, dtype=jnp.float32) / half)
    ang = pos.astype(jnp.float32)[:, None] * inv_freq[None, :]
    shape = (pos.shape[0],) + (1,) * (x.ndim - 3) + (half,)
    cos = jnp.cos(ang).reshape(shape)
    sin = jnp.sin(ang).reshape(shape)
    xf = x.astype(jnp.float32)
    x1, x2 = xf[..., :half], xf[..., half:]
    return jnp.concatenate([x1 * cos - x2 * sin, x2 * cos + x1 * sin], axis=-1).astype(x.dtype)


def _chunk_id(pos):
    return jnp.floor_divide(pos - N_META, CHUNK)


def _chunk_mask(qpos, kpos):
    return _chunk_id(kpos)[None, :] <= _chunk_id(qpos)[:, None]


def _t5_bucket(rel):
    nb = T5_BUCKETS // 2
    max_exact = nb // 2
    n = jnp.abs(rel)
    nf = jnp.maximum(n, 1).astype(jnp.float32)
    large = max_exact + (jnp.log(nf / max_exact) / math.log(T5_MAX_DIST / max_exact)
                         * (nb - max_exact)).astype(jnp.int32)
    large = jnp.minimum(large, nb - 1)
    return jnp.where(rel > 0, nb, 0) + jnp.where(n < max_exact, n, large)


def _query_blocks(fn, qpos, *qs):
    lq = qpos.shape[0]
    if lq <= QBLOCK:
        return fn(qpos, *qs)
    nb = -(-lq // QBLOCK)
    pad = nb * QBLOCK - lq
    qpos_b = jnp.pad(qpos, (0, pad), mode='edge').reshape(nb, QBLOCK)
    qs_b = tuple(
        jnp.moveaxis(jnp.pad(q, [(0, 0), (0, pad)] + [(0, 0)] * (q.ndim - 2))
                     .reshape((q.shape[0], nb, QBLOCK) + q.shape[2:]), 1, 0)
        for q in qs)
    out = lax.map(lambda a: fn(a[0], *a[1:]), (qpos_b,) + qs_b)
    out = jnp.moveaxis(out, 0, 1)
    out = out.reshape((out.shape[0], nb * QBLOCK) + out.shape[3:])
    return out[:, :lq]


def _mla_attend(qpos, q_lat, q_pe, kpos, ckv, kpe):
    s = (jnp.einsum('bqhc,bkc->bhqk', q_lat, ckv)
         + jnp.einsum('bqhr,bkr->bhqk', q_pe, kpe)).astype(jnp.float32)
    s = jnp.where(_chunk_mask(qpos, kpos)[None, None], s * (MLA_NOPE + MLA_ROPE) ** -0.5, NEG_INF)
    p = jax.nn.softmax(s, axis=-1).astype(ckv.dtype)
    return jnp.einsum('bhqk,bkc->bqhc', p, ckv)


def _sb_attend(qpos, q, kpos, k, v):
    z = jnp.einsum('bqhd,bkhd->bhqk', q, k).astype(jnp.float32) * SB_DH ** -0.5
    causal = (kpos[None, :] < qpos[:, None])[None, None]
    log_1m = jnp.where(causal, jax.nn.log_sigmoid(-z), 0.0)
    tail = lax.cumsum(log_1m, axis=3, reverse=True) - log_1m
    a = jnp.where(causal, jnp.exp(jax.nn.log_sigmoid(z) + tail), 0.0)
    return jnp.einsum('bhqk,bkhd->bqhd', a.astype(v.dtype), v)


def _diff_attend(qpos, q, kpos, k, v, lam, bias_tab):
    s = jnp.einsum('bqhid,bkhid->bhiqk', q, k).astype(jnp.float32) * DIFF_DQK ** -0.5
    bias = bias_tab[_t5_bucket(kpos[None, :] - qpos[:, None])]
    s = s + jnp.transpose(bias, (2, 0, 1))[None, :, None].astype(jnp.float32)
    s = jnp.where(_chunk_mask(qpos, kpos)[None, None, None], s, NEG_INF)
    p = jax.nn.softmax(s, axis=-1)
    a = p[:, :, 0] - lam * p[:, :, 1]
    return jnp.einsum('bhqk,bkhd->bqhd', a.astype(v.dtype), v)


def _dsa_attend(qpos, q, qi, wi, kpos, k, v, ki, bias_tab, topk):
    idx_s = jnp.einsum('bqhd,bkd->bqhk', qi, ki).astype(jnp.float32) * IDX_DIM ** -0.5
    score = jnp.einsum('bqh,bqhk->bqk', wi.astype(jnp.float32), jax.nn.relu(idx_s))
    score = jnp.where(_chunk_mask(qpos, kpos)[None], score, NEG_INF)
    _, sel = lax.top_k(score, topk)
    gather = jax.vmap(lambda arr, idx: arr[idx])
    k_sel = gather(k, sel)
    v_sel = gather(v, sel)
    pos_sel = kpos[sel]
    s = jnp.einsum('bqhd,bqthd->bhqt', q, k_sel).astype(jnp.float32) * DSA_DH ** -0.5
    bias = bias_tab[_t5_bucket(pos_sel - qpos[None, :, None])]
    s = s + jnp.transpose(bias, (0, 3, 1, 2)).astype(jnp.float32)
    adm = (_chunk_id(pos_sel) <= _chunk_id(qpos)[None, :, None])[:, None]
    p = jax.nn.softmax(jnp.where(adm, s, NEG_INF), axis=-1).astype(v.dtype)
    return jnp.einsum('bhqt,bqthd->bqhd', p, v_sel)


def _layer(x, qpos, kpos_past, past, layer_idx, topk, rel_bias,
           w_in, qn_g, w_uq, kvn_g, w_uk, w_uv, lam_p, subln_g, w_br, w_out,
           ln1_g, ln1_b, w_ff1, b_ff1, w_ff2, b_ff2, ln2_g, ln2_b):
    bsz, nq, _ = x.shape
    proj = jnp.einsum('bqd,de->bqe', x, w_in)
    (a_cq, a_ckv, a_kpe, b_q, b_k, b_v, c_q, c_k, c_v,
     d_q, d_k, d_v, d_qi, d_ki, d_wi, gates) = jnp.split(proj, IN_SPLITS, axis=-1)
    kpos = qpos if past is None else jnp.concatenate([kpos_past, qpos])

    def keys(new, i):
        return new if past is None else jnp.concatenate([past[i].astype(new.dtype), new], axis=1)

    q_a = jnp.einsum('bqc,ce->bqe', _rms_norm(a_cq, qn_g), w_uq)
    q_a = q_a.reshape(bsz, nq, MLA_HEADS, MLA_NOPE + MLA_ROPE)
    q_lat = jnp.einsum('bqhd,chd->bqhc', q_a[..., :MLA_NOPE], w_uk)
    q_pe = _rope(q_a[..., MLA_NOPE:], qpos)
    ckv_new = _rms_norm(a_ckv, kvn_g)
    kpe_new = _rope(a_kpe, qpos)
    ckv_all, kpe_all = keys(ckv_new, 0), keys(kpe_new, 1)
    o_lat = _query_blocks(lambda qp, ql, qr: _mla_attend(qp, ql, qr, kpos, ckv_all, kpe_all),
                          qpos, q_lat, q_pe)
    o_a = jnp.einsum('bqhc,chd->bqhd', o_lat, w_uv).reshape(bsz, nq, BR_W)

    sb_q = b_q.reshape(bsz, nq, SB_HEADS, SB_DH)
    sb_k_new = b_k.reshape(bsz, nq, SB_HEADS, SB_DH)
    sb_v_new = b_v.reshape(bsz, nq, SB_HEADS, SB_DH)
    sb_k_all, sb_v_all = keys(sb_k_new, 2), keys(sb_v_new, 3)
    o_b = _query_blocks(lambda qp, q: _sb_attend(qp, q, kpos, sb_k_all, sb_v_all),
                        qpos, sb_q).reshape(bsz, nq, BR_W)

    lam_init = 0.8 - 0.6 * math.exp(-0.3 * layer_idx)
    lp = lam_p.astype(jnp.float32)
    lam = jnp.exp(jnp.sum(lp[0] * lp[1])) - jnp.exp(jnp.sum(lp[2] * lp[3])) + lam_init
    df_q = c_q.reshape(bsz, nq, DIFF_HEADS, 2, DIFF_DQK)
    df_k_new = c_k.reshape(bsz, nq, DIFF_HEADS, 2, DIFF_DQK)
    df_v_new = c_v.reshape(bsz, nq, DIFF_HEADS, DIFF_DV)
    df_k_all, df_v_all = keys(df_k_new, 4), keys(df_v_new, 5)
    diff_bias = rel_bias[:, :DIFF_HEADS]
    o_c = _query_blocks(lambda qp, q: _diff_attend(qp, q, kpos, df_k_all, df_v_all, lam, diff_bias),
                        qpos, df_q)
    o_c = (_rms_norm(o_c, subln_g) * (1.0 - lam_init)).reshape(bsz, nq, BR_W)

    ds_q = d_q.reshape(bsz, nq, DSA_HEADS, DSA_DH)
    ds_k_new = d_k.reshape(bsz, nq, DSA_HEADS, DSA_DH)
    ds_v_new = d_v.reshape(bsz, nq, DSA_HEADS, DSA_DH)
    ds_qi = d_qi.reshape(bsz, nq, IDX_HEADS, IDX_DIM)
    ds_ki_new = d_ki
    ds_wi = d_wi * IDX_HEADS ** -0.5
    ds_k_all, ds_v_all, ds_ki_all = keys(ds_k_new, 6), keys(ds_v_new, 7), keys(ds_ki_new, 8)
    dsa_bias = rel_bias[:, DIFF_HEADS:]
    o_d = _query_blocks(
        lambda qp, q, qi, wi: _dsa_attend(qp, q, qi, wi, kpos, ds_k_all, ds_v_all, ds_ki_all, dsa_bias, topk),
        qpos, ds_q, ds_qi, ds_wi).reshape(bsz, nq, BR_W)

    o = jnp.stack([o_a, o_b, o_c, o_d], axis=2)
    br = jnp.einsum('bqnw,nwd->bqnd', o, w_br)
    g = jax.nn.sigmoid(gates.reshape(bsz, nq, N_BRANCH, D_MODEL))
    mix = jnp.einsum('bqd,de->bqe', jnp.sum(g * br, axis=2), w_out)
    x = _layer_norm(DN_ALPHA * x + mix, ln1_g, ln1_b)
    h = jnp.square(jax.nn.relu(jnp.einsum('bqd,df->bqf', x, w_ff1) + b_ff1))
    x = _layer_norm(DN_ALPHA * x + jnp.einsum('bqf,fd->bqd', h, w_ff2) + b_ff2, ln2_g, ln2_b)
    return x, (ckv_new, kpe_new, sb_k_new, sb_v_new, df_k_new, df_v_new, ds_k_new, ds_v_new, ds_ki_new)


def setup_inputs(seed: int = 0) -> dict:
    key = jax.random.key(seed)
    ks = jax.random.split(key, 40)

    def nrm(i, shape, scale):
        return jax.random.normal(ks[i], shape, jnp.float32) * scale

    def gain(i, shape):
        return 1.0 + nrm(i, shape, 0.02)

    c = (DEPTH, DEC_BATCH, PAST_LEN)
    return {
        'x_prompt': nrm(0, (BATCH, SEQ, D_MODEL), 1.0),
        'x_sample': nrm(1, (DEC_BATCH, DEC_SEQ, D_MODEL), 1.0),
        'cache_mla_kv': nrm(2, c + (MLA_KV_LORA,), 1.0),
        'cache_mla_pe': nrm(3, c + (MLA_ROPE,), 1.0),
        'cache_sb_k': nrm(4, c + (SB_HEADS, SB_DH), 1.0),
        'cache_sb_v': nrm(5, c + (SB_HEADS, SB_DH), 1.0),
        'cache_diff_k': nrm(6, c + (DIFF_HEADS, 2, DIFF_DQK), 1.0),
        'cache_diff_v': nrm(7, c + (DIFF_HEADS, DIFF_DV), 1.0),
        'cache_dsa_k': nrm(8, c + (DSA_HEADS, DSA_DH), 1.0),
        'cache_dsa_v': nrm(9, c + (DSA_HEADS, DSA_DH), 1.0),
        'cache_dsa_kidx': nrm(10, c + (IDX_DIM,), 1.0),
        'meta': nrm(11, (N_META, D_MODEL), 1.0),
        'ln_in_g': gain(12, (D_MODEL,)),
        'ln_in_b': nrm(13, (D_MODEL,), 0.02),
        'w_in': nrm(14, (DEPTH, D_MODEL, D_IN), D_MODEL ** -0.5),
        'mla_qnorm_g': gain(15, (DEPTH, MLA_Q_LORA)),
        'mla_w_uq': nrm(16, (DEPTH, MLA_Q_LORA, MLA_HEADS * (MLA_NOPE + MLA_ROPE)), MLA_Q_LORA ** -0.5),
        'mla_kvnorm_g': gain(17, (DEPTH, MLA_KV_LORA)),
        'mla_w_uk': nrm(18, (DEPTH, MLA_KV_LORA, MLA_HEADS, MLA_NOPE), MLA_KV_LORA ** -0.5),
        'mla_w_uv': nrm(19, (DEPTH, MLA_KV_LORA, MLA_HEADS, MLA_V), MLA_KV_LORA ** -0.5),
        'diff_lambda': nrm(20, (DEPTH, 4, DIFF_DQK), 0.1),
        'diff_subln_g': gain(21, (DEPTH, DIFF_DV)),
        'rel_bias': nrm(22, (T5_BUCKETS, DIFF_HEADS + DSA_HEADS), 0.1),
        'w_br': nrm(23, (DEPTH, N_BRANCH, BR_W, D_MODEL), BR_W ** -0.5),
        'w_out': nrm(24, (DEPTH, D_MODEL, D_MODEL), DN_BETA * D_MODEL ** -0.5),
        'ln1_g': gain(25, (DEPTH, D_MODEL)),
        'ln1_b': nrm(26, (DEPTH, D_MODEL), 0.02),
        'w_ff1': nrm(27, (DEPTH, D_MODEL, D_FF), D_MODEL ** -0.5),
        'b_ff1': nrm(28, (DEPTH, D_FF), 0.02),
        'w_ff2': nrm(29, (DEPTH, D_FF, D_MODEL), DN_BETA * D_FF ** -0.5),
        'b_ff2': nrm(30, (DEPTH, D_MODEL), 0.02),
        'ln2_g': gain(31, (DEPTH, D_MODEL)),
        'ln2_b': nrm(32, (DEPTH, D_MODEL), 0.02),
    }


def reference(x_prompt, x_sample, cache_mla_kv, cache_mla_pe, cache_sb_k, cache_sb_v,
              cache_diff_k, cache_diff_v, cache_dsa_k, cache_dsa_v, cache_dsa_kidx,
              meta, ln_in_g, ln_in_b, w_in, mla_qnorm_g, mla_w_uq, mla_kvnorm_g, mla_w_uk, mla_w_uv,
              diff_lambda, diff_subln_g, rel_bias, w_br, w_out, ln1_g, ln1_b,
              w_ff1, b_ff1, w_ff2, b_ff2, ln2_g, ln2_b):
    def run(x, qpos, kpos_past, caches, topk):
        x = _layer_norm(x, ln_in_g, ln_in_b)
        rows = []
        for l in range(DEPTH):
            past = None if caches is None else tuple(cc[l] for cc in caches)
            x, new = _layer(x, qpos, kpos_past, past, l, topk, rel_bias,
                            w_in[l], mla_qnorm_g[l], mla_w_uq[l], mla_kvnorm_g[l], mla_w_uk[l], mla_w_uv[l],
                            diff_lambda[l], diff_subln_g[l], w_br[l], w_out[l], ln1_g[l], ln1_b[l],
                            w_ff1[l], b_ff1[l], w_ff2[l], b_ff2[l], ln2_g[l], ln2_b[l])
            rows.append(new)
        return x, [jnp.stack([r[i] for r in rows], axis=0) for i in range(len(rows[0]))]

    bsz_p, seq_p, _ = x_prompt.shape
    meta_b = jnp.broadcast_to(meta[None].astype(x_prompt.dtype), (bsz_p, N_META, D_MODEL))
    xp = jnp.concatenate([meta_b, x_prompt], axis=1)
    pos_p = jnp.arange(N_META + seq_p, dtype=jnp.int32)
    yp, (p_mla_kv, p_mla_pe, p_sb_k, p_sb_v, p_diff_k, p_diff_v,
         p_dsa_k, p_dsa_v, p_dsa_kidx) = run(xp, pos_p, None, None, min(DSA_TOPK, SEQ // 4))
    y_prompt = yp[:, N_META:]

    past_len = cache_mla_kv.shape[2]
    dec_seq = x_sample.shape[1]
    pos_past = N_META + jnp.arange(past_len, dtype=jnp.int32)
    pos_s = N_META + past_len + jnp.arange(dec_seq, dtype=jnp.int32)
    caches = (cache_mla_kv, cache_mla_pe, cache_sb_k, cache_sb_v, cache_diff_k, cache_diff_v,
              cache_dsa_k, cache_dsa_v, cache_dsa_kidx)
    y_sample, (s_mla_kv, s_mla_pe, s_sb_k, s_sb_v, s_diff_k, s_diff_v,
               s_dsa_k, s_dsa_v, s_dsa_kidx) = run(x_sample, pos_s, pos_past, caches,
                                                   min(DSA_TOPK, (past_len + dec_seq) // 4))

    return (y_prompt, y_sample,
            p_mla_kv, p_mla_pe, p_sb_k, p_sb_v, p_diff_k, p_diff_v, p_dsa_k, p_dsa_v, p_dsa_kidx,
            s_mla_kv, s_mla_pe, s_sb_k, s_sb_v, s_diff_k, s_diff_v, s_dsa_k, s_dsa_v, s_dsa_kidx)
```

```python
import functools
import math
from typing import NamedTuple, Optional

import jax
import jax.numpy as jnp
import numpy as np
from jax import lax
from jax.experimental import pallas as pl
from jax.experimental.pallas import tpu as pltpu

D_MODEL = 1024
CHUNK = 64
N_META = 16
MLA_HEADS = 4
MLA_Q_LORA = 256
MLA_KV_LORA = 128
MLA_NOPE = 64
MLA_ROPE = 32
MLA_V = 64
ROPE_THETA = 10000.0
SB_HEADS = 4
SB_DH = 64
DIFF_HEADS = 4
DIFF_DQK = 32
DIFF_DV = 64
DSA_HEADS = 4
DSA_DH = 64
IDX_HEADS = 8
IDX_DIM = 32
DSA_TOPK = 256
N_BRANCH = 4
BR_W = 256
D_FF = 4 * D_MODEL
T5_BUCKETS = 32
T5_MAX_DIST = 128
LN_EPS = 1e-5
RMS_EPS = 1e-6
NEG_INF = -1e30
DEPTH = 2
DN_ALPHA = (2 * DEPTH) ** 0.25
IN_SIZES = (MLA_Q_LORA, MLA_KV_LORA, MLA_ROPE,
            SB_HEADS * SB_DH, SB_HEADS * SB_DH, SB_HEADS * SB_DH,
            DIFF_HEADS * 2 * DIFF_DQK, DIFF_HEADS * 2 * DIFF_DQK, DIFF_HEADS * DIFF_DV,
            DSA_HEADS * DSA_DH, DSA_HEADS * DSA_DH, DSA_HEADS * DSA_DH,
            IDX_HEADS * IDX_DIM, IDX_DIM, IDX_HEADS,
            N_BRANCH * D_MODEL)
IN_OFFS = tuple(int(s) for s in np.cumsum((0,) + IN_SIZES))

LANES = 128
KEY_BLOCK = 256
VMEM_LIMIT_MB = 56

BF = jnp.bfloat16
F32 = jnp.float32
I32 = jnp.int32

_NEG_BITS = int(np.float32(NEG_INF).view(np.int32))
KEY_NEG = _NEG_BITS ^ ((_NEG_BITS >> 31) & 0x7FFFFFFF)
INT_MIN = -(2 ** 31)


def _cparams(sem):
    return pltpu.CompilerParams(dimension_semantics=sem, vmem_limit_bytes=VMEM_LIMIT_MB * 1024 * 1024)


def _dot(a, b):
    return jnp.dot(a.astype(BF), b.astype(BF), preferred_element_type=F32)


def _dot_nt(a, b):
    return lax.dot_general(a.astype(BF), b.astype(BF), (((1,), (1,)), ((), ())),
                           preferred_element_type=F32)


def _layer_norm(x, g, b):
    mu = jnp.mean(x, axis=-1, keepdims=True)
    xc = x - mu
    var = jnp.mean(xc * xc, axis=-1, keepdims=True)
    return xc * lax.rsqrt(var + LN_EPS) * g + b


def _rms_norm(x, g):
    return x * lax.rsqrt(jnp.mean(x * x, axis=-1, keepdims=True) + RMS_EPS) * g


def _rope_lanes(x, cos, sin):
    lane = lax.broadcasted_iota(I32, x.shape, 1)
    swapped = jnp.where((lane & 31) < 16, pltpu.roll(x, LANES - 16, 1), pltpu.roll(x, 16, 1))
    return x * cos + swapped * sin


def _lane_group(shape, width):
    return lax.broadcasted_iota(I32, shape, 1) // width


class Geom(NamedTuple):
    tq: int
    ntile: int
    npast: Optional[int]
    n_meta: int
    qpos0: int
    lq: int
    lp: int
    topk: int


def _ln_kernel(x_ref, g_ref, b_ref, o_ref):
    o_ref[...] = _layer_norm(x_ref[...], g_ref[...], b_ref[...])


def _ln_call(x, g, b, tm):
    t, d = x.shape
    return pl.pallas_call(
        _ln_kernel, grid=(t // tm,),
        in_specs=[pl.BlockSpec((tm, d), lambda i: (i, 0)),
                  pl.BlockSpec((1, d), lambda i: (0, 0)),
                  pl.BlockSpec((1, d), lambda i: (0, 0))],
        out_specs=pl.BlockSpec((tm, d), lambda i: (i, 0)),
        out_shape=jax.ShapeDtypeStruct((t, d), F32),
        compiler_params=_cparams(("parallel",)), name="ln_in")(x, g.reshape(1, d), b.reshape(1, d))


_WIDE = (3, 4, 5, 6, 7, 8, 9, 10, 11, 12)
_MIX_COLS = 256 + 128 + 256 * len(_WIDE) + 3 * LANES


def _mix_weight(w_in_l):
    def seg(i, pad_to=None):
        w = w_in_l[:, IN_OFFS[i]:IN_OFFS[i + 1]]
        if pad_to is not None:
            w = jnp.pad(w, ((0, 0), (0, pad_to - w.shape[1])))
        return w
    cols = [seg(0), seg(1)] + [seg(i) for i in _WIDE] + [seg(2, LANES), seg(13, LANES), seg(14, LANES)]
    return jnp.concatenate(cols, axis=1).astype(BF)


def _proj_kernel(x_ref, w_ref, kvg_ref, cos_ref, sin_ref, cq_ref, ckv_ref, kpe_ref, *rest):
    wide_refs, (dki_ref, dwi_ref) = rest[:len(_WIDE)], rest[len(_WIDE):]
    xb = x_ref[...].astype(BF)

    def seg(off, width):
        return jnp.dot(xb, w_ref[:, off:off + width], preferred_element_type=F32)

    cq_ref[...] = seg(0, 256)
    ckv_ref[...] = _rms_norm(seg(256, 128), kvg_ref[...])
    off = 384
    for r in wide_refs:
        r[...] = seg(off, 256)
        off += 256
    kpe_ref[...] = _rope_lanes(seg(off, LANES), cos_ref[...], sin_ref[...])[:, :MLA_ROPE]
    dki_ref[...] = seg(off + LANES, LANES)[:, :IDX_DIM]
    dwi_ref[...] = seg(off + 2 * LANES, LANES)[:, :IDX_HEADS]


def _proj_call(x, w_mix, kvg, cos_k, sin_k, tm):
    t, d = x.shape
    widths = [256, 128, MLA_ROPE] + [256] * len(_WIDE) + [IDX_DIM, IDX_HEADS]
    row = lambda w: pl.BlockSpec((tm, w), lambda i: (i, 0))
    return pl.pallas_call(
        _proj_kernel, grid=(t // tm,),
        in_specs=[row(d), pl.BlockSpec((d, _MIX_COLS), lambda i: (0, 0)),
                  pl.BlockSpec((1, 128), lambda i: (0, 0)), row(LANES), row(LANES)],
        out_specs=[row(w) for w in widths],
        out_shape=[jax.ShapeDtypeStruct((t, w), F32) for w in widths],
        compiler_params=_cparams(("parallel",)), name="mix_proj")(x, w_mix, kvg, cos_k, sin_k)


def _merge_kernel(x_ref, oa_ref, ob_ref, oc_ref, od_ref, wg_ref, wbr_ref, wout_ref, g_ref, b_ref, o_ref):
    x = x_ref[...]
    xb = x.astype(BF)
    acc = None
    for n, o_n in enumerate((oa_ref, ob_ref, oc_ref, od_ref)):
        gate = jax.nn.sigmoid(jnp.dot(xb, wg_ref[:, n * D_MODEL:(n + 1) * D_MODEL],
                                      preferred_element_type=F32))
        br = jnp.dot(o_n[...].astype(BF), wbr_ref[n], preferred_element_type=F32)
        acc = gate * br if acc is None else acc + gate * br
    mix = jnp.dot(acc.astype(BF), wout_ref[...], preferred_element_type=F32)
    o_ref[...] = _layer_norm(DN_ALPHA * x + mix, g_ref[...], b_ref[...])


def _merge_call(x, o_a, o_b, o_c, o_d, wg, wbr, wout, g, b, tm):
    t, d = x.shape
    row = lambda w: pl.BlockSpec((tm, w), lambda i: (i, 0))
    return pl.pallas_call(
        _merge_kernel, grid=(t // tm,),
        in_specs=[row(d), row(BR_W), row(BR_W), row(BR_W), row(BR_W),
                  pl.BlockSpec((d, N_BRANCH * d), lambda i: (0, 0)),
                  pl.BlockSpec((N_BRANCH, BR_W, d), lambda i: (0, 0, 0)),
                  pl.BlockSpec((d, d), lambda i: (0, 0)),
                  pl.BlockSpec((1, d), lambda i: (0, 0)), pl.BlockSpec((1, d), lambda i: (0, 0))],
        out_specs=row(d), out_shape=jax.ShapeDtypeStruct((t, d), F32),
        compiler_params=_cparams(("parallel",)), name="merge")(x, o_a, o_b, o_c, o_d, wg, wbr, wout, g, b)


def _ffn_kernel(x_ref, w1_ref, b1_ref, w2_ref, b2_ref, g_ref, b_ref, o_ref):
    x = x_ref[...]
    xb = x.astype(BF)
    acc = None
    for c in range(D_FF // D_MODEL):
        sl = slice(c * D_MODEL, (c + 1) * D_MODEL)
        h = jnp.dot(xb, w1_ref[:, sl], preferred_element_type=F32) + b1_ref[:, sl]
        h = jnp.square(jnp.maximum(h, 0.0))
        y = jnp.dot(h.astype(BF), w2_ref[sl, :], preferred_element_type=F32)
        acc = y if acc is None else acc + y
    o_ref[...] = _layer_norm(DN_ALPHA * x + acc + b2_ref[...], g_ref[...], b_ref[...])


def _ffn_call(x, w1, b1, w2, b2, g, b, tm):
    t, d = x.shape
    row = pl.BlockSpec((tm, d), lambda i: (i, 0))
    vec = lambda w: pl.BlockSpec((1, w), lambda i: (0, 0))
    return pl.pallas_call(
        _ffn_kernel, grid=(t // tm,),
        in_specs=[row, pl.BlockSpec((d, D_FF), lambda i: (0, 0)), vec(D_FF),
                  pl.BlockSpec((D_FF, d), lambda i: (0, 0)), vec(d), vec(d), vec(d)],
        out_specs=row, out_shape=jax.ShapeDtypeStruct((t, d), F32),
        compiler_params=_cparams(("parallel",)), name="ffn")(x, w1, b1, w2, b2, g, b)


def _t5_bucket(rel):
    nb = T5_BUCKETS // 2
    max_exact = nb // 2
    n = jnp.abs(rel)
    nf = jnp.maximum(n, 1).astype(jnp.float32)
    large = max_exact + (jnp.log(nf / max_exact) / math.log(T5_MAX_DIST / max_exact)
                         * (nb - max_exact)).astype(jnp.int32)
    large = jnp.minimum(large, nb - 1)
    return jnp.where(rel > 0, nb, 0) + jnp.where(n < max_exact, n, large)


def _bucket_ids(rel0s, rows, cols):
    r = jnp.arange(rows, dtype=I32)[:, None]
    c = jnp.arange(cols, dtype=I32)[None, :]
    return jnp.stack([_t5_bucket(I32(rel0) + c - r) for rel0 in rel0s], axis=0)


def _bias_kernel(tab_ref, bk_ref, o_ref):
    bk = bk_ref[0]
    for h in range(DIFF_HEADS + DSA_HEADS):
        acc = jnp.zeros(bk.shape, F32)
        for b in range(T5_BUCKETS):
            acc = jnp.where(bk == b, tab_ref[b, h], acc)
        o_ref[0, h] = acc


def _bias_call(rel_bias, bucket_ids):
    n, rows, cols = bucket_ids.shape
    nh = DIFF_HEADS + DSA_HEADS
    return pl.pallas_call(
        _bias_kernel, grid=(n,),
        in_specs=[pl.BlockSpec(memory_space=pltpu.SMEM),
                  pl.BlockSpec((1, rows, cols), lambda i: (i, 0, 0))],
        out_specs=pl.BlockSpec((1, nh, rows, cols), lambda i: (i, 0, 0, 0)),
        out_shape=jax.ShapeDtypeStruct((n, nh, rows, cols), F32),
        compiler_params=_cparams(("parallel",)), name="rel_bias_table")(rel_bias, bucket_ids)


def _tile_walk(geom, frames_tile, meta_tile):
    i = pl.program_id(1)

    @pl.when(i < geom.ntile)
    def _():
        q0 = pl.multiple_of(i * geom.tq, geom.tq)
        npast = i if geom.npast is None else geom.npast
        frames_tile(q0, geom.tq, geom.qpos0 + i * geom.tq, npast)

    if geom.n_meta:
        @pl.when(i == geom.ntile)
        def _():
            meta_tile(geom.ntile * geom.tq, geom.n_meta, 0)


def _no_past(npast):
    return isinstance(npast, int) and npast == 0


def _past_loop(npast, body):
    if not _no_past(npast):
        lax.fori_loop(0, npast, body, 0)


def _chunk_mask(qpos, n):
    r = lax.broadcasted_iota(I32, (n, 1), 0)
    c = lax.broadcasted_iota(I32, (1, n), 1)
    return ((qpos - N_META + c) >> 6) <= ((qpos - N_META + r) >> 6)


def _causal_mask(n):
    return lax.broadcasted_iota(I32, (1, n), 1) < lax.broadcasted_iota(I32, (n, 1), 0)


def _flash_init(m_ref, l_ref, acc_ref, n):
    m_ref[:, 0:n, :] = jnp.full((m_ref.shape[0], n, 1), NEG_INF, F32)
    l_ref[:, 0:n, :] = jnp.zeros((l_ref.shape[0], n, 1), F32)
    acc_ref[:, 0:n, :] = jnp.zeros((acc_ref.shape[0], n, acc_ref.shape[2]), F32)


def _flash_update(m_ref, l_ref, acc_ref, idx, n, s, valid, v_b):
    m_old = m_ref[idx, 0:n, :]
    m_new = jnp.maximum(m_old, jnp.max(s, axis=1, keepdims=True))
    alpha = jnp.exp(m_old - m_new)
    p = jnp.exp(s - m_new)
    if valid is not None:
        p = jnp.where(valid, p, 0.0)
    l_ref[idx, 0:n, :] = alpha * l_ref[idx, 0:n, :] + jnp.sum(p, axis=1, keepdims=True)
    acc_ref[idx, 0:n, :] = alpha * acc_ref[idx, 0:n, :] + jnp.dot(p.astype(BF), v_b,
                                                                  preferred_element_type=F32)
    m_ref[idx, 0:n, :] = m_new


def _slab_spec(arr):
    return pl.BlockSpec((1,) + arr.shape[1:], lambda b, i: (b, 0, 0))


def _cache_spec(arr, layer):
    return pl.BlockSpec((1, 1) + arr.shape[2:], lambda b, i: (layer, b, 0, 0))


def _const_spec(arr):
    nd = arr.ndim
    return pl.BlockSpec(arr.shape, lambda b, i: (0,) * nd)


def _readers(geom, new_refs, past_refs):
    def new(k, r0, n):
        return new_refs[k][0, pl.ds(r0, n), :]
    if geom.npast is None:
        return new, new

    def past(k, r0, n):
        return past_refs[k][0, 0, pl.ds(r0, n), :]
    return new, past


def _attn_grid(geom, bsz):
    return (bsz, geom.ntile + (1 if geom.n_meta else 0))


def _mla_kernel(geom, *refs):
    n_past = 0 if geom.npast is None else 2
    (cq_ref, ckv_ref, kpe_ref), refs = refs[:3], refs[3:]
    past_refs, refs = refs[:n_past], refs[n_past:]
    (cos_ref, sin_ref, qg_ref, wn_ref, wp_ref, wuk_ref, wuv_ref, o_ref, m_s, l_s, acc_s) = refs
    new, past = _readers(geom, (ckv_ref, kpe_ref), past_refs)
    scale = (MLA_NOPE + MLA_ROPE) ** -0.5

    def tile(q0, n, qpos, npast, with_meta):
        qn = _rms_norm(cq_ref[0, pl.ds(q0, n), :], qg_ref[...])
        nope = _dot(qn, wn_ref[...])
        pe = _rope_lanes(_dot(qn, wp_ref[...]), cos_ref[pl.ds(q0, n), :], sin_ref[pl.ds(q0, n), :])
        head = _lane_group(nope.shape, MLA_NOPE)
        q_lat = [_dot(jnp.where(head == h, nope, 0.0), wuk_ref[...]).astype(BF) for h in range(MLA_HEADS)]
        q_pe = [pe[:, h * MLA_ROPE:(h + 1) * MLA_ROPE].astype(BF) for h in range(MLA_HEADS)]
        _flash_init(m_s, l_s, acc_s, n)

        def block(ckv, kpe, mask):
            ckv_b, kpe_b = ckv.astype(BF), kpe.astype(BF)
            for h in range(MLA_HEADS):
                s = (_dot_nt(q_lat[h], ckv_b) + _dot_nt(q_pe[h], kpe_b)) * scale
                if mask is not None:
                    s = jnp.where(mask, s, NEG_INF)
                _flash_update(m_s, l_s, acc_s, h, n, s, mask, ckv_b)

        block(new(0, q0, n), new(1, q0, n), _chunk_mask(qpos, n))

        def past_block(jj, carry):
            r0 = pl.multiple_of((npast - 1 - jj) * KEY_BLOCK, KEY_BLOCK)
            block(past(0, r0, KEY_BLOCK), past(1, r0, KEY_BLOCK), None)
            return carry
        _past_loop(npast, past_block)
        if with_meta:
            m0 = geom.ntile * geom.tq
            block(new(0, m0, geom.n_meta), new(1, m0, geom.n_meta), None)

        out = None
        ohead = _lane_group((n, BR_W), MLA_V)
        for h in range(MLA_HEADS):
            o_lat = acc_s[h, 0:n, :] / l_s[h, 0:n, :]
            o_h = jnp.where(ohead == h, _dot(o_lat, wuv_ref[...]), 0.0)
            out = o_h if out is None else out + o_h
        o_ref[0, pl.ds(q0, n), :] = out

    _tile_walk(geom,
               lambda q0, n, qpos, npast: tile(q0, n, qpos, npast, bool(geom.n_meta)),
               lambda q0, n, qpos: tile(q0, n, qpos, 0, False))


def _mla_call(geom, layer, cq, ckv, kpe, past, cos_q, sin_q, qg, wn, wp, wuk, wuv):
    bsz = cq.shape[0]
    ins = [cq, ckv, kpe] + list(past) + [cos_q, sin_q, qg, wn, wp, wuk, wuv]
    specs = ([_slab_spec(a) for a in (cq, ckv, kpe)] + [_cache_spec(a, layer) for a in past]
             + [_const_spec(a) for a in (cos_q, sin_q, qg, wn, wp, wuk, wuv)])
    return pl.pallas_call(
        functools.partial(_mla_kernel, geom), grid=_attn_grid(geom, bsz),
        in_specs=specs, out_specs=pl.BlockSpec((1, geom.lq, BR_W), lambda b, i: (b, 0, 0)),
        out_shape=jax.ShapeDtypeStruct((bsz, geom.lq, BR_W), F32),
        scratch_shapes=[pltpu.VMEM((MLA_HEADS, geom.tq, 1), F32), pltpu.VMEM((MLA_HEADS, geom.tq, 1), F32),
                        pltpu.VMEM((MLA_HEADS, geom.tq, MLA_KV_LORA), F32)],
        compiler_params=_cparams(("parallel", "arbitrary")), name="mla_attn")(*ins)


def _suffix_matrix(n):
    return jnp.where(lax.broadcasted_iota(I32, (n, n), 0) > lax.broadcasted_iota(I32, (n, n), 1),
                     1.0, 0.0).astype(BF)


def _sb_kernel(geom, *refs):
    n_past = 0 if geom.npast is None else 2
    (q_ref, k_ref, v_ref), refs = refs[:3], refs[3:]
    past_refs, (o_ref, acc_s, carry_s) = refs[:n_past], refs[n_past:]
    new, past = _readers(geom, (k_ref, v_ref), past_refs)
    scale = SB_DH ** -0.5

    def tile(q0, n, qpos, npast, with_meta):
        q = q_ref[0, pl.ds(q0, n), :]
        head = _lane_group(q.shape, SB_DH)
        q_h = [jnp.where(head == h, q, 0.0).astype(BF) for h in range(SB_HEADS)]
        acc_s[:, 0:n, :] = jnp.zeros((SB_HEADS, n, BR_W), F32)
        carry_s[:, 0:n, :] = jnp.zeros((SB_HEADS, n, 1), F32)

        def block(k, v, mask, suffix):
            k_b, v_b = k.astype(BF), v.astype(BF)
            for h in range(SB_HEADS):
                z = _dot_nt(q_h[h], k_b) * scale
                soft = jnp.log1p(jnp.exp(-jnp.abs(z)))
                log_b = jnp.minimum(z, 0.0) - soft
                log_1m = log_b - z
                if mask is not None:
                    log_1m = jnp.where(mask, log_1m, 0.0)
                hi = log_1m.astype(BF)
                lo = (log_1m - hi.astype(F32)).astype(BF)
                later = (jnp.dot(hi, suffix, preferred_element_type=F32)
                         + jnp.dot(lo, suffix, preferred_element_type=F32))
                a = jnp.exp(log_b + later + carry_s[h, 0:n, :])
                if mask is not None:
                    a = jnp.where(mask, a, 0.0)
                acc_s[h, 0:n, :] += jnp.dot(a.astype(BF), v_b, preferred_element_type=F32)
                carry_s[h, 0:n, :] += jnp.sum(log_1m, axis=1, keepdims=True)

        block(new(0, q0, n), new(1, q0, n), _causal_mask(n), _suffix_matrix(n))
        suffix_kb = _suffix_matrix(KEY_BLOCK)

        def past_block(jj, carry):
            r0 = pl.multiple_of((npast - 1 - jj) * KEY_BLOCK, KEY_BLOCK)
            block(past(0, r0, KEY_BLOCK), past(1, r0, KEY_BLOCK), None, suffix_kb)
            return carry
        _past_loop(npast, past_block)
        if with_meta:
            m0 = geom.ntile * geom.tq
            block(new(0, m0, geom.n_meta), new(1, m0, geom.n_meta), None, _suffix_matrix(geom.n_meta))

        out = None
        for h in range(SB_HEADS):
            o_h = jnp.where(head == h, acc_s[h, 0:n, :], 0.0)
            out = o_h if out is None else out + o_h
        o_ref[0, pl.ds(q0, n), :] = out

    _tile_walk(geom,
               lambda q0, n, qpos, npast: tile(q0, n, qpos, npast, bool(geom.n_meta)),
               lambda q0, n, qpos: tile(q0, n, qpos, 0, False))


def _sb_call(geom, layer, q, k, v, past):
    bsz = q.shape[0]
    ins = [q, k, v] + list(past)
    specs = [_slab_spec(a) for a in (q, k, v)] + [_cache_spec(a, layer) for a in past]
    return pl.pallas_call(
        functools.partial(_sb_kernel, geom), grid=_attn_grid(geom, bsz),
        in_specs=specs, out_specs=pl.BlockSpec((1, geom.lq, BR_W), lambda b, i: (b, 0, 0)),
        out_shape=jax.ShapeDtypeStruct((bsz, geom.lq, BR_W), F32),
        scratch_shapes=[pltpu.VMEM((SB_HEADS, geom.tq, BR_W), F32), pltpu.VMEM((SB_HEADS, geom.tq, 1), F32)],
        compiler_params=_cparams(("parallel", "arbitrary")), name="sb_attn")(*ins)


def _bias_readers(geom, bd_ref, bp_ref, bm_ref, bmeta_ref):
    def own(h, n):
        return bd_ref[0, h, 0:n, 0:n]

    def earlier(tile_idx, j, h):
        slot = j if geom.npast is not None else jnp.where(j == tile_idx - 1, 0, 1)
        return bp_ref[slot, h]

    def meta(tile_idx, h):
        return bm_ref[jnp.minimum(tile_idx, 1), h]

    def meta_own(h):
        return bmeta_ref[0, h]
    return own, earlier, meta, meta_own


def _bias_tables(geom, rel_bias):
    tq, kb = geom.tq, KEY_BLOCK
    if geom.npast is None:
        assert tq == kb and kb + 1 >= T5_MAX_DIST
        earlier_rel0 = [-kb, -2 * kb]
        meta_rel0 = [-geom.qpos0, -geom.qpos0 - tq]
        assert geom.qpos0 + tq - (geom.n_meta - 1) >= T5_MAX_DIST
    else:
        earlier_rel0 = [N_META + j * kb - geom.qpos0 for j in range(geom.npast)]
        meta_rel0 = None
    bd = _bias_call(rel_bias, _bucket_ids([0], tq, tq))
    bp = _bias_call(rel_bias, _bucket_ids(earlier_rel0, tq, kb))
    if meta_rel0 is None:
        return bd, bp, None, None
    bm = _bias_call(rel_bias, _bucket_ids(meta_rel0, tq, geom.n_meta))
    bmeta = _bias_call(rel_bias, _bucket_ids([0], geom.n_meta, geom.n_meta))
    return bd, bp, bm, bmeta


def _split_tables(tables, lo, hi):
    return [None if t is None else t[:, lo:hi] for t in tables]


def _diff_kernel(geom, lam_init, *refs):
    n_past = 0 if geom.npast is None else 2
    n_bias = 4 if geom.n_meta else 2
    (q_ref, k_ref, v_ref), refs = refs[:3], refs[3:]
    past_refs, refs = refs[:n_past], refs[n_past:]
    bias_refs, refs = list(refs[:n_bias]) + [None] * (4 - n_bias), refs[n_bias:]
    (lam_ref, sg_ref, o_ref, m_s, l_s, acc_s) = refs
    new, past = _readers(geom, (k_ref, v_ref), past_refs)
    b_own, b_earlier, b_meta, b_meta_own = _bias_readers(geom, *bias_refs)
    scale = DIFF_DQK ** -0.5
    npair = 2 * DIFF_HEADS
    tile_idx = pl.program_id(1)

    def tile(q0, n, qpos, npast, with_meta, is_meta):
        q = q_ref[0, pl.ds(q0, n), :]
        pair = _lane_group(q.shape, DIFF_DQK)
        q_p = [jnp.where(pair == p, q, 0.0).astype(BF) for p in range(npair)]
        _flash_init(m_s, l_s, acc_s, n)

        def block(k, v, mask, bias_of):
            k_b, v_b = k.astype(BF), v.astype(BF)
            for h in range(DIFF_HEADS):
                bias = bias_of(h)
                for i in range(2):
                    p = 2 * h + i
                    s = _dot_nt(q_p[p], k_b) * scale + bias
                    if mask is not None:
                        s = jnp.where(mask, s, NEG_INF)
                    _flash_update(m_s, l_s, acc_s, p, n, s, mask, v_b)

        block(new(0, q0, n), new(1, q0, n), _chunk_mask(qpos, n),
              (lambda h: b_meta_own(h)) if is_meta else (lambda h: b_own(h, n)))

        def past_block(jj, carry):
            j = npast - 1 - jj
            r0 = pl.multiple_of(j * KEY_BLOCK, KEY_BLOCK)
            block(past(0, r0, KEY_BLOCK), past(1, r0, KEY_BLOCK), None, lambda h: b_earlier(tile_idx, j, h))
            return carry
        _past_loop(npast, past_block)
        if with_meta:
            m0 = geom.ntile * geom.tq
            block(new(0, m0, geom.n_meta), new(1, m0, geom.n_meta), None, lambda h: b_meta(tile_idx, h))

        lp = lam_ref[...]
        lam = (jnp.exp(jnp.sum(lp[0:1] * lp[1:2], axis=1, keepdims=True))
               - jnp.exp(jnp.sum(lp[2:3] * lp[3:4], axis=1, keepdims=True)) + lam_init)
        head = _lane_group((n, BR_W), DIFF_DV)
        out = None
        for h in range(DIFF_HEADS):
            o_h = (acc_s[2 * h, 0:n, :] / l_s[2 * h, 0:n, :]
                   - lam * (acc_s[2 * h + 1, 0:n, :] / l_s[2 * h + 1, 0:n, :]))
            o_h = jnp.where(head == h, o_h, 0.0)
            ms = jnp.sum(o_h * o_h, axis=1, keepdims=True) * (1.0 / DIFF_DV)
            o_h = o_h * lax.rsqrt(ms + RMS_EPS) * sg_ref[...] * (1.0 - lam_init)
            out = o_h if out is None else out + o_h
        o_ref[0, pl.ds(q0, n), :] = out

    _tile_walk(geom,
               lambda q0, n, qpos, npast: tile(q0, n, qpos, npast, bool(geom.n_meta), False),
               lambda q0, n, qpos: tile(q0, n, qpos, 0, False, True))


def _diff_call(geom, layer, q, k, v, past, tables, lam_p, sg_tiled):
    bsz = q.shape[0]
    lam_init = 0.8 - 0.6 * math.exp(-0.3 * layer)
    tables = [t for t in tables if t is not None]
    ins = [q, k, v] + list(past) + tables + [lam_p, sg_tiled]
    specs = ([_slab_spec(a) for a in (q, k, v)] + [_cache_spec(a, layer) for a in past]
             + [_const_spec(a) for a in tables + [lam_p, sg_tiled]])
    npair = 2 * DIFF_HEADS
    return pl.pallas_call(
        functools.partial(_diff_kernel, geom, lam_init), grid=_attn_grid(geom, bsz),
        in_specs=specs, out_specs=pl.BlockSpec((1, geom.lq, BR_W), lambda b, i: (b, 0, 0)),
        out_shape=jax.ShapeDtypeStruct((bsz, geom.lq, BR_W), F32),
        scratch_shapes=[pltpu.VMEM((npair, geom.tq, 1), F32), pltpu.VMEM((npair, geom.tq, 1), F32),
                        pltpu.VMEM((npair, geom.tq, BR_W), F32)],
        compiler_params=_cparams(("parallel", "arbitrary")), name="diff_attn")(*ins)


def _sortable(x):
    b = lax.bitcast_convert_type(x + 0.0, I32)
    return b ^ ((b >> 31) & I32(0x7FFFFFFF))


def _dsa_kernel(geom, *refs):
    n_past = 0 if geom.npast is None else 3
    n_bias = 4 if geom.n_meta else 2
    (q_ref, qi_ref, wi_ref, k_ref, v_ref, ki_ref), refs = refs[:6], refs[6:]
    past_refs, refs = refs[:n_past], refs[n_past:]
    bias_refs, refs = list(refs[:n_bias]) + [None] * (4 - n_bias), refs[n_bias:]
    (o_ref, m_s, l_s, acc_s, kd_s, kp_s, km_s) = refs
    new, past = _readers(geom, (k_ref, v_ref, ki_ref), past_refs)
    b_own, b_earlier, b_meta, b_meta_own = _bias_readers(geom, *bias_refs)
    scale = DSA_DH ** -0.5
    tile_idx = pl.program_id(1)
    topk = float(geom.topk)
    past_pos0 = N_META
    index_bits = max(1, int(geom.lq + geom.lp + N_META).bit_length())

    def tile(q0, n, qpos, npast, with_meta, is_meta):
        nm = geom.n_meta
        m0 = geom.ntile * geom.tq
        qi = qi_ref[0, pl.ds(q0, n), :]
        wi = wi_ref[0, pl.ds(q0, n), :] * IDX_HEADS ** -0.5
        qi_h = [qi[:, h * IDX_DIM:(h + 1) * IDX_DIM].astype(BF) for h in range(IDX_HEADS)]
        wi_h = [wi[:, h:h + 1] for h in range(IDX_HEADS)]
        mask_d = _chunk_mask(qpos, n)

        def index_keys(ki):
            ki_b = ki.astype(BF)
            score = None
            for h in range(IDX_HEADS):
                t = wi_h[h] * jnp.maximum(_dot_nt(qi_h[h], ki_b) * IDX_DIM ** -0.5, 0.0)
                score = t if score is None else score + t
            return _sortable(score)

        kd_s[0:n, 0:n] = jnp.where(mask_d, index_keys(new(2, q0, n)), I32(KEY_NEG))

        def score_block(j, carry):
            r0 = pl.multiple_of(j * KEY_BLOCK, KEY_BLOCK)
            kp_s[j, 0:n, :] = index_keys(past(2, r0, KEY_BLOCK))
            return carry
        _past_loop(npast, score_block)
        if with_meta:
            km_s[0:n, :] = index_keys(new(2, m0, nm))

        def count(own_fn, earlier_fn, meta_fn):
            c = jnp.sum(own_fn(kd_s[0:n, 0:n]), axis=1, keepdims=True)
            if not _no_past(npast):
                part = lax.fori_loop(0, npast, lambda j, a: a + earlier_fn(kp_s[j, 0:n, :], j),
                                     jnp.zeros((n, KEY_BLOCK), F32))
                c = c + jnp.sum(part, axis=1, keepdims=True)
            if with_meta:
                c = c + jnp.sum(meta_fn(km_s[0:n, :]), axis=1, keepdims=True)
            return c

        def one(cond):
            return jnp.where(cond, 1.0, 0.0)

        def count_ge(cand):
            return count(lambda x: one(x >= cand), lambda x, j: one(x >= cand), lambda x: one(x >= cand))

        thr = jnp.where(count_ge(jnp.zeros((n, 1), I32)) >= topk, I32(0), I32(INT_MIN))

        def bit_step(b, thr):
            cand = thr | jnp.left_shift(I32(1), I32(30) - b)
            return jnp.where(count_ge(cand) >= topk, cand, thr)
        thr = lax.fori_loop(0, 31, bit_step, thr)

        need = topk - count(lambda x: one(x > thr), lambda x, j: one(x > thr), lambda x: one(x > thr))
        col_d = lax.broadcasted_iota(I32, (1, n), 1)
        col_p = lax.broadcasted_iota(I32, (1, KEY_BLOCK), 1)
        col_m = lax.broadcasted_iota(I32, (1, max(nm, 1)), 1)

        def ties_before(x):
            return count(lambda kk: one((kk == thr) & (col_d < x - qpos)),
                         lambda kk, j: one((kk == thr) & (col_p < x - (past_pos0 + j * KEY_BLOCK))),
                         lambda kk: one((kk == thr) & (col_m < x)))

        def pos_step(b, last_pos):
            cand = last_pos + jnp.left_shift(I32(1), I32(index_bits - 1) - b)
            return jnp.where(ties_before(cand) < need, cand, last_pos)
        last_pos = lax.fori_loop(0, index_bits, pos_step, jnp.zeros((n, 1), I32))

        def selected(kk, kpos0, cols):
            return (kk > thr) | ((kk == thr) & (cols <= last_pos - kpos0))

        q = q_ref[0, pl.ds(q0, n), :]
        head = _lane_group(q.shape, DSA_DH)
        q_h = [jnp.where(head == h, q, 0.0).astype(BF) for h in range(DSA_HEADS)]
        _flash_init(m_s, l_s, acc_s, n)

        def block(k, v, sel, bias_of):
            k_b, v_b = k.astype(BF), v.astype(BF)
            for h in range(DSA_HEADS):
                s = jnp.where(sel, _dot_nt(q_h[h], k_b) * scale + bias_of(h), NEG_INF)
                _flash_update(m_s, l_s, acc_s, h, n, s, sel, v_b)

        block(new(0, q0, n), new(1, q0, n), mask_d & selected(kd_s[0:n, 0:n], qpos, col_d),
              (lambda h: b_meta_own(h)) if is_meta else (lambda h: b_own(h, n)))

        def past_block(jj, carry):
            j = npast - 1 - jj
            r0 = pl.multiple_of(j * KEY_BLOCK, KEY_BLOCK)
            block(past(0, r0, KEY_BLOCK), past(1, r0, KEY_BLOCK),
                  selected(kp_s[j, 0:n, :], past_pos0 + j * KEY_BLOCK, col_p),
                  lambda h: b_earlier(tile_idx, j, h))
            return carry
        _past_loop(npast, past_block)
        if with_meta:
            block(new(0, m0, nm), new(1, m0, nm), selected(km_s[0:n, :], 0, col_m),
                  lambda h: b_meta(tile_idx, h))

        out = None
        for h in range(DSA_HEADS):
            o_h = jnp.where(head == h, acc_s[h, 0:n, :] / l_s[h, 0:n, :], 0.0)
            out = o_h if out is None else out + o_h
        o_ref[0, pl.ds(q0, n), :] = out

    _tile_walk(geom,
               lambda q0, n, qpos, npast: tile(q0, n, qpos, npast, bool(geom.n_meta), False),
               lambda q0, n, qpos: tile(q0, n, qpos, 0, False, True))


def _dsa_call(geom, layer, q, qi, wi, k, v, ki, past, tables):
    bsz = q.shape[0]
    tables = [t for t in tables if t is not None]
    ins = [q, qi, wi, k, v, ki] + list(past) + tables
    specs = ([_slab_spec(a) for a in (q, qi, wi, k, v, ki)] + [_cache_spec(a, layer) for a in past]
             + [_const_spec(a) for a in tables])
    nslots = max(geom.ntile if geom.npast is None else geom.npast, 1)
    return pl.pallas_call(
        functools.partial(_dsa_kernel, geom), grid=_attn_grid(geom, bsz),
        in_specs=specs, out_specs=pl.BlockSpec((1, geom.lq, BR_W), lambda b, i: (b, 0, 0)),
        out_shape=jax.ShapeDtypeStruct((bsz, geom.lq, BR_W), F32),
        scratch_shapes=[pltpu.VMEM((DSA_HEADS, geom.tq, 1), F32), pltpu.VMEM((DSA_HEADS, geom.tq, 1), F32),
                        pltpu.VMEM((DSA_HEADS, geom.tq, BR_W), F32),
                        pltpu.VMEM((geom.tq, geom.tq), I32),
                        pltpu.VMEM((nslots, geom.tq, KEY_BLOCK), I32),
                        pltpu.VMEM((geom.tq, max(geom.n_meta, 1)), I32)],
        compiler_params=_cparams(("parallel", "arbitrary")), name="dsa_attn")(*ins)


def _row_tile(t):
    for tm in (384, 256, 128, 64, 32, 16, 8):
        if t % tm == 0:
            return tm
    raise ValueError(f"token count {t} has no supported row tile")


def _rope_tables(pos):
    half = MLA_ROPE // 2
    inv_freq = ROPE_THETA ** (-jnp.arange(half, dtype=jnp.float32) / half)
    ang = pos.astype(jnp.float32)[:, None] * inv_freq[None, :]
    cos, sin = jnp.cos(ang), jnp.sin(ang)
    return jnp.concatenate([cos, cos], axis=1), jnp.concatenate([-sin, sin], axis=1)


def _run_group(geom, x, pos_rows, caches, weights):
    (ln_in_g, ln_in_b, w_in, mla_qnorm_g, mla_w_uq, mla_kvnorm_g, mla_w_uk, mla_w_uv, diff_lambda,
     diff_subln_g, rel_bias, w_br, w_out, ln1_g, ln1_b, w_ff1, b_ff1, w_ff2, b_ff2, ln2_g, ln2_b) = weights
    bsz, lq, d = x.shape
    t = bsz * lq
    tm = _row_tile(t)
    cos32, sin32 = _rope_tables(pos_rows)
    cos_q, sin_q = jnp.tile(cos32, (1, MLA_HEADS)), jnp.tile(sin32, (1, MLA_HEADS))
    pad = ((0, 0), (0, LANES - MLA_ROPE))
    cos_k = jnp.tile(jnp.pad(cos32, pad, constant_values=1.0), (bsz, 1))
    sin_k = jnp.tile(jnp.pad(sin32, pad), (bsz, 1))
    tables = _bias_tables(geom, rel_bias)
    diff_tables = _split_tables(tables, 0, DIFF_HEADS)
    dsa_tables = _split_tables(tables, DIFF_HEADS, DIFF_HEADS + DSA_HEADS)

    xf = _ln_call(x.reshape(t, d), ln_in_g, ln_in_b, tm)
    rows = []
    for l in range(DEPTH):
        w_mix = _mix_weight(w_in[l])
        (a_cq, ckv, kpe, b_q, b_k, b_v, c_q, c_k, c_v, d_q, d_k, d_v, d_qi, d_ki, d_wi) = _proj_call(
            xf, w_mix, mla_kvnorm_g[l].reshape(1, -1), cos_k, sin_k, tm)
        per = lambda a: a.reshape(bsz, lq, a.shape[-1])
        past = (lambda *idx: [caches[i] for i in idx]) if caches is not None else (lambda *idx: [])

        w_uq = mla_w_uq[l].reshape(MLA_Q_LORA, MLA_HEADS, MLA_NOPE + MLA_ROPE)
        wn = w_uq[:, :, :MLA_NOPE].reshape(MLA_Q_LORA, MLA_HEADS * MLA_NOPE).astype(BF)
        wp = w_uq[:, :, MLA_NOPE:].reshape(MLA_Q_LORA, MLA_HEADS * MLA_ROPE).astype(BF)
        wuk = mla_w_uk[l].reshape(MLA_KV_LORA, MLA_HEADS * MLA_NOPE).T.astype(BF)
        wuv = mla_w_uv[l].reshape(MLA_KV_LORA, MLA_HEADS * MLA_V).astype(BF)
        o_a = _mla_call(geom, l, per(a_cq), per(ckv), per(kpe), past(0, 1), cos_q, sin_q,
                        mla_qnorm_g[l].reshape(1, -1), wn, wp, wuk, wuv)
        o_b = _sb_call(geom, l, per(b_q), per(b_k), per(b_v), past(2, 3))
        o_c = _diff_call(geom, l, per(c_q), per(c_k), per(c_v), past(4, 5), diff_tables, diff_lambda[l],
                         jnp.tile(diff_subln_g[l], DIFF_HEADS).reshape(1, -1))
        o_d = _dsa_call(geom, l, per(d_q), per(d_qi), per(d_wi), per(d_k), per(d_v), per(d_ki),
                        past(6, 7, 8), dsa_tables)

        flat = lambda a: a.reshape(t, BR_W)
        wg = w_in[l][:, IN_OFFS[15]:IN_OFFS[16]].astype(BF)
        x1 = _merge_call(xf, flat(o_a), flat(o_b), flat(o_c), flat(o_d), wg, w_br[l].astype(BF),
                         w_out[l].astype(BF), ln1_g[l].reshape(1, -1), ln1_b[l].reshape(1, -1), tm)
        xf = _ffn_call(x1, w_ff1[l].astype(BF), b_ff1[l].reshape(1, -1), w_ff2[l].astype(BF),
                       b_ff2[l].reshape(1, -1), ln2_g[l].reshape(1, -1), ln2_b[l].reshape(1, -1), tm)
        rows.append([per(a) for a in (ckv, kpe, b_k, b_v, c_k, c_v, d_k, d_v, d_ki)])
    return xf.reshape(bsz, lq, d), rows


_ROW_TRAILING = ((MLA_KV_LORA,), (MLA_ROPE,), (SB_HEADS, SB_DH), (SB_HEADS, SB_DH),
                 (DIFF_HEADS, 2, DIFF_DQK), (DIFF_HEADS, DIFF_DV), (DSA_HEADS, DSA_DH),
                 (DSA_HEADS, DSA_DH), (IDX_DIM,))


def kernel(x_prompt, x_sample, cache_mla_kv, cache_mla_pe, cache_sb_k, cache_sb_v, cache_diff_k, cache_diff_v, cache_dsa_k, cache_dsa_v, cache_dsa_kidx, meta, ln_in_g, ln_in_b, w_in, mla_qnorm_g, mla_w_uq, mla_kvnorm_g, mla_w_uk, mla_w_uv, diff_lambda, diff_subln_g, rel_bias, w_br, w_out, ln1_g, ln1_b, w_ff1, b_ff1, w_ff2, b_ff2, ln2_g, ln2_b):
    weights = (ln_in_g, ln_in_b, w_in, mla_qnorm_g, mla_w_uq, mla_kvnorm_g, mla_w_uk, mla_w_uv, diff_lambda,
               diff_subln_g, rel_bias, w_br, w_out, ln1_g, ln1_b, w_ff1, b_ff1, w_ff2, b_ff2, ln2_g, ln2_b)
    assert w_in.shape[0] == DEPTH and x_prompt.shape[2] == D_MODEL

    bsz_p, seq_p, _ = x_prompt.shape
    assert seq_p % KEY_BLOCK == 0
    meta_b = jnp.broadcast_to(meta[None].astype(x_prompt.dtype), (bsz_p, N_META, D_MODEL))
    xp = jnp.concatenate([x_prompt, meta_b], axis=1)
    pos_p = jnp.concatenate([N_META + jnp.arange(seq_p, dtype=I32), jnp.arange(N_META, dtype=I32)])
    geom_p = Geom(tq=KEY_BLOCK, ntile=seq_p // KEY_BLOCK, npast=None, n_meta=N_META, qpos0=N_META,
                  lq=seq_p + N_META, lp=0, topk=min(DSA_TOPK, seq_p // 4))
    yp, rows_p = _run_group(geom_p, xp, pos_p, None, weights)
    y_prompt = yp[:, :seq_p]
    p_rows = []
    for i, trailing in enumerate(_ROW_TRAILING):
        stacked = jnp.stack([jnp.concatenate([r[i][:, seq_p:], r[i][:, :seq_p]], axis=1) for r in rows_p], axis=0)
        p_rows.append(stacked.reshape(stacked.shape[:3] + trailing))

    past_len = cache_mla_kv.shape[2]
    bsz_s, dec_seq, _ = x_sample.shape
    assert past_len % KEY_BLOCK == 0 and dec_seq % 8 == 0
    assert past_len % CHUNK == 0 and dec_seq <= CHUNK, "new frames must share one chunk"
    caches = [c.reshape(c.shape[:3] + (-1,)) for c in
              (cache_mla_kv, cache_mla_pe, cache_sb_k, cache_sb_v, cache_diff_k, cache_diff_v,
               cache_dsa_k, cache_dsa_v, cache_dsa_kidx)]
    pos_s = N_META + past_len + jnp.arange(dec_seq, dtype=I32)
    geom_s = Geom(tq=dec_seq, ntile=1, npast=past_len // KEY_BLOCK, n_meta=0, qpos0=N_META + past_len,
                  lq=dec_seq, lp=past_len, topk=min(DSA_TOPK, (past_len + dec_seq) // 4))
    y_sample, rows_s = _run_group(geom_s, x_sample, pos_s, caches, weights)
    s_rows = []
    for i, trailing in enumerate(_ROW_TRAILING):
        stacked = jnp.stack([r[i] for r in rows_s], axis=0)
        s_rows.append(stacked.reshape(stacked.shape[:3] + trailing))

    return (y_prompt, y_sample, *p_rows, *s_rows)
```

```python
import functools
import math
from typing import NamedTuple, Optional

import jax
import jax.numpy as jnp
import numpy as np
from jax import lax
from jax.experimental import pallas as pl
from jax.experimental.pallas import tpu as pltpu

D_MODEL = 1024
CHUNK = 64
N_META = 16
MLA_HEADS = 4
MLA_Q_LORA = 256
MLA_KV_LORA = 128
MLA_NOPE = 64
MLA_ROPE = 32
MLA_V = 64
ROPE_THETA = 10000.0
SB_HEADS = 4
SB_DH = 64
DIFF_HEADS = 4
DIFF_DQK = 32
DIFF_DV = 64
DSA_HEADS = 4
DSA_DH = 64
IDX_HEADS = 8
IDX_DIM = 32
DSA_TOPK = 256
N_BRANCH = 4
BR_W = 256
D_FF = 4 * D_MODEL
T5_BUCKETS = 32
T5_MAX_DIST = 128
LN_EPS = 1e-5
RMS_EPS = 1e-6
NEG_INF = -1e30
DEPTH = 2
DN_ALPHA = (2 * DEPTH) ** 0.25
IN_SIZES = (MLA_Q_LORA, MLA_KV_LORA, MLA_ROPE,
            SB_HEADS * SB_DH, SB_HEADS * SB_DH, SB_HEADS * SB_DH,
            DIFF_HEADS * 2 * DIFF_DQK, DIFF_HEADS * 2 * DIFF_DQK, DIFF_HEADS * DIFF_DV,
            DSA_HEADS * DSA_DH, DSA_HEADS * DSA_DH, DSA_HEADS * DSA_DH,
            IDX_HEADS * IDX_DIM, IDX_DIM, IDX_HEADS,
            N_BRANCH * D_MODEL)
IN_OFFS = tuple(int(s) for s in np.cumsum((0,) + IN_SIZES))

LANES = 128
KEY_BLOCK = 256
NARROW = LANES
VMEM_LIMIT_MB = 56

BF = jnp.bfloat16
F32 = jnp.float32
I32 = jnp.int32

_NEG_BITS = int(np.float32(NEG_INF).view(np.int32))
KEY_NEG = _NEG_BITS ^ ((_NEG_BITS >> 31) & 0x7FFFFFFF)
INT_MIN = -(2 ** 31)
INT_MAX = 2 ** 31 - 1


def _cparams(sem):
    return pltpu.CompilerParams(dimension_semantics=sem, vmem_limit_bytes=VMEM_LIMIT_MB * 1024 * 1024)


def _dot(a, b):
    return jnp.dot(a.astype(BF), b.astype(BF), preferred_element_type=F32)


def _dot_nt(a, b):
    return lax.dot_general(a.astype(BF), b.astype(BF), (((1,), (1,)), ((), ())),
                           preferred_element_type=F32)


def _layer_norm(x, g, b):
    mu = jnp.mean(x, axis=-1, keepdims=True)
    xc = x - mu
    var = jnp.mean(xc * xc, axis=-1, keepdims=True)
    return xc * lax.rsqrt(var + LN_EPS) * g + b


def _rms_norm(x, g):
    return x * lax.rsqrt(jnp.mean(x * x, axis=-1, keepdims=True) + RMS_EPS) * g


def _rope_lanes(x, cos, sin):
    lane = lax.broadcasted_iota(I32, x.shape, 1)
    swapped = jnp.where((lane & 31) < 16, pltpu.roll(x, LANES - 16, 1), pltpu.roll(x, 16, 1))
    return x * cos + swapped * sin


def _lane_group(shape, width):
    return lax.broadcasted_iota(I32, shape, 1) // width


def _halves(x, op):
    return op(x[:, :LANES], x[:, LANES:]) if x.shape[1] == 2 * LANES else x


def _rep(x, width):
    return jnp.concatenate([x, x], axis=1) if width == 2 * LANES else x


class Geom(NamedTuple):
    tq: int
    ntile: int
    npast: Optional[int]
    n_meta: int
    qpos0: int
    lq: int
    lp: int
    topk: int

    @property
    def own_wide(self):
        return self.tq == KEY_BLOCK

    @property
    def nslot(self):
        return max(self.ntile if self.npast is None else self.npast, 1)


def _ln_kernel(x_ref, g_ref, b_ref, o_ref):
    o_ref[...] = _layer_norm(x_ref[...], g_ref[...], b_ref[...])


def _ln_call(x, g, b, tm):
    t, d = x.shape
    return pl.pallas_call(
        _ln_kernel, grid=(t // tm,),
        in_specs=[pl.BlockSpec((tm, d), lambda i: (i, 0)),
                  pl.BlockSpec((1, d), lambda i: (0, 0)),
                  pl.BlockSpec((1, d), lambda i: (0, 0))],
        out_specs=pl.BlockSpec((tm, d), lambda i: (i, 0)),
        out_shape=jax.ShapeDtypeStruct((t, d), F32),
        compiler_params=_cparams(("parallel",)), name="ln_in")(x, g.reshape(1, d), b.reshape(1, d))


_WIDE = (3, 4, 5, 6, 7, 8, 9, 10, 11, 12)
_WIDE_F32 = (1, 2, 4, 5, 7, 8)
_MIX_COLS = 256 + 128 + 256 * len(_WIDE) + LANES + 256 + LANES


def _mix_weight(w_in_l):
    def seg(i, pad_to=None):
        w = w_in_l[:, IN_OFFS[i]:IN_OFFS[i + 1]]
        if pad_to is not None:
            w = jnp.pad(w, ((0, 0), (0, pad_to - w.shape[1])))
        return w
    cols = ([seg(0), seg(1)] + [seg(i) for i in _WIDE]
            + [seg(2, LANES), jnp.tile(seg(13), (1, IDX_HEADS)), seg(14, LANES)])
    return jnp.concatenate(cols, axis=1).astype(BF)


def _proj_kernel(x_ref, w_ref, kvg_ref, qg_ref, wn_ref, wp_ref, wuk_ref, cos_ref, sin_ref, *outs):
    (ckv_ref, kpe_ref, bk_ref, bv_ref, ck_ref, cv_ref, dk_ref, dv_ref, dki_ref, dwi_ref,
     qcat_ref, kcat_ref, *wide_bf) = outs
    kit_ref = wide_bf[-1]
    wide_bf = wide_bf[:-1]
    wide_f32 = dict(zip(_WIDE_F32, (bk_ref, bv_ref, ck_ref, cv_ref, dk_ref, dv_ref)))
    xb = x_ref[...].astype(BF)
    cos, sin = cos_ref[...], sin_ref[...]

    def seg(off, width):
        return jnp.dot(xb, w_ref[:, off:off + width], preferred_element_type=F32)

    qn = _rms_norm(seg(0, 256), qg_ref[...])
    nope = _dot(qn, wn_ref[...])
    pe = _rope_lanes(_dot(qn, wp_ref[...]), cos, sin)
    head = _lane_group(nope.shape, MLA_NOPE)
    lane = lax.broadcasted_iota(I32, pe.shape, 1)
    for h in range(MLA_HEADS):
        q_lat = _dot(jnp.where(head == h, nope, 0.0), wuk_ref[...])
        pe_h = pe if h == 0 else pltpu.roll(pe, LANES - h * MLA_ROPE, 1)
        qcat_ref[:, 2 * LANES * h:2 * LANES * h + LANES] = q_lat.astype(BF)
        qcat_ref[:, 2 * LANES * h + LANES:2 * LANES * (h + 1)] = jnp.where(lane < MLA_ROPE, pe_h, 0.0).astype(BF)

    ckv = _rms_norm(seg(256, 128), kvg_ref[...])
    ckv_ref[...] = ckv
    off = 384
    for n, r in enumerate(wide_bf):
        y = seg(off, 256)
        if n in wide_f32:
            wide_f32[n][...] = y
        r[...] = y.astype(BF)
        off += 256
    kpe = _rope_lanes(seg(off, LANES), cos, sin)
    kpe_ref[...] = kpe[:, :MLA_ROPE]
    kcat_ref[...] = jnp.concatenate([ckv, kpe], axis=1).astype(BF)
    kit = seg(off + LANES, 256)
    dki_ref[...] = kit[:, :IDX_DIM]
    kit_ref[...] = kit.astype(BF)
    dwi_ref[...] = seg(off + LANES + 256, LANES)[:, :IDX_HEADS]


def _proj_call(x, w_mix, kvg, qg, wn, wp, wuk, cos_k, sin_k, tm):
    t, d = x.shape
    f32_w = [128, MLA_ROPE, 256, 256, 256, 256, 256, 256, IDX_DIM, IDX_HEADS]
    bf_w = [4 * 256, 256] + [256] * len(_WIDE) + [256]
    row = lambda w: pl.BlockSpec((tm, w), lambda i: (i, 0))
    const = lambda a: pl.BlockSpec(a.shape, lambda i: (0, 0))
    return pl.pallas_call(
        _proj_kernel, grid=(t // tm,),
        in_specs=[row(d), const(w_mix), const(kvg), const(qg), const(wn), const(wp), const(wuk),
                  row(LANES), row(LANES)],
        out_specs=[row(w) for w in f32_w + bf_w],
        out_shape=([jax.ShapeDtypeStruct((t, w), F32) for w in f32_w]
                   + [jax.ShapeDtypeStruct((t, w), BF) for w in bf_w]),
        compiler_params=_cparams(("parallel",)), name="mix_proj")(x, w_mix, kvg, qg, wn, wp, wuk, cos_k, sin_k)


def _merge_kernel(x_ref, oa_ref, ob_ref, oc_ref, od_ref, wg_ref, wbr_ref, wout_ref, g_ref, b_ref, o_ref):
    x = x_ref[...]
    xb = x.astype(BF)
    acc = None
    for n, o_n in enumerate((oa_ref, ob_ref, oc_ref, od_ref)):
        gate = jax.nn.sigmoid(jnp.dot(xb, wg_ref[:, n * D_MODEL:(n + 1) * D_MODEL],
                                      preferred_element_type=F32))
        br = jnp.dot(o_n[...].astype(BF), wbr_ref[n], preferred_element_type=F32)
        acc = gate * br if acc is None else acc + gate * br
    mix = jnp.dot(acc.astype(BF), wout_ref[...], preferred_element_type=F32)
    o_ref[...] = _layer_norm(DN_ALPHA * x + mix, g_ref[...], b_ref[...])


def _merge_call(x, o_a, o_b, o_c, o_d, wg, wbr, wout, g, b, tm):
    t, d = x.shape
    row = lambda w: pl.BlockSpec((tm, w), lambda i: (i, 0))
    return pl.pallas_call(
        _merge_kernel, grid=(t // tm,),
        in_specs=[row(d), row(BR_W), row(BR_W), row(BR_W), row(BR_W),
                  pl.BlockSpec((d, N_BRANCH * d), lambda i: (0, 0)),
                  pl.BlockSpec((N_BRANCH, BR_W, d), lambda i: (0, 0, 0)),
                  pl.BlockSpec((d, d), lambda i: (0, 0)),
                  pl.BlockSpec((1, d), lambda i: (0, 0)), pl.BlockSpec((1, d), lambda i: (0, 0))],
        out_specs=row(d), out_shape=jax.ShapeDtypeStruct((t, d), F32),
        compiler_params=_cparams(("parallel",)), name="merge")(x, o_a, o_b, o_c, o_d, wg, wbr, wout, g, b)


def _ffn_kernel(x_ref, w1_ref, b1_ref, w2_ref, b2_ref, g_ref, b_ref, o_ref):
    x = x_ref[...]
    xb = x.astype(BF)
    acc = None
    for c in range(D_FF // D_MODEL):
        sl = slice(c * D_MODEL, (c + 1) * D_MODEL)
        h = jnp.dot(xb, w1_ref[:, sl], preferred_element_type=F32) + b1_ref[:, sl]
        h = jnp.square(jnp.maximum(h, 0.0))
        y = jnp.dot(h.astype(BF), w2_ref[sl, :], preferred_element_type=F32)
        acc = y if acc is None else acc + y
    o_ref[...] = _layer_norm(DN_ALPHA * x + acc + b2_ref[...], g_ref[...], b_ref[...])


def _ffn_call(x, w1, b1, w2, b2, g, b, tm):
    t, d = x.shape
    row = pl.BlockSpec((tm, d), lambda i: (i, 0))
    vec = lambda w: pl.BlockSpec((1, w), lambda i: (0, 0))
    return pl.pallas_call(
        _ffn_kernel, grid=(t // tm,),
        in_specs=[row, pl.BlockSpec((d, D_FF), lambda i: (0, 0)), vec(D_FF),
                  pl.BlockSpec((D_FF, d), lambda i: (0, 0)), vec(d), vec(d), vec(d)],
        out_specs=row, out_shape=jax.ShapeDtypeStruct((t, d), F32),
        compiler_params=_cparams(("parallel",)), name="ffn")(x, w1, b1, w2, b2, g, b)


def _t5_bucket(rel):
    nb = T5_BUCKETS // 2
    max_exact = nb // 2
    n = jnp.abs(rel)
    nf = jnp.maximum(n, 1).astype(jnp.float32)
    large = max_exact + (jnp.log(nf / max_exact) / math.log(T5_MAX_DIST / max_exact)
                         * (nb - max_exact)).astype(jnp.int32)
    large = jnp.minimum(large, nb - 1)
    return jnp.where(rel > 0, nb, 0) + jnp.where(n < max_exact, n, large)


def _bucket_ids(rel0s, rows, cols, valid_cols, qpos0_for_mask=None):
    r = jnp.arange(rows, dtype=I32)[:, None]
    c = jnp.arange(cols, dtype=I32)[None, :]
    ok = jnp.broadcast_to(c < valid_cols, (rows, cols))
    if qpos0_for_mask is not None:
        ok = ok & (((qpos0_for_mask - N_META + c) >> 6) <= ((qpos0_for_mask - N_META + r) >> 6))
    return jnp.stack([jnp.where(ok, _t5_bucket(I32(rel0) + c - r), -1) for rel0 in rel0s], axis=0)


def _bias_kernel(tab_ref, bk_ref, o_ref):
    bk = bk_ref[0]
    for h in range(DIFF_HEADS + DSA_HEADS):
        acc = jnp.where(bk < 0, NEG_INF, 0.0).astype(F32)
        for b in range(T5_BUCKETS):
            acc = jnp.where(bk == b, tab_ref[b, h], acc)
        o_ref[0, h] = acc


def _bias_call(rel_bias, bucket_ids):
    n, rows, cols = bucket_ids.shape
    nh = DIFF_HEADS + DSA_HEADS
    return pl.pallas_call(
        _bias_kernel, grid=(n,),
        in_specs=[pl.BlockSpec(memory_space=pltpu.SMEM),
                  pl.BlockSpec((1, rows, cols), lambda i: (i, 0, 0))],
        out_specs=pl.BlockSpec((1, nh, rows, cols), lambda i: (i, 0, 0, 0)),
        out_shape=jax.ShapeDtypeStruct((n, nh, rows, cols), F32),
        compiler_params=_cparams(("parallel",)), name="rel_bias_table")(rel_bias, bucket_ids)


def _bias_tables(geom, rel_bias):
    tq, kb = geom.tq, KEY_BLOCK
    own_cols = tq if geom.own_wide else NARROW
    own = _bias_call(rel_bias, _bucket_ids([0], tq, own_cols, tq, geom.qpos0))
    if geom.npast is None:
        assert tq == kb and kb + 1 >= T5_MAX_DIST
        assert geom.qpos0 + tq - (geom.n_meta - 1) >= T5_MAX_DIST
        past = _bias_call(rel_bias, _bucket_ids([-kb, -2 * kb], tq, kb, kb))
        meta = _bias_call(rel_bias, _bucket_ids([-geom.qpos0, -geom.qpos0 - tq], tq, NARROW, geom.n_meta))
        mown = _bias_call(rel_bias, _bucket_ids([0], geom.n_meta, NARROW, geom.n_meta, 0))
        return own, past, meta, mown
    past = _bias_call(rel_bias, _bucket_ids([N_META + j * kb - geom.qpos0 for j in range(geom.npast)], tq, kb, kb))
    return own, past, None, None


def _split_tables(tables, lo, hi):
    return [None if t is None else t[:, lo:hi] for t in tables]


def _no_past(npast):
    return isinstance(npast, int) and npast == 0


def _past_loop(npast, body):
    if not _no_past(npast):
        lax.fori_loop(0, npast, body, 0)


def _tile_walk(geom, frames_tile, meta_tile):
    i = pl.program_id(1)

    @pl.when(i < geom.ntile)
    def _():
        q0 = pl.multiple_of(i * geom.tq, geom.tq)
        npast = i if geom.npast is None else geom.npast
        frames_tile(q0, geom.tq, geom.qpos0 + i * geom.tq, npast)

    if geom.n_meta:
        @pl.when(i == geom.ntile)
        def _():
            meta_tile(geom.ntile * geom.tq, geom.n_meta, 0)


def _own_mask(qpos, n, cols, causal):
    r = lax.broadcasted_iota(I32, (n, 1), 0)
    c = lax.broadcasted_iota(I32, (1, cols), 1)
    if causal:
        return c < r
    return (c < n) & (((qpos - N_META + c) >> 6) <= ((qpos - N_META + r) >> 6))


def _slab_spec(arr):
    return pl.BlockSpec((1,) + arr.shape[1:], lambda b, i: (b, 0, 0))


def _cache_spec(arr, layer):
    return pl.BlockSpec((1, 1) + arr.shape[2:], lambda b, i: (layer, b, 0, 0))


def _past_spec(arr, layer):
    return _cache_spec(arr, layer) if arr.ndim == 4 else _slab_spec(arr)


def _const_spec(arr):
    nd = arr.ndim
    return pl.BlockSpec(arr.shape, lambda b, i: (0,) * nd)


def _readers(geom, new_refs, past_refs):
    def new(k, r0, n):
        return new_refs[k][0, pl.ds(r0, n), :]
    if geom.npast is None:
        return new, new

    def past(k, r0, n):
        ref = past_refs[k]
        return ref[0, 0, pl.ds(r0, n), :] if len(ref.shape) == 4 else ref[0, pl.ds(r0, n), :]
    return new, past


def _fill_narrow(dst_ref, rows):
    dst_ref[...] = jnp.zeros(dst_ref.shape, dst_ref.dtype)
    dst_ref[0:rows.shape[0], :] = rows.astype(dst_ref.dtype)


def _attn_grid(geom, bsz):
    return (bsz, geom.ntile + (1 if geom.n_meta else 0))


class Softmax:
    def __init__(self, refs, nmaps, n):
        self.sw, self.sn, self.mx, self.m, self.ls, self.acc = refs
        self.nmaps, self.n = nmaps, n
        self.mx[:, 0:n, :] = jnp.full((nmaps, n, LANES), NEG_INF, F32)
        self.ls[:, 0:n, :] = jnp.zeros((nmaps, n, LANES), F32)
        self.acc[:, 0:n, :] = jnp.zeros((nmaps, n, self.acc.shape[2]), F32)

    @staticmethod
    def scratch(geom, nmaps, vw):
        return [pltpu.VMEM((nmaps, geom.nslot, geom.tq, KEY_BLOCK), F32),
                pltpu.VMEM((nmaps, 2, geom.tq, NARROW), F32),
                pltpu.VMEM((nmaps, geom.tq, LANES), F32), pltpu.VMEM((nmaps, geom.tq, LANES), F32),
                pltpu.VMEM((nmaps, geom.tq, LANES), F32), pltpu.VMEM((nmaps, geom.tq, vw), F32)]

    def _store(self, p, slot, wide, s):
        if wide:
            self.sw[p, slot, 0:self.n, :] = s
        else:
            self.sn[p, slot, 0:self.n, :] = s

    def _load(self, p, slot, wide):
        return self.sw[p, slot, 0:self.n, :] if wide else self.sn[p, slot, 0:self.n, :]

    def scores(self, slot, wide, q_of, k_b, scale, bias_of=None, extra=None):
        n = self.n
        for p in range(self.nmaps):
            s = _dot_nt(q_of(p), k_b)
            if scale != 1.0:
                s = s * scale
            if bias_of is not None:
                s = s + bias_of(p)
            if extra is not None:
                s = s + extra
            self._store(p, slot, wide, s)
            self.mx[p, 0:n, :] = jnp.maximum(self.mx[p, 0:n, :], _halves(s, jnp.maximum))

    def finish_max(self):
        n = self.n
        for p in range(self.nmaps):
            m = jnp.max(self.mx[p, 0:n, :], axis=1, keepdims=True)
            self.m[p, 0:n, :] = jnp.broadcast_to(m, (n, LANES))

    def values(self, slot, wide, v_b):
        n = self.n
        for p in range(self.nmaps):
            s = self._load(p, slot, wide)
            e = jnp.exp(s - _rep(self.m[p, 0:n, :], s.shape[1]))
            self.ls[p, 0:n, :] += _halves(e, jnp.add)
            self.acc[p, 0:n, :] += jnp.dot(e.astype(BF), v_b, preferred_element_type=F32)

    def result(self, p):
        n = self.n
        return self.acc[p, 0:n, :] / jnp.sum(self.ls[p, 0:n, :], axis=1, keepdims=True)


def _mla_kernel(geom, *refs):
    n_past = 0 if geom.npast is None else 1
    (q_ref, kc_ref), refs = refs[:2], refs[2:]
    past_refs, refs = refs[:n_past], refs[n_past:]
    (wuv_ref, o_ref), refs = refs[:2], refs[2:]
    sm_refs, (kn_s, km_s) = refs[:6], refs[6:]
    new, past = _readers(geom, (kc_ref,), past_refs)
    scale = (MLA_NOPE + MLA_ROPE) ** -0.5

    def tile(q0, n, qpos, npast, with_meta, own_wide):
        sm = Softmax(sm_refs, MLA_HEADS, n)
        q_of = lambda h: q_ref[0, pl.ds(q0, n), 2 * LANES * h:2 * LANES * (h + 1)]
        own_slot = npast if own_wide else 0
        if own_wide:
            own_k = lambda: new(0, q0, n)
        else:
            _fill_narrow(kn_s, new(0, q0, n))
            own_k = lambda: kn_s[...]
        own_cols = n if own_wide else NARROW
        sm.scores(own_slot, own_wide, q_of, own_k(), scale,
                  extra=jnp.where(_own_mask(qpos, n, own_cols, False), 0.0, NEG_INF))

        def p1(j, carry):
            sm.scores(j, True, q_of, past(0, pl.multiple_of(j * KEY_BLOCK, KEY_BLOCK), KEY_BLOCK), scale)
            return carry
        _past_loop(npast, p1)
        if with_meta:
            _fill_narrow(km_s, new(0, geom.ntile * geom.tq, geom.n_meta))
            pad = jnp.where(lax.broadcasted_iota(I32, (1, NARROW), 1) < geom.n_meta, 0.0, NEG_INF)
            sm.scores(1, False, q_of, km_s[...], scale, extra=pad)
        sm.finish_max()

        sm.values(own_slot, own_wide, own_k())

        def p2(j, carry):
            sm.values(j, True, past(0, pl.multiple_of(j * KEY_BLOCK, KEY_BLOCK), KEY_BLOCK))
            return carry
        _past_loop(npast, p2)
        if with_meta:
            sm.values(1, False, km_s[...])

        out = None
        ohead = _lane_group((n, BR_W), MLA_V)
        for h in range(MLA_HEADS):
            o_h = jnp.where(ohead == h, _dot(sm.result(h), wuv_ref[...]), 0.0)
            out = o_h if out is None else out + o_h
        o_ref[0, pl.ds(q0, n), :] = out

    _tile_walk(geom,
               lambda q0, n, qpos, npast: tile(q0, n, qpos, npast, bool(geom.n_meta), geom.own_wide),
               lambda q0, n, qpos: tile(q0, n, qpos, 0, False, False))


def _mla_call(geom, layer, qcat, kcat, past, wuv_pad):
    bsz = qcat.shape[0]
    ins = [qcat, kcat] + list(past) + [wuv_pad]
    specs = ([_slab_spec(a) for a in (qcat, kcat)] + [_past_spec(a, layer) for a in past] + [_const_spec(wuv_pad)])
    return pl.pallas_call(
        functools.partial(_mla_kernel, geom), grid=_attn_grid(geom, bsz),
        in_specs=specs, out_specs=pl.BlockSpec((1, geom.lq, BR_W), lambda b, i: (b, 0, 0)),
        out_shape=jax.ShapeDtypeStruct((bsz, geom.lq, BR_W), F32),
        scratch_shapes=Softmax.scratch(geom, MLA_HEADS, 2 * LANES) + [pltpu.VMEM((NARROW, 2 * LANES), BF)] * 2,
        compiler_params=_cparams(("parallel", "arbitrary")), name="mla_attn")(*ins)


def _suffix_ones(n):
    later = lax.broadcasted_iota(I32, (n, n), 0) > lax.broadcasted_iota(I32, (n, n), 1)
    return jnp.concatenate([jnp.where(later, 1.0, 0.0), jnp.ones((n, LANES), F32)], axis=1).astype(BF)


def _sb_kernel(geom, *refs):
    n_past = 0 if geom.npast is None else 2
    (q_ref, k_ref, v_ref), refs = refs[:3], refs[3:]
    past_refs, (o_ref, acc_s, carry_s, qm_s, kn_s, vn_s, km_s, vm_s) = refs[:n_past], refs[n_past:]
    new, past = _readers(geom, (k_ref, v_ref), past_refs)

    def tile(q0, n, qpos, npast, with_meta, own_wide):
        q = q_ref[0, pl.ds(q0, n), :]
        head = _lane_group(q.shape, SB_DH)
        for h in range(SB_HEADS):
            qm_s[h, 0:n, :] = jnp.where(head == h, q * SB_DH ** -0.5, 0.0).astype(BF)
        acc_s[:, 0:n, :] = jnp.zeros((SB_HEADS, n, BR_W), F32)
        carry_s[:, 0:n, :] = jnp.zeros((SB_HEADS, n, LANES), F32)

        def block(k_b, v_b, mask, tri):
            w = k_b.shape[0]
            for h in range(SB_HEADS):
                z = _dot_nt(qm_s[h, 0:n, :], k_b)
                soft = jnp.log1p(jnp.exp(-jnp.abs(z)))
                log_b = jnp.minimum(z, 0.0) - soft
                log_1m = log_b - z
                if mask is not None:
                    log_1m = jnp.where(mask, log_1m, 0.0)
                hi = log_1m.astype(BF)
                lo = (log_1m - hi.astype(F32)).astype(BF)
                sums = (jnp.dot(hi, tri, preferred_element_type=F32)
                        + jnp.dot(lo, tri, preferred_element_type=F32))
                carry = carry_s[h, 0:n, :]
                a = jnp.exp(log_b + sums[:, :w] + _rep(carry, w))
                if mask is not None:
                    a = jnp.where(mask, a, 0.0)
                acc_s[h, 0:n, :] += jnp.dot(a.astype(BF), v_b, preferred_element_type=F32)
                carry_s[h, 0:n, :] = carry + sums[:, w:]

        if own_wide:
            block(new(0, q0, n), new(1, q0, n), _own_mask(qpos, n, n, True), _suffix_ones(n))
        else:
            _fill_narrow(kn_s, new(0, q0, n))
            _fill_narrow(vn_s, new(1, q0, n))
            block(kn_s[...], vn_s[...], _own_mask(qpos, n, NARROW, True), _suffix_ones(NARROW))
        tri_kb = _suffix_ones(KEY_BLOCK)

        def past_block(jj, carry):
            r0 = pl.multiple_of((npast - 1 - jj) * KEY_BLOCK, KEY_BLOCK)
            block(past(0, r0, KEY_BLOCK).astype(BF), past(1, r0, KEY_BLOCK).astype(BF), None, tri_kb)
            return carry
        _past_loop(npast, past_block)
        if with_meta:
            m0 = geom.ntile * geom.tq
            _fill_narrow(km_s, new(0, m0, geom.n_meta))
            _fill_narrow(vm_s, new(1, m0, geom.n_meta))
            pad = jnp.broadcast_to(lax.broadcasted_iota(I32, (1, NARROW), 1) < geom.n_meta, (n, NARROW))
            block(km_s[...], vm_s[...], pad, _suffix_ones(NARROW))

        out = None
        for h in range(SB_HEADS):
            o_h = jnp.where(head == h, acc_s[h, 0:n, :], 0.0)
            out = o_h if out is None else out + o_h
        o_ref[0, pl.ds(q0, n), :] = out

    _tile_walk(geom,
               lambda q0, n, qpos, npast: tile(q0, n, qpos, npast, bool(geom.n_meta), geom.own_wide),
               lambda q0, n, qpos: tile(q0, n, qpos, 0, False, False))


def _sb_call(geom, layer, q, k, v, past):
    bsz = q.shape[0]
    ins = [q, k, v] + list(past)
    specs = [_slab_spec(a) for a in (q, k, v)] + [_past_spec(a, layer) for a in past]
    return pl.pallas_call(
        functools.partial(_sb_kernel, geom), grid=_attn_grid(geom, bsz),
        in_specs=specs, out_specs=pl.BlockSpec((1, geom.lq, BR_W), lambda b, i: (b, 0, 0)),
        out_shape=jax.ShapeDtypeStruct((bsz, geom.lq, BR_W), F32),
        scratch_shapes=[pltpu.VMEM((SB_HEADS, geom.tq, BR_W), F32), pltpu.VMEM((SB_HEADS, geom.tq, LANES), F32),
                        pltpu.VMEM((SB_HEADS, geom.tq, BR_W), BF)] + [pltpu.VMEM((NARROW, BR_W), BF)] * 4,
        compiler_params=_cparams(("parallel", "arbitrary")), name="sb_attn")(*ins)


def _bias_readers(geom, own_ref, past_ref, meta_ref, mown_ref, tile_idx):
    def own(h, n):
        return own_ref[0, h, 0:n, :]

    def earlier(j, h):
        slot = j if geom.npast is not None else jnp.where(j == tile_idx - 1, 0, 1)
        return past_ref[slot, h]

    def meta(h):
        return meta_ref[jnp.minimum(tile_idx, 1), h]

    def meta_own(h):
        return mown_ref[0, h]
    return own, earlier, meta, meta_own


def _diff_kernel(geom, lam_init, *refs):
    n_past = 0 if geom.npast is None else 2
    n_bias = 4 if geom.n_meta else 2
    (q_ref, k_ref, v_ref), refs = refs[:3], refs[3:]
    past_refs, refs = refs[:n_past], refs[n_past:]
    bias_refs, refs = list(refs[:n_bias]) + [None] * (4 - n_bias), refs[n_bias:]
    (lam_ref, sg_ref, o_ref), refs = refs[:3], refs[3:]
    sm_refs, (qm_s, kn_s, vn_s, km_s, vm_s) = refs[:6], refs[6:]
    new, past = _readers(geom, (k_ref, v_ref), past_refs)
    b_own, b_earlier, b_meta, b_meta_own = _bias_readers(geom, *bias_refs, pl.program_id(1))
    scale = DIFF_DQK ** -0.5
    npair = 2 * DIFF_HEADS

    def tile(q0, n, qpos, npast, with_meta, own_wide, is_meta):
        sm = Softmax(sm_refs, npair, n)
        q = q_ref[0, pl.ds(q0, n), :]
        pair = _lane_group(q.shape, DIFF_DQK)
        for p in range(npair):
            qm_s[p, 0:n, :] = jnp.where(pair == p, q, jnp.zeros_like(q))
        q_of = lambda p: qm_s[p, 0:n, :]
        own_slot = npast if own_wide else 0
        if own_wide:
            own_k, own_v = (lambda: new(0, q0, n)), (lambda: new(1, q0, n))
        else:
            _fill_narrow(kn_s, new(0, q0, n))
            _fill_narrow(vn_s, new(1, q0, n))
            own_k, own_v = (lambda: kn_s[...]), (lambda: vn_s[...])
        sm.scores(own_slot, own_wide, q_of, own_k(), scale,
                  bias_of=(lambda p: b_meta_own(p // 2)) if is_meta else (lambda p: b_own(p // 2, n)))

        def p1(j, carry):
            sm.scores(j, True, q_of, past(0, pl.multiple_of(j * KEY_BLOCK, KEY_BLOCK), KEY_BLOCK), scale,
                      bias_of=lambda p: b_earlier(j, p // 2))
            return carry
        _past_loop(npast, p1)
        if with_meta:
            m0 = geom.ntile * geom.tq
            _fill_narrow(km_s, new(0, m0, geom.n_meta))
            _fill_narrow(vm_s, new(1, m0, geom.n_meta))
            sm.scores(1, False, q_of, km_s[...], scale, bias_of=lambda p: b_meta(p // 2))
        sm.finish_max()

        sm.values(own_slot, own_wide, own_v().astype(BF))

        def p2(j, carry):
            sm.values(j, True, past(1, pl.multiple_of(j * KEY_BLOCK, KEY_BLOCK), KEY_BLOCK).astype(BF))
            return carry
        _past_loop(npast, p2)
        if with_meta:
            sm.values(1, False, vm_s[...])

        lp = lam_ref[...]
        lam = (jnp.exp(jnp.sum(lp[0:1] * lp[1:2], axis=1, keepdims=True))
               - jnp.exp(jnp.sum(lp[2:3] * lp[3:4], axis=1, keepdims=True)) + lam_init)
        head = _lane_group((n, BR_W), DIFF_DV)
        out = None
        for h in range(DIFF_HEADS):
            o_h = jnp.where(head == h, sm.result(2 * h) - lam * sm.result(2 * h + 1), 0.0)
            ms = jnp.sum(o_h * o_h, axis=1, keepdims=True) * (1.0 / DIFF_DV)
            o_h = o_h * lax.rsqrt(ms + RMS_EPS) * sg_ref[...] * (1.0 - lam_init)
            out = o_h if out is None else out + o_h
        o_ref[0, pl.ds(q0, n), :] = out

    _tile_walk(geom,
               lambda q0, n, qpos, npast: tile(q0, n, qpos, npast, bool(geom.n_meta), geom.own_wide, False),
               lambda q0, n, qpos: tile(q0, n, qpos, 0, False, False, True))


def _diff_call(geom, layer, q, k, v, past, tables, lam_p, sg_tiled):
    bsz = q.shape[0]
    lam_init = 0.8 - 0.6 * math.exp(-0.3 * layer)
    tables = [t for t in tables if t is not None]
    ins = [q, k, v] + list(past) + tables + [lam_p, sg_tiled]
    specs = ([_slab_spec(a) for a in (q, k, v)] + [_past_spec(a, layer) for a in past]
             + [_const_spec(a) for a in tables + [lam_p, sg_tiled]])
    npair = 2 * DIFF_HEADS
    return pl.pallas_call(
        functools.partial(_diff_kernel, geom, lam_init), grid=_attn_grid(geom, bsz),
        in_specs=specs, out_specs=pl.BlockSpec((1, geom.lq, BR_W), lambda b, i: (b, 0, 0)),
        out_shape=jax.ShapeDtypeStruct((bsz, geom.lq, BR_W), F32),
        scratch_shapes=(Softmax.scratch(geom, npair, BR_W) + [pltpu.VMEM((npair, geom.tq, BR_W), BF)]
                        + [pltpu.VMEM((NARROW, BR_W), BF)] * 4),
        compiler_params=_cparams(("parallel", "arbitrary")), name="diff_attn")(*ins)


def _sortable(x):
    b = lax.bitcast_convert_type(x + 0.0, I32)
    return b ^ ((b >> 31) & I32(0x7FFFFFFF))


def _dsa_kernel(geom, *refs):
    n_past = 0 if geom.npast is None else 3
    n_bias = 4 if geom.n_meta else 2
    (q_ref, qi_ref, wi_ref, k_ref, v_ref, kit_ref), refs = refs[:6], refs[6:]
    past_refs, refs = refs[:n_past], refs[n_past:]
    bias_refs, refs = list(refs[:n_bias]) + [None] * (4 - n_bias), refs[n_bias:]
    o_ref, refs = refs[0], refs[1:]
    sm_refs, (qm_s, qim_s, wib_s, kw_s, kn_s, last_s, kno_s, vno_s, kio_s, knm_s, vnm_s, kim_s) = refs[:6], refs[6:]
    new, past = _readers(geom, (k_ref, v_ref, kit_ref), past_refs)
    b_own, b_earlier, b_meta, b_meta_own = _bias_readers(geom, *bias_refs, pl.program_id(1))
    topk = float(geom.topk)
    past_pos0 = N_META
    index_bits = max(1, int(geom.lq + geom.lp + N_META).bit_length())
    ones_count = jnp.ones((LANES, LANES), BF)

    def tile(q0, n, qpos, npast, with_meta, own_wide, is_meta):
        nm = geom.n_meta
        m0 = geom.ntile * geom.tq
        own_slot = npast if own_wide else 0
        own_cols = n if own_wide else NARROW

        qi = qi_ref[0, pl.ds(q0, n), :]
        igrp = _lane_group(qi.shape, IDX_DIM)
        wi = wi_ref[0, pl.ds(q0, n), :] * (IDX_HEADS ** -0.5 * IDX_DIM ** -0.5)
        for h in range(IDX_HEADS):
            qim_s[h, 0:n, :] = jnp.where(igrp == h, qi, jnp.zeros_like(qi))
            wib_s[h, 0:n, :] = jnp.broadcast_to(wi[:, h:h + 1], (n, LANES))
        q = q_ref[0, pl.ds(q0, n), :]
        head = _lane_group(q.shape, DSA_DH)
        for h in range(DSA_HEADS):
            qm_s[h, 0:n, :] = jnp.where(head == h, q * DSA_DH ** -0.5, 0.0).astype(BF)
        if own_wide:
            own_k, own_v, own_ki = (lambda: new(0, q0, n)), (lambda: new(1, q0, n)), (lambda: new(2, q0, n))
        else:
            _fill_narrow(kno_s, new(0, q0, n))
            _fill_narrow(vno_s, new(1, q0, n))
            _fill_narrow(kio_s, new(2, q0, n))
            own_k, own_v, own_ki = (lambda: kno_s[...]), (lambda: vno_s[...]), (lambda: kio_s[...])
        if with_meta:
            _fill_narrow(knm_s, new(0, m0, nm))
            _fill_narrow(vnm_s, new(1, m0, nm))
            _fill_narrow(kim_s, new(2, m0, nm))

        def index_keys(kit_b):
            score = None
            for h in range(IDX_HEADS):
                t = _rep(wib_s[h, 0:n, :], kit_b.shape[0]) * jnp.maximum(_dot_nt(qim_s[h, 0:n, :], kit_b), 0.0)
                score = t if score is None else score + t
            return _sortable(score)

        own_keys = jnp.where(_own_mask(qpos, n, own_cols, False), index_keys(own_ki()), I32(KEY_NEG))
        if own_wide:
            kw_s[own_slot, 0:n, :] = own_keys
        else:
            pad = lax.broadcasted_iota(I32, (1, NARROW), 1) < n
            kn_s[0, 0:n, :] = jnp.where(pad, own_keys, I32(INT_MIN))

        def score_block(j, carry):
            kw_s[j, 0:n, :] = index_keys(past(2, pl.multiple_of(j * KEY_BLOCK, KEY_BLOCK), KEY_BLOCK).astype(BF))
            return carry
        _past_loop(npast, score_block)
        if with_meta:
            pad = lax.broadcasted_iota(I32, (1, NARROW), 1) < nm
            kn_s[1, 0:n, :] = jnp.where(pad, index_keys(kim_s[...]), I32(INT_MIN))
        nwide = npast + 1 if own_wide else npast

        def one(cond):
            return jnp.where(cond, 1.0, 0.0)

        def count(wide_fn, own_narrow_fn, meta_fn):
            part = jnp.zeros((n, LANES), F32)
            if not _no_past(nwide):
                part = lax.fori_loop(0, nwide, lambda j, a: a + wide_fn(kw_s[j, 0:n, :], j), part)
            if not own_wide:
                part = part + own_narrow_fn(kn_s[0, 0:n, :])
            if with_meta:
                part = part + meta_fn(kn_s[1, 0:n, :])
            return jnp.dot(part.astype(BF), ones_count, preferred_element_type=F32)

        def count_cmp(cmp):
            return count(lambda x, j: _halves(one(cmp(x, KEY_BLOCK)), jnp.add),
                         lambda x: one(cmp(x, NARROW)), lambda x: one(cmp(x, NARROW)))

        zero = jnp.zeros((n, LANES), I32)
        c0 = count_cmp(lambda x, w: x >= _rep(zero, w))
        thr0 = jnp.where(c0 >= topk, I32(0), I32(INT_MIN))
        cnt0 = jnp.where(c0 >= topk, c0, topk + 1.0)

        def bit_step(b, state):
            thr, cnt = state
            cand = thr | jnp.left_shift(I32(1), I32(30) - b)
            c = count_cmp(lambda x, w: x >= _rep(cand, w))
            keep = c >= topk
            return jnp.where(keep, cand, thr), jnp.where(keep, c, cnt)
        thr, cnt = lax.fori_loop(0, 31, bit_step, (thr0, cnt0))

        last_s[0:n, :] = jnp.full((n, LANES), INT_MAX, I32)
        has_ties = jnp.max(one((cnt > topk) & (thr > I32(KEY_NEG)))) > 0.0

        @pl.when(has_ties)
        def _():
            need = topk - count_cmp(lambda x, w: x > _rep(thr, w))
            colw = lax.broadcasted_iota(I32, (1, KEY_BLOCK), 1)
            coln = lax.broadcasted_iota(I32, (1, NARROW), 1)

            def wide_pos0(j):
                own_pos = qpos if own_wide else 0
                return jnp.where(j == own_slot, own_pos, past_pos0 + j * KEY_BLOCK) if own_wide \
                    else past_pos0 + j * KEY_BLOCK

            def ties_before(x):
                return count(lambda kk, j: _halves(one((kk == _rep(thr, KEY_BLOCK))
                                                       & (colw < _rep(x, KEY_BLOCK) - wide_pos0(j))), jnp.add),
                             lambda kk: one((kk == thr) & (coln < x - qpos)),
                             lambda kk: one((kk == thr) & (coln < x)))

            def pos_step(b, last_pos):
                cand = last_pos + jnp.left_shift(I32(1), I32(index_bits - 1) - b)
                return jnp.where(ties_before(cand) < need, cand, last_pos)
            last_s[0:n, :] = lax.fori_loop(0, index_bits, pos_step, jnp.zeros((n, LANES), I32))

        last_pos = last_s[0:n, :]

        def sel_bias(kk, kpos0):
            w = kk.shape[1]
            cols = lax.broadcasted_iota(I32, (1, w), 1)
            t = _rep(thr, w)
            sel = (kk > t) | ((kk == t) & (cols <= _rep(last_pos, w) - kpos0))
            return jnp.where(sel, 0.0, NEG_INF)

        sm = Softmax(sm_refs, DSA_HEADS, n)
        q_of = lambda h: qm_s[h, 0:n, :]
        own_kk = kw_s[own_slot, 0:n, :] if own_wide else kn_s[0, 0:n, :]
        sm.scores(own_slot, own_wide, q_of, own_k(), 1.0,
                  bias_of=(lambda h: b_meta_own(h)) if is_meta else (lambda h: b_own(h, n)),
                  extra=sel_bias(own_kk, qpos))

        def p1(j, carry):
            sm.scores(j, True, q_of, past(0, pl.multiple_of(j * KEY_BLOCK, KEY_BLOCK), KEY_BLOCK).astype(BF), 1.0,
                      bias_of=lambda h: b_earlier(j, h),
                      extra=sel_bias(kw_s[j, 0:n, :], past_pos0 + j * KEY_BLOCK))
            return carry
        _past_loop(npast, p1)
        if with_meta:
            sm.scores(1, False, q_of, knm_s[...], 1.0, bias_of=lambda h: b_meta(h),
                      extra=sel_bias(kn_s[1, 0:n, :], 0))
        sm.finish_max()

        sm.values(own_slot, own_wide, own_v().astype(BF))

        def p2(j, carry):
            sm.values(j, True, past(1, pl.multiple_of(j * KEY_BLOCK, KEY_BLOCK), KEY_BLOCK).astype(BF))
            return carry
        _past_loop(npast, p2)
        if with_meta:
            sm.values(1, False, vnm_s[...])

        out = None
        for h in range(DSA_HEADS):
            o_h = jnp.where(head == h, sm.result(h), 0.0)
            out = o_h if out is None else out + o_h
        o_ref[0, pl.ds(q0, n), :] = out

    _tile_walk(geom,
               lambda q0, n, qpos, npast: tile(q0, n, qpos, npast, bool(geom.n_meta), geom.own_wide, False),
               lambda q0, n, qpos: tile(q0, n, qpos, 0, False, False, True))


def _dsa_call(geom, layer, q, qi, wi, k, v, kit, past, tables):
    bsz = q.shape[0]
    tables = [t for t in tables if t is not None]
    ins = [q, qi, wi, k, v, kit] + list(past) + tables
    specs = ([_slab_spec(a) for a in (q, qi, wi, k, v, kit)] + [_past_spec(a, layer) for a in past]
             + [_const_spec(a) for a in tables])
    tq = geom.tq
    return pl.pallas_call(
        functools.partial(_dsa_kernel, geom), grid=_attn_grid(geom, bsz),
        in_specs=specs, out_specs=pl.BlockSpec((1, geom.lq, BR_W), lambda b, i: (b, 0, 0)),
        out_shape=jax.ShapeDtypeStruct((bsz, geom.lq, BR_W), F32),
        scratch_shapes=(Softmax.scratch(geom, DSA_HEADS, BR_W)
                        + [pltpu.VMEM((DSA_HEADS, tq, BR_W), BF), pltpu.VMEM((IDX_HEADS, tq, BR_W), BF),
                           pltpu.VMEM((IDX_HEADS, tq, LANES), F32),
                           pltpu.VMEM((geom.nslot, tq, KEY_BLOCK), I32), pltpu.VMEM((2, tq, NARROW), I32),
                           pltpu.VMEM((tq, LANES), I32)]
                        + [pltpu.VMEM((NARROW, BR_W), BF)] * 6),
        compiler_params=_cparams(("parallel", "arbitrary")), name="dsa_attn")(*ins)


def _row_tile(t):
    for tm in (384, 256, 128, 64, 32, 16, 8):
        if t % tm == 0:
            return tm
    raise ValueError(f"token count {t} has no supported row tile")


def _rope_tables(pos):
    half = MLA_ROPE // 2
    inv_freq = ROPE_THETA ** (-jnp.arange(half, dtype=jnp.float32) / half)
    ang = pos.astype(jnp.float32)[:, None] * inv_freq[None, :]
    cos, sin = jnp.cos(ang), jnp.sin(ang)
    return jnp.concatenate([cos, cos], axis=1), jnp.concatenate([-sin, sin], axis=1)


def _run_group(geom, x, pos_rows, caches, weights):
    (ln_in_g, ln_in_b, w_in, mla_qnorm_g, mla_w_uq, mla_kvnorm_g, mla_w_uk, mla_w_uv, diff_lambda,
     diff_subln_g, rel_bias, w_br, w_out, ln1_g, ln1_b, w_ff1, b_ff1, w_ff2, b_ff2, ln2_g, ln2_b) = weights
    bsz, lq, d = x.shape
    t = bsz * lq
    tm = _row_tile(t)
    cos32, sin32 = _rope_tables(pos_rows)
    cos_k = jnp.tile(cos32, (bsz, MLA_HEADS))
    sin_k = jnp.tile(sin32, (bsz, MLA_HEADS))
    tables = _bias_tables(geom, rel_bias)
    diff_tables = _split_tables(tables, 0, DIFF_HEADS)
    dsa_tables = _split_tables(tables, DIFF_HEADS, DIFF_HEADS + DSA_HEADS)

    xf = _ln_call(x.reshape(t, d), ln_in_g, ln_in_b, tm)
    rows = []
    for l in range(DEPTH):
        w_uq = mla_w_uq[l].reshape(MLA_Q_LORA, MLA_HEADS, MLA_NOPE + MLA_ROPE)
        wn = w_uq[:, :, :MLA_NOPE].reshape(MLA_Q_LORA, MLA_HEADS * MLA_NOPE).astype(BF)
        wp = w_uq[:, :, MLA_NOPE:].reshape(MLA_Q_LORA, MLA_HEADS * MLA_ROPE).astype(BF)
        wuk = mla_w_uk[l].reshape(MLA_KV_LORA, MLA_HEADS * MLA_NOPE).T.astype(BF)
        wuv = jnp.pad(mla_w_uv[l].reshape(MLA_KV_LORA, MLA_HEADS * MLA_V),
                      ((0, 2 * LANES - MLA_KV_LORA), (0, 0))).astype(BF)
        (ckv, kpe, b_k, b_v, c_k, c_v, d_k, d_v, d_ki, d_wi,
         qcat, kcat, bq_b, bk_b, bv_b, cq_b, ck_b, cv_b, dq_b, dk_b, dv_b, dqi_b, kit_b) = _proj_call(
            xf, _mix_weight(w_in[l]), mla_kvnorm_g[l].reshape(1, -1), mla_qnorm_g[l].reshape(1, -1),
            wn, wp, wuk, cos_k, sin_k, tm)
        per = lambda a: a.reshape(bsz, lq, a.shape[-1])
        if caches is None:
            past = lambda *idx: []
            mla_past, kit_past = [], []
        else:
            past = lambda *idx: [caches[i] for i in idx]
            zeros = jnp.zeros(caches[0].shape[1:3] + (2 * LANES - MLA_KV_LORA - MLA_ROPE,), F32)
            mla_past = [jnp.concatenate([caches[0][l], caches[1][l], zeros], axis=-1).astype(BF)]
            kit_past = [jnp.tile(caches[8][l], (1, 1, IDX_HEADS)).astype(BF)]

        o_a = _mla_call(geom, l, per(qcat), per(kcat), mla_past, wuv)
        o_b = _sb_call(geom, l, per(bq_b), per(bk_b), per(bv_b), past(2, 3))
        o_c = _diff_call(geom, l, per(cq_b), per(ck_b), per(cv_b), past(4, 5), diff_tables, diff_lambda[l],
                         jnp.tile(diff_subln_g[l], DIFF_HEADS).reshape(1, -1))
        o_d = _dsa_call(geom, l, per(dq_b), per(dqi_b), per(d_wi), per(dk_b), per(dv_b), per(kit_b),
                        past(6, 7) + kit_past, dsa_tables)

        flat = lambda a: a.reshape(t, BR_W)
        wg = w_in[l][:, IN_OFFS[15]:IN_OFFS[16]].astype(BF)
        x1 = _merge_call(xf, flat(o_a), flat(o_b), flat(o_c), flat(o_d), wg, w_br[l].astype(BF),
                         w_out[l].astype(BF), ln1_g[l].reshape(1, -1), ln1_b[l].reshape(1, -1), tm)
        xf = _ffn_call(x1, w_ff1[l].astype(BF), b_ff1[l].reshape(1, -1), w_ff2[l].astype(BF),
                       b_ff2[l].reshape(1, -1), ln2_g[l].reshape(1, -1), ln2_b[l].reshape(1, -1), tm)
        rows.append([per(a) for a in (ckv, kpe, b_k, b_v, c_k, c_v, d_k, d_v, d_ki)])
    return xf.reshape(bsz, lq, d), rows


_ROW_TRAILING = ((MLA_KV_LORA,), (MLA_ROPE,), (SB_HEADS, SB_DH), (SB_HEADS, SB_DH),
                 (DIFF_HEADS, 2, DIFF_DQK), (DIFF_HEADS, DIFF_DV), (DSA_HEADS, DSA_DH),
                 (DSA_HEADS, DSA_DH), (IDX_DIM,))


def kernel(x_prompt, x_sample, cache_mla_kv, cache_mla_pe, cache_sb_k, cache_sb_v, cache_diff_k, cache_diff_v, cache_dsa_k, cache_dsa_v, cache_dsa_kidx, meta, ln_in_g, ln_in_b, w_in, mla_qnorm_g, mla_w_uq, mla_kvnorm_g, mla_w_uk, mla_w_uv, diff_lambda, diff_subln_g, rel_bias, w_br, w_out, ln1_g, ln1_b, w_ff1, b_ff1, w_ff2, b_ff2, ln2_g, ln2_b):
    weights = (ln_in_g, ln_in_b, w_in, mla_qnorm_g, mla_w_uq, mla_kvnorm_g, mla_w_uk, mla_w_uv, diff_lambda,
               diff_subln_g, rel_bias, w_br, w_out, ln1_g, ln1_b, w_ff1, b_ff1, w_ff2, b_ff2, ln2_g, ln2_b)
    assert w_in.shape[0] == DEPTH and x_prompt.shape[2] == D_MODEL

    bsz_p, seq_p, _ = x_prompt.shape
    assert seq_p % KEY_BLOCK == 0
    meta_b = jnp.broadcast_to(meta[None].astype(x_prompt.dtype), (bsz_p, N_META, D_MODEL))
    xp = jnp.concatenate([x_prompt, meta_b], axis=1)
    pos_p = jnp.concatenate([N_META + jnp.arange(seq_p, dtype=I32), jnp.arange(N_META, dtype=I32)])
    geom_p = Geom(tq=KEY_BLOCK, ntile=seq_p // KEY_BLOCK, npast=None, n_meta=N_META, qpos0=N_META,
                  lq=seq_p + N_META, lp=0, topk=min(DSA_TOPK, seq_p // 4))
    yp, rows_p = _run_group(geom_p, xp, pos_p, None, weights)
    y_prompt = yp[:, :seq_p]
    p_rows = []
    for i, trailing in enumerate(_ROW_TRAILING):
        stacked = jnp.stack([jnp.concatenate([r[i][:, seq_p:], r[i][:, :seq_p]], axis=1) for r in rows_p], axis=0)
        p_rows.append(stacked.reshape(stacked.shape[:3] + trailing))

    past_len = cache_mla_kv.shape[2]
    bsz_s, dec_seq, _ = x_sample.shape
    assert past_len % KEY_BLOCK == 0 and dec_seq % 16 == 0 and dec_seq <= NARROW
    assert past_len % CHUNK == 0 and dec_seq <= CHUNK, "new frames must share one chunk"
    caches = [c.reshape(c.shape[:3] + (-1,)) for c in
              (cache_mla_kv, cache_mla_pe, cache_sb_k, cache_sb_v, cache_diff_k, cache_diff_v,
               cache_dsa_k, cache_dsa_v, cache_dsa_kidx)]
    pos_s = N_META + past_len + jnp.arange(dec_seq, dtype=I32)
    geom_s = Geom(tq=dec_seq, ntile=1, npast=past_len // KEY_BLOCK, n_meta=0, qpos0=N_META + past_len,
                  lq=dec_seq, lp=past_len, topk=min(DSA_TOPK, (past_len + dec_seq) // 4))
    y_sample, rows_s = _run_group(geom_s, x_sample, pos_s, caches, weights)
    s_rows = []
    for i, trailing in enumerate(_ROW_TRAILING):
        stacked = jnp.stack([r[i] for r in rows_s], axis=0)
        s_rows.append(stacked.reshape(stacked.shape[:3] + trailing))

    return (y_prompt, y_sample, *p_rows, *s_rows)
```

```python
import functools
import math
from typing import NamedTuple, Optional

import jax
import jax.numpy as jnp
import numpy as np
from jax import lax
from jax.experimental import pallas as pl
from jax.experimental.pallas import tpu as pltpu

D_MODEL = 1024
CHUNK = 64
N_META = 16
MLA_HEADS = 4
MLA_Q_LORA = 256
MLA_KV_LORA = 128
MLA_NOPE = 64
MLA_ROPE = 32
MLA_V = 64
ROPE_THETA = 10000.0
SB_HEADS = 4
SB_DH = 64
DIFF_HEADS = 4
DIFF_DQK = 32
DIFF_DV = 64
DSA_HEADS = 4
DSA_DH = 64
IDX_HEADS = 8
IDX_DIM = 32
DSA_TOPK = 256
N_BRANCH = 4
BR_W = 256
D_FF = 4 * D_MODEL
T5_BUCKETS = 32
T5_MAX_DIST = 128
LN_EPS = 1e-5
RMS_EPS = 1e-6
NEG_INF = -1e30
DEPTH = 2
DN_ALPHA = (2 * DEPTH) ** 0.25
IN_SIZES = (MLA_Q_LORA, MLA_KV_LORA, MLA_ROPE,
            SB_HEADS * SB_DH, SB_HEADS * SB_DH, SB_HEADS * SB_DH,
            DIFF_HEADS * 2 * DIFF_DQK, DIFF_HEADS * 2 * DIFF_DQK, DIFF_HEADS * DIFF_DV,
            DSA_HEADS * DSA_DH, DSA_HEADS * DSA_DH, DSA_HEADS * DSA_DH,
            IDX_HEADS * IDX_DIM, IDX_DIM, IDX_HEADS,
            N_BRANCH * D_MODEL)
IN_OFFS = tuple(int(s) for s in np.cumsum((0,) + IN_SIZES))

LANES = 128
KEY_BLOCK = 256
NARROW = LANES
VMEM_LIMIT_MB = 56

BF = jnp.bfloat16
F32 = jnp.float32
I32 = jnp.int32
I16 = jnp.int16
HALF16 = 2 ** 15

_NEG_BITS = int(np.float32(NEG_INF).view(np.int32))
KEY_NEG = _NEG_BITS ^ ((_NEG_BITS >> 31) & 0x7FFFFFFF)
INT_MIN = -(2 ** 31)
INT_MAX = 2 ** 31 - 1


def _cparams(sem):
    return pltpu.CompilerParams(dimension_semantics=sem, vmem_limit_bytes=VMEM_LIMIT_MB * 1024 * 1024)


def _dot(a, b):
    return jnp.dot(a.astype(BF), b.astype(BF), preferred_element_type=F32)


def _dot_nt(a, b):
    return lax.dot_general(a.astype(BF), b.astype(BF), (((1,), (1,)), ((), ())),
                           preferred_element_type=F32)


def _layer_norm(x, g, b):
    mu = jnp.mean(x, axis=-1, keepdims=True)
    xc = x - mu
    var = jnp.mean(xc * xc, axis=-1, keepdims=True)
    return xc * lax.rsqrt(var + LN_EPS) * g + b


def _rms_norm(x, g):
    return x * lax.rsqrt(jnp.mean(x * x, axis=-1, keepdims=True) + RMS_EPS) * g


def _rope_lanes(x, cos, sin):
    lane = lax.broadcasted_iota(I32, x.shape, 1)
    swapped = jnp.where((lane & 31) < 16, pltpu.roll(x, LANES - 16, 1), pltpu.roll(x, 16, 1))
    return x * cos + swapped * sin


def _lane_group(shape, width):
    return lax.broadcasted_iota(I32, shape, 1) // width


def _halves(x, op):
    return op(x[:, :LANES], x[:, LANES:]) if x.shape[1] == 2 * LANES else x


def _rep(x, width):
    return jnp.concatenate([x, x], axis=1) if width == 2 * LANES else x


class Geom(NamedTuple):
    tq: int
    ntile: int
    npast: Optional[int]
    n_meta: int
    qpos0: int
    row0: int
    lq: int
    lp: int
    topk: int

    @property
    def own_wide(self):
        return self.tq == KEY_BLOCK

    @property
    def nslot(self):
        return max(self.ntile if self.npast is None else self.npast, 1)


def _ln_kernel(x_ref, g_ref, b_ref, o_ref):
    o_ref[...] = _layer_norm(x_ref[...], g_ref[...], b_ref[...])


def _ln_call(x, g, b, tm):
    t, d = x.shape
    return pl.pallas_call(
        _ln_kernel, grid=(t // tm,),
        in_specs=[pl.BlockSpec((tm, d), lambda i: (i, 0)),
                  pl.BlockSpec((1, d), lambda i: (0, 0)),
                  pl.BlockSpec((1, d), lambda i: (0, 0))],
        out_specs=pl.BlockSpec((tm, d), lambda i: (i, 0)),
        out_shape=jax.ShapeDtypeStruct((t, d), F32),
        compiler_params=_cparams(("parallel",)), name="ln_in")(x, g.reshape(1, d), b.reshape(1, d))


_WIDE = (3, 4, 5, 6, 7, 8, 9, 10, 11, 12)
_WIDE_F32 = (1, 2, 4, 5, 7, 8)
_MIX_COLS = 256 + 128 + 256 * len(_WIDE) + LANES + 256 + LANES


def _mix_weight(w_in_l):
    def seg(i, pad_to=None):
        w = w_in_l[:, IN_OFFS[i]:IN_OFFS[i + 1]]
        if pad_to is not None:
            w = jnp.pad(w, ((0, 0), (0, pad_to - w.shape[1])))
        return w
    cols = ([seg(0), seg(1)] + [seg(i) for i in _WIDE]
            + [seg(2, LANES), jnp.tile(seg(13), (1, IDX_HEADS)), seg(14, LANES)])
    return jnp.concatenate(cols, axis=1).astype(BF)


def _proj_kernel(x_ref, w_ref, kvg_ref, qg_ref, wn_ref, wp_ref, wuk_ref, cos_ref, sin_ref, *outs):
    (ckv_ref, kpe_ref, bk_ref, bv_ref, ck_ref, cv_ref, dk_ref, dv_ref, dki_ref, dwi_ref,
     qcat_ref, kcat_ref, *wide_bf) = outs
    kit_ref = wide_bf[-1]
    wide_bf = wide_bf[:-1]
    wide_f32 = dict(zip(_WIDE_F32, (bk_ref, bv_ref, ck_ref, cv_ref, dk_ref, dv_ref)))
    xb = x_ref[...].astype(BF)
    cos, sin = cos_ref[...], sin_ref[...]

    def seg(off, width):
        return jnp.dot(xb, w_ref[:, off:off + width], preferred_element_type=F32)

    qn = _rms_norm(seg(0, 256), qg_ref[...])
    nope = _dot(qn, wn_ref[...])
    pe = _rope_lanes(_dot(qn, wp_ref[...]), cos, sin)
    head = _lane_group(nope.shape, MLA_NOPE)
    lane = lax.broadcasted_iota(I32, pe.shape, 1)
    for h in range(MLA_HEADS):
        q_lat = _dot(jnp.where(head == h, nope, 0.0), wuk_ref[...])
        pe_h = pe if h == 0 else pltpu.roll(pe, LANES - h * MLA_ROPE, 1)
        qcat_ref[:, 2 * LANES * h:2 * LANES * h + LANES] = q_lat.astype(BF)
        qcat_ref[:, 2 * LANES * h + LANES:2 * LANES * (h + 1)] = jnp.where(lane < MLA_ROPE, pe_h, 0.0).astype(BF)

    ckv = _rms_norm(seg(256, 128), kvg_ref[...])
    ckv_ref[...] = ckv
    off = 384
    for n, r in enumerate(wide_bf):
        y = seg(off, 256)
        if n in wide_f32:
            wide_f32[n][...] = y
        r[...] = y.astype(BF)
        off += 256
    kpe = _rope_lanes(seg(off, LANES), cos, sin)
    kpe_ref[...] = kpe[:, :MLA_ROPE]
    kcat_ref[...] = jnp.concatenate([ckv, kpe], axis=1).astype(BF)
    kit = seg(off + LANES, 256)
    dki_ref[...] = kit[:, :IDX_DIM]
    kit_ref[...] = kit.astype(BF)
    dwi_ref[...] = seg(off + LANES + 256, LANES)[:, :IDX_HEADS]


def _proj_call(x, w_mix, kvg, qg, wn, wp, wuk, cos_k, sin_k, tm):
    t, d = x.shape
    f32_w = [128, MLA_ROPE, 256, 256, 256, 256, 256, 256, IDX_DIM, IDX_HEADS]
    bf_w = [4 * 256, 256] + [256] * len(_WIDE) + [256]
    row = lambda w: pl.BlockSpec((tm, w), lambda i: (i, 0))
    const = lambda a: pl.BlockSpec(a.shape, lambda i: (0, 0))
    return pl.pallas_call(
        _proj_kernel, grid=(t // tm,),
        in_specs=[row(d), const(w_mix), const(kvg), const(qg), const(wn), const(wp), const(wuk),
                  row(LANES), row(LANES)],
        out_specs=[row(w) for w in f32_w + bf_w],
        out_shape=([jax.ShapeDtypeStruct((t, w), F32) for w in f32_w]
                   + [jax.ShapeDtypeStruct((t, w), BF) for w in bf_w]),
        compiler_params=_cparams(("parallel",)), name="mix_proj")(x, w_mix, kvg, qg, wn, wp, wuk, cos_k, sin_k)


def _merge_kernel(x_ref, oa_ref, ob_ref, oc_ref, od_ref, wg_ref, wbr_ref, wout_ref, g_ref, b_ref, o_ref):
    x = x_ref[...]
    xb = x.astype(BF)
    acc = None
    for n, o_n in enumerate((oa_ref, ob_ref, oc_ref, od_ref)):
        gate = jax.nn.sigmoid(jnp.dot(xb, wg_ref[:, n * D_MODEL:(n + 1) * D_MODEL],
                                      preferred_element_type=F32))
        br = jnp.dot(o_n[...].astype(BF), wbr_ref[n], preferred_element_type=F32)
        acc = gate * br if acc is None else acc + gate * br
    mix = jnp.dot(acc.astype(BF), wout_ref[...], preferred_element_type=F32)
    o_ref[...] = _layer_norm(DN_ALPHA * x + mix, g_ref[...], b_ref[...])


def _merge_call(x, o_a, o_b, o_c, o_d, wg, wbr, wout, g, b, tm):
    t, d = x.shape
    row = lambda w: pl.BlockSpec((tm, w), lambda i: (i, 0))
    return pl.pallas_call(
        _merge_kernel, grid=(t // tm,),
        in_specs=[row(d), row(BR_W), row(BR_W), row(BR_W), row(BR_W),
                  pl.BlockSpec((d, N_BRANCH * d), lambda i: (0, 0)),
                  pl.BlockSpec((N_BRANCH, BR_W, d), lambda i: (0, 0, 0)),
                  pl.BlockSpec((d, d), lambda i: (0, 0)),
                  pl.BlockSpec((1, d), lambda i: (0, 0)), pl.BlockSpec((1, d), lambda i: (0, 0))],
        out_specs=row(d), out_shape=jax.ShapeDtypeStruct((t, d), F32),
        compiler_params=_cparams(("parallel",)), name="merge")(x, o_a, o_b, o_c, o_d, wg, wbr, wout, g, b)


def _ffn_kernel(x_ref, w1_ref, b1_ref, w2_ref, b2_ref, g_ref, b_ref, o_ref):
    x = x_ref[...]
    xb = x.astype(BF)
    acc = None
    for c in range(D_FF // D_MODEL):
        sl = slice(c * D_MODEL, (c + 1) * D_MODEL)
        h = jnp.dot(xb, w1_ref[:, sl], preferred_element_type=F32) + b1_ref[:, sl]
        h = jnp.square(jnp.maximum(h, 0.0))
        y = jnp.dot(h.astype(BF), w2_ref[sl, :], preferred_element_type=F32)
        acc = y if acc is None else acc + y
    o_ref[...] = _layer_norm(DN_ALPHA * x + acc + b2_ref[...], g_ref[...], b_ref[...])


def _ffn_call(x, w1, b1, w2, b2, g, b, tm):
    t, d = x.shape
    row = pl.BlockSpec((tm, d), lambda i: (i, 0))
    vec = lambda w: pl.BlockSpec((1, w), lambda i: (0, 0))
    return pl.pallas_call(
        _ffn_kernel, grid=(t // tm,),
        in_specs=[row, pl.BlockSpec((d, D_FF), lambda i: (0, 0)), vec(D_FF),
                  pl.BlockSpec((D_FF, d), lambda i: (0, 0)), vec(d), vec(d), vec(d)],
        out_specs=row, out_shape=jax.ShapeDtypeStruct((t, d), F32),
        compiler_params=_cparams(("parallel",)), name="ffn")(x, w1, b1, w2, b2, g, b)


def _t5_bucket(rel):
    nb = T5_BUCKETS // 2
    max_exact = nb // 2
    n = jnp.abs(rel)
    nf = jnp.maximum(n, 1).astype(jnp.float32)
    large = max_exact + (jnp.log(nf / max_exact) / math.log(T5_MAX_DIST / max_exact)
                         * (nb - max_exact)).astype(jnp.int32)
    large = jnp.minimum(large, nb - 1)
    return jnp.where(rel > 0, nb, 0) + jnp.where(n < max_exact, n, large)


def _bucket_ids(rel0s, rows, cols, valid_cols, qpos0_for_mask=None):
    r = jnp.arange(rows, dtype=I32)[:, None]
    c = jnp.arange(cols, dtype=I32)[None, :]
    ok = jnp.broadcast_to(c < valid_cols, (rows, cols))
    if qpos0_for_mask is not None:
        ok = ok & (((qpos0_for_mask - N_META + c) >> 6) <= ((qpos0_for_mask - N_META + r) >> 6))
    return jnp.stack([jnp.where(ok, _t5_bucket(I32(rel0) + c - r), -1) for rel0 in rel0s], axis=0)


def _bias_kernel(tab_ref, bk_ref, o_ref):
    bk = bk_ref[0]
    for h in range(DIFF_HEADS + DSA_HEADS):
        acc = jnp.where(bk < 0, NEG_INF, 0.0).astype(F32)
        for b in range(T5_BUCKETS):
            acc = jnp.where(bk == b, tab_ref[b, h], acc)
        o_ref[0, h] = acc


def _bias_call(rel_bias, bucket_ids):
    n, rows, cols = bucket_ids.shape
    nh = DIFF_HEADS + DSA_HEADS
    return pl.pallas_call(
        _bias_kernel, grid=(n,),
        in_specs=[pl.BlockSpec(memory_space=pltpu.SMEM),
                  pl.BlockSpec((1, rows, cols), lambda i: (i, 0, 0))],
        out_specs=pl.BlockSpec((1, nh, rows, cols), lambda i: (i, 0, 0, 0)),
        out_shape=jax.ShapeDtypeStruct((n, nh, rows, cols), F32),
        compiler_params=_cparams(("parallel",)), name="rel_bias_table")(rel_bias, bucket_ids)


def _bias_tables(geom, rel_bias):
    tq, kb = geom.tq, KEY_BLOCK
    own_cols = tq if geom.own_wide else NARROW
    own = _bias_call(rel_bias, _bucket_ids([0], tq, own_cols, tq, geom.qpos0))
    if geom.npast is None:
        assert tq == kb and kb + 1 >= T5_MAX_DIST
        assert geom.qpos0 + tq - (geom.n_meta - 1) >= T5_MAX_DIST
        past = _bias_call(rel_bias, _bucket_ids([-kb, -2 * kb], tq, kb, kb))
        meta = _bias_call(rel_bias, _bucket_ids([-geom.qpos0, -geom.qpos0 - tq], tq, NARROW, geom.n_meta))
        mown = _bias_call(rel_bias, _bucket_ids([0], geom.n_meta, NARROW, geom.n_meta, 0))
        return own, past, meta, mown
    past = _bias_call(rel_bias, _bucket_ids([N_META + j * kb - geom.qpos0 for j in range(geom.npast)], tq, kb, kb))
    return own, past, None, None


def _split_tables(tables, lo, hi):
    return [None if t is None else t[:, lo:hi] for t in tables]


def _no_past(npast):
    return isinstance(npast, int) and npast == 0


STATIC_UNROLL = 4


def _aligned(x, m):
    return x if isinstance(x, int) else pl.multiple_of(x, m)


def _past_loop(npast, body):
    if isinstance(npast, int) and npast <= STATIC_UNROLL:
        for j in range(npast):
            body(j, 0)
    else:
        lax.fori_loop(0, npast, body, 0)


def _tile_walk(geom, frames_tile, meta_tile):
    i = pl.program_id(1)

    @pl.when(i < geom.ntile)
    def _():
        q0 = _aligned(geom.row0 + i * geom.tq, 16)
        npast = i if geom.npast is None else geom.npast
        frames_tile(q0, geom.tq, geom.qpos0 + i * geom.tq, npast)

    if geom.n_meta:
        @pl.when(i == geom.ntile)
        def _():
            meta_tile(0, geom.n_meta, 0)


def _own_mask(qpos, n, cols, causal):
    r = lax.broadcasted_iota(I32, (n, 1), 0)
    c = lax.broadcasted_iota(I32, (1, cols), 1)
    if causal:
        return c < r
    return (c < n) & (((qpos - N_META + c) >> 6) <= ((qpos - N_META + r) >> 6))


def _slab_spec(arr):
    return pl.BlockSpec((1,) + arr.shape[1:], lambda b, i: (b, 0, 0))


def _cache_spec(arr, layer):
    return pl.BlockSpec((1, 1) + arr.shape[2:], lambda b, i: (layer, b, 0, 0))


def _past_spec(arr, layer):
    return _cache_spec(arr, layer) if arr.ndim == 4 else _slab_spec(arr)


def _const_spec(arr):
    nd = arr.ndim
    return pl.BlockSpec(arr.shape, lambda b, i: (0,) * nd)


def _readers(geom, new_refs, past_refs):
    def new(k, r0, n):
        return new_refs[k][0, pl.ds(r0, n), :]

    def past(k, j):
        if geom.npast is None:
            return new_refs[k][0, pl.ds(_aligned(geom.row0 + j * KEY_BLOCK, 16), KEY_BLOCK), :]
        ref, r0 = past_refs[k], _aligned(j * KEY_BLOCK, KEY_BLOCK)
        return ref[0, 0, pl.ds(r0, KEY_BLOCK), :] if len(ref.shape) == 4 else ref[0, pl.ds(r0, KEY_BLOCK), :]
    return new, past


def _fill_narrow(dst_ref, rows):
    dst_ref[...] = jnp.zeros(dst_ref.shape, dst_ref.dtype)
    dst_ref[0:rows.shape[0], :] = rows.astype(dst_ref.dtype)


def _attn_grid(geom, bsz):
    return (bsz, geom.ntile + (1 if geom.n_meta else 0))


class Softmax:
    def __init__(self, refs, nmaps, n):
        self.sw, self.sn, self.mx, self.m, self.ls, self.acc = refs
        self.nmaps, self.n = nmaps, n
        self.mx[:, 0:n, :] = jnp.full((nmaps, n, LANES), NEG_INF, F32)
        self.ls[:, 0:n, :] = jnp.zeros((nmaps, n, LANES), F32)
        self.acc[:, 0:n, :] = jnp.zeros((nmaps, n, self.acc.shape[2]), F32)

    @staticmethod
    def scratch(geom, nmaps, vw):
        return [pltpu.VMEM((nmaps, geom.nslot, geom.tq, KEY_BLOCK), F32),
                pltpu.VMEM((nmaps, 2, geom.tq, NARROW), F32),
                pltpu.VMEM((nmaps, geom.tq, LANES), F32), pltpu.VMEM((nmaps, geom.tq, LANES), F32),
                pltpu.VMEM((nmaps, geom.tq, LANES), F32), pltpu.VMEM((nmaps, geom.tq, vw), F32)]

    def _store(self, p, slot, wide, s):
        if wide:
            self.sw[p, slot, 0:self.n, :] = s
        else:
            self.sn[p, slot, 0:self.n, :] = s

    def _load(self, p, slot, wide):
        return self.sw[p, slot, 0:self.n, :] if wide else self.sn[p, slot, 0:self.n, :]

    def scores(self, slot, wide, q_of, k_b, scale, bias_of=None, extra=None):
        n = self.n
        for p in range(self.nmaps):
            s = _dot_nt(q_of(p), k_b)
            if scale != 1.0:
                s = s * scale
            if bias_of is not None:
                s = s + bias_of(p)
            if extra is not None:
                s = s + extra
            self._store(p, slot, wide, s)
            self.mx[p, 0:n, :] = jnp.maximum(self.mx[p, 0:n, :], _halves(s, jnp.maximum))

    def finish_max(self):
        n = self.n
        for p in range(self.nmaps):
            m = jnp.max(self.mx[p, 0:n, :], axis=1, keepdims=True)
            self.m[p, 0:n, :] = jnp.broadcast_to(m, (n, LANES))

    def values(self, slot, wide, v_b):
        n = self.n
        for p in range(self.nmaps):
            s = self._load(p, slot, wide)
            e = jnp.exp(s - _rep(self.m[p, 0:n, :], s.shape[1]))
            self.ls[p, 0:n, :] += _halves(e, jnp.add)
            self.acc[p, 0:n, :] += jnp.dot(e.astype(BF), v_b, preferred_element_type=F32)

    def result(self, p):
        n = self.n
        return self.acc[p, 0:n, :] / jnp.sum(self.ls[p, 0:n, :], axis=1, keepdims=True)


def _mla_kernel(geom, *refs):
    n_past = 0 if geom.npast is None else 1
    (q_ref, kc_ref), refs = refs[:2], refs[2:]
    past_refs, refs = refs[:n_past], refs[n_past:]
    (wuv_ref, o_ref), refs = refs[:2], refs[2:]
    sm_refs, (kn_s, km_s) = refs[:6], refs[6:]
    new, past = _readers(geom, (kc_ref,), past_refs)
    scale = (MLA_NOPE + MLA_ROPE) ** -0.5

    def tile(q0, n, qpos, npast, with_meta, own_wide):
        sm = Softmax(sm_refs, MLA_HEADS, n)
        q_of = lambda h: q_ref[0, pl.ds(q0, n), 2 * LANES * h:2 * LANES * (h + 1)]
        own_slot = npast if own_wide else 0
        if own_wide:
            own_k = lambda: new(0, q0, n)
        else:
            _fill_narrow(kn_s, new(0, q0, n))
            own_k = lambda: kn_s[...]
        own_cols = n if own_wide else NARROW
        sm.scores(own_slot, own_wide, q_of, own_k(), scale,
                  extra=jnp.where(_own_mask(qpos, n, own_cols, False), 0.0, NEG_INF))

        def p1(j, carry):
            sm.scores(j, True, q_of, past(0, j), scale)
            return carry
        _past_loop(npast, p1)
        if with_meta:
            _fill_narrow(km_s, new(0, 0, geom.n_meta))
            pad = jnp.where(lax.broadcasted_iota(I32, (1, NARROW), 1) < geom.n_meta, 0.0, NEG_INF)
            sm.scores(1, False, q_of, km_s[...], scale, extra=pad)
        sm.finish_max()

        sm.values(own_slot, own_wide, own_k())

        def p2(j, carry):
            sm.values(j, True, past(0, j))
            return carry
        _past_loop(npast, p2)
        if with_meta:
            sm.values(1, False, km_s[...])

        out = None
        ohead = _lane_group((n, BR_W), MLA_V)
        for h in range(MLA_HEADS):
            o_h = jnp.where(ohead == h, _dot(sm.result(h), wuv_ref[...]), 0.0)
            out = o_h if out is None else out + o_h
        o_ref[0, pl.ds(q0, n), :] = out

    _tile_walk(geom,
               lambda q0, n, qpos, npast: tile(q0, n, qpos, npast, bool(geom.n_meta), geom.own_wide),
               lambda q0, n, qpos: tile(q0, n, qpos, 0, False, False))


def _mla_call(geom, layer, qcat, kcat, past, wuv_pad):
    bsz = qcat.shape[0]
    ins = [qcat, kcat] + list(past) + [wuv_pad]
    specs = ([_slab_spec(a) for a in (qcat, kcat)] + [_past_spec(a, layer) for a in past] + [_const_spec(wuv_pad)])
    return pl.pallas_call(
        functools.partial(_mla_kernel, geom), grid=_attn_grid(geom, bsz),
        in_specs=specs, out_specs=pl.BlockSpec((1, geom.lq, BR_W), lambda b, i: (b, 0, 0)),
        out_shape=jax.ShapeDtypeStruct((bsz, geom.lq, BR_W), F32),
        scratch_shapes=Softmax.scratch(geom, MLA_HEADS, 2 * LANES) + [pltpu.VMEM((NARROW, 2 * LANES), BF)] * 2,
        compiler_params=_cparams(("parallel", "arbitrary")), name="mla_attn")(*ins)


def _suffix_ones(n):
    later = lax.broadcasted_iota(I32, (n, n), 0) > lax.broadcasted_iota(I32, (n, n), 1)
    return jnp.concatenate([jnp.where(later, 1.0, 0.0), jnp.ones((n, LANES), F32)], axis=1).astype(BF)


def _sb_kernel(geom, *refs):
    n_past = 0 if geom.npast is None else 2
    (q_ref, k_ref, v_ref), refs = refs[:3], refs[3:]
    past_refs, (o_ref, acc_s, carry_s, qm_s, kn_s, vn_s, km_s, vm_s) = refs[:n_past], refs[n_past:]
    new, past = _readers(geom, (k_ref, v_ref), past_refs)

    def tile(q0, n, qpos, npast, with_meta, own_wide):
        q = q_ref[0, pl.ds(q0, n), :]
        head = _lane_group(q.shape, SB_DH)
        for h in range(SB_HEADS):
            qm_s[h, 0:n, :] = jnp.where(head == h, q * SB_DH ** -0.5, 0.0).astype(BF)
        acc_s[:, 0:n, :] = jnp.zeros((SB_HEADS, n, BR_W), F32)
        carry_s[:, 0:n, :] = jnp.zeros((SB_HEADS, n, LANES), F32)

        def block(k_b, v_b, mask, tri):
            w = k_b.shape[0]
            for h in range(SB_HEADS):
                z = _dot_nt(qm_s[h, 0:n, :], k_b)
                soft = jnp.log(1.0 + jnp.exp(-jnp.abs(z)))
                log_b = jnp.minimum(z, 0.0) - soft
                log_1m = log_b - z
                if mask is not None:
                    log_1m = jnp.where(mask, log_1m, 0.0)
                hi = log_1m.astype(BF)
                lo = (log_1m - hi.astype(F32)).astype(BF)
                sums = (jnp.dot(hi, tri, preferred_element_type=F32)
                        + jnp.dot(lo, tri, preferred_element_type=F32))
                carry = carry_s[h, 0:n, :]
                a = jnp.exp(log_b + sums[:, :w] + _rep(carry, w))
                if mask is not None:
                    a = jnp.where(mask, a, 0.0)
                acc_s[h, 0:n, :] += jnp.dot(a.astype(BF), v_b, preferred_element_type=F32)
                carry_s[h, 0:n, :] = carry + sums[:, w:]

        if own_wide:
            block(new(0, q0, n), new(1, q0, n), _own_mask(qpos, n, n, True), _suffix_ones(n))
        else:
            _fill_narrow(kn_s, new(0, q0, n))
            _fill_narrow(vn_s, new(1, q0, n))
            block(kn_s[...], vn_s[...], _own_mask(qpos, n, NARROW, True), _suffix_ones(NARROW))
        tri_kb = _suffix_ones(KEY_BLOCK)

        def past_block(jj, carry):
            j = npast - 1 - jj
            block(past(0, j).astype(BF), past(1, j).astype(BF), None, tri_kb)
            return carry
        _past_loop(npast, past_block)
        if with_meta:
            m0 = 0
            _fill_narrow(km_s, new(0, m0, geom.n_meta))
            _fill_narrow(vm_s, new(1, m0, geom.n_meta))
            pad = jnp.broadcast_to(lax.broadcasted_iota(I32, (1, NARROW), 1) < geom.n_meta, (n, NARROW))
            block(km_s[...], vm_s[...], pad, _suffix_ones(NARROW))

        out = None
        for h in range(SB_HEADS):
            o_h = jnp.where(head == h, acc_s[h, 0:n, :], 0.0)
            out = o_h if out is None else out + o_h
        o_ref[0, pl.ds(q0, n), :] = out

    _tile_walk(geom,
               lambda q0, n, qpos, npast: tile(q0, n, qpos, npast, bool(geom.n_meta), geom.own_wide),
               lambda q0, n, qpos: tile(q0, n, qpos, 0, False, False))


def _sb_call(geom, layer, q, k, v, past):
    bsz = q.shape[0]
    ins = [q, k, v] + list(past)
    specs = [_slab_spec(a) for a in (q, k, v)] + [_past_spec(a, layer) for a in past]
    return pl.pallas_call(
        functools.partial(_sb_kernel, geom), grid=_attn_grid(geom, bsz),
        in_specs=specs, out_specs=pl.BlockSpec((1, geom.lq, BR_W), lambda b, i: (b, 0, 0)),
        out_shape=jax.ShapeDtypeStruct((bsz, geom.lq, BR_W), F32),
        scratch_shapes=[pltpu.VMEM((SB_HEADS, geom.tq, BR_W), F32), pltpu.VMEM((SB_HEADS, geom.tq, LANES), F32),
                        pltpu.VMEM((SB_HEADS, geom.tq, BR_W), BF)] + [pltpu.VMEM((NARROW, BR_W), BF)] * 4,
        compiler_params=_cparams(("parallel", "arbitrary")), name="sb_attn")(*ins)


def _bias_readers(geom, own_ref, past_ref, meta_ref, mown_ref, tile_idx):
    def own(h, n):
        return own_ref[0, h, 0:n, :]

    def earlier(j, h):
        slot = j if geom.npast is not None else jnp.where(j == tile_idx - 1, 0, 1)
        return past_ref[slot, h]

    def meta(h):
        return meta_ref[jnp.minimum(tile_idx, 1), h]

    def meta_own(h):
        return mown_ref[0, h]
    return own, earlier, meta, meta_own


def _diff_kernel(geom, lam_init, *refs):
    n_past = 0 if geom.npast is None else 2
    n_bias = 4 if geom.n_meta else 2
    (q_ref, k_ref, v_ref), refs = refs[:3], refs[3:]
    past_refs, refs = refs[:n_past], refs[n_past:]
    bias_refs, refs = list(refs[:n_bias]) + [None] * (4 - n_bias), refs[n_bias:]
    (lam_ref, sg_ref, o_ref), refs = refs[:3], refs[3:]
    sm_refs, (qm_s, kn_s, vn_s, km_s, vm_s) = refs[:6], refs[6:]
    new, past = _readers(geom, (k_ref, v_ref), past_refs)
    b_own, b_earlier, b_meta, b_meta_own = _bias_readers(geom, *bias_refs, pl.program_id(1))
    scale = DIFF_DQK ** -0.5
    npair = 2 * DIFF_HEADS

    def tile(q0, n, qpos, npast, with_meta, own_wide, is_meta):
        sm = Softmax(sm_refs, npair, n)
        q = q_ref[0, pl.ds(q0, n), :]
        pair = _lane_group(q.shape, DIFF_DQK)
        for p in range(npair):
            qm_s[p, 0:n, :] = jnp.where(pair == p, q, jnp.zeros_like(q))
        q_of = lambda p: qm_s[p, 0:n, :]
        own_slot = npast if own_wide else 0
        if own_wide:
            own_k, own_v = (lambda: new(0, q0, n)), (lambda: new(1, q0, n))
        else:
            _fill_narrow(kn_s, new(0, q0, n))
            _fill_narrow(vn_s, new(1, q0, n))
            own_k, own_v = (lambda: kn_s[...]), (lambda: vn_s[...])
        sm.scores(own_slot, own_wide, q_of, own_k(), scale,
                  bias_of=(lambda p: b_meta_own(p // 2)) if is_meta else (lambda p: b_own(p // 2, n)))

        def p1(j, carry):
            sm.scores(j, True, q_of, past(0, j), scale,
                      bias_of=lambda p: b_earlier(j, p // 2))
            return carry
        _past_loop(npast, p1)
        if with_meta:
            m0 = 0
            _fill_narrow(km_s, new(0, m0, geom.n_meta))
            _fill_narrow(vm_s, new(1, m0, geom.n_meta))
            sm.scores(1, False, q_of, km_s[...], scale, bias_of=lambda p: b_meta(p // 2))
        sm.finish_max()

        sm.values(own_slot, own_wide, own_v().astype(BF))

        def p2(j, carry):
            sm.values(j, True, past(1, j).astype(BF))
            return carry
        _past_loop(npast, p2)
        if with_meta:
            sm.values(1, False, vm_s[...])

        lp = lam_ref[...]
        lam = (jnp.exp(jnp.sum(lp[0:1] * lp[1:2], axis=1, keepdims=True))
               - jnp.exp(jnp.sum(lp[2:3] * lp[3:4], axis=1, keepdims=True)) + lam_init)
        head = _lane_group((n, BR_W), DIFF_DV)
        out = None
        for h in range(DIFF_HEADS):
            o_h = jnp.where(head == h, sm.result(2 * h) - lam * sm.result(2 * h + 1), 0.0)
            ms = jnp.sum(o_h * o_h, axis=1, keepdims=True) * (1.0 / DIFF_DV)
            o_h = o_h * lax.rsqrt(ms + RMS_EPS) * sg_ref[...] * (1.0 - lam_init)
            out = o_h if out is None else out + o_h
        o_ref[0, pl.ds(q0, n), :] = out

    _tile_walk(geom,
               lambda q0, n, qpos, npast: tile(q0, n, qpos, npast, bool(geom.n_meta), geom.own_wide, False),
               lambda q0, n, qpos: tile(q0, n, qpos, 0, False, False, True))


def _diff_call(geom, layer, q, k, v, past, tables, lam_p, sg_tiled):
    bsz = q.shape[0]
    lam_init = 0.8 - 0.6 * math.exp(-0.3 * layer)
    tables = [t for t in tables if t is not None]
    ins = [q, k, v] + list(past) + tables + [lam_p, sg_tiled]
    specs = ([_slab_spec(a) for a in (q, k, v)] + [_past_spec(a, layer) for a in past]
             + [_const_spec(a) for a in tables + [lam_p, sg_tiled]])
    npair = 2 * DIFF_HEADS
    return pl.pallas_call(
        functools.partial(_diff_kernel, geom, lam_init), grid=_attn_grid(geom, bsz),
        in_specs=specs, out_specs=pl.BlockSpec((1, geom.lq, BR_W), lambda b, i: (b, 0, 0)),
        out_shape=jax.ShapeDtypeStruct((bsz, geom.lq, BR_W), F32),
        scratch_shapes=(Softmax.scratch(geom, npair, BR_W) + [pltpu.VMEM((npair, geom.tq, BR_W), BF)]
                        + [pltpu.VMEM((NARROW, BR_W), BF)] * 4),
        compiler_params=_cparams(("parallel", "arbitrary")), name="diff_attn")(*ins)


def _sortable(x):
    b = lax.bitcast_convert_type(x + 0.0, I32)
    return b ^ ((b >> 31) & I32(0x7FFFFFFF))


def _dsa_kernel(geom, *refs):
    n_past = 0 if geom.npast is None else 3
    n_bias = 4 if geom.n_meta else 2
    (q_ref, qi_ref, wi_ref, k_ref, v_ref, kit_ref), refs = refs[:6], refs[6:]
    past_refs, refs = refs[:n_past], refs[n_past:]
    bias_refs, refs = list(refs[:n_bias]) + [None] * (4 - n_bias), refs[n_bias:]
    o_ref, refs = refs[0], refs[1:]
    sm_refs, (qm_s, qim_s, wib_s, kw_s, kn_s, kh_s, kl_s, khn_s, kln_s, last_s,
              kno_s, vno_s, kio_s, knm_s, vnm_s, kim_s) = refs[:6], refs[6:]
    new, past = _readers(geom, (k_ref, v_ref, kit_ref), past_refs)
    b_own, b_earlier, b_meta, b_meta_own = _bias_readers(geom, *bias_refs, pl.program_id(1))
    topk = float(geom.topk)
    past_pos0 = N_META
    index_bits = max(1, int(geom.lq + geom.lp + N_META).bit_length())
    ones_count = jnp.ones((LANES, LANES), BF)

    def tile(q0, n, qpos, npast, with_meta, own_wide, is_meta):
        nm = geom.n_meta
        m0 = 0
        own_slot = npast if own_wide else 0
        own_cols = n if own_wide else NARROW

        qi = qi_ref[0, pl.ds(q0, n), :]
        igrp = _lane_group(qi.shape, IDX_DIM)
        wi = wi_ref[0, pl.ds(q0, n), :] * (IDX_HEADS ** -0.5 * IDX_DIM ** -0.5)
        for h in range(IDX_HEADS):
            qim_s[h, 0:n, :] = jnp.where(igrp == h, qi, jnp.zeros_like(qi))
            wib_s[h, 0:n, :] = jnp.broadcast_to(wi[:, h:h + 1], (n, LANES))
        q = q_ref[0, pl.ds(q0, n), :]
        head = _lane_group(q.shape, DSA_DH)
        for h in range(DSA_HEADS):
            qm_s[h, 0:n, :] = jnp.where(head == h, q * DSA_DH ** -0.5, 0.0).astype(BF)
        if own_wide:
            own_k, own_v, own_ki = (lambda: new(0, q0, n)), (lambda: new(1, q0, n)), (lambda: new(2, q0, n))
        else:
            _fill_narrow(kno_s, new(0, q0, n))
            _fill_narrow(vno_s, new(1, q0, n))
            _fill_narrow(kio_s, new(2, q0, n))
            own_k, own_v, own_ki = (lambda: kno_s[...]), (lambda: vno_s[...]), (lambda: kio_s[...])
        if with_meta:
            _fill_narrow(knm_s, new(0, m0, nm))
            _fill_narrow(vnm_s, new(1, m0, nm))
            _fill_narrow(kim_s, new(2, m0, nm))

        def index_keys(kit_b):
            score = None
            for h in range(IDX_HEADS):
                t = _rep(wib_s[h, 0:n, :], kit_b.shape[0]) * jnp.maximum(_dot_nt(qim_s[h, 0:n, :], kit_b), 0.0)
                score = t if score is None else score + t
            return _sortable(score)

        def store_keys(wide, slot, keys):
            hi = (keys >> 16).astype(I16)
            lo = ((keys & 0xFFFF) - HALF16).astype(I16)
            if wide:
                kw_s[slot, 0:n, :], kh_s[slot, 0:n, :], kl_s[slot, 0:n, :] = keys, hi, lo
            else:
                kn_s[slot, 0:n, :], khn_s[slot, 0:n, :], kln_s[slot, 0:n, :] = keys, hi, lo

        own_keys = jnp.where(_own_mask(qpos, n, own_cols, False), index_keys(own_ki()), I32(KEY_NEG))
        if own_wide:
            store_keys(True, own_slot, own_keys)
        else:
            pad = lax.broadcasted_iota(I32, (1, NARROW), 1) < n
            store_keys(False, 0, jnp.where(pad, own_keys, I32(INT_MIN)))

        def score_block(j, carry):
            store_keys(True, j, index_keys(past(2, j).astype(BF)))
            return carry
        _past_loop(npast, score_block)
        if with_meta:
            pad = lax.broadcasted_iota(I32, (1, NARROW), 1) < nm
            store_keys(False, 1, jnp.where(pad, index_keys(kim_s[...]), I32(INT_MIN)))
        nwide = npast + 1 if own_wide else npast

        def wide_sweep(init, fn):
            if isinstance(nwide, int) and nwide <= STATIC_UNROLL:
                for j in range(nwide):
                    init = fn(j, init)
                return init
            return lax.fori_loop(0, nwide, fn, init)

        def to_counts(part):
            return jnp.dot(part.astype(F32).astype(BF), ones_count, preferred_element_type=F32)

        def one(cond):
            return jnp.where(cond, 1.0, 0.0)

        def count(wide_fn, own_narrow_fn, meta_fn):
            part = wide_sweep(jnp.zeros((n, LANES), F32), lambda j, a: a + wide_fn(kw_s[j, 0:n, :], j))
            if not own_wide:
                part = part + own_narrow_fn(kn_s[0, 0:n, :])
            if with_meta:
                part = part + meta_fn(kn_s[1, 0:n, :])
            return to_counts(part)

        def count_cmp(cmp):
            return count(lambda x, j: _halves(one(cmp(x, KEY_BLOCK)), jnp.add),
                         lambda x: one(cmp(x, NARROW)), lambda x: one(cmp(x, NARROW)))

        def count16(wide_ref, narrow_ref, cands):
            def fold(x, c):
                return _halves(jnp.where(x >= _rep(c, x.shape[1]), I16(1), I16(0)), jnp.add)
            accs = wide_sweep(tuple(jnp.zeros((n, LANES), I16) for _ in cands),
                              lambda j, accs: tuple(a + fold(wide_ref[j, 0:n, :], c) for a, c in zip(accs, cands)))
            if not own_wide:
                accs = tuple(a + fold(narrow_ref[0, 0:n, :], c) for a, c in zip(accs, cands))
            if with_meta:
                accs = tuple(a + fold(narrow_ref[1, 0:n, :], c) for a, c in zip(accs, cands))
            return [to_counts(a) for a in accs]

        def greedy16(wide_ref, narrow_ref, need):
            def two_bits(p, base):
                b1 = jnp.left_shift(I32(1), I32(15) - 2 * p)
                b0 = jnp.left_shift(I32(1), I32(14) - 2 * p)
                cu = [base | b0, base | b1, base | b1 | b0]
                c01, c10, c11 = count16(wide_ref, narrow_ref, [(c - HALF16).astype(I16) for c in cu])
                return jnp.where(c11 >= need, cu[2], jnp.where(c10 >= need, cu[1], jnp.where(c01 >= need, cu[0], base)))
            return lax.fori_loop(0, 8, two_bits, jnp.zeros((n, LANES), I32))

        hi_u = greedy16(kh_s, khn_s, topk)
        top16 = 2 * HALF16 - 1
        above = count16(kh_s, khn_s, [(jnp.minimum(hi_u + 1, top16) - HALF16).astype(I16)])[0]
        above = jnp.where(hi_u == top16, 0.0, above)
        hi16 = (hi_u - HALF16).astype(I16)

        def regroup(j, carry):
            kl_s[j, 0:n, :] = jnp.where(kh_s[j, 0:n, :] == _rep(hi16, KEY_BLOCK), kl_s[j, 0:n, :], I16(-HALF16))
            return carry
        wide_sweep(0, regroup)
        for slot in ([] if own_wide else [0]) + ([1] if with_meta else []):
            kln_s[slot, 0:n, :] = jnp.where(khn_s[slot, 0:n, :] == hi16, kln_s[slot, 0:n, :], I16(-HALF16))
        lo_u = greedy16(kl_s, kln_s, topk - above)
        thr = jnp.left_shift(hi_u - HALF16, 16) | lo_u
        cnt = count_cmp(lambda x, w: x >= _rep(thr, w))

        last_s[0:n, :] = jnp.full((n, LANES), INT_MAX, I32)
        has_ties = jnp.max(one((cnt > topk) & (thr > I32(KEY_NEG)))) > 0.0

        @pl.when(has_ties)
        def _():
            need = topk - count_cmp(lambda x, w: x > _rep(thr, w))
            colw = lax.broadcasted_iota(I32, (1, KEY_BLOCK), 1)
            coln = lax.broadcasted_iota(I32, (1, NARROW), 1)

            def wide_pos0(j):
                return past_pos0 + j * KEY_BLOCK

            def ties_before(x):
                return count(lambda kk, j: _halves(one((kk == _rep(thr, KEY_BLOCK))
                                                       & (colw < _rep(x, KEY_BLOCK) - wide_pos0(j))), jnp.add),
                             lambda kk: one((kk == thr) & (coln < x - qpos)),
                             lambda kk: one((kk == thr) & (coln < x)))

            def pos_step(b, last_pos):
                cand = last_pos + jnp.left_shift(I32(1), I32(index_bits - 1) - b)
                return jnp.where(ties_before(cand) < need, cand, last_pos)
            last_s[0:n, :] = lax.fori_loop(0, index_bits, pos_step, jnp.zeros((n, LANES), I32))

        last_pos = last_s[0:n, :]

        def sel_bias(kk, kpos0):
            w = kk.shape[1]
            cols = lax.broadcasted_iota(I32, (1, w), 1)
            t = _rep(thr, w)
            sel = (kk > t) | ((kk == t) & (cols <= _rep(last_pos, w) - kpos0))
            return jnp.where(sel, 0.0, NEG_INF)

        sm = Softmax(sm_refs, DSA_HEADS, n)
        q_of = lambda h: qm_s[h, 0:n, :]
        own_kk = kw_s[own_slot, 0:n, :] if own_wide else kn_s[0, 0:n, :]
        sm.scores(own_slot, own_wide, q_of, own_k(), 1.0,
                  bias_of=(lambda h: b_meta_own(h)) if is_meta else (lambda h: b_own(h, n)),
                  extra=sel_bias(own_kk, qpos))

        def p1(j, carry):
            sm.scores(j, True, q_of, past(0, j).astype(BF), 1.0,
                      bias_of=lambda h: b_earlier(j, h),
                      extra=sel_bias(kw_s[j, 0:n, :], past_pos0 + j * KEY_BLOCK))
            return carry
        _past_loop(npast, p1)
        if with_meta:
            sm.scores(1, False, q_of, knm_s[...], 1.0, bias_of=lambda h: b_meta(h),
                      extra=sel_bias(kn_s[1, 0:n, :], 0))
        sm.finish_max()

        sm.values(own_slot, own_wide, own_v().astype(BF))

        def p2(j, carry):
            sm.values(j, True, past(1, j).astype(BF))
            return carry
        _past_loop(npast, p2)
        if with_meta:
            sm.values(1, False, vnm_s[...])

        out = None
        for h in range(DSA_HEADS):
            o_h = jnp.where(head == h, sm.result(h), 0.0)
            out = o_h if out is None else out + o_h
        o_ref[0, pl.ds(q0, n), :] = out

    _tile_walk(geom,
               lambda q0, n, qpos, npast: tile(q0, n, qpos, npast, bool(geom.n_meta), geom.own_wide, False),
               lambda q0, n, qpos: tile(q0, n, qpos, 0, False, False, True))


def _dsa_call(geom, layer, q, qi, wi, k, v, kit, past, tables):
    bsz = q.shape[0]
    tables = [t for t in tables if t is not None]
    ins = [q, qi, wi, k, v, kit] + list(past) + tables
    specs = ([_slab_spec(a) for a in (q, qi, wi, k, v, kit)] + [_past_spec(a, layer) for a in past]
             + [_const_spec(a) for a in tables])
    tq = geom.tq
    return pl.pallas_call(
        functools.partial(_dsa_kernel, geom), grid=_attn_grid(geom, bsz),
        in_specs=specs, out_specs=pl.BlockSpec((1, geom.lq, BR_W), lambda b, i: (b, 0, 0)),
        out_shape=jax.ShapeDtypeStruct((bsz, geom.lq, BR_W), F32),
        scratch_shapes=(Softmax.scratch(geom, DSA_HEADS, BR_W)
                        + [pltpu.VMEM((DSA_HEADS, tq, BR_W), BF), pltpu.VMEM((IDX_HEADS, tq, BR_W), BF),
                           pltpu.VMEM((IDX_HEADS, tq, LANES), F32),
                           pltpu.VMEM((geom.nslot, tq, KEY_BLOCK), I32), pltpu.VMEM((2, tq, NARROW), I32),
                           pltpu.VMEM((geom.nslot, tq, KEY_BLOCK), I16), pltpu.VMEM((geom.nslot, tq, KEY_BLOCK), I16),
                           pltpu.VMEM((2, tq, NARROW), I16), pltpu.VMEM((2, tq, NARROW), I16),
                           pltpu.VMEM((tq, LANES), I32)]
                        + [pltpu.VMEM((NARROW, BR_W), BF)] * 6),
        compiler_params=_cparams(("parallel", "arbitrary")), name="dsa_attn")(*ins)


def _row_tile(t):
    for tm in (384, 256, 128, 64, 32, 16, 8):
        if t % tm == 0:
            return tm
    raise ValueError(f"token count {t} has no supported row tile")


def _rope_tables(pos):
    half = MLA_ROPE // 2
    inv_freq = ROPE_THETA ** (-jnp.arange(half, dtype=jnp.float32) / half)
    ang = pos.astype(jnp.float32)[:, None] * inv_freq[None, :]
    cos, sin = jnp.cos(ang), jnp.sin(ang)
    return jnp.concatenate([cos, cos], axis=1), jnp.concatenate([-sin, sin], axis=1)


def _run_group(geom, x, pos_rows, caches, weights):
    (ln_in_g, ln_in_b, w_in, mla_qnorm_g, mla_w_uq, mla_kvnorm_g, mla_w_uk, mla_w_uv, diff_lambda,
     diff_subln_g, rel_bias, w_br, w_out, ln1_g, ln1_b, w_ff1, b_ff1, w_ff2, b_ff2, ln2_g, ln2_b) = weights
    bsz, lq, d = x.shape
    t = bsz * lq
    tm = _row_tile(t)
    cos32, sin32 = _rope_tables(pos_rows)
    cos_k = jnp.tile(cos32, (bsz, MLA_HEADS))
    sin_k = jnp.tile(sin32, (bsz, MLA_HEADS))
    tables = _bias_tables(geom, rel_bias)
    diff_tables = _split_tables(tables, 0, DIFF_HEADS)
    dsa_tables = _split_tables(tables, DIFF_HEADS, DIFF_HEADS + DSA_HEADS)

    xf = _ln_call(x.reshape(t, d), ln_in_g, ln_in_b, tm)
    rows = []
    for l in range(DEPTH):
        w_uq = mla_w_uq[l].reshape(MLA_Q_LORA, MLA_HEADS, MLA_NOPE + MLA_ROPE)
        wn = w_uq[:, :, :MLA_NOPE].reshape(MLA_Q_LORA, MLA_HEADS * MLA_NOPE).astype(BF)
        wp = w_uq[:, :, MLA_NOPE:].reshape(MLA_Q_LORA, MLA_HEADS * MLA_ROPE).astype(BF)
        wuk = mla_w_uk[l].reshape(MLA_KV_LORA, MLA_HEADS * MLA_NOPE).T.astype(BF)
        wuv = jnp.pad(mla_w_uv[l].reshape(MLA_KV_LORA, MLA_HEADS * MLA_V),
                      ((0, 2 * LANES - MLA_KV_LORA), (0, 0))).astype(BF)
        (ckv, kpe, b_k, b_v, c_k, c_v, d_k, d_v, d_ki, d_wi,
         qcat, kcat, bq_b, bk_b, bv_b, cq_b, ck_b, cv_b, dq_b, dk_b, dv_b, dqi_b, kit_b) = _proj_call(
            xf, _mix_weight(w_in[l]), mla_kvnorm_g[l].reshape(1, -1), mla_qnorm_g[l].reshape(1, -1),
            wn, wp, wuk, cos_k, sin_k, tm)
        per = lambda a: a.reshape(bsz, lq, a.shape[-1])
        if caches is None:
            past = lambda *idx: []
            mla_past, kit_past = [], []
        else:
            past = lambda *idx: [caches[i] for i in idx]
            zeros = jnp.zeros(caches[0].shape[1:3] + (2 * LANES - MLA_KV_LORA - MLA_ROPE,), F32)
            mla_past = [jnp.concatenate([caches[0][l], caches[1][l], zeros], axis=-1).astype(BF)]
            kit_past = [jnp.tile(caches[8][l], (1, 1, IDX_HEADS)).astype(BF)]

        o_a = _mla_call(geom, l, per(qcat), per(kcat), mla_past, wuv)
        o_b = _sb_call(geom, l, per(bq_b), per(bk_b), per(bv_b), past(2, 3))
        o_c = _diff_call(geom, l, per(cq_b), per(ck_b), per(cv_b), past(4, 5), diff_tables, diff_lambda[l],
                         jnp.tile(diff_subln_g[l], DIFF_HEADS).reshape(1, -1))
        o_d = _dsa_call(geom, l, per(dq_b), per(dqi_b), per(d_wi), per(dk_b), per(dv_b), per(kit_b),
                        past(6, 7) + kit_past, dsa_tables)

        flat = lambda a: a.reshape(t, BR_W)
        wg = w_in[l][:, IN_OFFS[15]:IN_OFFS[16]].astype(BF)
        x1 = _merge_call(xf, flat(o_a), flat(o_b), flat(o_c), flat(o_d), wg, w_br[l].astype(BF),
                         w_out[l].astype(BF), ln1_g[l].reshape(1, -1), ln1_b[l].reshape(1, -1), tm)
        xf = _ffn_call(x1, w_ff1[l].astype(BF), b_ff1[l].reshape(1, -1), w_ff2[l].astype(BF),
                       b_ff2[l].reshape(1, -1), ln2_g[l].reshape(1, -1), ln2_b[l].reshape(1, -1), tm)
        rows.append([per(a) for a in (ckv, kpe, b_k, b_v, c_k, c_v, d_k, d_v, d_ki)])
    return xf.reshape(bsz, lq, d), rows


_ROW_TRAILING = ((MLA_KV_LORA,), (MLA_ROPE,), (SB_HEADS, SB_DH), (SB_HEADS, SB_DH),
                 (DIFF_HEADS, 2, DIFF_DQK), (DIFF_HEADS, DIFF_DV), (DSA_HEADS, DSA_DH),
                 (DSA_HEADS, DSA_DH), (IDX_DIM,))


def kernel(x_prompt, x_sample, cache_mla_kv, cache_mla_pe, cache_sb_k, cache_sb_v, cache_diff_k, cache_diff_v, cache_dsa_k, cache_dsa_v, cache_dsa_kidx, meta, ln_in_g, ln_in_b, w_in, mla_qnorm_g, mla_w_uq, mla_kvnorm_g, mla_w_uk, mla_w_uv, diff_lambda, diff_subln_g, rel_bias, w_br, w_out, ln1_g, ln1_b, w_ff1, b_ff1, w_ff2, b_ff2, ln2_g, ln2_b):
    weights = (ln_in_g, ln_in_b, w_in, mla_qnorm_g, mla_w_uq, mla_kvnorm_g, mla_w_uk, mla_w_uv, diff_lambda,
               diff_subln_g, rel_bias, w_br, w_out, ln1_g, ln1_b, w_ff1, b_ff1, w_ff2, b_ff2, ln2_g, ln2_b)
    assert w_in.shape[0] == DEPTH and x_prompt.shape[2] == D_MODEL

    bsz_p, seq_p, _ = x_prompt.shape
    assert seq_p % KEY_BLOCK == 0
    meta_b = jnp.broadcast_to(meta[None].astype(x_prompt.dtype), (bsz_p, N_META, D_MODEL))
    xp = jnp.concatenate([meta_b, x_prompt], axis=1)
    pos_p = jnp.arange(N_META + seq_p, dtype=I32)
    geom_p = Geom(tq=KEY_BLOCK, ntile=seq_p // KEY_BLOCK, npast=None, n_meta=N_META, qpos0=N_META,
                  row0=N_META, lq=seq_p + N_META, lp=0, topk=min(DSA_TOPK, seq_p // 4))
    yp, rows_p = _run_group(geom_p, xp, pos_p, None, weights)
    y_prompt = yp[:, N_META:]
    p_rows = []
    for i, trailing in enumerate(_ROW_TRAILING):
        stacked = jnp.stack([r[i] for r in rows_p], axis=0)
        p_rows.append(stacked.reshape(stacked.shape[:3] + trailing))

    past_len = cache_mla_kv.shape[2]
    bsz_s, dec_seq, _ = x_sample.shape
    assert past_len % KEY_BLOCK == 0 and dec_seq % 16 == 0 and dec_seq <= NARROW
    assert past_len % CHUNK == 0 and dec_seq <= CHUNK, "new frames must share one chunk"
    caches = [c.reshape(c.shape[:3] + (-1,)) for c in
              (cache_mla_kv, cache_mla_pe, cache_sb_k, cache_sb_v, cache_diff_k, cache_diff_v,
               cache_dsa_k, cache_dsa_v, cache_dsa_kidx)]
    pos_s = N_META + past_len + jnp.arange(dec_seq, dtype=I32)
    geom_s = Geom(tq=dec_seq, ntile=1, npast=past_len // KEY_BLOCK, n_meta=0, qpos0=N_META + past_len,
                  row0=0, lq=dec_seq, lp=past_len, topk=min(DSA_TOPK, (past_len + dec_seq) // 4))
    y_sample, rows_s = _run_group(geom_s, x_sample, pos_s, caches, weights)
    s_rows = []
    for i, trailing in enumerate(_ROW_TRAILING):
        stacked = jnp.stack([r[i] for r in rows_s], axis=0)
        s_rows.append(stacked.reshape(stacked.shape[:3] + trailing))

    return (y_prompt, y_sample, *p_rows, *s_rows)
```

```python
import functools
import math
from typing import NamedTuple, Optional

import jax
import jax.numpy as jnp
import numpy as np
from jax import lax
from jax.experimental import pallas as pl
from jax.experimental.pallas import tpu as pltpu

D_MODEL = 1024
CHUNK = 64
N_META = 16
MLA_HEADS = 4
MLA_Q_LORA = 256
MLA_KV_LORA = 128
MLA_NOPE = 64
MLA_ROPE = 32
MLA_V = 64
ROPE_THETA = 10000.0
SB_HEADS = 4
SB_DH = 64
DIFF_HEADS = 4
DIFF_DQK = 32
DIFF_DV = 64
DSA_HEADS = 4
DSA_DH = 64
IDX_HEADS = 8
IDX_DIM = 32
DSA_TOPK = 256
N_BRANCH = 4
BR_W = 256
D_FF = 4 * D_MODEL
T5_BUCKETS = 32
T5_MAX_DIST = 128
LN_EPS = 1e-5
RMS_EPS = 1e-6
NEG_INF = -1e30
DEPTH = 2
DN_ALPHA = (2 * DEPTH) ** 0.25
IN_SIZES = (MLA_Q_LORA, MLA_KV_LORA, MLA_ROPE,
            SB_HEADS * SB_DH, SB_HEADS * SB_DH, SB_HEADS * SB_DH,
            DIFF_HEADS * 2 * DIFF_DQK, DIFF_HEADS * 2 * DIFF_DQK, DIFF_HEADS * DIFF_DV,
            DSA_HEADS * DSA_DH, DSA_HEADS * DSA_DH, DSA_HEADS * DSA_DH,
            IDX_HEADS * IDX_DIM, IDX_DIM, IDX_HEADS,
            N_BRANCH * D_MODEL)
IN_OFFS = tuple(int(s) for s in np.cumsum((0,) + IN_SIZES))

LANES = 128
KEY_BLOCK = 256
NARROW = LANES
VMEM_LIMIT_MB = 56

BF = jnp.bfloat16
F32 = jnp.float32
I32 = jnp.int32

_NEG_BITS = int(np.float32(NEG_INF).view(np.int32))
KEY_NEG = _NEG_BITS ^ ((_NEG_BITS >> 31) & 0x7FFFFFFF)
INT_MIN = -(2 ** 31)
INT_MAX = 2 ** 31 - 1


def _cparams(sem):
    return pltpu.CompilerParams(dimension_semantics=sem, vmem_limit_bytes=VMEM_LIMIT_MB * 1024 * 1024)


def _dot(a, b):
    return jnp.dot(a.astype(BF), b.astype(BF), preferred_element_type=F32)


def _dot_nt(a, b):
    return lax.dot_general(a.astype(BF), b.astype(BF), (((1,), (1,)), ((), ())),
                           preferred_element_type=F32)


def _layer_norm(x, g, b):
    mu = jnp.mean(x, axis=-1, keepdims=True)
    xc = x - mu
    var = jnp.mean(xc * xc, axis=-1, keepdims=True)
    return xc * lax.rsqrt(var + LN_EPS) * g + b


def _rms_norm(x, g):
    return x * lax.rsqrt(jnp.mean(x * x, axis=-1, keepdims=True) + RMS_EPS) * g


def _rope_lanes(x, cos, sin):
    lane = lax.broadcasted_iota(I32, x.shape, 1)
    swapped = jnp.where((lane & 31) < 16, pltpu.roll(x, LANES - 16, 1), pltpu.roll(x, 16, 1))
    return x * cos + swapped * sin


def _lane_group(shape, width):
    return lax.broadcasted_iota(I32, shape, 1) // width


def _halves(x, op):
    return op(x[:, :LANES], x[:, LANES:]) if x.shape[1] == 2 * LANES else x


def _rep(x, width):
    return jnp.concatenate([x, x], axis=1) if width == 2 * LANES else x


class Geom(NamedTuple):
    tq: int
    ntile: int
    npast: Optional[int]
    n_meta: int
    qpos0: int
    row0: int
    lq: int
    lp: int
    topk: int

    @property
    def own_wide(self):
        return self.tq == KEY_BLOCK

    @property
    def nslot(self):
        return max(self.ntile if self.npast is None else self.npast, 1)


def _ln_kernel(x_ref, g_ref, b_ref, o_ref):
    o_ref[...] = _layer_norm(x_ref[...], g_ref[...], b_ref[...])


def _ln_call(x, g, b, tm):
    t, d = x.shape
    return pl.pallas_call(
        _ln_kernel, grid=(t // tm,),
        in_specs=[pl.BlockSpec((tm, d), lambda i: (i, 0)),
                  pl.BlockSpec((1, d), lambda i: (0, 0)),
                  pl.BlockSpec((1, d), lambda i: (0, 0))],
        out_specs=pl.BlockSpec((tm, d), lambda i: (i, 0)),
        out_shape=jax.ShapeDtypeStruct((t, d), F32),
        compiler_params=_cparams(("parallel",)), name="ln_in")(x, g.reshape(1, d), b.reshape(1, d))


_WIDE = (3, 4, 5, 6, 7, 8, 9, 10, 11, 12)
_WIDE_F32 = (1, 2, 4, 5, 7, 8)
_MIX_COLS = 256 + 128 + 256 * len(_WIDE) + LANES + 256 + LANES


def _mix_weight(w_in_l):
    def seg(i, pad_to=None):
        w = w_in_l[:, IN_OFFS[i]:IN_OFFS[i + 1]]
        if pad_to is not None:
            w = jnp.pad(w, ((0, 0), (0, pad_to - w.shape[1])))
        return w
    cols = ([seg(0), seg(1)] + [seg(i) for i in _WIDE]
            + [seg(2, LANES), jnp.tile(seg(13), (1, IDX_HEADS)), seg(14, LANES)])
    return jnp.concatenate(cols, axis=1).astype(BF)


def _proj_kernel(x_ref, w_ref, kvg_ref, qg_ref, wn_ref, wp_ref, wuk_ref, cos_ref, sin_ref, *outs):
    (ckv_ref, kpe_ref, bk_ref, bv_ref, ck_ref, cv_ref, dk_ref, dv_ref, dki_ref, dwi_ref,
     qcat_ref, kcat_ref, *wide_bf) = outs
    kit_ref = wide_bf[-1]
    wide_bf = wide_bf[:-1]
    wide_f32 = dict(zip(_WIDE_F32, (bk_ref, bv_ref, ck_ref, cv_ref, dk_ref, dv_ref)))
    xb = x_ref[...].astype(BF)
    cos, sin = cos_ref[...], sin_ref[...]

    def seg(off, width):
        return jnp.dot(xb, w_ref[:, off:off + width], preferred_element_type=F32)

    qn = _rms_norm(seg(0, 256), qg_ref[...])
    nope = _dot(qn, wn_ref[...])
    pe = _rope_lanes(_dot(qn, wp_ref[...]), cos, sin)
    head = _lane_group(nope.shape, MLA_NOPE)
    lane = lax.broadcasted_iota(I32, pe.shape, 1)
    for h in range(MLA_HEADS):
        q_lat = _dot(jnp.where(head == h, nope, 0.0), wuk_ref[...])
        pe_h = pe if h == 0 else pltpu.roll(pe, LANES - h * MLA_ROPE, 1)
        qcat_ref[:, 2 * LANES * h:2 * LANES * h + LANES] = q_lat.astype(BF)
        qcat_ref[:, 2 * LANES * h + LANES:2 * LANES * (h + 1)] = jnp.where(lane < MLA_ROPE, pe_h, 0.0).astype(BF)

    ckv = _rms_norm(seg(256, 128), kvg_ref[...])
    ckv_ref[...] = ckv
    off = 384
    for n, r in enumerate(wide_bf):
        y = seg(off, 256)
        if n in wide_f32:
            wide_f32[n][...] = y
        r[...] = y.astype(BF)
        off += 256
    kpe = _rope_lanes(seg(off, LANES), cos, sin)
    kpe_ref[...] = kpe[:, :MLA_ROPE]
    kcat_ref[...] = jnp.concatenate([ckv, kpe], axis=1).astype(BF)
    kit = seg(off + LANES, 256)
    dki_ref[...] = kit[:, :IDX_DIM]
    kit_ref[...] = kit.astype(BF)
    dwi_ref[...] = seg(off + LANES + 256, LANES)[:, :IDX_HEADS]


def _proj_call(x, w_mix, kvg, qg, wn, wp, wuk, cos_k, sin_k, tm):
    t, d = x.shape
    f32_w = [128, MLA_ROPE, 256, 256, 256, 256, 256, 256, IDX_DIM, IDX_HEADS]
    bf_w = [4 * 256, 256] + [256] * len(_WIDE) + [256]
    row = lambda w: pl.BlockSpec((tm, w), lambda i: (i, 0))
    const = lambda a: pl.BlockSpec(a.shape, lambda i: (0, 0))
    return pl.pallas_call(
        _proj_kernel, grid=(t // tm,),
        in_specs=[row(d), const(w_mix), const(kvg), const(qg), const(wn), const(wp), const(wuk),
                  row(LANES), row(LANES)],
        out_specs=[row(w) for w in f32_w + bf_w],
        out_shape=([jax.ShapeDtypeStruct((t, w), F32) for w in f32_w]
                   + [jax.ShapeDtypeStruct((t, w), BF) for w in bf_w]),
        compiler_params=_cparams(("parallel",)), name="mix_proj")(x, w_mix, kvg, qg, wn, wp, wuk, cos_k, sin_k)


def _merge_kernel(x_ref, oa_ref, ob_ref, oc_ref, od_ref, wg_ref, wbr_ref, wout_ref, g_ref, b_ref, o_ref):
    x = x_ref[...]
    xb = x.astype(BF)
    acc = None
    for n, o_n in enumerate((oa_ref, ob_ref, oc_ref, od_ref)):
        gate = jax.nn.sigmoid(jnp.dot(xb, wg_ref[:, n * D_MODEL:(n + 1) * D_MODEL],
                                      preferred_element_type=F32))
        br = jnp.dot(o_n[...].astype(BF), wbr_ref[n], preferred_element_type=F32)
        acc = gate * br if acc is None else acc + gate * br
    mix = jnp.dot(acc.astype(BF), wout_ref[...], preferred_element_type=F32)
    o_ref[...] = _layer_norm(DN_ALPHA * x + mix, g_ref[...], b_ref[...])


def _merge_call(x, o_a, o_b, o_c, o_d, wg, wbr, wout, g, b, tm):
    t, d = x.shape
    row = lambda w: pl.BlockSpec((tm, w), lambda i: (i, 0))
    return pl.pallas_call(
        _merge_kernel, grid=(t // tm,),
        in_specs=[row(d), row(BR_W), row(BR_W), row(BR_W), row(BR_W),
                  pl.BlockSpec((d, N_BRANCH * d), lambda i: (0, 0)),
                  pl.BlockSpec((N_BRANCH, BR_W, d), lambda i: (0, 0, 0)),
                  pl.BlockSpec((d, d), lambda i: (0, 0)),
                  pl.BlockSpec((1, d), lambda i: (0, 0)), pl.BlockSpec((1, d), lambda i: (0, 0))],
        out_specs=row(d), out_shape=jax.ShapeDtypeStruct((t, d), F32),
        compiler_params=_cparams(("parallel",)), name="merge")(x, o_a, o_b, o_c, o_d, wg, wbr, wout, g, b)


def _ffn_kernel(x_ref, w1_ref, b1_ref, w2_ref, b2_ref, g_ref, b_ref, o_ref):
    x = x_ref[...]
    xb = x.astype(BF)
    acc = None
    for c in range(D_FF // D_MODEL):
        sl = slice(c * D_MODEL, (c + 1) * D_MODEL)
        h = jnp.dot(xb, w1_ref[:, sl], preferred_element_type=F32) + b1_ref[:, sl]
        h = jnp.square(jnp.maximum(h, 0.0))
        y = jnp.dot(h.astype(BF), w2_ref[sl, :], preferred_element_type=F32)
        acc = y if acc is None else acc + y
    o_ref[...] = _layer_norm(DN_ALPHA * x + acc + b2_ref[...], g_ref[...], b_ref[...])


def _ffn_call(x, w1, b1, w2, b2, g, b, tm):
    t, d = x.shape
    row = pl.BlockSpec((tm, d), lambda i: (i, 0))
    vec = lambda w: pl.BlockSpec((1, w), lambda i: (0, 0))
    return pl.pallas_call(
        _ffn_kernel, grid=(t // tm,),
        in_specs=[row, pl.BlockSpec((d, D_FF), lambda i: (0, 0)), vec(D_FF),
                  pl.BlockSpec((D_FF, d), lambda i: (0, 0)), vec(d), vec(d), vec(d)],
        out_specs=row, out_shape=jax.ShapeDtypeStruct((t, d), F32),
        compiler_params=_cparams(("parallel",)), name="ffn")(x, w1, b1, w2, b2, g, b)


def _t5_bucket(rel):
    nb = T5_BUCKETS // 2
    max_exact = nb // 2
    n = jnp.abs(rel)
    nf = jnp.maximum(n, 1).astype(jnp.float32)
    large = max_exact + (jnp.log(nf / max_exact) / math.log(T5_MAX_DIST / max_exact)
                         * (nb - max_exact)).astype(jnp.int32)
    large = jnp.minimum(large, nb - 1)
    return jnp.where(rel > 0, nb, 0) + jnp.where(n < max_exact, n, large)


def _bucket_ids(rel0s, rows, cols, valid_cols, qpos0_for_mask=None):
    r = jnp.arange(rows, dtype=I32)[:, None]
    c = jnp.arange(cols, dtype=I32)[None, :]
    ok = jnp.broadcast_to(c < valid_cols, (rows, cols))
    if qpos0_for_mask is not None:
        ok = ok & (((qpos0_for_mask - N_META + c) >> 6) <= ((qpos0_for_mask - N_META + r) >> 6))
    return jnp.stack([jnp.where(ok, _t5_bucket(I32(rel0) + c - r), -1) for rel0 in rel0s], axis=0)


def _bias_kernel(tab_ref, bk_ref, o_ref):
    bk = bk_ref[0]
    for h in range(DIFF_HEADS + DSA_HEADS):
        acc = jnp.where(bk < 0, NEG_INF, 0.0).astype(F32)
        for b in range(T5_BUCKETS):
            acc = jnp.where(bk == b, tab_ref[b, h], acc)
        o_ref[0, h] = acc


def _bias_call(rel_bias, bucket_ids):
    n, rows, cols = bucket_ids.shape
    nh = DIFF_HEADS + DSA_HEADS
    return pl.pallas_call(
        _bias_kernel, grid=(n,),
        in_specs=[pl.BlockSpec(memory_space=pltpu.SMEM),
                  pl.BlockSpec((1, rows, cols), lambda i: (i, 0, 0))],
        out_specs=pl.BlockSpec((1, nh, rows, cols), lambda i: (i, 0, 0, 0)),
        out_shape=jax.ShapeDtypeStruct((n, nh, rows, cols), F32),
        compiler_params=_cparams(("parallel",)), name="rel_bias_table")(rel_bias, bucket_ids)


def _bias_tables(geom, rel_bias):
    tq, kb = geom.tq, KEY_BLOCK
    own_cols = tq if geom.own_wide else NARROW
    own = _bias_call(rel_bias, _bucket_ids([0], tq, own_cols, tq, geom.qpos0))
    if geom.npast is None:
        assert tq == kb and kb + 1 >= T5_MAX_DIST
        assert geom.qpos0 + tq - (geom.n_meta - 1) >= T5_MAX_DIST
        past = _bias_call(rel_bias, _bucket_ids([-kb, -2 * kb], tq, kb, kb))
        meta = _bias_call(rel_bias, _bucket_ids([-geom.qpos0, -geom.qpos0 - tq], tq, NARROW, geom.n_meta))
        mown = _bias_call(rel_bias, _bucket_ids([0], geom.n_meta, NARROW, geom.n_meta, 0))
        return own, past, meta, mown
    past = _bias_call(rel_bias, _bucket_ids([N_META + j * kb - geom.qpos0 for j in range(geom.npast)], tq, kb, kb))
    return own, past, None, None


def _split_tables(tables, lo, hi):
    return [None if t is None else t[:, lo:hi] for t in tables]


def _no_past(npast):
    return isinstance(npast, int) and npast == 0


STATIC_UNROLL = 4


def _aligned(x, m):
    return x if isinstance(x, int) else pl.multiple_of(x, m)


def _past_loop(npast, body):
    if isinstance(npast, int) and npast <= STATIC_UNROLL:
        for j in range(npast):
            body(j, 0)
    else:
        lax.fori_loop(0, npast, body, 0)


def _tile_walk(geom, frames_tile, meta_tile):
    i = pl.program_id(1)

    @pl.when(i < geom.ntile)
    def _():
        q0 = _aligned(geom.row0 + i * geom.tq, 16)
        npast = i if geom.npast is None else geom.npast
        frames_tile(q0, geom.tq, geom.qpos0 + i * geom.tq, npast)

    if geom.n_meta:
        @pl.when(i == geom.ntile)
        def _():
            meta_tile(0, geom.n_meta, 0)


def _own_mask(qpos, n, cols, causal):
    r = lax.broadcasted_iota(I32, (n, 1), 0)
    c = lax.broadcasted_iota(I32, (1, cols), 1)
    if causal:
        return c < r
    return (c < n) & (((qpos - N_META + c) >> 6) <= ((qpos - N_META + r) >> 6))


def _slab_spec(arr):
    return pl.BlockSpec((1,) + arr.shape[1:], lambda b, i: (b, 0, 0))


def _cache_spec(arr, layer):
    return pl.BlockSpec((1, 1) + arr.shape[2:], lambda b, i: (layer, b, 0, 0))


def _past_spec(arr, layer):
    return _cache_spec(arr, layer) if arr.ndim == 4 else _slab_spec(arr)


def _const_spec(arr):
    nd = arr.ndim
    return pl.BlockSpec(arr.shape, lambda b, i: (0,) * nd)


def _readers(geom, new_refs, past_refs):
    def new(k, r0, n):
        return new_refs[k][0, pl.ds(r0, n), :]

    def past(k, j):
        if geom.npast is None:
            return new_refs[k][0, pl.ds(_aligned(geom.row0 + j * KEY_BLOCK, 16), KEY_BLOCK), :]
        ref, r0 = past_refs[k], _aligned(j * KEY_BLOCK, KEY_BLOCK)
        return ref[0, 0, pl.ds(r0, KEY_BLOCK), :] if len(ref.shape) == 4 else ref[0, pl.ds(r0, KEY_BLOCK), :]
    return new, past


def _fill_narrow(dst_ref, rows):
    dst_ref[...] = jnp.zeros(dst_ref.shape, dst_ref.dtype)
    dst_ref[0:rows.shape[0], :] = rows.astype(dst_ref.dtype)


def _attn_grid(geom, bsz):
    return (bsz, geom.ntile + (1 if geom.n_meta else 0))


def _maps_dot(lhs, n, w, rhs, transpose_rhs=False):
    dot = _dot_nt if transpose_rhs else _dot
    if isinstance(lhs, (list, tuple)):
        return [dot(x, rhs) for x in lhs]
    nmaps = lhs.shape[0]
    if n % LANES == 0:
        out = dot(lhs[:, 0:n, 0:w].reshape(nmaps * n, w), rhs)
        return [out[p * n:(p + 1) * n] for p in range(nmaps)]
    return [dot(lhs[p, 0:n, 0:w], rhs) for p in range(nmaps)]


class Softmax:
    NREFS = 7

    def __init__(self, refs, nmaps, n):
        self.sw, self.sn, self.mx, self.m, self.ls, self.acc, self.e = refs
        self.nmaps, self.n = nmaps, n
        self.mx[:, 0:n, :] = jnp.full((nmaps, n, LANES), NEG_INF, F32)
        self.ls[:, 0:n, :] = jnp.zeros((nmaps, n, LANES), F32)
        self.acc[:, 0:n, :] = jnp.zeros((nmaps, n, self.acc.shape[2]), F32)

    @staticmethod
    def scratch(geom, nmaps, vw):
        return [pltpu.VMEM((nmaps, geom.nslot, geom.tq, KEY_BLOCK), F32),
                pltpu.VMEM((nmaps, 2, geom.tq, NARROW), F32),
                pltpu.VMEM((nmaps, geom.tq, LANES), F32), pltpu.VMEM((nmaps, geom.tq, LANES), F32),
                pltpu.VMEM((nmaps, geom.tq, LANES), F32), pltpu.VMEM((nmaps, geom.tq, vw), F32),
                pltpu.VMEM((nmaps, geom.tq, KEY_BLOCK), BF)]

    def _store(self, p, slot, wide, s):
        if wide:
            self.sw[p, slot, 0:self.n, :] = s
        else:
            self.sn[p, slot, 0:self.n, :] = s

    def _load(self, p, slot, wide):
        return self.sw[p, slot, 0:self.n, :] if wide else self.sn[p, slot, 0:self.n, :]

    def scores(self, slot, wide, q_maps, k_b, scale, bias_of=None, extra=None):
        n = self.n
        d = q_maps[0].shape[-1] if isinstance(q_maps, (list, tuple)) else q_maps.shape[2]
        s_all = _maps_dot(q_maps, n, d, k_b, transpose_rhs=True)
        for p in range(self.nmaps):
            s = s_all[p]
            if scale != 1.0:
                s = s * scale
            if bias_of is not None:
                s = s + bias_of(p)
            if extra is not None:
                s = s + extra
            self._store(p, slot, wide, s)
            self.mx[p, 0:n, :] = jnp.maximum(self.mx[p, 0:n, :], _halves(s, jnp.maximum))

    def finish_max(self):
        n = self.n
        for p in range(self.nmaps):
            m = jnp.max(self.mx[p, 0:n, :], axis=1, keepdims=True)
            self.m[p, 0:n, :] = jnp.broadcast_to(m, (n, LANES))

    def values(self, slot, wide, v_b):
        n, w = self.n, (KEY_BLOCK if wide else NARROW)
        staged = n % LANES == 0
        es = []
        for p in range(self.nmaps):
            s = self._load(p, slot, wide)
            e = jnp.exp(s - _rep(self.m[p, 0:n, :], w))
            self.ls[p, 0:n, :] += _halves(e, jnp.add)
            if staged:
                self.e[p, 0:n, 0:w] = e.astype(BF)
            else:
                es.append(e)
        for p, pv in enumerate(_maps_dot(self.e if staged else es, n, w, v_b)):
            self.acc[p, 0:n, :] += pv

    def result(self, p):
        n = self.n
        return self.acc[p, 0:n, :] / jnp.sum(self.ls[p, 0:n, :], axis=1, keepdims=True)


def _mla_kernel(geom, *refs):
    n_past = 0 if geom.npast is None else 1
    (q_ref, kc_ref), refs = refs[:2], refs[2:]
    past_refs, refs = refs[:n_past], refs[n_past:]
    (wuv_ref, o_ref), refs = refs[:2], refs[2:]
    sm_refs, (qm_s, kn_s, km_s) = refs[:Softmax.NREFS], refs[Softmax.NREFS:]
    new, past = _readers(geom, (kc_ref,), past_refs)
    scale = (MLA_NOPE + MLA_ROPE) ** -0.5

    def tile(q0, n, qpos, npast, with_meta, own_wide):
        sm = Softmax(sm_refs, MLA_HEADS, n)
        q_heads = lambda: [q_ref[0, pl.ds(q0, n), 2 * LANES * h:2 * LANES * (h + 1)] for h in range(MLA_HEADS)]
        if n % LANES == 0:
            for h, q_h in enumerate(q_heads()):
                qm_s[h, 0:n, :] = q_h
        q_src = lambda: qm_s if n % LANES == 0 else q_heads()
        own_slot = npast if own_wide else 0
        if own_wide:
            own_k = lambda: new(0, q0, n)
        else:
            _fill_narrow(kn_s, new(0, q0, n))
            own_k = lambda: kn_s[...]
        own_cols = n if own_wide else NARROW
        sm.scores(own_slot, own_wide, q_src(), own_k(), scale,
                  extra=jnp.where(_own_mask(qpos, n, own_cols, False), 0.0, NEG_INF))

        def p1(j, carry):
            sm.scores(j, True, q_src(), past(0, j), scale)
            return carry
        _past_loop(npast, p1)
        if with_meta:
            _fill_narrow(km_s, new(0, 0, geom.n_meta))
            pad = jnp.where(lax.broadcasted_iota(I32, (1, NARROW), 1) < geom.n_meta, 0.0, NEG_INF)
            sm.scores(1, False, q_src(), km_s[...], scale, extra=pad)
        sm.finish_max()

        sm.values(own_slot, own_wide, own_k())

        def p2(j, carry):
            sm.values(j, True, past(0, j))
            return carry
        _past_loop(npast, p2)
        if with_meta:
            sm.values(1, False, km_s[...])

        out = None
        ohead = _lane_group((n, BR_W), MLA_V)
        for h in range(MLA_HEADS):
            o_h = jnp.where(ohead == h, _dot(sm.result(h), wuv_ref[...]), 0.0)
            out = o_h if out is None else out + o_h
        o_ref[0, pl.ds(q0, n), :] = out

    _tile_walk(geom,
               lambda q0, n, qpos, npast: tile(q0, n, qpos, npast, bool(geom.n_meta), geom.own_wide),
               lambda q0, n, qpos: tile(q0, n, qpos, 0, False, False))


def _mla_call(geom, layer, qcat, kcat, past, wuv_pad):
    bsz = qcat.shape[0]
    ins = [qcat, kcat] + list(past) + [wuv_pad]
    specs = ([_slab_spec(a) for a in (qcat, kcat)] + [_past_spec(a, layer) for a in past] + [_const_spec(wuv_pad)])
    return pl.pallas_call(
        functools.partial(_mla_kernel, geom), grid=_attn_grid(geom, bsz),
        in_specs=specs, out_specs=pl.BlockSpec((1, geom.lq, BR_W), lambda b, i: (b, 0, 0)),
        out_shape=jax.ShapeDtypeStruct((bsz, geom.lq, BR_W), F32),
        scratch_shapes=(Softmax.scratch(geom, MLA_HEADS, 2 * LANES)
                        + [pltpu.VMEM((MLA_HEADS, geom.tq, 2 * LANES), BF)] + [pltpu.VMEM((NARROW, 2 * LANES), BF)] * 2),
        compiler_params=_cparams(("parallel", "arbitrary")), name="mla_attn")(*ins)


def _suffix_ones(n):
    later = lax.broadcasted_iota(I32, (n, n), 0) > lax.broadcasted_iota(I32, (n, n), 1)
    return jnp.concatenate([jnp.where(later, 1.0, 0.0), jnp.ones((n, LANES), F32)], axis=1).astype(BF)


def _sb_kernel(geom, *refs):
    n_past = 0 if geom.npast is None else 2
    (q_ref, k_ref, v_ref), refs = refs[:3], refs[3:]
    past_refs, (o_ref, acc_s, carry_s, qm_s, hl_s, lb_s, a_s, tri_s, kn_s, vn_s, km_s, vm_s) = \
        refs[:n_past], refs[n_past:]
    new, past = _readers(geom, (k_ref, v_ref), past_refs)

    def tile(q0, n, qpos, npast, with_meta, own_wide):
        q = q_ref[0, pl.ds(q0, n), :]
        head = _lane_group(q.shape, SB_DH)
        for h in range(SB_HEADS):
            qm_s[h, 0:n, :] = jnp.where(head == h, q * SB_DH ** -0.5, 0.0).astype(BF)
        acc_s[:, 0:n, :] = jnp.zeros((SB_HEADS, n, BR_W), F32)
        carry_s[:, 0:n, :] = jnp.zeros((SB_HEADS, n, LANES), F32)

        def block(k_b, v_b, mask, tri):
            w = k_b.shape[0]
            staged = n % LANES == 0
            z_all = _maps_dot(qm_s, n, BR_W, k_b, transpose_rhs=True)
            halves, log_bs = [], []
            for h in range(SB_HEADS):
                z = z_all[h]
                soft = jnp.log(1.0 + jnp.exp(-jnp.abs(z)))
                log_b = jnp.minimum(z, 0.0) - soft
                log_1m = log_b - z
                if mask is not None:
                    log_1m = jnp.where(mask, log_1m, 0.0)
                hi = log_1m.astype(BF)
                lo = (log_1m - hi.astype(F32)).astype(BF)
                if staged:
                    hl_s[2 * h, 0:n, 0:w], hl_s[2 * h + 1, 0:n, 0:w], lb_s[h, 0:n, 0:w] = hi, lo, log_b
                else:
                    halves += [hi, lo]
                    log_bs.append(log_b)
            sums_all = _maps_dot(hl_s if staged else halves, n, w, tri)
            weights = []
            for h in range(SB_HEADS):
                sums = sums_all[2 * h] + sums_all[2 * h + 1]
                carry = carry_s[h, 0:n, :]
                log_b = lb_s[h, 0:n, 0:w] if staged else log_bs[h]
                a = jnp.exp(log_b + sums[:, :w] + _rep(carry, w))
                if mask is not None:
                    a = jnp.where(mask, a, 0.0)
                if staged:
                    a_s[h, 0:n, 0:w] = a.astype(BF)
                else:
                    weights.append(a)
                carry_s[h, 0:n, :] = carry + sums[:, w:]
            for h, pv in enumerate(_maps_dot(a_s if staged else weights, n, w, v_b)):
                acc_s[h, 0:n, :] += pv

        if own_wide:
            block(new(0, q0, n), new(1, q0, n), _own_mask(qpos, n, n, True), _suffix_ones(n))
        else:
            _fill_narrow(kn_s, new(0, q0, n))
            _fill_narrow(vn_s, new(1, q0, n))
            block(kn_s[...], vn_s[...], _own_mask(qpos, n, NARROW, True), _suffix_ones(NARROW))
        if _no_past(npast):
            tri_kb = None
        elif n % LANES == 0:
            tri_s[...] = _suffix_ones(KEY_BLOCK)
            tri_kb = lambda: tri_s[...]
        else:
            tri_value = _suffix_ones(KEY_BLOCK)
            tri_kb = lambda: tri_value

        def past_block(jj, carry):
            j = npast - 1 - jj
            block(past(0, j).astype(BF), past(1, j).astype(BF), None, tri_kb())
            return carry
        _past_loop(npast, past_block)
        if with_meta:
            m0 = 0
            _fill_narrow(km_s, new(0, m0, geom.n_meta))
            _fill_narrow(vm_s, new(1, m0, geom.n_meta))
            pad = jnp.broadcast_to(lax.broadcasted_iota(I32, (1, NARROW), 1) < geom.n_meta, (n, NARROW))
            block(km_s[...], vm_s[...], pad, _suffix_ones(NARROW))

        out = None
        for h in range(SB_HEADS):
            o_h = jnp.where(head == h, acc_s[h, 0:n, :], 0.0)
            out = o_h if out is None else out + o_h
        o_ref[0, pl.ds(q0, n), :] = out

    _tile_walk(geom,
               lambda q0, n, qpos, npast: tile(q0, n, qpos, npast, bool(geom.n_meta), geom.own_wide),
               lambda q0, n, qpos: tile(q0, n, qpos, 0, False, False))


def _sb_call(geom, layer, q, k, v, past):
    bsz = q.shape[0]
    ins = [q, k, v] + list(past)
    specs = [_slab_spec(a) for a in (q, k, v)] + [_past_spec(a, layer) for a in past]
    return pl.pallas_call(
        functools.partial(_sb_kernel, geom), grid=_attn_grid(geom, bsz),
        in_specs=specs, out_specs=pl.BlockSpec((1, geom.lq, BR_W), lambda b, i: (b, 0, 0)),
        out_shape=jax.ShapeDtypeStruct((bsz, geom.lq, BR_W), F32),
        scratch_shapes=[pltpu.VMEM((SB_HEADS, geom.tq, BR_W), F32), pltpu.VMEM((SB_HEADS, geom.tq, LANES), F32),
                        pltpu.VMEM((SB_HEADS, geom.tq, BR_W), BF),
                        pltpu.VMEM((2 * SB_HEADS, geom.tq, KEY_BLOCK), BF), pltpu.VMEM((SB_HEADS, geom.tq, KEY_BLOCK), F32),
                        pltpu.VMEM((SB_HEADS, geom.tq, KEY_BLOCK), BF), pltpu.VMEM((KEY_BLOCK, KEY_BLOCK + LANES), BF)]
                       + [pltpu.VMEM((NARROW, BR_W), BF)] * 4,
        compiler_params=_cparams(("parallel", "arbitrary")), name="sb_attn")(*ins)


def _bias_readers(geom, own_ref, past_ref, meta_ref, mown_ref, tile_idx):
    def own(h, n):
        return own_ref[0, h, 0:n, :]

    def earlier(j, h):
        slot = j if geom.npast is not None else jnp.where(j == tile_idx - 1, 0, 1)
        return past_ref[slot, h]

    def meta(h):
        return meta_ref[jnp.minimum(tile_idx, 1), h]

    def meta_own(h):
        return mown_ref[0, h]
    return own, earlier, meta, meta_own


def _diff_kernel(geom, lam_init, *refs):
    n_past = 0 if geom.npast is None else 2
    n_bias = 4 if geom.n_meta else 2
    (q_ref, k_ref, v_ref), refs = refs[:3], refs[3:]
    past_refs, refs = refs[:n_past], refs[n_past:]
    bias_refs, refs = list(refs[:n_bias]) + [None] * (4 - n_bias), refs[n_bias:]
    (lam_ref, sg_ref, o_ref), refs = refs[:3], refs[3:]
    sm_refs, (qm_s, kn_s, vn_s, km_s, vm_s) = refs[:Softmax.NREFS], refs[Softmax.NREFS:]
    new, past = _readers(geom, (k_ref, v_ref), past_refs)
    b_own, b_earlier, b_meta, b_meta_own = _bias_readers(geom, *bias_refs, pl.program_id(1))
    scale = DIFF_DQK ** -0.5
    npair = 2 * DIFF_HEADS

    def tile(q0, n, qpos, npast, with_meta, own_wide, is_meta):
        sm = Softmax(sm_refs, npair, n)
        q = q_ref[0, pl.ds(q0, n), :]
        pair = _lane_group(q.shape, DIFF_DQK)
        for p in range(npair):
            qm_s[p, 0:n, :] = jnp.where(pair == p, q, jnp.zeros_like(q))
        q_of = qm_s
        own_slot = npast if own_wide else 0
        if own_wide:
            own_k, own_v = (lambda: new(0, q0, n)), (lambda: new(1, q0, n))
        else:
            _fill_narrow(kn_s, new(0, q0, n))
            _fill_narrow(vn_s, new(1, q0, n))
            own_k, own_v = (lambda: kn_s[...]), (lambda: vn_s[...])
        sm.scores(own_slot, own_wide, q_of, own_k(), scale,
                  bias_of=(lambda p: b_meta_own(p // 2)) if is_meta else (lambda p: b_own(p // 2, n)))

        def p1(j, carry):
            sm.scores(j, True, q_of, past(0, j), scale,
                      bias_of=lambda p: b_earlier(j, p // 2))
            return carry
        _past_loop(npast, p1)
        if with_meta:
            m0 = 0
            _fill_narrow(km_s, new(0, m0, geom.n_meta))
            _fill_narrow(vm_s, new(1, m0, geom.n_meta))
            sm.scores(1, False, q_of, km_s[...], scale, bias_of=lambda p: b_meta(p // 2))
        sm.finish_max()

        sm.values(own_slot, own_wide, own_v().astype(BF))

        def p2(j, carry):
            sm.values(j, True, past(1, j).astype(BF))
            return carry
        _past_loop(npast, p2)
        if with_meta:
            sm.values(1, False, vm_s[...])

        lp = lam_ref[...]
        lam = (jnp.exp(jnp.sum(lp[0:1] * lp[1:2], axis=1, keepdims=True))
               - jnp.exp(jnp.sum(lp[2:3] * lp[3:4], axis=1, keepdims=True)) + lam_init)
        head = _lane_group((n, BR_W), DIFF_DV)
        out = None
        for h in range(DIFF_HEADS):
            o_h = jnp.where(head == h, sm.result(2 * h) - lam * sm.result(2 * h + 1), 0.0)
            ms = jnp.sum(o_h * o_h, axis=1, keepdims=True) * (1.0 / DIFF_DV)
            o_h = o_h * lax.rsqrt(ms + RMS_EPS) * sg_ref[...] * (1.0 - lam_init)
            out = o_h if out is None else out + o_h
        o_ref[0, pl.ds(q0, n), :] = out

    _tile_walk(geom,
               lambda q0, n, qpos, npast: tile(q0, n, qpos, npast, bool(geom.n_meta), geom.own_wide, False),
               lambda q0, n, qpos: tile(q0, n, qpos, 0, False, False, True))


def _diff_call(geom, layer, q, k, v, past, tables, lam_p, sg_tiled):
    bsz = q.shape[0]
    lam_init = 0.8 - 0.6 * math.exp(-0.3 * layer)
    tables = [t for t in tables if t is not None]
    ins = [q, k, v] + list(past) + tables + [lam_p, sg_tiled]
    specs = ([_slab_spec(a) for a in (q, k, v)] + [_past_spec(a, layer) for a in past]
             + [_const_spec(a) for a in tables + [lam_p, sg_tiled]])
    npair = 2 * DIFF_HEADS
    return pl.pallas_call(
        functools.partial(_diff_kernel, geom, lam_init), grid=_attn_grid(geom, bsz),
        in_specs=specs, out_specs=pl.BlockSpec((1, geom.lq, BR_W), lambda b, i: (b, 0, 0)),
        out_shape=jax.ShapeDtypeStruct((bsz, geom.lq, BR_W), F32),
        scratch_shapes=(Softmax.scratch(geom, npair, BR_W) + [pltpu.VMEM((npair, geom.tq, BR_W), BF)]
                        + [pltpu.VMEM((NARROW, BR_W), BF)] * 4),
        compiler_params=_cparams(("parallel", "arbitrary")), name="diff_attn")(*ins)


def _sortable(x):
    b = lax.bitcast_convert_type(x + 0.0, I32)
    return b ^ ((b >> 31) & I32(0x7FFFFFFF))


def _dsa_kernel(geom, *refs):
    n_past = 0 if geom.npast is None else 3
    n_bias = 4 if geom.n_meta else 2
    (q_ref, qi_ref, wi_ref, k_ref, v_ref, kit_ref), refs = refs[:6], refs[6:]
    past_refs, refs = refs[:n_past], refs[n_past:]
    bias_refs, refs = list(refs[:n_bias]) + [None] * (4 - n_bias), refs[n_bias:]
    o_ref, refs = refs[0], refs[1:]
    sm_refs, (qm_s, qim_s, wib_s, kw_s, kn_s, last_s,
              kno_s, vno_s, kio_s, knm_s, vnm_s, kim_s) = refs[:Softmax.NREFS], refs[Softmax.NREFS:]
    new, past = _readers(geom, (k_ref, v_ref, kit_ref), past_refs)
    b_own, b_earlier, b_meta, b_meta_own = _bias_readers(geom, *bias_refs, pl.program_id(1))
    topk = float(geom.topk)
    past_pos0 = N_META
    index_bits = max(1, int(geom.lq + geom.lp + N_META).bit_length())
    ones_count = jnp.ones((LANES, LANES), BF)

    def tile(q0, n, qpos, npast, with_meta, own_wide, is_meta):
        nm = geom.n_meta
        m0 = 0
        own_slot = npast if own_wide else 0
        own_cols = n if own_wide else NARROW

        qi = qi_ref[0, pl.ds(q0, n), :]
        igrp = _lane_group(qi.shape, IDX_DIM)
        wi = wi_ref[0, pl.ds(q0, n), :] * (IDX_HEADS ** -0.5 * IDX_DIM ** -0.5)
        for h in range(IDX_HEADS):
            qim_s[h, 0:n, :] = jnp.where(igrp == h, qi, jnp.zeros_like(qi))
            wib_s[h, 0:n, :] = jnp.broadcast_to(wi[:, h:h + 1], (n, LANES))
        q = q_ref[0, pl.ds(q0, n), :]
        head = _lane_group(q.shape, DSA_DH)
        for h in range(DSA_HEADS):
            qm_s[h, 0:n, :] = jnp.where(head == h, q * DSA_DH ** -0.5, 0.0).astype(BF)
        if own_wide:
            own_k, own_v, own_ki = (lambda: new(0, q0, n)), (lambda: new(1, q0, n)), (lambda: new(2, q0, n))
        else:
            _fill_narrow(kno_s, new(0, q0, n))
            _fill_narrow(vno_s, new(1, q0, n))
            _fill_narrow(kio_s, new(2, q0, n))
            own_k, own_v, own_ki = (lambda: kno_s[...]), (lambda: vno_s[...]), (lambda: kio_s[...])
        if with_meta:
            _fill_narrow(knm_s, new(0, m0, nm))
            _fill_narrow(vnm_s, new(1, m0, nm))
            _fill_narrow(kim_s, new(2, m0, nm))

        def index_keys(kit_b):
            raw = _maps_dot(qim_s, n, BR_W, kit_b, transpose_rhs=True)
            score = None
            for h in range(IDX_HEADS):
                t = _rep(wib_s[h, 0:n, :], kit_b.shape[0]) * jnp.maximum(raw[h], 0.0)
                score = t if score is None else score + t
            return _sortable(score)

        own_keys = jnp.where(_own_mask(qpos, n, own_cols, False), index_keys(own_ki()), I32(KEY_NEG))
        if own_wide:
            kw_s[own_slot, 0:n, :] = own_keys
        else:
            pad = lax.broadcasted_iota(I32, (1, NARROW), 1) < n
            kn_s[0, 0:n, :] = jnp.where(pad, own_keys, I32(INT_MIN))

        def score_block(j, carry):
            kw_s[j, 0:n, :] = index_keys(past(2, j).astype(BF))
            return carry
        _past_loop(npast, score_block)
        if with_meta:
            pad = lax.broadcasted_iota(I32, (1, NARROW), 1) < nm
            kn_s[1, 0:n, :] = jnp.where(pad, index_keys(kim_s[...]), I32(INT_MIN))
        nwide = npast + 1 if own_wide else npast

        def wide_sweep(init, fn):
            if isinstance(nwide, int) and nwide <= STATIC_UNROLL:
                for j in range(nwide):
                    init = fn(j, init)
                return init
            return lax.fori_loop(0, nwide, fn, init)

        def one(cond):
            return jnp.where(cond, 1.0, 0.0)

        def count(wide_fn, own_narrow_fn, meta_fn):
            part = wide_sweep(jnp.zeros((n, LANES), F32), lambda j, a: a + wide_fn(kw_s[j, 0:n, :], j))
            if not own_wide:
                part = part + own_narrow_fn(kn_s[0, 0:n, :])
            if with_meta:
                part = part + meta_fn(kn_s[1, 0:n, :])
            return jnp.dot(part.astype(BF), ones_count, preferred_element_type=F32)

        def count_cmp(cmp):
            return count(lambda x, j: _halves(one(cmp(x, KEY_BLOCK)), jnp.add),
                         lambda x: one(cmp(x, NARROW)), lambda x: one(cmp(x, NARROW)))

        zero = jnp.zeros((n, LANES), I32)
        c0 = count_cmp(lambda x, w: x >= _rep(zero, w))
        thr0 = jnp.where(c0 >= topk, I32(0), I32(INT_MIN))
        cnt0 = jnp.where(c0 >= topk, c0, topk + 1.0)

        def bit_step(b, state):
            thr, cnt = state
            cand = thr | jnp.left_shift(I32(1), I32(30) - b)
            c = count_cmp(lambda x, w: x >= _rep(cand, w))
            keep = c >= topk
            return jnp.where(keep, cand, thr), jnp.where(keep, c, cnt)
        thr, cnt = lax.fori_loop(0, 31, bit_step, (thr0, cnt0))

        last_s[0:n, :] = jnp.full((n, LANES), INT_MAX, I32)
        has_ties = jnp.max(one((cnt > topk) & (thr > I32(KEY_NEG)))) > 0.0

        @pl.when(has_ties)
        def _():
            need = topk - count_cmp(lambda x, w: x > _rep(thr, w))
            colw = lax.broadcasted_iota(I32, (1, KEY_BLOCK), 1)
            coln = lax.broadcasted_iota(I32, (1, NARROW), 1)

            def wide_pos0(j):
                return past_pos0 + j * KEY_BLOCK

            def ties_before(x):
                return count(lambda kk, j: _halves(one((kk == _rep(thr, KEY_BLOCK))
                                                       & (colw < _rep(x, KEY_BLOCK) - wide_pos0(j))), jnp.add),
                             lambda kk: one((kk == thr) & (coln < x - qpos)),
                             lambda kk: one((kk == thr) & (coln < x)))

            def pos_step(b, last_pos):
                cand = last_pos + jnp.left_shift(I32(1), I32(index_bits - 1) - b)
                return jnp.where(ties_before(cand) < need, cand, last_pos)
            last_s[0:n, :] = lax.fori_loop(0, index_bits, pos_step, jnp.zeros((n, LANES), I32))

        last_pos = last_s[0:n, :]

        def sel_bias(kk, kpos0):
            w = kk.shape[1]
            cols = lax.broadcasted_iota(I32, (1, w), 1)
            t = _rep(thr, w)
            sel = (kk > t) | ((kk == t) & (cols <= _rep(last_pos, w) - kpos0))
            return jnp.where(sel, 0.0, NEG_INF)

        sm = Softmax(sm_refs, DSA_HEADS, n)
        q_of = qm_s
        own_kk = kw_s[own_slot, 0:n, :] if own_wide else kn_s[0, 0:n, :]
        sm.scores(own_slot, own_wide, q_of, own_k(), 1.0,
                  bias_of=(lambda h: b_meta_own(h)) if is_meta else (lambda h: b_own(h, n)),
                  extra=sel_bias(own_kk, qpos))

        def p1(j, carry):
            sm.scores(j, True, q_of, past(0, j).astype(BF), 1.0,
                      bias_of=lambda h: b_earlier(j, h),
                      extra=sel_bias(kw_s[j, 0:n, :], past_pos0 + j * KEY_BLOCK))
            return carry
        _past_loop(npast, p1)
        if with_meta:
            sm.scores(1, False, q_of, knm_s[...], 1.0, bias_of=lambda h: b_meta(h),
                      extra=sel_bias(kn_s[1, 0:n, :], 0))
        sm.finish_max()

        sm.values(own_slot, own_wide, own_v().astype(BF))

        def p2(j, carry):
            sm.values(j, True, past(1, j).astype(BF))
            return carry
        _past_loop(npast, p2)
        if with_meta:
            sm.values(1, False, vnm_s[...])

        out = None
        for h in range(DSA_HEADS):
            o_h = jnp.where(head == h, sm.result(h), 0.0)
            out = o_h if out is None else out + o_h
        o_ref[0, pl.ds(q0, n), :] = out

    _tile_walk(geom,
               lambda q0, n, qpos, npast: tile(q0, n, qpos, npast, bool(geom.n_meta), geom.own_wide, False),
               lambda q0, n, qpos: tile(q0, n, qpos, 0, False, False, True))


def _dsa_call(geom, layer, q, qi, wi, k, v, kit, past, tables):
    bsz = q.shape[0]
    tables = [t for t in tables if t is not None]
    ins = [q, qi, wi, k, v, kit] + list(past) + tables
    specs = ([_slab_spec(a) for a in (q, qi, wi, k, v, kit)] + [_past_spec(a, layer) for a in past]
             + [_const_spec(a) for a in tables])
    tq = geom.tq
    return pl.pallas_call(
        functools.partial(_dsa_kernel, geom), grid=_attn_grid(geom, bsz),
        in_specs=specs, out_specs=pl.BlockSpec((1, geom.lq, BR_W), lambda b, i: (b, 0, 0)),
        out_shape=jax.ShapeDtypeStruct((bsz, geom.lq, BR_W), F32),
        scratch_shapes=(Softmax.scratch(geom, DSA_HEADS, BR_W)
                        + [pltpu.VMEM((DSA_HEADS, tq, BR_W), BF), pltpu.VMEM((IDX_HEADS, tq, BR_W), BF),
                           pltpu.VMEM((IDX_HEADS, tq, LANES), F32),
                           pltpu.VMEM((geom.nslot, tq, KEY_BLOCK), I32), pltpu.VMEM((2, tq, NARROW), I32),
                           pltpu.VMEM((tq, LANES), I32)]
                        + [pltpu.VMEM((NARROW, BR_W), BF)] * 6),
        compiler_params=_cparams(("parallel", "arbitrary")), name="dsa_attn")(*ins)


def _row_tile(t):
    for tm in (384, 256, 128, 64, 32, 16, 8):
        if t % tm == 0:
            return tm
    raise ValueError(f"token count {t} has no supported row tile")


def _rope_tables(pos):
    half = MLA_ROPE // 2
    inv_freq = ROPE_THETA ** (-jnp.arange(half, dtype=jnp.float32) / half)
    ang = pos.astype(jnp.float32)[:, None] * inv_freq[None, :]
    cos, sin = jnp.cos(ang), jnp.sin(ang)
    return jnp.concatenate([cos, cos], axis=1), jnp.concatenate([-sin, sin], axis=1)


def _run_group(geom, x, pos_rows, caches, weights):
    (ln_in_g, ln_in_b, w_in, mla_qnorm_g, mla_w_uq, mla_kvnorm_g, mla_w_uk, mla_w_uv, diff_lambda,
     diff_subln_g, rel_bias, w_br, w_out, ln1_g, ln1_b, w_ff1, b_ff1, w_ff2, b_ff2, ln2_g, ln2_b) = weights
    bsz, lq, d = x.shape
    t = bsz * lq
    tm = _row_tile(t)
    cos32, sin32 = _rope_tables(pos_rows)
    cos_k = jnp.tile(cos32, (bsz, MLA_HEADS))
    sin_k = jnp.tile(sin32, (bsz, MLA_HEADS))
    tables = _bias_tables(geom, rel_bias)
    diff_tables = _split_tables(tables, 0, DIFF_HEADS)
    dsa_tables = _split_tables(tables, DIFF_HEADS, DIFF_HEADS + DSA_HEADS)

    xf = _ln_call(x.reshape(t, d), ln_in_g, ln_in_b, tm)
    rows = []
    for l in range(DEPTH):
        w_uq = mla_w_uq[l].reshape(MLA_Q_LORA, MLA_HEADS, MLA_NOPE + MLA_ROPE)
        wn = w_uq[:, :, :MLA_NOPE].reshape(MLA_Q_LORA, MLA_HEADS * MLA_NOPE).astype(BF)
        wp = w_uq[:, :, MLA_NOPE:].reshape(MLA_Q_LORA, MLA_HEADS * MLA_ROPE).astype(BF)
        wuk = mla_w_uk[l].reshape(MLA_KV_LORA, MLA_HEADS * MLA_NOPE).T.astype(BF)
        wuv = jnp.pad(mla_w_uv[l].reshape(MLA_KV_LORA, MLA_HEADS * MLA_V),
                      ((0, 2 * LANES - MLA_KV_LORA), (0, 0))).astype(BF)
        (ckv, kpe, b_k, b_v, c_k, c_v, d_k, d_v, d_ki, d_wi,
         qcat, kcat, bq_b, bk_b, bv_b, cq_b, ck_b, cv_b, dq_b, dk_b, dv_b, dqi_b, kit_b) = _proj_call(
            xf, _mix_weight(w_in[l]), mla_kvnorm_g[l].reshape(1, -1), mla_qnorm_g[l].reshape(1, -1),
            wn, wp, wuk, cos_k, sin_k, tm)
        per = lambda a: a.reshape(bsz, lq, a.shape[-1])
        if caches is None:
            past = lambda *idx: []
            mla_past, kit_past = [], []
        else:
            past = lambda *idx: [caches[i] for i in idx]
            zeros = jnp.zeros(caches[0].shape[1:3] + (2 * LANES - MLA_KV_LORA - MLA_ROPE,), F32)
            mla_past = [jnp.concatenate([caches[0][l], caches[1][l], zeros], axis=-1).astype(BF)]
            kit_past = [jnp.tile(caches[8][l], (1, 1, IDX_HEADS)).astype(BF)]

        o_a = _mla_call(geom, l, per(qcat), per(kcat), mla_past, wuv)
        o_b = _sb_call(geom, l, per(bq_b), per(bk_b), per(bv_b), past(2, 3))
        o_c = _diff_call(geom, l, per(cq_b), per(ck_b), per(cv_b), past(4, 5), diff_tables, diff_lambda[l],
                         jnp.tile(diff_subln_g[l], DIFF_HEADS).reshape(1, -1))
        o_d = _dsa_call(geom, l, per(dq_b), per(dqi_b), per(d_wi), per(dk_b), per(dv_b), per(kit_b),
                        past(6, 7) + kit_past, dsa_tables)

        flat = lambda a: a.reshape(t, BR_W)
        wg = w_in[l][:, IN_OFFS[15]:IN_OFFS[16]].astype(BF)
        x1 = _merge_call(xf, flat(o_a), flat(o_b), flat(o_c), flat(o_d), wg, w_br[l].astype(BF),
                         w_out[l].astype(BF), ln1_g[l].reshape(1, -1), ln1_b[l].reshape(1, -1), tm)
        xf = _ffn_call(x1, w_ff1[l].astype(BF), b_ff1[l].reshape(1, -1), w_ff2[l].astype(BF),
                       b_ff2[l].reshape(1, -1), ln2_g[l].reshape(1, -1), ln2_b[l].reshape(1, -1), tm)
        rows.append([per(a) for a in (ckv, kpe, b_k, b_v, c_k, c_v, d_k, d_v, d_ki)])
    return xf.reshape(bsz, lq, d), rows


_ROW_TRAILING = ((MLA_KV_LORA,), (MLA_ROPE,), (SB_HEADS, SB_DH), (SB_HEADS, SB_DH),
                 (DIFF_HEADS, 2, DIFF_DQK), (DIFF_HEADS, DIFF_DV), (DSA_HEADS, DSA_DH),
                 (DSA_HEADS, DSA_DH), (IDX_DIM,))


def kernel(x_prompt, x_sample, cache_mla_kv, cache_mla_pe, cache_sb_k, cache_sb_v, cache_diff_k, cache_diff_v, cache_dsa_k, cache_dsa_v, cache_dsa_kidx, meta, ln_in_g, ln_in_b, w_in, mla_qnorm_g, mla_w_uq, mla_kvnorm_g, mla_w_uk, mla_w_uv, diff_lambda, diff_subln_g, rel_bias, w_br, w_out, ln1_g, ln1_b, w_ff1, b_ff1, w_ff2, b_ff2, ln2_g, ln2_b):
    weights = (ln_in_g, ln_in_b, w_in, mla_qnorm_g, mla_w_uq, mla_kvnorm_g, mla_w_uk, mla_w_uv, diff_lambda,
               diff_subln_g, rel_bias, w_br, w_out, ln1_g, ln1_b, w_ff1, b_ff1, w_ff2, b_ff2, ln2_g, ln2_b)
    assert w_in.shape[0] == DEPTH and x_prompt.shape[2] == D_MODEL

    bsz_p, seq_p, _ = x_prompt.shape
    assert seq_p % KEY_BLOCK == 0
    meta_b = jnp.broadcast_to(meta[None].astype(x_prompt.dtype), (bsz_p, N_META, D_MODEL))
    xp = jnp.concatenate([meta_b, x_prompt], axis=1)
    pos_p = jnp.arange(N_META + seq_p, dtype=I32)
    geom_p = Geom(tq=KEY_BLOCK, ntile=seq_p // KEY_BLOCK, npast=None, n_meta=N_META, qpos0=N_META,
                  row0=N_META, lq=seq_p + N_META, lp=0, topk=min(DSA_TOPK, seq_p // 4))
    yp, rows_p = _run_group(geom_p, xp, pos_p, None, weights)
    y_prompt = yp[:, N_META:]
    p_rows = []
    for i, trailing in enumerate(_ROW_TRAILING):
        stacked = jnp.stack([r[i] for r in rows_p], axis=0)
        p_rows.append(stacked.reshape(stacked.shape[:3] + trailing))

    past_len = cache_mla_kv.shape[2]
    bsz_s, dec_seq, _ = x_sample.shape
    assert past_len % KEY_BLOCK == 0 and dec_seq % 16 == 0 and dec_seq <= NARROW
    assert past_len % CHUNK == 0 and dec_seq <= CHUNK, "new frames must share one chunk"
    caches = [c.reshape(c.shape[:3] + (-1,)) for c in
              (cache_mla_kv, cache_mla_pe, cache_sb_k, cache_sb_v, cache_diff_k, cache_diff_v,
               cache_dsa_k, cache_dsa_v, cache_dsa_kidx)]
    pos_s = N_META + past_len + jnp.arange(dec_seq, dtype=I32)
    geom_s = Geom(tq=dec_seq, ntile=1, npast=past_len // KEY_BLOCK, n_meta=0, qpos0=N_META + past_len,
                  row0=0, lq=dec_seq, lp=past_len, topk=min(DSA_TOPK, (past_len + dec_seq) // 4))
    y_sample, rows_s = _run_group(geom_s, x_sample, pos_s, caches, weights)
    s_rows = []
    for i, trailing in enumerate(_ROW_TRAILING):
        stacked = jnp.stack([r[i] for r in rows_s], axis=0)
        s_rows.append(stacked.reshape(stacked.shape[:3] + trailing))

    return (y_prompt, y_sample, *p_rows, *s_rows)
```

```python
import functools
import math
from typing import NamedTuple, Optional

import jax
import jax.numpy as jnp
import numpy as np
from jax import lax
from jax.experimental import pallas as pl
from jax.experimental.pallas import tpu as pltpu

D_MODEL = 1024
CHUNK = 64
N_META = 16
MLA_HEADS = 4
MLA_Q_LORA = 256
MLA_KV_LORA = 128
MLA_NOPE = 64
MLA_ROPE = 32
MLA_V = 64
ROPE_THETA = 10000.0
SB_HEADS = 4
SB_DH = 64
DIFF_HEADS = 4
DIFF_DQK = 32
DIFF_DV = 64
DSA_HEADS = 4
DSA_DH = 64
IDX_HEADS = 8
IDX_DIM = 32
DSA_TOPK = 256
N_BRANCH = 4
BR_W = 256
D_FF = 4 * D_MODEL
T5_BUCKETS = 32
T5_MAX_DIST = 128
LN_EPS = 1e-5
RMS_EPS = 1e-6
NEG_INF = -1e30
DEPTH = 2
DN_ALPHA = (2 * DEPTH) ** 0.25
IN_SIZES = (MLA_Q_LORA, MLA_KV_LORA, MLA_ROPE,
            SB_HEADS * SB_DH, SB_HEADS * SB_DH, SB_HEADS * SB_DH,
            DIFF_HEADS * 2 * DIFF_DQK, DIFF_HEADS * 2 * DIFF_DQK, DIFF_HEADS * DIFF_DV,
            DSA_HEADS * DSA_DH, DSA_HEADS * DSA_DH, DSA_HEADS * DSA_DH,
            IDX_HEADS * IDX_DIM, IDX_DIM, IDX_HEADS,
            N_BRANCH * D_MODEL)
IN_OFFS = tuple(int(s) for s in np.cumsum((0,) + IN_SIZES))

LANES = 128
KEY_BLOCK = 256
NARROW = LANES
VMEM_LIMIT_MB = 56

BF = jnp.bfloat16
F32 = jnp.float32
I32 = jnp.int32

_NEG_BITS = int(np.float32(NEG_INF).view(np.int32))
KEY_NEG = _NEG_BITS ^ ((_NEG_BITS >> 31) & 0x7FFFFFFF)
INT_MIN = -(2 ** 31)
INT_MAX = 2 ** 31 - 1


def _cparams(sem):
    return pltpu.CompilerParams(dimension_semantics=sem, vmem_limit_bytes=VMEM_LIMIT_MB * 1024 * 1024)


def _dot(a, b):
    return jnp.dot(a.astype(BF), b.astype(BF), preferred_element_type=F32)


def _dot_nt(a, b):
    return lax.dot_general(a.astype(BF), b.astype(BF), (((1,), (1,)), ((), ())),
                           preferred_element_type=F32)


def _layer_norm(x, g, b):
    mu = jnp.mean(x, axis=-1, keepdims=True)
    xc = x - mu
    var = jnp.mean(xc * xc, axis=-1, keepdims=True)
    return xc * lax.rsqrt(var + LN_EPS) * g + b


def _rms_norm(x, g):
    return x * lax.rsqrt(jnp.mean(x * x, axis=-1, keepdims=True) + RMS_EPS) * g


def _rope_lanes(x, cos, sin):
    lane = lax.broadcasted_iota(I32, x.shape, 1)
    swapped = jnp.where((lane & 31) < 16, pltpu.roll(x, LANES - 16, 1), pltpu.roll(x, 16, 1))
    return x * cos + swapped * sin


def _lane_group(shape, width):
    return lax.broadcasted_iota(I32, shape, 1) // width


def _halves(x, op):
    return op(x[:, :LANES], x[:, LANES:]) if x.shape[1] == 2 * LANES else x


def _rep(x, width):
    return jnp.concatenate([x, x], axis=1) if width == 2 * LANES else x


class Geom(NamedTuple):
    tq: int
    ntile: int
    npast: Optional[int]
    n_meta: int
    qpos0: int
    row0: int
    lq: int
    lp: int
    topk: int

    @property
    def own_wide(self):
        return self.tq == KEY_BLOCK

    @property
    def nslot(self):
        return max(self.ntile if self.npast is None else self.npast, 1)


def _ln_kernel(x_ref, g_ref, b_ref, o_ref):
    o_ref[...] = _layer_norm(x_ref[...], g_ref[...], b_ref[...])


def _ln_call(x, g, b, tm):
    t, d = x.shape
    return pl.pallas_call(
        _ln_kernel, grid=(t // tm,),
        in_specs=[pl.BlockSpec((tm, d), lambda i: (i, 0)),
                  pl.BlockSpec((1, d), lambda i: (0, 0)),
                  pl.BlockSpec((1, d), lambda i: (0, 0))],
        out_specs=pl.BlockSpec((tm, d), lambda i: (i, 0)),
        out_shape=jax.ShapeDtypeStruct((t, d), F32),
        compiler_params=_cparams(("parallel",)), name="ln_in")(x, g.reshape(1, d), b.reshape(1, d))


_WIDE = (3, 4, 5, 6, 7, 8, 9, 10, 11, 12)
_WIDE_F32 = (1, 2, 4, 5, 7, 8)
_MIX_COLS = 256 + 128 + 256 * len(_WIDE) + LANES + 256 + LANES
_Q_SCALE = {0: SB_DH ** -0.5, 3: DIFF_DQK ** -0.5, 6: DSA_DH ** -0.5}
MLA_SCALE = (MLA_NOPE + MLA_ROPE) ** -0.5
ONES_LANE = 2 * LANES - 1


def _mix_weight(w_in_l):
    def seg(i, pad_to=None):
        w = w_in_l[:, IN_OFFS[i]:IN_OFFS[i + 1]]
        if pad_to is not None:
            w = jnp.pad(w, ((0, 0), (0, pad_to - w.shape[1])))
        return w
    cols = ([seg(0), seg(1)] + [seg(i) for i in _WIDE]
            + [seg(2, LANES), jnp.tile(seg(13), (1, IDX_HEADS)), seg(14, LANES)])
    return jnp.concatenate(cols, axis=1).astype(BF)


def _proj_kernel(x_ref, w_ref, kvg_ref, qg_ref, wn_ref, wp_ref, wuk_ref, cos_ref, sin_ref, *outs):
    (ckv_ref, kpe_ref, bk_ref, bv_ref, ck_ref, cv_ref, dk_ref, dv_ref, dki_ref, dwi_ref,
     qcat_ref, kcat_ref, *wide_bf) = outs
    kit_ref = wide_bf[-1]
    wide_bf = wide_bf[:-1]
    wide_f32 = dict(zip(_WIDE_F32, (bk_ref, bv_ref, ck_ref, cv_ref, dk_ref, dv_ref)))
    xb = x_ref[...].astype(BF)
    cos, sin = cos_ref[...], sin_ref[...]

    def seg(off, width):
        return jnp.dot(xb, w_ref[:, off:off + width], preferred_element_type=F32)

    qn = _rms_norm(seg(0, 256), qg_ref[...])
    nope = _dot(qn, wn_ref[...])
    pe = _rope_lanes(_dot(qn, wp_ref[...]), cos, sin)
    head = _lane_group(nope.shape, MLA_NOPE)
    lane = lax.broadcasted_iota(I32, pe.shape, 1)
    for h in range(MLA_HEADS):
        q_lat = _dot(jnp.where(head == h, nope, 0.0), wuk_ref[...]) * MLA_SCALE
        pe_h = (pe if h == 0 else pltpu.roll(pe, LANES - h * MLA_ROPE, 1)) * MLA_SCALE
        qcat_ref[:, 2 * LANES * h:2 * LANES * h + LANES] = q_lat.astype(BF)
        qcat_ref[:, 2 * LANES * h + LANES:2 * LANES * (h + 1)] = jnp.where(lane < MLA_ROPE, pe_h, 0.0).astype(BF)

    ckv = _rms_norm(seg(256, 128), kvg_ref[...])
    ckv_ref[...] = ckv
    off = 384
    for n, r in enumerate(wide_bf):
        y = seg(off, 256)
        if n in wide_f32:
            wide_f32[n][...] = y
        r[...] = (y * _Q_SCALE[n]).astype(BF) if n in _Q_SCALE else y.astype(BF)
        off += 256
    kpe = _rope_lanes(seg(off, LANES), cos, sin)
    kpe_ref[...] = kpe[:, :MLA_ROPE]
    kcat = jnp.concatenate([ckv, kpe], axis=1)
    kcat_ref[...] = jnp.where(lax.broadcasted_iota(I32, kcat.shape, 1) == ONES_LANE, 1.0, kcat).astype(BF)
    kit = seg(off + LANES, 256)
    dki_ref[...] = kit[:, :IDX_DIM]
    kit_ref[...] = kit.astype(BF)
    dwi_ref[...] = seg(off + LANES + 256, LANES)[:, :IDX_HEADS]


def _proj_call(x, w_mix, kvg, qg, wn, wp, wuk, cos_k, sin_k, tm):
    t, d = x.shape
    f32_w = [128, MLA_ROPE, 256, 256, 256, 256, 256, 256, IDX_DIM, IDX_HEADS]
    bf_w = [4 * 256, 256] + [256] * len(_WIDE) + [256]
    row = lambda w: pl.BlockSpec((tm, w), lambda i: (i, 0))
    const = lambda a: pl.BlockSpec(a.shape, lambda i: (0, 0))
    return pl.pallas_call(
        _proj_kernel, grid=(t // tm,),
        in_specs=[row(d), const(w_mix), const(kvg), const(qg), const(wn), const(wp), const(wuk),
                  row(LANES), row(LANES)],
        out_specs=[row(w) for w in f32_w + bf_w],
        out_shape=([jax.ShapeDtypeStruct((t, w), F32) for w in f32_w]
                   + [jax.ShapeDtypeStruct((t, w), BF) for w in bf_w]),
        compiler_params=_cparams(("parallel",)), name="mix_proj")(x, w_mix, kvg, qg, wn, wp, wuk, cos_k, sin_k)


def _merge_kernel(x_ref, oa_ref, ob_ref, oc_ref, od_ref, wg_ref, wbr_ref, wout_ref, g_ref, b_ref, o_ref):
    x = x_ref[...]
    xb = x.astype(BF)
    acc = None
    for n, o_n in enumerate((oa_ref, ob_ref, oc_ref, od_ref)):
        gate = jax.nn.sigmoid(jnp.dot(xb, wg_ref[:, n * D_MODEL:(n + 1) * D_MODEL],
                                      preferred_element_type=F32))
        br = jnp.dot(o_n[...].astype(BF), wbr_ref[n], preferred_element_type=F32)
        acc = gate * br if acc is None else acc + gate * br
    mix = jnp.dot(acc.astype(BF), wout_ref[...], preferred_element_type=F32)
    o_ref[...] = _layer_norm(DN_ALPHA * x + mix, g_ref[...], b_ref[...])


def _merge_call(x, o_a, o_b, o_c, o_d, wg, wbr, wout, g, b, tm):
    t, d = x.shape
    row = lambda w: pl.BlockSpec((tm, w), lambda i: (i, 0))
    return pl.pallas_call(
        _merge_kernel, grid=(t // tm,),
        in_specs=[row(d), row(BR_W), row(BR_W), row(BR_W), row(BR_W),
                  pl.BlockSpec((d, N_BRANCH * d), lambda i: (0, 0)),
                  pl.BlockSpec((N_BRANCH, BR_W, d), lambda i: (0, 0, 0)),
                  pl.BlockSpec((d, d), lambda i: (0, 0)),
                  pl.BlockSpec((1, d), lambda i: (0, 0)), pl.BlockSpec((1, d), lambda i: (0, 0))],
        out_specs=row(d), out_shape=jax.ShapeDtypeStruct((t, d), F32),
        compiler_params=_cparams(("parallel",)), name="merge")(x, o_a, o_b, o_c, o_d, wg, wbr, wout, g, b)


def _ffn_kernel(x_ref, w1_ref, b1_ref, w2_ref, b2_ref, g_ref, b_ref, o_ref):
    x = x_ref[...]
    xb = x.astype(BF)
    acc = None
    for c in range(D_FF // D_MODEL):
        sl = slice(c * D_MODEL, (c + 1) * D_MODEL)
        h = jnp.dot(xb, w1_ref[:, sl], preferred_element_type=F32) + b1_ref[:, sl]
        h = jnp.square(jnp.maximum(h, 0.0))
        y = jnp.dot(h.astype(BF), w2_ref[sl, :], preferred_element_type=F32)
        acc = y if acc is None else acc + y
    o_ref[...] = _layer_norm(DN_ALPHA * x + acc + b2_ref[...], g_ref[...], b_ref[...])


def _ffn_call(x, w1, b1, w2, b2, g, b, tm):
    t, d = x.shape
    row = pl.BlockSpec((tm, d), lambda i: (i, 0))
    vec = lambda w: pl.BlockSpec((1, w), lambda i: (0, 0))
    return pl.pallas_call(
        _ffn_kernel, grid=(t // tm,),
        in_specs=[row, pl.BlockSpec((d, D_FF), lambda i: (0, 0)), vec(D_FF),
                  pl.BlockSpec((D_FF, d), lambda i: (0, 0)), vec(d), vec(d), vec(d)],
        out_specs=row, out_shape=jax.ShapeDtypeStruct((t, d), F32),
        compiler_params=_cparams(("parallel",)), name="ffn")(x, w1, b1, w2, b2, g, b)


def _t5_bucket(rel):
    nb = T5_BUCKETS // 2
    max_exact = nb // 2
    n = jnp.abs(rel)
    nf = jnp.maximum(n, 1).astype(jnp.float32)
    large = max_exact + (jnp.log(nf / max_exact) / math.log(T5_MAX_DIST / max_exact)
                         * (nb - max_exact)).astype(jnp.int32)
    large = jnp.minimum(large, nb - 1)
    return jnp.where(rel > 0, nb, 0) + jnp.where(n < max_exact, n, large)


def _bucket_ids(rel0s, rows, cols, valid_cols, qpos0_for_mask=None):
    r = jnp.arange(rows, dtype=I32)[:, None]
    c = jnp.arange(cols, dtype=I32)[None, :]
    ok = jnp.broadcast_to(c < valid_cols, (rows, cols))
    if qpos0_for_mask is not None:
        ok = ok & (((qpos0_for_mask - N_META + c) >> 6) <= ((qpos0_for_mask - N_META + r) >> 6))
    return jnp.stack([jnp.where(ok, _t5_bucket(I32(rel0) + c - r), -1) for rel0 in rel0s], axis=0)


def _bias_kernel(tab_ref, bk_ref, o_ref):
    bk = bk_ref[0]
    for h in range(DIFF_HEADS + DSA_HEADS):
        acc = jnp.where(bk < 0, NEG_INF, 0.0).astype(F32)
        for b in range(T5_BUCKETS):
            acc = jnp.where(bk == b, tab_ref[b, h], acc)
        o_ref[0, h] = acc


def _bias_call(rel_bias, bucket_ids):
    n, rows, cols = bucket_ids.shape
    nh = DIFF_HEADS + DSA_HEADS
    return pl.pallas_call(
        _bias_kernel, grid=(n,),
        in_specs=[pl.BlockSpec(memory_space=pltpu.SMEM),
                  pl.BlockSpec((1, rows, cols), lambda i: (i, 0, 0))],
        out_specs=pl.BlockSpec((1, nh, rows, cols), lambda i: (i, 0, 0, 0)),
        out_shape=jax.ShapeDtypeStruct((n, nh, rows, cols), F32),
        compiler_params=_cparams(("parallel",)), name="rel_bias_table")(rel_bias, bucket_ids)


def _bias_tables(geom, rel_bias):
    tq, kb = geom.tq, KEY_BLOCK
    own_cols = tq if geom.own_wide else NARROW
    own = _bias_call(rel_bias, _bucket_ids([0], tq, own_cols, tq, geom.qpos0))
    if geom.npast is None:
        assert tq == kb and kb + 1 >= T5_MAX_DIST
        assert geom.qpos0 + tq - (geom.n_meta - 1) >= T5_MAX_DIST
        past = _bias_call(rel_bias, _bucket_ids([-kb, -2 * kb], tq, kb, kb))
        meta = _bias_call(rel_bias, _bucket_ids([-geom.qpos0, -geom.qpos0 - tq], tq, NARROW, geom.n_meta))
        mown = _bias_call(rel_bias, _bucket_ids([0], geom.n_meta, NARROW, geom.n_meta, 0))
        return own, past, meta, mown
    past = _bias_call(rel_bias, _bucket_ids([N_META + j * kb - geom.qpos0 for j in range(geom.npast)], tq, kb, kb))
    return own, past, None, None


def _split_tables(tables, lo, hi):
    return [None if t is None else t[:, lo:hi] for t in tables]


def _no_past(npast):
    return isinstance(npast, int) and npast == 0


STATIC_UNROLL = 4


def _aligned(x, m):
    return x if isinstance(x, int) else pl.multiple_of(x, m)


def _past_loop(npast, body):
    if isinstance(npast, int) and npast <= STATIC_UNROLL:
        for j in range(npast):
            body(j, 0)
    else:
        lax.fori_loop(0, npast, body, 0)


def _tile_walk(geom, frames_tile, meta_tile):
    i = pl.program_id(1)

    @pl.when(i < geom.ntile)
    def _():
        q0 = _aligned(geom.row0 + i * geom.tq, 16)
        npast = i if geom.npast is None else geom.npast
        frames_tile(q0, geom.tq, geom.qpos0 + i * geom.tq, npast)

    if geom.n_meta:
        @pl.when(i == geom.ntile)
        def _():
            meta_tile(0, geom.n_meta, 0)


def _own_mask(qpos, n, cols, causal):
    r = lax.broadcasted_iota(I32, (n, 1), 0)
    c = lax.broadcasted_iota(I32, (1, cols), 1)
    if causal:
        return c < r
    return (c < n) & (((qpos - N_META + c) >> 6) <= ((qpos - N_META + r) >> 6))


def _slab_spec(arr):
    return pl.BlockSpec((1,) + arr.shape[1:], lambda b, i: (b, 0, 0))


def _cache_spec(arr, layer):
    return pl.BlockSpec((1, 1) + arr.shape[2:], lambda b, i: (layer, b, 0, 0))


def _past_spec(arr, layer):
    return _cache_spec(arr, layer) if arr.ndim == 4 else _slab_spec(arr)


def _const_spec(arr):
    nd = arr.ndim
    return pl.BlockSpec(arr.shape, lambda b, i: (0,) * nd)


def _readers(geom, new_refs, past_refs):
    def new(k, r0, n):
        return new_refs[k][0, pl.ds(r0, n), :]

    def past(k, j):
        if geom.npast is None:
            return new_refs[k][0, pl.ds(_aligned(geom.row0 + j * KEY_BLOCK, 16), KEY_BLOCK), :]
        ref, r0 = past_refs[k], _aligned(j * KEY_BLOCK, KEY_BLOCK)
        return ref[0, 0, pl.ds(r0, KEY_BLOCK), :] if len(ref.shape) == 4 else ref[0, pl.ds(r0, KEY_BLOCK), :]
    return new, past


def _fill_narrow(dst_ref, rows):
    dst_ref[...] = jnp.zeros(dst_ref.shape, dst_ref.dtype)
    dst_ref[0:rows.shape[0], :] = rows.astype(dst_ref.dtype)


def _attn_grid(geom, bsz):
    return (bsz, geom.ntile + (1 if geom.n_meta else 0))


def _maps_dot(lhs, n, w, rhs, transpose_rhs=False):
    dot = _dot_nt if transpose_rhs else _dot
    if isinstance(lhs, (list, tuple)):
        return [dot(x, rhs) for x in lhs]
    nmaps = lhs.shape[0]
    if n % LANES == 0:
        out = dot(lhs[:, 0:n, 0:w].reshape(nmaps * n, w), rhs)
        return [out[p * n:(p + 1) * n] for p in range(nmaps)]
    return [dot(lhs[p, 0:n, 0:w], rhs) for p in range(nmaps)]


class Softmax:
    NREFS = 7

    def __init__(self, refs, nmaps, n, ones_lane=None):
        self.sw, self.sn, self.mx, self.m, self.ls, self.acc, self.e = refs
        self.nmaps, self.n, self.ones_lane = nmaps, n, ones_lane
        self.mx[:, 0:n, :] = jnp.full((nmaps, n, LANES), NEG_INF, F32)
        self.ls[:, 0:n, :] = jnp.zeros((nmaps, n, LANES), F32)
        self.acc[:, 0:n, :] = jnp.zeros((nmaps, n, self.acc.shape[2]), F32)

    @staticmethod
    def scratch(geom, nmaps, vw):
        return [pltpu.VMEM((nmaps, geom.nslot, geom.tq, KEY_BLOCK), F32),
                pltpu.VMEM((nmaps, 2, geom.tq, NARROW), F32),
                pltpu.VMEM((nmaps, geom.tq, LANES), F32), pltpu.VMEM((nmaps, geom.tq, LANES), F32),
                pltpu.VMEM((nmaps, geom.tq, LANES), F32), pltpu.VMEM((nmaps, geom.tq, vw), F32),
                pltpu.VMEM((nmaps, geom.tq, KEY_BLOCK), BF)]

    def _store(self, p, slot, wide, s):
        if wide:
            self.sw[p, slot, 0:self.n, :] = s
        else:
            self.sn[p, slot, 0:self.n, :] = s

    def _load(self, p, slot, wide):
        return self.sw[p, slot, 0:self.n, :] if wide else self.sn[p, slot, 0:self.n, :]

    def scores(self, slot, wide, q_maps, k_b, scale, bias_of=None, extra=None):
        n = self.n
        d = q_maps[0].shape[-1] if isinstance(q_maps, (list, tuple)) else q_maps.shape[2]
        s_all = _maps_dot(q_maps, n, d, k_b, transpose_rhs=True)
        for p in range(self.nmaps):
            s = s_all[p]
            if scale != 1.0:
                s = s * scale
            if bias_of is not None:
                s = s + bias_of(p)
            if extra is not None:
                s = s + extra
            self._store(p, slot, wide, s)
            self.mx[p, 0:n, :] = jnp.maximum(self.mx[p, 0:n, :], _halves(s, jnp.maximum))

    def finish_max(self):
        n = self.n
        for p in range(self.nmaps):
            m = jnp.max(self.mx[p, 0:n, :], axis=1, keepdims=True)
            self.m[p, 0:n, :] = jnp.broadcast_to(m, (n, LANES))

    def values(self, slot, wide, v_b):
        n, w = self.n, (KEY_BLOCK if wide else NARROW)
        staged = n % LANES == 0
        es = []
        for p in range(self.nmaps):
            s = self._load(p, slot, wide)
            e = jnp.exp(s - _rep(self.m[p, 0:n, :], w))
            if self.ones_lane is None:
                self.ls[p, 0:n, :] += _halves(e, jnp.add)
            if staged:
                self.e[p, 0:n, 0:w] = e.astype(BF)
            else:
                es.append(e)
        for p, pv in enumerate(_maps_dot(self.e if staged else es, n, w, v_b)):
            self.acc[p, 0:n, :] += pv

    def result(self, p):
        n = self.n
        acc = self.acc[p, 0:n, :]
        if self.ones_lane is not None:
            return acc / acc[:, self.ones_lane:self.ones_lane + 1]
        return acc / jnp.sum(self.ls[p, 0:n, :], axis=1, keepdims=True)


def _mla_kernel(geom, *refs):
    n_past = 0 if geom.npast is None else 1
    (q_ref, kc_ref), refs = refs[:2], refs[2:]
    past_refs, refs = refs[:n_past], refs[n_past:]
    (wuv_ref, o_ref), refs = refs[:2], refs[2:]
    sm_refs, (qm_s, kn_s, km_s) = refs[:Softmax.NREFS], refs[Softmax.NREFS:]
    new, past = _readers(geom, (kc_ref,), past_refs)
    scale = 1.0

    def tile(q0, n, qpos, npast, with_meta, own_wide):
        sm = Softmax(sm_refs, MLA_HEADS, n, ones_lane=ONES_LANE)
        q_heads = lambda: [q_ref[0, pl.ds(q0, n), 2 * LANES * h:2 * LANES * (h + 1)] for h in range(MLA_HEADS)]
        if n % LANES == 0:
            for h, q_h in enumerate(q_heads()):
                qm_s[h, 0:n, :] = q_h
        q_src = lambda: qm_s if n % LANES == 0 else q_heads()
        own_slot = npast if own_wide else 0
        if own_wide:
            own_k = lambda: new(0, q0, n)
        else:
            _fill_narrow(kn_s, new(0, q0, n))
            own_k = lambda: kn_s[...]
        own_cols = n if own_wide else NARROW
        sm.scores(own_slot, own_wide, q_src(), own_k(), scale,
                  extra=jnp.where(_own_mask(qpos, n, own_cols, False), 0.0, NEG_INF))

        def p1(j, carry):
            sm.scores(j, True, q_src(), past(0, j), scale)
            return carry
        _past_loop(npast, p1)
        if with_meta:
            _fill_narrow(km_s, new(0, 0, geom.n_meta))
            pad = jnp.where(lax.broadcasted_iota(I32, (1, NARROW), 1) < geom.n_meta, 0.0, NEG_INF)
            sm.scores(1, False, q_src(), km_s[...], scale, extra=pad)
        sm.finish_max()

        sm.values(own_slot, own_wide, own_k())

        def p2(j, carry):
            sm.values(j, True, past(0, j))
            return carry
        _past_loop(npast, p2)
        if with_meta:
            sm.values(1, False, km_s[...])

        out = None
        ohead = _lane_group((n, BR_W), MLA_V)
        for h in range(MLA_HEADS):
            o_h = jnp.where(ohead == h, _dot(sm.result(h), wuv_ref[...]), 0.0)
            out = o_h if out is None else out + o_h
        o_ref[0, pl.ds(q0, n), :] = out

    _tile_walk(geom,
               lambda q0, n, qpos, npast: tile(q0, n, qpos, npast, bool(geom.n_meta), geom.own_wide),
               lambda q0, n, qpos: tile(q0, n, qpos, 0, False, False))


def _mla_call(geom, layer, qcat, kcat, past, wuv_pad):
    bsz = qcat.shape[0]
    ins = [qcat, kcat] + list(past) + [wuv_pad]
    specs = ([_slab_spec(a) for a in (qcat, kcat)] + [_past_spec(a, layer) for a in past] + [_const_spec(wuv_pad)])
    return pl.pallas_call(
        functools.partial(_mla_kernel, geom), grid=_attn_grid(geom, bsz),
        in_specs=specs, out_specs=pl.BlockSpec((1, geom.lq, BR_W), lambda b, i: (b, 0, 0)),
        out_shape=jax.ShapeDtypeStruct((bsz, geom.lq, BR_W), F32),
        scratch_shapes=(Softmax.scratch(geom, MLA_HEADS, 2 * LANES)
                        + [pltpu.VMEM((MLA_HEADS, geom.tq, 2 * LANES), BF)] + [pltpu.VMEM((NARROW, 2 * LANES), BF)] * 2),
        compiler_params=_cparams(("parallel", "arbitrary")), name="mla_attn")(*ins)


def _suffix_matrix(n):
    later = lax.broadcasted_iota(I32, (n, n), 0) > lax.broadcasted_iota(I32, (n, n), 1)
    return jnp.where(later, 1.0, 0.0).astype(BF)


def _sb_kernel(geom, *refs):
    n_past = 0 if geom.npast is None else 2
    (q_ref, k_ref, v_ref), refs = refs[:3], refs[3:]
    past_refs, (o_ref, acc_s, carry_s, qm_s, hl_s, lb_s, a_s, tri_s, kn_s, vn_s, km_s, vm_s) = \
        refs[:n_past], refs[n_past:]
    new, past = _readers(geom, (k_ref, v_ref), past_refs)

    def tile(q0, n, qpos, npast, with_meta, own_wide):
        q = q_ref[0, pl.ds(q0, n), :]
        head = _lane_group(q.shape, SB_DH)
        for h in range(SB_HEADS):
            qm_s[h, 0:n, :] = jnp.where(head == h, q, jnp.zeros_like(q))
        acc_s[:, 0:n, :] = jnp.zeros((SB_HEADS, n, BR_W), F32)
        carry_s[:, 0:n, :] = jnp.zeros((SB_HEADS, n, LANES), F32)

        def block(k_b, v_b, mask, tri):
            w = k_b.shape[0]
            staged = n % LANES == 0
            z_all = _maps_dot(qm_s, n, BR_W, k_b, transpose_rhs=True)
            halves, log_bs, firsts = [], [], []
            for h in range(SB_HEADS):
                z = z_all[h]
                soft = jnp.log(1.0 + jnp.exp(-jnp.abs(z)))
                log_b = jnp.minimum(z, 0.0) - soft
                log_1m = log_b - z
                if mask is not None:
                    log_1m = jnp.where(mask, log_1m, 0.0)
                firsts.append(log_1m[:, 0:1])
                hi = log_1m.astype(BF)
                lo = (log_1m - hi.astype(F32)).astype(BF)
                if staged:
                    hl_s[2 * h, 0:n, 0:w], hl_s[2 * h + 1, 0:n, 0:w], lb_s[h, 0:n, 0:w] = hi, lo, log_b
                else:
                    halves += [hi, lo]
                    log_bs.append(log_b)
            sums_all = _maps_dot(hl_s if staged else halves, n, w, tri)
            weights = []
            for h in range(SB_HEADS):
                later = sums_all[2 * h] + sums_all[2 * h + 1]
                carry = carry_s[h, 0:n, :]
                log_b = lb_s[h, 0:n, 0:w] if staged else log_bs[h]
                a = jnp.exp(log_b + later + _rep(carry, w))
                if mask is not None:
                    a = jnp.where(mask, a, 0.0)
                if staged:
                    a_s[h, 0:n, 0:w] = a.astype(BF)
                else:
                    weights.append(a)
                carry_s[h, 0:n, :] = carry + jnp.broadcast_to(later[:, 0:1] + firsts[h], (n, LANES))
            for h, pv in enumerate(_maps_dot(a_s if staged else weights, n, w, v_b)):
                acc_s[h, 0:n, :] += pv

        if own_wide:
            block(new(0, q0, n), new(1, q0, n), _own_mask(qpos, n, n, True), _suffix_matrix(n))
        else:
            _fill_narrow(kn_s, new(0, q0, n))
            _fill_narrow(vn_s, new(1, q0, n))
            block(kn_s[...], vn_s[...], _own_mask(qpos, n, NARROW, True), _suffix_matrix(NARROW))
        if _no_past(npast):
            tri_kb = None
        elif n % LANES == 0:
            tri_s[...] = _suffix_matrix(KEY_BLOCK)
            tri_kb = lambda: tri_s[...]
        else:
            tri_value = _suffix_matrix(KEY_BLOCK)
            tri_kb = lambda: tri_value

        def past_block(jj, carry):
            j = npast - 1 - jj
            block(past(0, j).astype(BF), past(1, j).astype(BF), None, tri_kb())
            return carry
        _past_loop(npast, past_block)
        if with_meta:
            m0 = 0
            _fill_narrow(km_s, new(0, m0, geom.n_meta))
            _fill_narrow(vm_s, new(1, m0, geom.n_meta))
            pad = jnp.broadcast_to(lax.broadcasted_iota(I32, (1, NARROW), 1) < geom.n_meta, (n, NARROW))
            block(km_s[...], vm_s[...], pad, _suffix_matrix(NARROW))

        out = None
        for h in range(SB_HEADS):
            o_h = jnp.where(head == h, acc_s[h, 0:n, :], 0.0)
            out = o_h if out is None else out + o_h
        o_ref[0, pl.ds(q0, n), :] = out

    _tile_walk(geom,
               lambda q0, n, qpos, npast: tile(q0, n, qpos, npast, bool(geom.n_meta), geom.own_wide),
               lambda q0, n, qpos: tile(q0, n, qpos, 0, False, False))


def _sb_call(geom, layer, q, k, v, past):
    bsz = q.shape[0]
    ins = [q, k, v] + list(past)
    specs = [_slab_spec(a) for a in (q, k, v)] + [_past_spec(a, layer) for a in past]
    return pl.pallas_call(
        functools.partial(_sb_kernel, geom), grid=_attn_grid(geom, bsz),
        in_specs=specs, out_specs=pl.BlockSpec((1, geom.lq, BR_W), lambda b, i: (b, 0, 0)),
        out_shape=jax.ShapeDtypeStruct((bsz, geom.lq, BR_W), F32),
        scratch_shapes=[pltpu.VMEM((SB_HEADS, geom.tq, BR_W), F32), pltpu.VMEM((SB_HEADS, geom.tq, LANES), F32),
                        pltpu.VMEM((SB_HEADS, geom.tq, BR_W), BF),
                        pltpu.VMEM((2 * SB_HEADS, geom.tq, KEY_BLOCK), BF), pltpu.VMEM((SB_HEADS, geom.tq, KEY_BLOCK), F32),
                        pltpu.VMEM((SB_HEADS, geom.tq, KEY_BLOCK), BF), pltpu.VMEM((KEY_BLOCK, KEY_BLOCK), BF)]
                       + [pltpu.VMEM((NARROW, BR_W), BF)] * 4,
        compiler_params=_cparams(("parallel", "arbitrary")), name="sb_attn")(*ins)


def _bias_readers(geom, own_ref, past_ref, meta_ref, mown_ref, tile_idx):
    def own(h, n):
        return own_ref[0, h, 0:n, :]

    def earlier(j, h):
        slot = j if geom.npast is not None else jnp.where(j == tile_idx - 1, 0, 1)
        return past_ref[slot, h]

    def meta(h):
        return meta_ref[jnp.minimum(tile_idx, 1), h]

    def meta_own(h):
        return mown_ref[0, h]
    return own, earlier, meta, meta_own


def _diff_kernel(geom, lam_init, *refs):
    n_past = 0 if geom.npast is None else 2
    n_bias = 4 if geom.n_meta else 2
    (q_ref, k_ref, v_ref), refs = refs[:3], refs[3:]
    past_refs, refs = refs[:n_past], refs[n_past:]
    bias_refs, refs = list(refs[:n_bias]) + [None] * (4 - n_bias), refs[n_bias:]
    (lam_ref, sg_ref, o_ref), refs = refs[:3], refs[3:]
    sm_refs, (qm_s, kn_s, vn_s, km_s, vm_s) = refs[:Softmax.NREFS], refs[Softmax.NREFS:]
    new, past = _readers(geom, (k_ref, v_ref), past_refs)
    b_own, b_earlier, b_meta, b_meta_own = _bias_readers(geom, *bias_refs, pl.program_id(1))
    scale = 1.0
    npair = 2 * DIFF_HEADS

    def tile(q0, n, qpos, npast, with_meta, own_wide, is_meta):
        sm = Softmax(sm_refs, npair, n)
        q = q_ref[0, pl.ds(q0, n), :]
        pair = _lane_group(q.shape, DIFF_DQK)
        for p in range(npair):
            qm_s[p, 0:n, :] = jnp.where(pair == p, q, jnp.zeros_like(q))
        q_of = qm_s
        own_slot = npast if own_wide else 0
        if own_wide:
            own_k, own_v = (lambda: new(0, q0, n)), (lambda: new(1, q0, n))
        else:
            _fill_narrow(kn_s, new(0, q0, n))
            _fill_narrow(vn_s, new(1, q0, n))
            own_k, own_v = (lambda: kn_s[...]), (lambda: vn_s[...])
        sm.scores(own_slot, own_wide, q_of, own_k(), scale,
                  bias_of=(lambda p: b_meta_own(p // 2)) if is_meta else (lambda p: b_own(p // 2, n)))

        def p1(j, carry):
            sm.scores(j, True, q_of, past(0, j), scale,
                      bias_of=lambda p: b_earlier(j, p // 2))
            return carry
        _past_loop(npast, p1)
        if with_meta:
            m0 = 0
            _fill_narrow(km_s, new(0, m0, geom.n_meta))
            _fill_narrow(vm_s, new(1, m0, geom.n_meta))
            sm.scores(1, False, q_of, km_s[...], scale, bias_of=lambda p: b_meta(p // 2))
        sm.finish_max()

        sm.values(own_slot, own_wide, own_v().astype(BF))

        def p2(j, carry):
            sm.values(j, True, past(1, j).astype(BF))
            return carry
        _past_loop(npast, p2)
        if with_meta:
            sm.values(1, False, vm_s[...])

        lp = lam_ref[...]
        lam = (jnp.exp(jnp.sum(lp[0:1] * lp[1:2], axis=1, keepdims=True))
               - jnp.exp(jnp.sum(lp[2:3] * lp[3:4], axis=1, keepdims=True)) + lam_init)
        head = _lane_group((n, BR_W), DIFF_DV)
        out = None
        for h in range(DIFF_HEADS):
            o_h = jnp.where(head == h, sm.result(2 * h) - lam * sm.result(2 * h + 1), 0.0)
            ms = jnp.sum(o_h * o_h, axis=1, keepdims=True) * (1.0 / DIFF_DV)
            o_h = o_h * lax.rsqrt(ms + RMS_EPS) * sg_ref[...] * (1.0 - lam_init)
            out = o_h if out is None else out + o_h
        o_ref[0, pl.ds(q0, n), :] = out

    _tile_walk(geom,
               lambda q0, n, qpos, npast: tile(q0, n, qpos, npast, bool(geom.n_meta), geom.own_wide, False),
               lambda q0, n, qpos: tile(q0, n, qpos, 0, False, False, True))


def _diff_call(geom, layer, q, k, v, past, tables, lam_p, sg_tiled):
    bsz = q.shape[0]
    lam_init = 0.8 - 0.6 * math.exp(-0.3 * layer)
    tables = [t for t in tables if t is not None]
    ins = [q, k, v] + list(past) + tables + [lam_p, sg_tiled]
    specs = ([_slab_spec(a) for a in (q, k, v)] + [_past_spec(a, layer) for a in past]
             + [_const_spec(a) for a in tables + [lam_p, sg_tiled]])
    npair = 2 * DIFF_HEADS
    return pl.pallas_call(
        functools.partial(_diff_kernel, geom, lam_init), grid=_attn_grid(geom, bsz),
        in_specs=specs, out_specs=pl.BlockSpec((1, geom.lq, BR_W), lambda b, i: (b, 0, 0)),
        out_shape=jax.ShapeDtypeStruct((bsz, geom.lq, BR_W), F32),
        scratch_shapes=(Softmax.scratch(geom, npair, BR_W) + [pltpu.VMEM((npair, geom.tq, BR_W), BF)]
                        + [pltpu.VMEM((NARROW, BR_W), BF)] * 4),
        compiler_params=_cparams(("parallel", "arbitrary")), name="diff_attn")(*ins)


def _sortable(x):
    b = lax.bitcast_convert_type(x + 0.0, I32)
    return b ^ ((b >> 31) & I32(0x7FFFFFFF))


def _dsa_kernel(geom, *refs):
    n_past = 0 if geom.npast is None else 3
    n_bias = 4 if geom.n_meta else 2
    (q_ref, qi_ref, wi_ref, k_ref, v_ref, kit_ref), refs = refs[:6], refs[6:]
    past_refs, refs = refs[:n_past], refs[n_past:]
    bias_refs, refs = list(refs[:n_bias]) + [None] * (4 - n_bias), refs[n_bias:]
    o_ref, refs = refs[0], refs[1:]
    sm_refs, (qm_s, qim_s, wib_s, kw_s, kn_s, last_s,
              kno_s, vno_s, kio_s, knm_s, vnm_s, kim_s) = refs[:Softmax.NREFS], refs[Softmax.NREFS:]
    new, past = _readers(geom, (k_ref, v_ref, kit_ref), past_refs)
    b_own, b_earlier, b_meta, b_meta_own = _bias_readers(geom, *bias_refs, pl.program_id(1))
    topk = float(geom.topk)
    past_pos0 = N_META
    index_bits = max(1, int(geom.lq + geom.lp + N_META).bit_length())
    ones_count = jnp.ones((LANES, LANES), BF)

    def tile(q0, n, qpos, npast, with_meta, own_wide, is_meta):
        nm = geom.n_meta
        m0 = 0
        own_slot = npast if own_wide else 0
        own_cols = n if own_wide else NARROW

        qi = qi_ref[0, pl.ds(q0, n), :]
        igrp = _lane_group(qi.shape, IDX_DIM)
        wi = wi_ref[0, pl.ds(q0, n), :] * (IDX_HEADS ** -0.5 * IDX_DIM ** -0.5)
        for h in range(IDX_HEADS):
            qim_s[h, 0:n, :] = jnp.where(igrp == h, qi, jnp.zeros_like(qi))
            wib_s[h, 0:n, :] = jnp.broadcast_to(wi[:, h:h + 1], (n, LANES))
        q = q_ref[0, pl.ds(q0, n), :]
        head = _lane_group(q.shape, DSA_DH)
        for h in range(DSA_HEADS):
            qm_s[h, 0:n, :] = jnp.where(head == h, q, jnp.zeros_like(q))
        if own_wide:
            own_k, own_v, own_ki = (lambda: new(0, q0, n)), (lambda: new(1, q0, n)), (lambda: new(2, q0, n))
        else:
            _fill_narrow(kno_s, new(0, q0, n))
            _fill_narrow(vno_s, new(1, q0, n))
            _fill_narrow(kio_s, new(2, q0, n))
            own_k, own_v, own_ki = (lambda: kno_s[...]), (lambda: vno_s[...]), (lambda: kio_s[...])
        if with_meta:
            _fill_narrow(knm_s, new(0, m0, nm))
            _fill_narrow(vnm_s, new(1, m0, nm))
            _fill_narrow(kim_s, new(2, m0, nm))

        def index_keys(kit_b):
            raw = _maps_dot(qim_s, n, BR_W, kit_b, transpose_rhs=True)
            score = None
            for h in range(IDX_HEADS):
                t = _rep(wib_s[h, 0:n, :], kit_b.shape[0]) * jnp.maximum(raw[h], 0.0)
                score = t if score is None else score + t
            return _sortable(score)

        own_keys = jnp.where(_own_mask(qpos, n, own_cols, False), index_keys(own_ki()), I32(KEY_NEG))
        if own_wide:
            kw_s[own_slot, 0:n, :] = own_keys
        else:
            pad = lax.broadcasted_iota(I32, (1, NARROW), 1) < n
            kn_s[0, 0:n, :] = jnp.where(pad, own_keys, I32(INT_MIN))

        def score_block(j, carry):
            kw_s[j, 0:n, :] = index_keys(past(2, j).astype(BF))
            return carry
        _past_loop(npast, score_block)
        if with_meta:
            pad = lax.broadcasted_iota(I32, (1, NARROW), 1) < nm
            kn_s[1, 0:n, :] = jnp.where(pad, index_keys(kim_s[...]), I32(INT_MIN))
        nwide = npast + 1 if own_wide else npast

        def wide_sweep(init, fn):
            if isinstance(nwide, int) and nwide <= STATIC_UNROLL:
                for j in range(nwide):
                    init = fn(j, init)
                return init
            return lax.fori_loop(0, nwide, fn, init)

        def one(cond):
            return jnp.where(cond, 1.0, 0.0)

        def partial_counts(r0, nr, wide_fn, own_narrow_fn, meta_fn):
            part = wide_sweep(jnp.zeros((nr, LANES), F32), lambda j, a: a + wide_fn(kw_s[j, r0:r0 + nr, :], j))
            if not own_wide:
                part = part + own_narrow_fn(kn_s[0, r0:r0 + nr, :])
            if with_meta:
                part = part + meta_fn(kn_s[1, r0:r0 + nr, :])
            return part

        def lane_sums(part):
            return jnp.dot(part.astype(BF), ones_count, preferred_element_type=F32)

        def count(wide_fn, own_narrow_fn, meta_fn):
            return lane_sums(partial_counts(0, n, wide_fn, own_narrow_fn, meta_fn))

        def count_cmp(cmp):
            return count(lambda x, j: _halves(one(cmp(x, KEY_BLOCK)), jnp.add),
                         lambda x: one(cmp(x, NARROW)), lambda x: one(cmp(x, NARROW)))

        def ge_part(r0, nr, cand):
            return partial_counts(r0, nr, lambda x, j: _halves(one(x >= _rep(cand, KEY_BLOCK)), jnp.add),
                                  lambda x: one(x >= cand), lambda x: one(x >= cand))

        def sign_step(r0, nr):
            c0 = lane_sums(ge_part(r0, nr, jnp.zeros((nr, LANES), I32)))
            return jnp.where(c0 >= topk, I32(0), I32(INT_MIN)), jnp.where(c0 >= topk, c0, topk + 1.0)

        def accept(thr, cnt, cand, c):
            keep = c >= topk
            return jnp.where(keep, cand, thr), jnp.where(keep, c, cnt)

        def bit_step(b, state):
            thr, cnt = state
            cand = thr | jnp.left_shift(I32(1), I32(30) - b)
            return accept(thr, cnt, cand, lane_sums(ge_part(0, n, cand)))
        thr, cnt = lax.fori_loop(0, 31, bit_step, sign_step(0, n))

        last_s[0:n, :] = jnp.full((n, LANES), INT_MAX, I32)
        has_ties = jnp.max(one((cnt > topk) & (thr > I32(KEY_NEG)))) > 0.0

        @pl.when(has_ties)
        def _():
            need = topk - count_cmp(lambda x, w: x > _rep(thr, w))
            colw = lax.broadcasted_iota(I32, (1, KEY_BLOCK), 1)
            coln = lax.broadcasted_iota(I32, (1, NARROW), 1)

            def wide_pos0(j):
                return past_pos0 + j * KEY_BLOCK

            def ties_before(x):
                return count(lambda kk, j: _halves(one((kk == _rep(thr, KEY_BLOCK))
                                                       & (colw < _rep(x, KEY_BLOCK) - wide_pos0(j))), jnp.add),
                             lambda kk: one((kk == thr) & (coln < x - qpos)),
                             lambda kk: one((kk == thr) & (coln < x)))

            def pos_step(b, last_pos):
                cand = last_pos + jnp.left_shift(I32(1), I32(index_bits - 1) - b)
                return jnp.where(ties_before(cand) < need, cand, last_pos)
            last_s[0:n, :] = lax.fori_loop(0, index_bits, pos_step, jnp.zeros((n, LANES), I32))

        last_pos = last_s[0:n, :]

        def sel_bias(kk, kpos0):
            w = kk.shape[1]
            cols = lax.broadcasted_iota(I32, (1, w), 1)
            t = _rep(thr, w)
            sel = (kk > t) | ((kk == t) & (cols <= _rep(last_pos, w) - kpos0))
            return jnp.where(sel, 0.0, NEG_INF)

        sm = Softmax(sm_refs, DSA_HEADS, n)
        q_of = qm_s
        own_kk = kw_s[own_slot, 0:n, :] if own_wide else kn_s[0, 0:n, :]
        sm.scores(own_slot, own_wide, q_of, own_k(), 1.0,
                  bias_of=(lambda h: b_meta_own(h)) if is_meta else (lambda h: b_own(h, n)),
                  extra=sel_bias(own_kk, qpos))

        def p1(j, carry):
            sm.scores(j, True, q_of, past(0, j).astype(BF), 1.0,
                      bias_of=lambda h: b_earlier(j, h),
                      extra=sel_bias(kw_s[j, 0:n, :], past_pos0 + j * KEY_BLOCK))
            return carry
        _past_loop(npast, p1)
        if with_meta:
            sm.scores(1, False, q_of, knm_s[...], 1.0, bias_of=lambda h: b_meta(h),
                      extra=sel_bias(kn_s[1, 0:n, :], 0))
        sm.finish_max()

        sm.values(own_slot, own_wide, own_v().astype(BF))

        def p2(j, carry):
            sm.values(j, True, past(1, j).astype(BF))
            return carry
        _past_loop(npast, p2)
        if with_meta:
            sm.values(1, False, vnm_s[...])

        out = None
        for h in range(DSA_HEADS):
            o_h = jnp.where(head == h, sm.result(h), 0.0)
            out = o_h if out is None else out + o_h
        o_ref[0, pl.ds(q0, n), :] = out

    _tile_walk(geom,
               lambda q0, n, qpos, npast: tile(q0, n, qpos, npast, bool(geom.n_meta), geom.own_wide, False),
               lambda q0, n, qpos: tile(q0, n, qpos, 0, False, False, True))


def _dsa_call(geom, layer, q, qi, wi, k, v, kit, past, tables):
    bsz = q.shape[0]
    tables = [t for t in tables if t is not None]
    ins = [q, qi, wi, k, v, kit] + list(past) + tables
    specs = ([_slab_spec(a) for a in (q, qi, wi, k, v, kit)] + [_past_spec(a, layer) for a in past]
             + [_const_spec(a) for a in tables])
    tq = geom.tq
    return pl.pallas_call(
        functools.partial(_dsa_kernel, geom), grid=_attn_grid(geom, bsz),
        in_specs=specs, out_specs=pl.BlockSpec((1, geom.lq, BR_W), lambda b, i: (b, 0, 0)),
        out_shape=jax.ShapeDtypeStruct((bsz, geom.lq, BR_W), F32),
        scratch_shapes=(Softmax.scratch(geom, DSA_HEADS, BR_W)
                        + [pltpu.VMEM((DSA_HEADS, tq, BR_W), BF), pltpu.VMEM((IDX_HEADS, tq, BR_W), BF),
                           pltpu.VMEM((IDX_HEADS, tq, LANES), F32),
                           pltpu.VMEM((geom.nslot, tq, KEY_BLOCK), I32), pltpu.VMEM((2, tq, NARROW), I32),
                           pltpu.VMEM((tq, LANES), I32)]
                        + [pltpu.VMEM((NARROW, BR_W), BF)] * 6),
        compiler_params=_cparams(("parallel", "arbitrary")), name="dsa_attn")(*ins)


def _row_tile(t):
    for tm in (768, 384, 256, 128, 64, 32, 16, 8):
        if t % tm == 0:
            return tm
    raise ValueError(f"token count {t} has no supported row tile")


def _rope_tables(pos):
    half = MLA_ROPE // 2
    inv_freq = ROPE_THETA ** (-jnp.arange(half, dtype=jnp.float32) / half)
    ang = pos.astype(jnp.float32)[:, None] * inv_freq[None, :]
    cos, sin = jnp.cos(ang), jnp.sin(ang)
    return jnp.concatenate([cos, cos], axis=1), jnp.concatenate([-sin, sin], axis=1)


def _run_group(geom, x, pos_rows, caches, weights):
    (ln_in_g, ln_in_b, w_in, mla_qnorm_g, mla_w_uq, mla_kvnorm_g, mla_w_uk, mla_w_uv, diff_lambda,
     diff_subln_g, rel_bias, w_br, w_out, ln1_g, ln1_b, w_ff1, b_ff1, w_ff2, b_ff2, ln2_g, ln2_b) = weights
    bsz, lq, d = x.shape
    t = bsz * lq
    tm = _row_tile(t)
    cos32, sin32 = _rope_tables(pos_rows)
    cos_k = jnp.tile(cos32, (bsz, MLA_HEADS))
    sin_k = jnp.tile(sin32, (bsz, MLA_HEADS))
    tables = _bias_tables(geom, rel_bias)
    diff_tables = _split_tables(tables, 0, DIFF_HEADS)
    dsa_tables = _split_tables(tables, DIFF_HEADS, DIFF_HEADS + DSA_HEADS)

    xf = _ln_call(x.reshape(t, d), ln_in_g, ln_in_b, tm)
    rows = []
    for l in range(DEPTH):
        w_uq = mla_w_uq[l].reshape(MLA_Q_LORA, MLA_HEADS, MLA_NOPE + MLA_ROPE)
        wn = w_uq[:, :, :MLA_NOPE].reshape(MLA_Q_LORA, MLA_HEADS * MLA_NOPE).astype(BF)
        wp = w_uq[:, :, MLA_NOPE:].reshape(MLA_Q_LORA, MLA_HEADS * MLA_ROPE).astype(BF)
        wuk = mla_w_uk[l].reshape(MLA_KV_LORA, MLA_HEADS * MLA_NOPE).T.astype(BF)
        wuv = jnp.pad(mla_w_uv[l].reshape(MLA_KV_LORA, MLA_HEADS * MLA_V),
                      ((0, 2 * LANES - MLA_KV_LORA), (0, 0))).astype(BF)
        (ckv, kpe, b_k, b_v, c_k, c_v, d_k, d_v, d_ki, d_wi,
         qcat, kcat, bq_b, bk_b, bv_b, cq_b, ck_b, cv_b, dq_b, dk_b, dv_b, dqi_b, kit_b) = _proj_call(
            xf, _mix_weight(w_in[l]), mla_kvnorm_g[l].reshape(1, -1), mla_qnorm_g[l].reshape(1, -1),
            wn, wp, wuk, cos_k, sin_k, tm)
        per = lambda a: a.reshape(bsz, lq, a.shape[-1])
        if caches is None:
            past = lambda *idx: []
            mla_past, kit_past = [], []
        else:
            past = lambda *idx: [caches[i] for i in idx]
            fill = jnp.zeros(caches[0].shape[1:3] + (2 * LANES - MLA_KV_LORA - MLA_ROPE,), F32).at[..., -1].set(1.0)
            mla_past = [jnp.concatenate([caches[0][l], caches[1][l], fill], axis=-1).astype(BF)]
            kit_past = [jnp.tile(caches[8][l], (1, 1, IDX_HEADS)).astype(BF)]

        o_a = _mla_call(geom, l, per(qcat), per(kcat), mla_past, wuv)
        o_b = _sb_call(geom, l, per(bq_b), per(bk_b), per(bv_b), past(2, 3))
        o_c = _diff_call(geom, l, per(cq_b), per(ck_b), per(cv_b), past(4, 5), diff_tables, diff_lambda[l],
                         jnp.tile(diff_subln_g[l], DIFF_HEADS).reshape(1, -1))
        o_d = _dsa_call(geom, l, per(dq_b), per(dqi_b), per(d_wi), per(dk_b), per(dv_b), per(kit_b),
                        past(6, 7) + kit_past, dsa_tables)

        flat = lambda a: a.reshape(t, BR_W)
        wg = w_in[l][:, IN_OFFS[15]:IN_OFFS[16]].astype(BF)
        x1 = _merge_call(xf, flat(o_a), flat(o_b), flat(o_c), flat(o_d), wg, w_br[l].astype(BF),
                         w_out[l].astype(BF), ln1_g[l].reshape(1, -1), ln1_b[l].reshape(1, -1), tm)
        xf = _ffn_call(x1, w_ff1[l].astype(BF), b_ff1[l].reshape(1, -1), w_ff2[l].astype(BF),
                       b_ff2[l].reshape(1, -1), ln2_g[l].reshape(1, -1), ln2_b[l].reshape(1, -1), tm)
        rows.append([per(a) for a in (ckv, kpe, b_k, b_v, c_k, c_v, d_k, d_v, d_ki)])
    return xf.reshape(bsz, lq, d), rows


_ROW_TRAILING = ((MLA_KV_LORA,), (MLA_ROPE,), (SB_HEADS, SB_DH), (SB_HEADS, SB_DH),
                 (DIFF_HEADS, 2, DIFF_DQK), (DIFF_HEADS, DIFF_DV), (DSA_HEADS, DSA_DH),
                 (DSA_HEADS, DSA_DH), (IDX_DIM,))


def kernel(x_prompt, x_sample, cache_mla_kv, cache_mla_pe, cache_sb_k, cache_sb_v, cache_diff_k, cache_diff_v, cache_dsa_k, cache_dsa_v, cache_dsa_kidx, meta, ln_in_g, ln_in_b, w_in, mla_qnorm_g, mla_w_uq, mla_kvnorm_g, mla_w_uk, mla_w_uv, diff_lambda, diff_subln_g, rel_bias, w_br, w_out, ln1_g, ln1_b, w_ff1, b_ff1, w_ff2, b_ff2, ln2_g, ln2_b):
    weights = (ln_in_g, ln_in_b, w_in, mla_qnorm_g, mla_w_uq, mla_kvnorm_g, mla_w_uk, mla_w_uv, diff_lambda,
               diff_subln_g, rel_bias, w_br, w_out, ln1_g, ln1_b, w_ff1, b_ff1, w_ff2, b_ff2, ln2_g, ln2_b)
    assert w_in.shape[0] == DEPTH and x_prompt.shape[2] == D_MODEL

    bsz_p, seq_p, _ = x_prompt.shape
    assert seq_p % KEY_BLOCK == 0
    meta_b = jnp.broadcast_to(meta[None].astype(x_prompt.dtype), (bsz_p, N_META, D_MODEL))
    xp = jnp.concatenate([meta_b, x_prompt], axis=1)
    pos_p = jnp.arange(N_META + seq_p, dtype=I32)
    geom_p = Geom(tq=KEY_BLOCK, ntile=seq_p // KEY_BLOCK, npast=None, n_meta=N_META, qpos0=N_META,
                  row0=N_META, lq=seq_p + N_META, lp=0, topk=min(DSA_TOPK, seq_p // 4))
    yp, rows_p = _run_group(geom_p, xp, pos_p, None, weights)
    y_prompt = yp[:, N_META:]
    p_rows = []
    for i, trailing in enumerate(_ROW_TRAILING):
        stacked = jnp.stack([r[i] for r in rows_p], axis=0)
        p_rows.append(stacked.reshape(stacked.shape[:3] + trailing))

    past_len = cache_mla_kv.shape[2]
    bsz_s, dec_seq, _ = x_sample.shape
    assert past_len % KEY_BLOCK == 0 and dec_seq % 16 == 0 and dec_seq <= NARROW
    assert past_len % CHUNK == 0 and dec_seq <= CHUNK, "new frames must share one chunk"
    caches = [c.reshape(c.shape[:3] + (-1,)) for c in
              (cache_mla_kv, cache_mla_pe, cache_sb_k, cache_sb_v, cache_diff_k, cache_diff_v,
               cache_dsa_k, cache_dsa_v, cache_dsa_kidx)]
    pos_s = N_META + past_len + jnp.arange(dec_seq, dtype=I32)
    geom_s = Geom(tq=dec_seq, ntile=1, npast=past_len // KEY_BLOCK, n_meta=0, qpos0=N_META + past_len,
                  row0=0, lq=dec_seq, lp=past_len, topk=min(DSA_TOPK, (past_len + dec_seq) // 4))
    y_sample, rows_s = _run_group(geom_s, x_sample, pos_s, caches, weights)
    s_rows = []
    for i, trailing in enumerate(_ROW_TRAILING):
        stacked = jnp.stack([r[i] for r in rows_s], axis=0)
        s_rows.append(stacked.reshape(stacked.shape[:3] + trailing))

    return (y_prompt, y_sample, *p_rows, *s_rows)
```

```python
import functools
import math
from typing import NamedTuple, Optional

import jax
import jax.numpy as jnp
import numpy as np
from jax import lax
from jax.experimental import pallas as pl
from jax.experimental.pallas import tpu as pltpu

D_MODEL = 1024
CHUNK = 64
N_META = 16
MLA_HEADS = 4
MLA_Q_LORA = 256
MLA_KV_LORA = 128
MLA_NOPE = 64
MLA_ROPE = 32
MLA_V = 64
ROPE_THETA = 10000.0
SB_HEADS = 4
SB_DH = 64
DIFF_HEADS = 4
DIFF_DQK = 32
DIFF_DV = 64
DSA_HEADS = 4
DSA_DH = 64
IDX_HEADS = 8
IDX_DIM = 32
DSA_TOPK = 256
N_BRANCH = 4
BR_W = 256
D_FF = 4 * D_MODEL
T5_BUCKETS = 32
T5_MAX_DIST = 128
LN_EPS = 1e-5
RMS_EPS = 1e-6
NEG_INF = -1e30
DEPTH = 2
DN_ALPHA = (2 * DEPTH) ** 0.25
IN_SIZES = (MLA_Q_LORA, MLA_KV_LORA, MLA_ROPE,
            SB_HEADS * SB_DH, SB_HEADS * SB_DH, SB_HEADS * SB_DH,
            DIFF_HEADS * 2 * DIFF_DQK, DIFF_HEADS * 2 * DIFF_DQK, DIFF_HEADS * DIFF_DV,
            DSA_HEADS * DSA_DH, DSA_HEADS * DSA_DH, DSA_HEADS * DSA_DH,
            IDX_HEADS * IDX_DIM, IDX_DIM, IDX_HEADS,
            N_BRANCH * D_MODEL)
IN_OFFS = tuple(int(s) for s in np.cumsum((0,) + IN_SIZES))

LANES = 128
KEY_BLOCK = 256
NARROW = LANES
VMEM_LIMIT_MB = 56

BF = jnp.bfloat16
F32 = jnp.float32
I32 = jnp.int32

_NEG_BITS = int(np.float32(NEG_INF).view(np.int32))
KEY_NEG = _NEG_BITS ^ ((_NEG_BITS >> 31) & 0x7FFFFFFF)
INT_MIN = -(2 ** 31)
INT_MAX = 2 ** 31 - 1


def _cparams(sem):
    return pltpu.CompilerParams(dimension_semantics=sem, vmem_limit_bytes=VMEM_LIMIT_MB * 1024 * 1024)


def _dot(a, b):
    return jnp.dot(a.astype(BF), b.astype(BF), preferred_element_type=F32)


def _dot_nt(a, b):
    return lax.dot_general(a.astype(BF), b.astype(BF), (((1,), (1,)), ((), ())),
                           preferred_element_type=F32)


def _layer_norm(x, g, b):
    mu = jnp.mean(x, axis=-1, keepdims=True)
    xc = x - mu
    var = jnp.mean(xc * xc, axis=-1, keepdims=True)
    return xc * lax.rsqrt(var + LN_EPS) * g + b


def _rms_norm(x, g):
    return x * lax.rsqrt(jnp.mean(x * x, axis=-1, keepdims=True) + RMS_EPS) * g


def _rope_lanes(x, cos, sin):
    lane = lax.broadcasted_iota(I32, x.shape, 1)
    swapped = jnp.where((lane & 31) < 16, pltpu.roll(x, LANES - 16, 1), pltpu.roll(x, 16, 1))
    return x * cos + swapped * sin


def _lane_group(shape, width):
    return lax.broadcasted_iota(I32, shape, 1) // width


def _halves(x, op):
    return op(x[:, :LANES], x[:, LANES:]) if x.shape[1] == 2 * LANES else x


def _rep(x, width):
    return jnp.concatenate([x, x], axis=1) if width == 2 * LANES else x


class Geom(NamedTuple):
    tq: int
    ntile: int
    npast: Optional[int]
    n_meta: int
    qpos0: int
    row0: int
    lq: int
    lp: int
    topk: int

    @property
    def own_wide(self):
        return self.tq == KEY_BLOCK

    @property
    def nslot(self):
        return max(self.ntile if self.npast is None else self.npast, 1)


def _ln_kernel(x_ref, g_ref, b_ref, o_ref):
    o_ref[...] = _layer_norm(x_ref[...], g_ref[...], b_ref[...])


def _ln_call(x, g, b, tm):
    t, d = x.shape
    return pl.pallas_call(
        _ln_kernel, grid=(t // tm,),
        in_specs=[pl.BlockSpec((tm, d), lambda i: (i, 0)),
                  pl.BlockSpec((1, d), lambda i: (0, 0)),
                  pl.BlockSpec((1, d), lambda i: (0, 0))],
        out_specs=pl.BlockSpec((tm, d), lambda i: (i, 0)),
        out_shape=jax.ShapeDtypeStruct((t, d), F32),
        compiler_params=_cparams(("parallel",)), name="ln_in")(x, g.reshape(1, d), b.reshape(1, d))


_WIDE = (3, 4, 5, 6, 7, 8, 9, 10, 11, 12)
_WIDE_F32 = (1, 2, 4, 5, 7, 8)
_MIX_COLS = 256 + 128 + 256 * len(_WIDE) + LANES + 256 + LANES
_Q_SCALE = {0: SB_DH ** -0.5, 3: DIFF_DQK ** -0.5, 6: DSA_DH ** -0.5}
MLA_SCALE = (MLA_NOPE + MLA_ROPE) ** -0.5
ONES_LANE = 2 * LANES - 1


def _mix_weight(w_in_l):
    def seg(i, pad_to=None):
        w = w_in_l[:, IN_OFFS[i]:IN_OFFS[i + 1]]
        if pad_to is not None:
            w = jnp.pad(w, ((0, 0), (0, pad_to - w.shape[1])))
        return w
    cols = ([seg(0), seg(1)] + [seg(i) for i in _WIDE]
            + [seg(2, LANES), jnp.tile(seg(13), (1, IDX_HEADS)), seg(14, LANES)])
    return jnp.concatenate(cols, axis=1).astype(BF)


def _proj_kernel(x_ref, w_ref, kvg_ref, qg_ref, wn_ref, wp_ref, wuk_ref, cos_ref, sin_ref, *outs):
    (ckv_ref, kpe_ref, bk_ref, bv_ref, ck_ref, cv_ref, dk_ref, dv_ref, dki_ref, dwi_ref,
     qcat_ref, kcat_ref, *wide_bf) = outs
    kit_ref = wide_bf[-1]
    wide_bf = wide_bf[:-1]
    wide_f32 = dict(zip(_WIDE_F32, (bk_ref, bv_ref, ck_ref, cv_ref, dk_ref, dv_ref)))
    xb = x_ref[...].astype(BF)
    cos, sin = cos_ref[...], sin_ref[...]

    def seg(off, width):
        return jnp.dot(xb, w_ref[:, off:off + width], preferred_element_type=F32)

    qn = _rms_norm(seg(0, 256), qg_ref[...])
    nope = _dot(qn, wn_ref[...])
    pe = _rope_lanes(_dot(qn, wp_ref[...]), cos, sin)
    head = _lane_group(nope.shape, MLA_NOPE)
    lane = lax.broadcasted_iota(I32, pe.shape, 1)
    for h in range(MLA_HEADS):
        q_lat = _dot(jnp.where(head == h, nope, 0.0), wuk_ref[...]) * MLA_SCALE
        pe_h = (pe if h == 0 else pltpu.roll(pe, LANES - h * MLA_ROPE, 1)) * MLA_SCALE
        qcat_ref[:, 2 * LANES * h:2 * LANES * h + LANES] = q_lat.astype(BF)
        qcat_ref[:, 2 * LANES * h + LANES:2 * LANES * (h + 1)] = jnp.where(lane < MLA_ROPE, pe_h, 0.0).astype(BF)

    ckv = _rms_norm(seg(256, 128), kvg_ref[...])
    ckv_ref[...] = ckv
    off = 384
    for n, r in enumerate(wide_bf):
        y = seg(off, 256)
        if n in wide_f32:
            wide_f32[n][...] = y
        r[...] = (y * _Q_SCALE[n]).astype(BF) if n in _Q_SCALE else y.astype(BF)
        off += 256
    kpe = _rope_lanes(seg(off, LANES), cos, sin)
    kpe_ref[...] = kpe[:, :MLA_ROPE]
    kcat = jnp.concatenate([ckv, kpe], axis=1)
    kcat_ref[...] = jnp.where(lax.broadcasted_iota(I32, kcat.shape, 1) == ONES_LANE, 1.0, kcat).astype(BF)
    kit = seg(off + LANES, 256)
    dki_ref[...] = kit[:, :IDX_DIM]
    kit_ref[...] = kit.astype(BF)
    dwi_ref[...] = seg(off + LANES + 256, LANES)[:, :IDX_HEADS]


def _proj_call(x, w_mix, kvg, qg, wn, wp, wuk, cos_k, sin_k, tm):
    t, d = x.shape
    f32_w = [128, MLA_ROPE, 256, 256, 256, 256, 256, 256, IDX_DIM, IDX_HEADS]
    bf_w = [4 * 256, 256] + [256] * len(_WIDE) + [256]
    row = lambda w: pl.BlockSpec((tm, w), lambda i: (i, 0))
    const = lambda a: pl.BlockSpec(a.shape, lambda i: (0, 0))
    return pl.pallas_call(
        _proj_kernel, grid=(t // tm,),
        in_specs=[row(d), const(w_mix), const(kvg), const(qg), const(wn), const(wp), const(wuk),
                  row(LANES), row(LANES)],
        out_specs=[row(w) for w in f32_w + bf_w],
        out_shape=([jax.ShapeDtypeStruct((t, w), F32) for w in f32_w]
                   + [jax.ShapeDtypeStruct((t, w), BF) for w in bf_w]),
        compiler_params=_cparams(("parallel",)), name="mix_proj")(x, w_mix, kvg, qg, wn, wp, wuk, cos_k, sin_k)


def _merge_kernel(x_ref, oa_ref, ob_ref, oc_ref, od_ref, wg_ref, wbr_ref, wout_ref, g_ref, b_ref, o_ref):
    x = x_ref[...]
    xb = x.astype(BF)
    acc = None
    for n, o_n in enumerate((oa_ref, ob_ref, oc_ref, od_ref)):
        gate = jax.nn.sigmoid(jnp.dot(xb, wg_ref[:, n * D_MODEL:(n + 1) * D_MODEL],
                                      preferred_element_type=F32))
        br = jnp.dot(o_n[...].astype(BF), wbr_ref[n], preferred_element_type=F32)
        acc = gate * br if acc is None else acc + gate * br
    mix = jnp.dot(acc.astype(BF), wout_ref[...], preferred_element_type=F32)
    o_ref[...] = _layer_norm(DN_ALPHA * x + mix, g_ref[...], b_ref[...])


def _merge_call(x, o_a, o_b, o_c, o_d, wg, wbr, wout, g, b, tm):
    t, d = x.shape
    row = lambda w: pl.BlockSpec((tm, w), lambda i: (i, 0))
    return pl.pallas_call(
        _merge_kernel, grid=(t // tm,),
        in_specs=[row(d), row(BR_W), row(BR_W), row(BR_W), row(BR_W),
                  pl.BlockSpec((d, N_BRANCH * d), lambda i: (0, 0)),
                  pl.BlockSpec((N_BRANCH, BR_W, d), lambda i: (0, 0, 0)),
                  pl.BlockSpec((d, d), lambda i: (0, 0)),
                  pl.BlockSpec((1, d), lambda i: (0, 0)), pl.BlockSpec((1, d), lambda i: (0, 0))],
        out_specs=row(d), out_shape=jax.ShapeDtypeStruct((t, d), F32),
        compiler_params=_cparams(("parallel",)), name="merge")(x, o_a, o_b, o_c, o_d, wg, wbr, wout, g, b)


def _ffn_kernel(x_ref, w1_ref, b1_ref, w2_ref, b2_ref, g_ref, b_ref, o_ref):
    x = x_ref[...]
    xb = x.astype(BF)
    acc = None
    for c in range(D_FF // D_MODEL):
        sl = slice(c * D_MODEL, (c + 1) * D_MODEL)
        h = jnp.dot(xb, w1_ref[:, sl], preferred_element_type=F32) + b1_ref[:, sl]
        h = jnp.square(jnp.maximum(h, 0.0))
        y = jnp.dot(h.astype(BF), w2_ref[sl, :], preferred_element_type=F32)
        acc = y if acc is None else acc + y
    o_ref[...] = _layer_norm(DN_ALPHA * x + acc + b2_ref[...], g_ref[...], b_ref[...])


def _ffn_call(x, w1, b1, w2, b2, g, b, tm):
    t, d = x.shape
    row = pl.BlockSpec((tm, d), lambda i: (i, 0))
    vec = lambda w: pl.BlockSpec((1, w), lambda i: (0, 0))
    return pl.pallas_call(
        _ffn_kernel, grid=(t // tm,),
        in_specs=[row, pl.BlockSpec((d, D_FF), lambda i: (0, 0)), vec(D_FF),
                  pl.BlockSpec((D_FF, d), lambda i: (0, 0)), vec(d), vec(d), vec(d)],
        out_specs=row, out_shape=jax.ShapeDtypeStruct((t, d), F32),
        compiler_params=_cparams(("parallel",)), name="ffn")(x, w1, b1, w2, b2, g, b)


def _t5_bucket(rel):
    nb = T5_BUCKETS // 2
    max_exact = nb // 2
    n = jnp.abs(rel)
    nf = jnp.maximum(n, 1).astype(jnp.float32)
    large = max_exact + (jnp.log(nf / max_exact) / math.log(T5_MAX_DIST / max_exact)
                         * (nb - max_exact)).astype(jnp.int32)
    large = jnp.minimum(large, nb - 1)
    return jnp.where(rel > 0, nb, 0) + jnp.where(n < max_exact, n, large)


def _bucket_ids(rel0s, rows, cols, valid_cols, qpos0_for_mask=None):
    r = jnp.arange(rows, dtype=I32)[:, None]
    c = jnp.arange(cols, dtype=I32)[None, :]
    ok = jnp.broadcast_to(c < valid_cols, (rows, cols))
    if qpos0_for_mask is not None:
        ok = ok & (((qpos0_for_mask - N_META + c) >> 6) <= ((qpos0_for_mask - N_META + r) >> 6))
    return jnp.stack([jnp.where(ok, _t5_bucket(I32(rel0) + c - r), -1) for rel0 in rel0s], axis=0)


def _bias_kernel(tab_ref, bk_ref, o_ref):
    bk = bk_ref[0]
    for h in range(DIFF_HEADS + DSA_HEADS):
        acc = jnp.where(bk < 0, NEG_INF, 0.0).astype(F32)
        for b in range(T5_BUCKETS):
            acc = jnp.where(bk == b, tab_ref[b, h], acc)
        o_ref[0, h] = acc


def _bias_call(rel_bias, bucket_ids):
    n, rows, cols = bucket_ids.shape
    nh = DIFF_HEADS + DSA_HEADS
    return pl.pallas_call(
        _bias_kernel, grid=(n,),
        in_specs=[pl.BlockSpec(memory_space=pltpu.SMEM),
                  pl.BlockSpec((1, rows, cols), lambda i: (i, 0, 0))],
        out_specs=pl.BlockSpec((1, nh, rows, cols), lambda i: (i, 0, 0, 0)),
        out_shape=jax.ShapeDtypeStruct((n, nh, rows, cols), F32),
        compiler_params=_cparams(("parallel",)), name="rel_bias_table")(rel_bias, bucket_ids)


def _bias_tables(geom, rel_bias):
    tq, kb = geom.tq, KEY_BLOCK
    own_cols = tq if geom.own_wide else NARROW
    own = _bias_call(rel_bias, _bucket_ids([0], tq, own_cols, tq, geom.qpos0))
    if geom.npast is None:
        assert tq == kb and kb + 1 >= T5_MAX_DIST
        assert geom.qpos0 + tq - (geom.n_meta - 1) >= T5_MAX_DIST
        past = _bias_call(rel_bias, _bucket_ids([-kb, -2 * kb], tq, kb, kb))
        meta = _bias_call(rel_bias, _bucket_ids([-geom.qpos0, -geom.qpos0 - tq], tq, NARROW, geom.n_meta))
        mown = _bias_call(rel_bias, _bucket_ids([0], geom.n_meta, NARROW, geom.n_meta, 0))
        return own, past, meta, mown
    past = _bias_call(rel_bias, _bucket_ids([N_META + j * kb - geom.qpos0 for j in range(geom.npast)], tq, kb, kb))
    return own, past, None, None


def _split_tables(tables, lo, hi):
    return [None if t is None else t[:, lo:hi] for t in tables]


def _no_past(npast):
    return isinstance(npast, int) and npast == 0


STATIC_UNROLL = 4


def _aligned(x, m):
    return x if isinstance(x, int) else pl.multiple_of(x, m)


def _fold_blocks(count, fn, init):
    if isinstance(count, int) and count <= STATIC_UNROLL:
        for j in range(count):
            init = fn(j, init)
        return init
    carry = lax.fori_loop(0, count // 2, lambda t, c: fn(2 * t + 1, fn(2 * t, c)), init)
    return lax.cond(count % 2 == 1, lambda c: fn(count - 1, c), lambda c: c, carry)


def _past_loop(npast, body):
    _fold_blocks(npast, body, 0)


def _tile_walk(geom, frames_tile, meta_tile):
    i = pl.program_id(1)

    @pl.when(i < geom.ntile)
    def _():
        q0 = _aligned(geom.row0 + i * geom.tq, 16)
        npast = i if geom.npast is None else geom.npast
        frames_tile(q0, geom.tq, geom.qpos0 + i * geom.tq, npast)

    if geom.n_meta:
        @pl.when(i == geom.ntile)
        def _():
            meta_tile(0, geom.n_meta, 0)


def _own_mask(qpos, n, cols, causal):
    r = lax.broadcasted_iota(I32, (n, 1), 0)
    c = lax.broadcasted_iota(I32, (1, cols), 1)
    if causal:
        return c < r
    return (c < n) & (((qpos - N_META + c) >> 6) <= ((qpos - N_META + r) >> 6))


def _slab_spec(arr):
    return pl.BlockSpec((1,) + arr.shape[1:], lambda b, i: (b, 0, 0))


def _cache_spec(arr, layer):
    return pl.BlockSpec((1, 1) + arr.shape[2:], lambda b, i: (layer, b, 0, 0))


def _past_spec(arr, layer):
    return _cache_spec(arr, layer) if arr.ndim == 4 else _slab_spec(arr)


def _const_spec(arr):
    nd = arr.ndim
    return pl.BlockSpec(arr.shape, lambda b, i: (0,) * nd)


def _readers(geom, new_refs, past_refs):
    def new(k, r0, n):
        return new_refs[k][0, pl.ds(r0, n), :]

    def past(k, j):
        if geom.npast is None:
            return new_refs[k][0, pl.ds(_aligned(geom.row0 + j * KEY_BLOCK, 16), KEY_BLOCK), :]
        ref, r0 = past_refs[k], _aligned(j * KEY_BLOCK, KEY_BLOCK)
        return ref[0, 0, pl.ds(r0, KEY_BLOCK), :] if len(ref.shape) == 4 else ref[0, pl.ds(r0, KEY_BLOCK), :]
    return new, past


def _fill_narrow(dst_ref, rows):
    dst_ref[...] = jnp.zeros(dst_ref.shape, dst_ref.dtype)
    dst_ref[0:rows.shape[0], :] = rows.astype(dst_ref.dtype)


def _attn_grid(geom, bsz):
    return (bsz, geom.ntile + (1 if geom.n_meta else 0))


def _maps_dot(lhs, n, w, rhs, transpose_rhs=False):
    dot = _dot_nt if transpose_rhs else _dot
    if isinstance(lhs, (list, tuple)):
        return [dot(x, rhs) for x in lhs]
    nmaps = lhs.shape[0]
    if n % LANES == 0:
        out = dot(lhs[:, 0:n, 0:w].reshape(nmaps * n, w), rhs)
        return [out[p * n:(p + 1) * n] for p in range(nmaps)]
    return [dot(lhs[p, 0:n, 0:w], rhs) for p in range(nmaps)]


class Softmax:
    NREFS = 7

    def __init__(self, refs, nmaps, n, ones_lane=None):
        self.sw, self.sn, self.mx, self.m, self.ls, self.acc, self.e = refs
        self.nmaps, self.n, self.ones_lane = nmaps, n, ones_lane
        self.mx[:, 0:n, :] = jnp.full((nmaps, n, LANES), NEG_INF, F32)
        self.ls[:, 0:n, :] = jnp.zeros((nmaps, n, LANES), F32)
        self.acc[:, 0:n, :] = jnp.zeros((nmaps, n, self.acc.shape[2]), F32)

    @staticmethod
    def scratch(geom, nmaps, vw):
        return [pltpu.VMEM((nmaps, geom.nslot, geom.tq, KEY_BLOCK), F32),
                pltpu.VMEM((nmaps, 2, geom.tq, NARROW), F32),
                pltpu.VMEM((nmaps, geom.tq, LANES), F32), pltpu.VMEM((nmaps, geom.tq, LANES), F32),
                pltpu.VMEM((nmaps, geom.tq, LANES), F32), pltpu.VMEM((nmaps, geom.tq, vw), F32),
                pltpu.VMEM((nmaps, geom.tq, KEY_BLOCK), BF)]

    def _store(self, p, slot, wide, s):
        if wide:
            self.sw[p, slot, 0:self.n, :] = s
        else:
            self.sn[p, slot, 0:self.n, :] = s

    def _load(self, p, slot, wide):
        return self.sw[p, slot, 0:self.n, :] if wide else self.sn[p, slot, 0:self.n, :]

    def scores(self, slot, wide, q_maps, k_b, scale, bias_of=None, extra=None):
        n = self.n
        d = q_maps[0].shape[-1] if isinstance(q_maps, (list, tuple)) else q_maps.shape[2]
        s_all = _maps_dot(q_maps, n, d, k_b, transpose_rhs=True)
        for p in range(self.nmaps):
            s = s_all[p]
            if scale != 1.0:
                s = s * scale
            if bias_of is not None:
                s = s + bias_of(p)
            if extra is not None:
                s = s + extra
            self._store(p, slot, wide, s)
            self.mx[p, 0:n, :] = jnp.maximum(self.mx[p, 0:n, :], _halves(s, jnp.maximum))

    def finish_max(self):
        n = self.n
        for p in range(self.nmaps):
            m = jnp.max(self.mx[p, 0:n, :], axis=1, keepdims=True)
            self.m[p, 0:n, :] = jnp.broadcast_to(m, (n, LANES))

    def values(self, slot, wide, v_b):
        n, w = self.n, (KEY_BLOCK if wide else NARROW)
        staged = n % LANES == 0
        es = []
        for p in range(self.nmaps):
            s = self._load(p, slot, wide)
            e = jnp.exp(s - _rep(self.m[p, 0:n, :], w))
            if self.ones_lane is None:
                self.ls[p, 0:n, :] += _halves(e, jnp.add)
            if staged:
                self.e[p, 0:n, 0:w] = e.astype(BF)
            else:
                es.append(e)
        for p, pv in enumerate(_maps_dot(self.e if staged else es, n, w, v_b)):
            self.acc[p, 0:n, :] += pv

    def result(self, p):
        n = self.n
        acc = self.acc[p, 0:n, :]
        if self.ones_lane is not None:
            return acc / acc[:, self.ones_lane:self.ones_lane + 1]
        return acc / jnp.sum(self.ls[p, 0:n, :], axis=1, keepdims=True)


def _mla_kernel(geom, *refs):
    n_past = 0 if geom.npast is None else 1
    (q_ref, kc_ref), refs = refs[:2], refs[2:]
    past_refs, refs = refs[:n_past], refs[n_past:]
    (wuv_ref, o_ref), refs = refs[:2], refs[2:]
    sm_refs, (qm_s, kn_s, km_s) = refs[:Softmax.NREFS], refs[Softmax.NREFS:]
    new, past = _readers(geom, (kc_ref,), past_refs)
    scale = 1.0

    def tile(q0, n, qpos, npast, with_meta, own_wide):
        sm = Softmax(sm_refs, MLA_HEADS, n, ones_lane=ONES_LANE)
        q_heads = lambda: [q_ref[0, pl.ds(q0, n), 2 * LANES * h:2 * LANES * (h + 1)] for h in range(MLA_HEADS)]
        if n % LANES == 0:
            for h, q_h in enumerate(q_heads()):
                qm_s[h, 0:n, :] = q_h
        q_src = lambda: qm_s if n % LANES == 0 else q_heads()
        own_slot = npast if own_wide else 0
        if own_wide:
            own_k = lambda: new(0, q0, n)
        else:
            _fill_narrow(kn_s, new(0, q0, n))
            own_k = lambda: kn_s[...]
        own_cols = n if own_wide else NARROW
        sm.scores(own_slot, own_wide, q_src(), own_k(), scale,
                  extra=jnp.where(_own_mask(qpos, n, own_cols, False), 0.0, NEG_INF))

        def p1(j, carry):
            sm.scores(j, True, q_src(), past(0, j), scale)
            return carry
        _past_loop(npast, p1)
        if with_meta:
            _fill_narrow(km_s, new(0, 0, geom.n_meta))
            pad = jnp.where(lax.broadcasted_iota(I32, (1, NARROW), 1) < geom.n_meta, 0.0, NEG_INF)
            sm.scores(1, False, q_src(), km_s[...], scale, extra=pad)
        sm.finish_max()

        sm.values(own_slot, own_wide, own_k())

        def p2(j, carry):
            sm.values(j, True, past(0, j))
            return carry
        _past_loop(npast, p2)
        if with_meta:
            sm.values(1, False, km_s[...])

        out = None
        ohead = _lane_group((n, BR_W), MLA_V)
        for h in range(MLA_HEADS):
            o_h = jnp.where(ohead == h, _dot(sm.result(h), wuv_ref[...]), 0.0)
            out = o_h if out is None else out + o_h
        o_ref[0, pl.ds(q0, n), :] = out

    _tile_walk(geom,
               lambda q0, n, qpos, npast: tile(q0, n, qpos, npast, bool(geom.n_meta), geom.own_wide),
               lambda q0, n, qpos: tile(q0, n, qpos, 0, False, False))


def _mla_call(geom, layer, qcat, kcat, past, wuv_pad):
    bsz = qcat.shape[0]
    ins = [qcat, kcat] + list(past) + [wuv_pad]
    specs = ([_slab_spec(a) for a in (qcat, kcat)] + [_past_spec(a, layer) for a in past] + [_const_spec(wuv_pad)])
    return pl.pallas_call(
        functools.partial(_mla_kernel, geom), grid=_attn_grid(geom, bsz),
        in_specs=specs, out_specs=pl.BlockSpec((1, geom.lq, BR_W), lambda b, i: (b, 0, 0)),
        out_shape=jax.ShapeDtypeStruct((bsz, geom.lq, BR_W), F32),
        scratch_shapes=(Softmax.scratch(geom, MLA_HEADS, 2 * LANES)
                        + [pltpu.VMEM((MLA_HEADS, geom.tq, 2 * LANES), BF)] + [pltpu.VMEM((NARROW, 2 * LANES), BF)] * 2),
        compiler_params=_cparams(("parallel", "arbitrary")), name="mla_attn")(*ins)


def _suffix_matrix(n):
    later = lax.broadcasted_iota(I32, (n, n), 0) > lax.broadcasted_iota(I32, (n, n), 1)
    return jnp.where(later, 1.0, 0.0).astype(BF)


def _sb_kernel(geom, *refs):
    n_past = 0 if geom.npast is None else 2
    (q_ref, k_ref, v_ref), refs = refs[:3], refs[3:]
    past_refs, (o_ref, acc_s, carry_s, qm_s, hl_s, lb_s, a_s, tri_s, kn_s, vn_s, km_s, vm_s) = \
        refs[:n_past], refs[n_past:]
    new, past = _readers(geom, (k_ref, v_ref), past_refs)

    def tile(q0, n, qpos, npast, with_meta, own_wide):
        q = q_ref[0, pl.ds(q0, n), :]
        head = _lane_group(q.shape, SB_DH)
        for h in range(SB_HEADS):
            qm_s[h, 0:n, :] = jnp.where(head == h, q, jnp.zeros_like(q))
        acc_s[:, 0:n, :] = jnp.zeros((SB_HEADS, n, BR_W), F32)
        carry_s[:, 0:n, :] = jnp.zeros((SB_HEADS, n, LANES), F32)

        def block(k_b, v_b, mask, tri):
            w = k_b.shape[0]
            staged = n % LANES == 0
            z_all = _maps_dot(qm_s, n, BR_W, k_b, transpose_rhs=True)
            halves, log_bs, firsts = [], [], []
            for h in range(SB_HEADS):
                z = z_all[h]
                soft = jnp.log(1.0 + jnp.exp(-jnp.abs(z)))
                log_b = jnp.minimum(z, 0.0) - soft
                log_1m = log_b - z
                if mask is not None:
                    log_1m = jnp.where(mask, log_1m, 0.0)
                firsts.append(log_1m[:, 0:1])
                hi = log_1m.astype(BF)
                lo = (log_1m - hi.astype(F32)).astype(BF)
                if staged:
                    hl_s[2 * h, 0:n, 0:w], hl_s[2 * h + 1, 0:n, 0:w], lb_s[h, 0:n, 0:w] = hi, lo, log_b
                else:
                    halves += [hi, lo]
                    log_bs.append(log_b)
            sums_all = _maps_dot(hl_s if staged else halves, n, w, tri)
            weights = []
            for h in range(SB_HEADS):
                later = sums_all[2 * h] + sums_all[2 * h + 1]
                carry = carry_s[h, 0:n, :]
                log_b = lb_s[h, 0:n, 0:w] if staged else log_bs[h]
                a = jnp.exp(log_b + later + _rep(carry, w))
                if mask is not None:
                    a = jnp.where(mask, a, 0.0)
                if staged:
                    a_s[h, 0:n, 0:w] = a.astype(BF)
                else:
                    weights.append(a)
                carry_s[h, 0:n, :] = carry + jnp.broadcast_to(later[:, 0:1] + firsts[h], (n, LANES))
            for h, pv in enumerate(_maps_dot(a_s if staged else weights, n, w, v_b)):
                acc_s[h, 0:n, :] += pv

        if own_wide:
            block(new(0, q0, n), new(1, q0, n), _own_mask(qpos, n, n, True), _suffix_matrix(n))
        else:
            _fill_narrow(kn_s, new(0, q0, n))
            _fill_narrow(vn_s, new(1, q0, n))
            block(kn_s[...], vn_s[...], _own_mask(qpos, n, NARROW, True), _suffix_matrix(NARROW))
        if _no_past(npast):
            tri_kb = None
        elif n % LANES == 0:
            tri_s[...] = _suffix_matrix(KEY_BLOCK)
            tri_kb = lambda: tri_s[...]
        else:
            tri_value = _suffix_matrix(KEY_BLOCK)
            tri_kb = lambda: tri_value

        def past_block(jj, carry):
            j = npast - 1 - jj
            block(past(0, j).astype(BF), past(1, j).astype(BF), None, tri_kb())
            return carry
        _past_loop(npast, past_block)
        if with_meta:
            m0 = 0
            _fill_narrow(km_s, new(0, m0, geom.n_meta))
            _fill_narrow(vm_s, new(1, m0, geom.n_meta))
            pad = jnp.broadcast_to(lax.broadcasted_iota(I32, (1, NARROW), 1) < geom.n_meta, (n, NARROW))
            block(km_s[...], vm_s[...], pad, _suffix_matrix(NARROW))

        out = None
        for h in range(SB_HEADS):
            o_h = jnp.where(head == h, acc_s[h, 0:n, :], 0.0)
            out = o_h if out is None else out + o_h
        o_ref[0, pl.ds(q0, n), :] = out

    _tile_walk(geom,
               lambda q0, n, qpos, npast: tile(q0, n, qpos, npast, bool(geom.n_meta), geom.own_wide),
               lambda q0, n, qpos: tile(q0, n, qpos, 0, False, False))


def _sb_call(geom, layer, q, k, v, past):
    bsz = q.shape[0]
    ins = [q, k, v] + list(past)
    specs = [_slab_spec(a) for a in (q, k, v)] + [_past_spec(a, layer) for a in past]
    return pl.pallas_call(
        functools.partial(_sb_kernel, geom), grid=_attn_grid(geom, bsz),
        in_specs=specs, out_specs=pl.BlockSpec((1, geom.lq, BR_W), lambda b, i: (b, 0, 0)),
        out_shape=jax.ShapeDtypeStruct((bsz, geom.lq, BR_W), F32),
        scratch_shapes=[pltpu.VMEM((SB_HEADS, geom.tq, BR_W), F32), pltpu.VMEM((SB_HEADS, geom.tq, LANES), F32),
                        pltpu.VMEM((SB_HEADS, geom.tq, BR_W), BF),
                        pltpu.VMEM((2 * SB_HEADS, geom.tq, KEY_BLOCK), BF), pltpu.VMEM((SB_HEADS, geom.tq, KEY_BLOCK), F32),
                        pltpu.VMEM((SB_HEADS, geom.tq, KEY_BLOCK), BF), pltpu.VMEM((KEY_BLOCK, KEY_BLOCK), BF)]
                       + [pltpu.VMEM((NARROW, BR_W), BF)] * 4,
        compiler_params=_cparams(("parallel", "arbitrary")), name="sb_attn")(*ins)


def _bias_readers(geom, own_ref, past_ref, meta_ref, mown_ref, tile_idx):
    def own(h, n):
        return own_ref[0, h, 0:n, :]

    def earlier(j, h):
        slot = j if geom.npast is not None else jnp.where(j == tile_idx - 1, 0, 1)
        return past_ref[slot, h]

    def meta(h):
        return meta_ref[jnp.minimum(tile_idx, 1), h]

    def meta_own(h):
        return mown_ref[0, h]
    return own, earlier, meta, meta_own


def _diff_kernel(geom, lam_init, *refs):
    n_past = 0 if geom.npast is None else 2
    n_bias = 4 if geom.n_meta else 2
    (q_ref, k_ref, v_ref), refs = refs[:3], refs[3:]
    past_refs, refs = refs[:n_past], refs[n_past:]
    bias_refs, refs = list(refs[:n_bias]) + [None] * (4 - n_bias), refs[n_bias:]
    (lam_ref, sg_ref, o_ref), refs = refs[:3], refs[3:]
    sm_refs, (qm_s, kn_s, vn_s, km_s, vm_s) = refs[:Softmax.NREFS], refs[Softmax.NREFS:]
    new, past = _readers(geom, (k_ref, v_ref), past_refs)
    b_own, b_earlier, b_meta, b_meta_own = _bias_readers(geom, *bias_refs, pl.program_id(1))
    scale = 1.0
    npair = 2 * DIFF_HEADS

    def tile(q0, n, qpos, npast, with_meta, own_wide, is_meta):
        sm = Softmax(sm_refs, npair, n)
        q = q_ref[0, pl.ds(q0, n), :]
        pair = _lane_group(q.shape, DIFF_DQK)
        for p in range(npair):
            qm_s[p, 0:n, :] = jnp.where(pair == p, q, jnp.zeros_like(q))
        q_of = qm_s
        own_slot = npast if own_wide else 0
        if own_wide:
            own_k, own_v = (lambda: new(0, q0, n)), (lambda: new(1, q0, n))
        else:
            _fill_narrow(kn_s, new(0, q0, n))
            _fill_narrow(vn_s, new(1, q0, n))
            own_k, own_v = (lambda: kn_s[...]), (lambda: vn_s[...])
        sm.scores(own_slot, own_wide, q_of, own_k(), scale,
                  bias_of=(lambda p: b_meta_own(p // 2)) if is_meta else (lambda p: b_own(p // 2, n)))

        def p1(j, carry):
            sm.scores(j, True, q_of, past(0, j), scale,
                      bias_of=lambda p: b_earlier(j, p // 2))
            return carry
        _past_loop(npast, p1)
        if with_meta:
            m0 = 0
            _fill_narrow(km_s, new(0, m0, geom.n_meta))
            _fill_narrow(vm_s, new(1, m0, geom.n_meta))
            sm.scores(1, False, q_of, km_s[...], scale, bias_of=lambda p: b_meta(p // 2))
        sm.finish_max()

        sm.values(own_slot, own_wide, own_v().astype(BF))

        def p2(j, carry):
            sm.values(j, True, past(1, j).astype(BF))
            return carry
        _past_loop(npast, p2)
        if with_meta:
            sm.values(1, False, vm_s[...])

        lp = lam_ref[...]
        lam = (jnp.exp(jnp.sum(lp[0:1] * lp[1:2], axis=1, keepdims=True))
               - jnp.exp(jnp.sum(lp[2:3] * lp[3:4], axis=1, keepdims=True)) + lam_init)
        head = _lane_group((n, BR_W), DIFF_DV)
        out = None
        for h in range(DIFF_HEADS):
            o_h = jnp.where(head == h, sm.result(2 * h) - lam * sm.result(2 * h + 1), 0.0)
            ms = jnp.sum(o_h * o_h, axis=1, keepdims=True) * (1.0 / DIFF_DV)
            o_h = o_h * lax.rsqrt(ms + RMS_EPS) * sg_ref[...] * (1.0 - lam_init)
            out = o_h if out is None else out + o_h
        o_ref[0, pl.ds(q0, n), :] = out

    _tile_walk(geom,
               lambda q0, n, qpos, npast: tile(q0, n, qpos, npast, bool(geom.n_meta), geom.own_wide, False),
               lambda q0, n, qpos: tile(q0, n, qpos, 0, False, False, True))


def _diff_call(geom, layer, q, k, v, past, tables, lam_p, sg_tiled):
    bsz = q.shape[0]
    lam_init = 0.8 - 0.6 * math.exp(-0.3 * layer)
    tables = [t for t in tables if t is not None]
    ins = [q, k, v] + list(past) + tables + [lam_p, sg_tiled]
    specs = ([_slab_spec(a) for a in (q, k, v)] + [_past_spec(a, layer) for a in past]
             + [_const_spec(a) for a in tables + [lam_p, sg_tiled]])
    npair = 2 * DIFF_HEADS
    return pl.pallas_call(
        functools.partial(_diff_kernel, geom, lam_init), grid=_attn_grid(geom, bsz),
        in_specs=specs, out_specs=pl.BlockSpec((1, geom.lq, BR_W), lambda b, i: (b, 0, 0)),
        out_shape=jax.ShapeDtypeStruct((bsz, geom.lq, BR_W), F32),
        scratch_shapes=(Softmax.scratch(geom, npair, BR_W) + [pltpu.VMEM((npair, geom.tq, BR_W), BF)]
                        + [pltpu.VMEM((NARROW, BR_W), BF)] * 4),
        compiler_params=_cparams(("parallel", "arbitrary")), name="diff_attn")(*ins)


def _sortable(x):
    b = lax.bitcast_convert_type(x + 0.0, I32)
    return b ^ ((b >> 31) & I32(0x7FFFFFFF))


def _dsa_kernel(geom, *refs):
    n_past = 0 if geom.npast is None else 3
    n_bias = 4 if geom.n_meta else 2
    (q_ref, qi_ref, wi_ref, k_ref, v_ref, kit_ref), refs = refs[:6], refs[6:]
    past_refs, refs = refs[:n_past], refs[n_past:]
    bias_refs, refs = list(refs[:n_bias]) + [None] * (4 - n_bias), refs[n_bias:]
    o_ref, refs = refs[0], refs[1:]
    sm_refs, (qm_s, qim_s, wib_s, kw_s, kn_s, last_s,
              kno_s, vno_s, kio_s, knm_s, vnm_s, kim_s) = refs[:Softmax.NREFS], refs[Softmax.NREFS:]
    new, past = _readers(geom, (k_ref, v_ref, kit_ref), past_refs)
    b_own, b_earlier, b_meta, b_meta_own = _bias_readers(geom, *bias_refs, pl.program_id(1))
    topk = float(geom.topk)
    past_pos0 = N_META
    index_bits = max(1, int(geom.lq + geom.lp + N_META).bit_length())
    ones_count = jnp.ones((LANES, LANES), BF)

    def tile(q0, n, qpos, npast, with_meta, own_wide, is_meta):
        nm = geom.n_meta
        m0 = 0
        own_slot = npast if own_wide else 0
        own_cols = n if own_wide else NARROW

        qi = qi_ref[0, pl.ds(q0, n), :]
        igrp = _lane_group(qi.shape, IDX_DIM)
        wi = wi_ref[0, pl.ds(q0, n), :] * (IDX_HEADS ** -0.5 * IDX_DIM ** -0.5)
        for h in range(IDX_HEADS):
            qim_s[h, 0:n, :] = jnp.where(igrp == h, qi, jnp.zeros_like(qi))
            wib_s[h, 0:n, :] = jnp.broadcast_to(wi[:, h:h + 1], (n, LANES))
        q = q_ref[0, pl.ds(q0, n), :]
        head = _lane_group(q.shape, DSA_DH)
        for h in range(DSA_HEADS):
            qm_s[h, 0:n, :] = jnp.where(head == h, q, jnp.zeros_like(q))
        if own_wide:
            own_k, own_v, own_ki = (lambda: new(0, q0, n)), (lambda: new(1, q0, n)), (lambda: new(2, q0, n))
        else:
            _fill_narrow(kno_s, new(0, q0, n))
            _fill_narrow(vno_s, new(1, q0, n))
            _fill_narrow(kio_s, new(2, q0, n))
            own_k, own_v, own_ki = (lambda: kno_s[...]), (lambda: vno_s[...]), (lambda: kio_s[...])
        if with_meta:
            _fill_narrow(knm_s, new(0, m0, nm))
            _fill_narrow(vnm_s, new(1, m0, nm))
            _fill_narrow(kim_s, new(2, m0, nm))

        def index_keys(kit_b):
            raw = _maps_dot(qim_s, n, BR_W, kit_b, transpose_rhs=True)
            score = None
            for h in range(IDX_HEADS):
                t = _rep(wib_s[h, 0:n, :], kit_b.shape[0]) * jnp.maximum(raw[h], 0.0)
                score = t if score is None else score + t
            return _sortable(score)

        own_keys = jnp.where(_own_mask(qpos, n, own_cols, False), index_keys(own_ki()), I32(KEY_NEG))
        if own_wide:
            kw_s[own_slot, 0:n, :] = own_keys
        else:
            pad = lax.broadcasted_iota(I32, (1, NARROW), 1) < n
            kn_s[0, 0:n, :] = jnp.where(pad, own_keys, I32(INT_MIN))

        def score_block(j, carry):
            kw_s[j, 0:n, :] = index_keys(past(2, j).astype(BF))
            return carry
        _past_loop(npast, score_block)
        if with_meta:
            pad = lax.broadcasted_iota(I32, (1, NARROW), 1) < nm
            kn_s[1, 0:n, :] = jnp.where(pad, index_keys(kim_s[...]), I32(INT_MIN))
        nwide = npast + 1 if own_wide else npast

        def wide_sweep(init, fn):
            return _fold_blocks(nwide, fn, init)

        def one(cond):
            return jnp.where(cond, 1.0, 0.0)

        def partial_counts(r0, nr, wide_fn, own_narrow_fn, meta_fn):
            part = wide_sweep(jnp.zeros((nr, LANES), F32), lambda j, a: a + wide_fn(kw_s[j, r0:r0 + nr, :], j))
            if not own_wide:
                part = part + own_narrow_fn(kn_s[0, r0:r0 + nr, :])
            if with_meta:
                part = part + meta_fn(kn_s[1, r0:r0 + nr, :])
            return part

        def lane_sums(part):
            return jnp.dot(part.astype(BF), ones_count, preferred_element_type=F32)

        def count(wide_fn, own_narrow_fn, meta_fn):
            return lane_sums(partial_counts(0, n, wide_fn, own_narrow_fn, meta_fn))

        def count_cmp(cmp):
            return count(lambda x, j: _halves(one(cmp(x, KEY_BLOCK)), jnp.add),
                         lambda x: one(cmp(x, NARROW)), lambda x: one(cmp(x, NARROW)))

        def ge_part(r0, nr, cand):
            return partial_counts(r0, nr, lambda x, j: _halves(one(x >= _rep(cand, KEY_BLOCK)), jnp.add),
                                  lambda x: one(x >= cand), lambda x: one(x >= cand))

        def sign_step(r0, nr):
            c0 = lane_sums(ge_part(r0, nr, jnp.zeros((nr, LANES), I32)))
            return jnp.where(c0 >= topk, I32(0), I32(INT_MIN)), jnp.where(c0 >= topk, c0, topk + 1.0)

        def accept(thr, cnt, cand, c):
            keep = c >= topk
            return jnp.where(keep, cand, thr), jnp.where(keep, c, cnt)

        def bit_step(b, state):
            thr, cnt = state
            cand = thr | jnp.left_shift(I32(1), I32(30) - b)
            return accept(thr, cnt, cand, lane_sums(ge_part(0, n, cand)))
        thr, cnt = lax.fori_loop(0, 31, bit_step, sign_step(0, n))

        last_s[0:n, :] = jnp.full((n, LANES), INT_MAX, I32)
        has_ties = jnp.max(one((cnt > topk) & (thr > I32(KEY_NEG)))) > 0.0

        @pl.when(has_ties)
        def _():
            need = topk - count_cmp(lambda x, w: x > _rep(thr, w))
            colw = lax.broadcasted_iota(I32, (1, KEY_BLOCK), 1)
            coln = lax.broadcasted_iota(I32, (1, NARROW), 1)

            def wide_pos0(j):
                return past_pos0 + j * KEY_BLOCK

            def ties_before(x):
                return count(lambda kk, j: _halves(one((kk == _rep(thr, KEY_BLOCK))
                                                       & (colw < _rep(x, KEY_BLOCK) - wide_pos0(j))), jnp.add),
                             lambda kk: one((kk == thr) & (coln < x - qpos)),
                             lambda kk: one((kk == thr) & (coln < x)))

            def pos_step(b, last_pos):
                cand = last_pos + jnp.left_shift(I32(1), I32(index_bits - 1) - b)
                return jnp.where(ties_before(cand) < need, cand, last_pos)
            last_s[0:n, :] = lax.fori_loop(0, index_bits, pos_step, jnp.zeros((n, LANES), I32))

        last_pos = last_s[0:n, :]

        def sel_bias(kk, kpos0):
            w = kk.shape[1]
            cols = lax.broadcasted_iota(I32, (1, w), 1)
            t = _rep(thr, w)
            sel = (kk > t) | ((kk == t) & (cols <= _rep(last_pos, w) - kpos0))
            return jnp.where(sel, 0.0, NEG_INF)

        sm = Softmax(sm_refs, DSA_HEADS, n)
        q_of = qm_s
        own_kk = kw_s[own_slot, 0:n, :] if own_wide else kn_s[0, 0:n, :]
        sm.scores(own_slot, own_wide, q_of, own_k(), 1.0,
                  bias_of=(lambda h: b_meta_own(h)) if is_meta else (lambda h: b_own(h, n)),
                  extra=sel_bias(own_kk, qpos))

        def p1(j, carry):
            sm.scores(j, True, q_of, past(0, j).astype(BF), 1.0,
                      bias_of=lambda h: b_earlier(j, h),
                      extra=sel_bias(kw_s[j, 0:n, :], past_pos0 + j * KEY_BLOCK))
            return carry
        _past_loop(npast, p1)
        if with_meta:
            sm.scores(1, False, q_of, knm_s[...], 1.0, bias_of=lambda h: b_meta(h),
                      extra=sel_bias(kn_s[1, 0:n, :], 0))
        sm.finish_max()

        sm.values(own_slot, own_wide, own_v().astype(BF))

        def p2(j, carry):
            sm.values(j, True, past(1, j).astype(BF))
            return carry
        _past_loop(npast, p2)
        if with_meta:
            sm.values(1, False, vnm_s[...])

        out = None
        for h in range(DSA_HEADS):
            o_h = jnp.where(head == h, sm.result(h), 0.0)
            out = o_h if out is None else out + o_h
        o_ref[0, pl.ds(q0, n), :] = out

    _tile_walk(geom,
               lambda q0, n, qpos, npast: tile(q0, n, qpos, npast, bool(geom.n_meta), geom.own_wide, False),
               lambda q0, n, qpos: tile(q0, n, qpos, 0, False, False, True))


def _dsa_call(geom, layer, q, qi, wi, k, v, kit, past, tables):
    bsz = q.shape[0]
    tables = [t for t in tables if t is not None]
    ins = [q, qi, wi, k, v, kit] + list(past) + tables
    specs = ([_slab_spec(a) for a in (q, qi, wi, k, v, kit)] + [_past_spec(a, layer) for a in past]
             + [_const_spec(a) for a in tables])
    tq = geom.tq
    return pl.pallas_call(
        functools.partial(_dsa_kernel, geom), grid=_attn_grid(geom, bsz),
        in_specs=specs, out_specs=pl.BlockSpec((1, geom.lq, BR_W), lambda b, i: (b, 0, 0)),
        out_shape=jax.ShapeDtypeStruct((bsz, geom.lq, BR_W), F32),
        scratch_shapes=(Softmax.scratch(geom, DSA_HEADS, BR_W)
                        + [pltpu.VMEM((DSA_HEADS, tq, BR_W), BF), pltpu.VMEM((IDX_HEADS, tq, BR_W), BF),
                           pltpu.VMEM((IDX_HEADS, tq, LANES), F32),
                           pltpu.VMEM((geom.nslot, tq, KEY_BLOCK), I32), pltpu.VMEM((2, tq, NARROW), I32),
                           pltpu.VMEM((tq, LANES), I32)]
                        + [pltpu.VMEM((NARROW, BR_W), BF)] * 6),
        compiler_params=_cparams(("parallel", "arbitrary")), name="dsa_attn")(*ins)


def _row_tile(t):
    for tm in (768, 384, 256, 128, 64, 32, 16, 8):
        if t % tm == 0:
            return tm
    raise ValueError(f"token count {t} has no supported row tile")


def _rope_tables(pos):
    half = MLA_ROPE // 2
    inv_freq = ROPE_THETA ** (-jnp.arange(half, dtype=jnp.float32) / half)
    ang = pos.astype(jnp.float32)[:, None] * inv_freq[None, :]
    cos, sin = jnp.cos(ang), jnp.sin(ang)
    return jnp.concatenate([cos, cos], axis=1), jnp.concatenate([-sin, sin], axis=1)


def _run_group(geom, x, pos_rows, caches, weights):
    (ln_in_g, ln_in_b, w_in, mla_qnorm_g, mla_w_uq, mla_kvnorm_g, mla_w_uk, mla_w_uv, diff_lambda,
     diff_subln_g, rel_bias, w_br, w_out, ln1_g, ln1_b, w_ff1, b_ff1, w_ff2, b_ff2, ln2_g, ln2_b) = weights
    bsz, lq, d = x.shape
    t = bsz * lq
    tm = _row_tile(t)
    cos32, sin32 = _rope_tables(pos_rows)
    cos_k = jnp.tile(cos32, (bsz, MLA_HEADS))
    sin_k = jnp.tile(sin32, (bsz, MLA_HEADS))
    tables = _bias_tables(geom, rel_bias)
    diff_tables = _split_tables(tables, 0, DIFF_HEADS)
    dsa_tables = _split_tables(tables, DIFF_HEADS, DIFF_HEADS + DSA_HEADS)

    xf = _ln_call(x.reshape(t, d), ln_in_g, ln_in_b, tm)
    rows = []
    for l in range(DEPTH):
        w_uq = mla_w_uq[l].reshape(MLA_Q_LORA, MLA_HEADS, MLA_NOPE + MLA_ROPE)
        wn = w_uq[:, :, :MLA_NOPE].reshape(MLA_Q_LORA, MLA_HEADS * MLA_NOPE).astype(BF)
        wp = w_uq[:, :, MLA_NOPE:].reshape(MLA_Q_LORA, MLA_HEADS * MLA_ROPE).astype(BF)
        wuk = mla_w_uk[l].reshape(MLA_KV_LORA, MLA_HEADS * MLA_NOPE).T.astype(BF)
        wuv = jnp.pad(mla_w_uv[l].reshape(MLA_KV_LORA, MLA_HEADS * MLA_V),
                      ((0, 2 * LANES - MLA_KV_LORA), (0, 0))).astype(BF)
        (ckv, kpe, b_k, b_v, c_k, c_v, d_k, d_v, d_ki, d_wi,
         qcat, kcat, bq_b, bk_b, bv_b, cq_b, ck_b, cv_b, dq_b, dk_b, dv_b, dqi_b, kit_b) = _proj_call(
            xf, _mix_weight(w_in[l]), mla_kvnorm_g[l].reshape(1, -1), mla_qnorm_g[l].reshape(1, -1),
            wn, wp, wuk, cos_k, sin_k, tm)
        per = lambda a: a.reshape(bsz, lq, a.shape[-1])
        if caches is None:
            past = lambda *idx: []
            mla_past, kit_past = [], []
        else:
            past = lambda *idx: [caches[i] for i in idx]
            fill = jnp.zeros(caches[0].shape[1:3] + (2 * LANES - MLA_KV_LORA - MLA_ROPE,), F32).at[..., -1].set(1.0)
            mla_past = [jnp.concatenate([caches[0][l], caches[1][l], fill], axis=-1).astype(BF)]
            kit_past = [jnp.tile(caches[8][l], (1, 1, IDX_HEADS)).astype(BF)]

        o_a = _mla_call(geom, l, per(qcat), per(kcat), mla_past, wuv)
        o_b = _sb_call(geom, l, per(bq_b), per(bk_b), per(bv_b), past(2, 3))
        o_c = _diff_call(geom, l, per(cq_b), per(ck_b), per(cv_b), past(4, 5), diff_tables, diff_lambda[l],
                         jnp.tile(diff_subln_g[l], DIFF_HEADS).reshape(1, -1))
        o_d = _dsa_call(geom, l, per(dq_b), per(dqi_b), per(d_wi), per(dk_b), per(dv_b), per(kit_b),
                        past(6, 7) + kit_past, dsa_tables)

        flat = lambda a: a.reshape(t, BR_W)
        wg = w_in[l][:, IN_OFFS[15]:IN_OFFS[16]].astype(BF)
        x1 = _merge_call(xf, flat(o_a), flat(o_b), flat(o_c), flat(o_d), wg, w_br[l].astype(BF),
                         w_out[l].astype(BF), ln1_g[l].reshape(1, -1), ln1_b[l].reshape(1, -1), tm)
        xf = _ffn_call(x1, w_ff1[l].astype(BF), b_ff1[l].reshape(1, -1), w_ff2[l].astype(BF),
                       b_ff2[l].reshape(1, -1), ln2_g[l].reshape(1, -1), ln2_b[l].reshape(1, -1), tm)
        rows.append([per(a) for a in (ckv, kpe, b_k, b_v, c_k, c_v, d_k, d_v, d_ki)])
    return xf.reshape(bsz, lq, d), rows


_ROW_TRAILING = ((MLA_KV_LORA,), (MLA_ROPE,), (SB_HEADS, SB_DH), (SB_HEADS, SB_DH),
                 (DIFF_HEADS, 2, DIFF_DQK), (DIFF_HEADS, DIFF_DV), (DSA_HEADS, DSA_DH),
                 (DSA_HEADS, DSA_DH), (IDX_DIM,))


def kernel(x_prompt, x_sample, cache_mla_kv, cache_mla_pe, cache_sb_k, cache_sb_v, cache_diff_k, cache_diff_v, cache_dsa_k, cache_dsa_v, cache_dsa_kidx, meta, ln_in_g, ln_in_b, w_in, mla_qnorm_g, mla_w_uq, mla_kvnorm_g, mla_w_uk, mla_w_uv, diff_lambda, diff_subln_g, rel_bias, w_br, w_out, ln1_g, ln1_b, w_ff1, b_ff1, w_ff2, b_ff2, ln2_g, ln2_b):
    weights = (ln_in_g, ln_in_b, w_in, mla_qnorm_g, mla_w_uq, mla_kvnorm_g, mla_w_uk, mla_w_uv, diff_lambda,
               diff_subln_g, rel_bias, w_br, w_out, ln1_g, ln1_b, w_ff1, b_ff1, w_ff2, b_ff2, ln2_g, ln2_b)
    assert w_in.shape[0] == DEPTH and x_prompt.shape[2] == D_MODEL

    bsz_p, seq_p, _ = x_prompt.shape
    assert seq_p % KEY_BLOCK == 0
    meta_b = jnp.broadcast_to(meta[None].astype(x_prompt.dtype), (bsz_p, N_META, D_MODEL))
    xp = jnp.concatenate([meta_b, x_prompt], axis=1)
    pos_p = jnp.arange(N_META + seq_p, dtype=I32)
    geom_p = Geom(tq=KEY_BLOCK, ntile=seq_p // KEY_BLOCK, npast=None, n_meta=N_META, qpos0=N_META,
                  row0=N_META, lq=seq_p + N_META, lp=0, topk=min(DSA_TOPK, seq_p // 4))
    yp, rows_p = _run_group(geom_p, xp, pos_p, None, weights)
    y_prompt = yp[:, N_META:]
    p_rows = []
    for i, trailing in enumerate(_ROW_TRAILING):
        stacked = jnp.stack([r[i] for r in rows_p], axis=0)
        p_rows.append(stacked.reshape(stacked.shape[:3] + trailing))

    past_len = cache_mla_kv.shape[2]
    bsz_s, dec_seq, _ = x_sample.shape
    assert past_len % KEY_BLOCK == 0 and dec_seq % 16 == 0 and dec_seq <= NARROW
    assert past_len % CHUNK == 0 and dec_seq <= CHUNK, "new frames must share one chunk"
    caches = [c.reshape(c.shape[:3] + (-1,)) for c in
              (cache_mla_kv, cache_mla_pe, cache_sb_k, cache_sb_v, cache_diff_k, cache_diff_v,
               cache_dsa_k, cache_dsa_v, cache_dsa_kidx)]
    pos_s = N_META + past_len + jnp.arange(dec_seq, dtype=I32)
    geom_s = Geom(tq=dec_seq, ntile=1, npast=past_len // KEY_BLOCK, n_meta=0, qpos0=N_META + past_len,
                  row0=0, lq=dec_seq, lp=past_len, topk=min(DSA_TOPK, (past_len + dec_seq) // 4))
    y_sample, rows_s = _run_group(geom_s, x_sample, pos_s, caches, weights)
    s_rows = []
    for i, trailing in enumerate(_ROW_TRAILING):
        stacked = jnp.stack([r[i] for r in rows_s], axis=0)
        s_rows.append(stacked.reshape(stacked.shape[:3] + trailing))

    return (y_prompt, y_sample, *p_rows, *s_rows)
```

```python
import functools
import math
from typing import NamedTuple, Optional

import jax
import jax.numpy as jnp
import numpy as np
from jax import lax
from jax.experimental import pallas as pl
from jax.experimental.pallas import tpu as pltpu

D_MODEL = 1024
CHUNK = 64
N_META = 16
MLA_HEADS = 4
MLA_Q_LORA = 256
MLA_KV_LORA = 128
MLA_NOPE = 64
MLA_ROPE = 32
MLA_V = 64
ROPE_THETA = 10000.0
SB_HEADS = 4
SB_DH = 64
DIFF_HEADS = 4
DIFF_DQK = 32
DIFF_DV = 64
DSA_HEADS = 4
DSA_DH = 64
IDX_HEADS = 8
IDX_DIM = 32
DSA_TOPK = 256
N_BRANCH = 4
BR_W = 256
D_FF = 4 * D_MODEL
T5_BUCKETS = 32
T5_MAX_DIST = 128
LN_EPS = 1e-5
RMS_EPS = 1e-6
NEG_INF = -1e30
DEPTH = 2
DN_ALPHA = (2 * DEPTH) ** 0.25
IN_SIZES = (MLA_Q_LORA, MLA_KV_LORA, MLA_ROPE,
            SB_HEADS * SB_DH, SB_HEADS * SB_DH, SB_HEADS * SB_DH,
            DIFF_HEADS * 2 * DIFF_DQK, DIFF_HEADS * 2 * DIFF_DQK, DIFF_HEADS * DIFF_DV,
            DSA_HEADS * DSA_DH, DSA_HEADS * DSA_DH, DSA_HEADS * DSA_DH,
            IDX_HEADS * IDX_DIM, IDX_DIM, IDX_HEADS,
            N_BRANCH * D_MODEL)
IN_OFFS = tuple(int(s) for s in np.cumsum((0,) + IN_SIZES))

LANES = 128
KEY_BLOCK = 256
NARROW = LANES
VMEM_LIMIT_MB = 56

BF = jnp.bfloat16
F32 = jnp.float32
I32 = jnp.int32

_NEG_BITS = int(np.float32(NEG_INF).view(np.int32))
KEY_NEG = _NEG_BITS ^ ((_NEG_BITS >> 31) & 0x7FFFFFFF)
INT_MIN = -(2 ** 31)
INT_MAX = 2 ** 31 - 1


def _cparams(sem):
    return pltpu.CompilerParams(dimension_semantics=sem, vmem_limit_bytes=VMEM_LIMIT_MB * 1024 * 1024)


def _dot(a, b):
    return jnp.dot(a.astype(BF), b.astype(BF), preferred_element_type=F32)


def _dot_nt(a, b):
    return lax.dot_general(a.astype(BF), b.astype(BF), (((1,), (1,)), ((), ())),
                           preferred_element_type=F32)


def _layer_norm(x, g, b):
    mu = jnp.mean(x, axis=-1, keepdims=True)
    xc = x - mu
    var = jnp.mean(xc * xc, axis=-1, keepdims=True)
    return xc * lax.rsqrt(var + LN_EPS) * g + b


def _rms_norm(x, g):
    return x * lax.rsqrt(jnp.mean(x * x, axis=-1, keepdims=True) + RMS_EPS) * g


def _rope_lanes(x, cos, sin):
    lane = lax.broadcasted_iota(I32, x.shape, 1)
    swapped = jnp.where((lane & 31) < 16, pltpu.roll(x, LANES - 16, 1), pltpu.roll(x, 16, 1))
    return x * cos + swapped * sin


def _lane_group(shape, width):
    return lax.broadcasted_iota(I32, shape, 1) // width


def _halves(x, op):
    return op(x[:, :LANES], x[:, LANES:]) if x.shape[1] == 2 * LANES else x


def _rep(x, width):
    return jnp.concatenate([x, x], axis=1) if width == 2 * LANES else x


class Geom(NamedTuple):
    tq: int
    ntile: int
    npast: Optional[int]
    n_meta: int
    qpos0: int
    row0: int
    lq: int
    lp: int
    topk: int

    @property
    def own_wide(self):
        return self.tq == KEY_BLOCK

    @property
    def nslot(self):
        return max(self.ntile if self.npast is None else self.npast, 1)


def _ln_kernel(x_ref, g_ref, b_ref, o_ref):
    o_ref[...] = _layer_norm(x_ref[...], g_ref[...], b_ref[...])


def _ln_call(x, g, b, tm):
    t, d = x.shape
    return pl.pallas_call(
        _ln_kernel, grid=(t // tm,),
        in_specs=[pl.BlockSpec((tm, d), lambda i: (i, 0)),
                  pl.BlockSpec((1, d), lambda i: (0, 0)),
                  pl.BlockSpec((1, d), lambda i: (0, 0))],
        out_specs=pl.BlockSpec((tm, d), lambda i: (i, 0)),
        out_shape=jax.ShapeDtypeStruct((t, d), F32),
        compiler_params=_cparams(("parallel",)), name="ln_in")(x, g.reshape(1, d), b.reshape(1, d))


_WIDE = (3, 4, 5, 6, 7, 8, 9, 10, 11, 12)
_WIDE_F32 = (1, 2, 4, 5, 7, 8)
_MIX_COLS = 256 + 128 + 256 * len(_WIDE) + LANES + 256 + LANES
_Q_SCALE = {0: SB_DH ** -0.5, 3: DIFF_DQK ** -0.5, 6: DSA_DH ** -0.5}
MLA_SCALE = (MLA_NOPE + MLA_ROPE) ** -0.5
ONES_LANE = 2 * LANES - 1


def _mix_weight(w_in_l):
    def seg(i, pad_to=None):
        w = w_in_l[:, IN_OFFS[i]:IN_OFFS[i + 1]]
        if pad_to is not None:
            w = jnp.pad(w, ((0, 0), (0, pad_to - w.shape[1])))
        return w
    cols = ([seg(0), seg(1)] + [seg(i) for i in _WIDE]
            + [seg(2, LANES), jnp.tile(seg(13), (1, IDX_HEADS)), seg(14, LANES)])
    return jnp.concatenate(cols, axis=1).astype(BF)


def _proj_kernel(x_ref, w_ref, kvg_ref, qg_ref, wn_ref, wp_ref, wuk_ref, cos_ref, sin_ref, *outs):
    (ckv_ref, kpe_ref, bk_ref, bv_ref, ck_ref, cv_ref, dk_ref, dv_ref, dki_ref, dwi_ref,
     qcat_ref, kcat_ref, *wide_bf) = outs
    kit_ref = wide_bf[-1]
    wide_bf = wide_bf[:-1]
    wide_f32 = dict(zip(_WIDE_F32, (bk_ref, bv_ref, ck_ref, cv_ref, dk_ref, dv_ref)))
    xb = x_ref[...].astype(BF)
    cos, sin = cos_ref[...], sin_ref[...]

    def seg(off, width):
        return jnp.dot(xb, w_ref[:, off:off + width], preferred_element_type=F32)

    qn = _rms_norm(seg(0, 256), qg_ref[...])
    nope = _dot(qn, wn_ref[...])
    pe = _rope_lanes(_dot(qn, wp_ref[...]), cos, sin)
    head = _lane_group(nope.shape, MLA_NOPE)
    lane = lax.broadcasted_iota(I32, pe.shape, 1)
    for h in range(MLA_HEADS):
        q_lat = _dot(jnp.where(head == h, nope, 0.0), wuk_ref[...]) * MLA_SCALE
        pe_h = (pe if h == 0 else pltpu.roll(pe, LANES - h * MLA_ROPE, 1)) * MLA_SCALE
        qcat_ref[:, 2 * LANES * h:2 * LANES * h + LANES] = q_lat.astype(BF)
        qcat_ref[:, 2 * LANES * h + LANES:2 * LANES * (h + 1)] = jnp.where(lane < MLA_ROPE, pe_h, 0.0).astype(BF)

    ckv = _rms_norm(seg(256, 128), kvg_ref[...])
    ckv_ref[...] = ckv
    off = 384
    for n, r in enumerate(wide_bf):
        y = seg(off, 256)
        if n in wide_f32:
            wide_f32[n][...] = y
        r[...] = (y * _Q_SCALE[n]).astype(BF) if n in _Q_SCALE else y.astype(BF)
        off += 256
    kpe = _rope_lanes(seg(off, LANES), cos, sin)
    kpe_ref[...] = kpe[:, :MLA_ROPE]
    kcat = jnp.concatenate([ckv, kpe], axis=1)
    kcat_ref[...] = jnp.where(lax.broadcasted_iota(I32, kcat.shape, 1) == ONES_LANE, 1.0, kcat).astype(BF)
    kit = seg(off + LANES, 256)
    dki_ref[...] = kit[:, :IDX_DIM]
    kit_ref[...] = kit.astype(BF)
    dwi_ref[...] = seg(off + LANES + 256, LANES)[:, :IDX_HEADS]


def _proj_call(x, w_mix, kvg, qg, wn, wp, wuk, cos_k, sin_k, tm):
    t, d = x.shape
    f32_w = [128, MLA_ROPE, 256, 256, 256, 256, 256, 256, IDX_DIM, IDX_HEADS]
    bf_w = [4 * 256, 256] + [256] * len(_WIDE) + [256]
    row = lambda w: pl.BlockSpec((tm, w), lambda i: (i, 0))
    const = lambda a: pl.BlockSpec(a.shape, lambda i: (0, 0))
    return pl.pallas_call(
        _proj_kernel, grid=(t // tm,),
        in_specs=[row(d), const(w_mix), const(kvg), const(qg), const(wn), const(wp), const(wuk),
                  row(LANES), row(LANES)],
        out_specs=[row(w) for w in f32_w + bf_w],
        out_shape=([jax.ShapeDtypeStruct((t, w), F32) for w in f32_w]
                   + [jax.ShapeDtypeStruct((t, w), BF) for w in bf_w]),
        compiler_params=_cparams(("parallel",)), name="mix_proj")(x, w_mix, kvg, qg, wn, wp, wuk, cos_k, sin_k)


def _merge_kernel(x_ref, oa_ref, ob_ref, oc_ref, od_ref, wg_ref, wbr_ref, wout_ref, g_ref, b_ref, o_ref):
    x = x_ref[...]
    xb = x.astype(BF)
    acc = None
    for n, o_n in enumerate((oa_ref, ob_ref, oc_ref, od_ref)):
        gate = jax.nn.sigmoid(jnp.dot(xb, wg_ref[:, n * D_MODEL:(n + 1) * D_MODEL],
                                      preferred_element_type=F32))
        br = jnp.dot(o_n[...].astype(BF), wbr_ref[n], preferred_element_type=F32)
        acc = gate * br if acc is None else acc + gate * br
    mix = jnp.dot(acc.astype(BF), wout_ref[...], preferred_element_type=F32)
    o_ref[...] = _layer_norm(DN_ALPHA * x + mix, g_ref[...], b_ref[...])


def _merge_call(x, o_a, o_b, o_c, o_d, wg, wbr, wout, g, b, tm):
    t, d = x.shape
    row = lambda w: pl.BlockSpec((tm, w), lambda i: (i, 0))
    return pl.pallas_call(
        _merge_kernel, grid=(t // tm,),
        in_specs=[row(d), row(BR_W), row(BR_W), row(BR_W), row(BR_W),
                  pl.BlockSpec((d, N_BRANCH * d), lambda i: (0, 0)),
                  pl.BlockSpec((N_BRANCH, BR_W, d), lambda i: (0, 0, 0)),
                  pl.BlockSpec((d, d), lambda i: (0, 0)),
                  pl.BlockSpec((1, d), lambda i: (0, 0)), pl.BlockSpec((1, d), lambda i: (0, 0))],
        out_specs=row(d), out_shape=jax.ShapeDtypeStruct((t, d), F32),
        compiler_params=_cparams(("parallel",)), name="merge")(x, o_a, o_b, o_c, o_d, wg, wbr, wout, g, b)


def _ffn_kernel(x_ref, w1_ref, b1_ref, w2_ref, b2_ref, g_ref, b_ref, o_ref):
    x = x_ref[...]
    xb = x.astype(BF)
    acc = None
    for c in range(D_FF // D_MODEL):
        sl = slice(c * D_MODEL, (c + 1) * D_MODEL)
        h = jnp.dot(xb, w1_ref[:, sl], preferred_element_type=F32) + b1_ref[:, sl]
        h = jnp.square(jnp.maximum(h, 0.0))
        y = jnp.dot(h.astype(BF), w2_ref[sl, :], preferred_element_type=F32)
        acc = y if acc is None else acc + y
    o_ref[...] = _layer_norm(DN_ALPHA * x + acc + b2_ref[...], g_ref[...], b_ref[...])


def _ffn_call(x, w1, b1, w2, b2, g, b, tm):
    t, d = x.shape
    row = pl.BlockSpec((tm, d), lambda i: (i, 0))
    vec = lambda w: pl.BlockSpec((1, w), lambda i: (0, 0))
    return pl.pallas_call(
        _ffn_kernel, grid=(t // tm,),
        in_specs=[row, pl.BlockSpec((d, D_FF), lambda i: (0, 0)), vec(D_FF),
                  pl.BlockSpec((D_FF, d), lambda i: (0, 0)), vec(d), vec(d), vec(d)],
        out_specs=row, out_shape=jax.ShapeDtypeStruct((t, d), F32),
        compiler_params=_cparams(("parallel",)), name="ffn")(x, w1, b1, w2, b2, g, b)


def _t5_bucket(rel):
    nb = T5_BUCKETS // 2
    max_exact = nb // 2
    n = jnp.abs(rel)
    nf = jnp.maximum(n, 1).astype(jnp.float32)
    large = max_exact + (jnp.log(nf / max_exact) / math.log(T5_MAX_DIST / max_exact)
                         * (nb - max_exact)).astype(jnp.int32)
    large = jnp.minimum(large, nb - 1)
    return jnp.where(rel > 0, nb, 0) + jnp.where(n < max_exact, n, large)


def _bucket_ids(rel0s, rows, cols, valid_cols, qpos0_for_mask=None):
    r = jnp.arange(rows, dtype=I32)[:, None]
    c = jnp.arange(cols, dtype=I32)[None, :]
    ok = jnp.broadcast_to(c < valid_cols, (rows, cols))
    if qpos0_for_mask is not None:
        ok = ok & (((qpos0_for_mask - N_META + c) >> 6) <= ((qpos0_for_mask - N_META + r) >> 6))
    return jnp.stack([jnp.where(ok, _t5_bucket(I32(rel0) + c - r), -1) for rel0 in rel0s], axis=0)


def _bias_kernel(tab_ref, bk_ref, o_ref):
    bk = bk_ref[0]
    for h in range(DIFF_HEADS + DSA_HEADS):
        acc = jnp.where(bk < 0, NEG_INF, 0.0).astype(F32)
        for b in range(T5_BUCKETS):
            acc = jnp.where(bk == b, tab_ref[b, h], acc)
        o_ref[0, h] = acc


def _bias_call(rel_bias, bucket_ids):
    n, rows, cols = bucket_ids.shape
    nh = DIFF_HEADS + DSA_HEADS
    return pl.pallas_call(
        _bias_kernel, grid=(n,),
        in_specs=[pl.BlockSpec(memory_space=pltpu.SMEM),
                  pl.BlockSpec((1, rows, cols), lambda i: (i, 0, 0))],
        out_specs=pl.BlockSpec((1, nh, rows, cols), lambda i: (i, 0, 0, 0)),
        out_shape=jax.ShapeDtypeStruct((n, nh, rows, cols), F32),
        compiler_params=_cparams(("parallel",)), name="rel_bias_table")(rel_bias, bucket_ids)


def _bias_tables(geom, rel_bias):
    tq, kb = geom.tq, KEY_BLOCK
    own_cols = tq if geom.own_wide else NARROW
    own = _bias_call(rel_bias, _bucket_ids([0], tq, own_cols, tq, geom.qpos0))
    if geom.npast is None:
        assert tq == kb and kb + 1 >= T5_MAX_DIST
        assert geom.qpos0 + tq - (geom.n_meta - 1) >= T5_MAX_DIST
        past = _bias_call(rel_bias, _bucket_ids([-kb, -2 * kb], tq, kb, kb))
        meta = _bias_call(rel_bias, _bucket_ids([-geom.qpos0, -geom.qpos0 - tq], tq, NARROW, geom.n_meta))
        mown = _bias_call(rel_bias, _bucket_ids([0], geom.n_meta, NARROW, geom.n_meta, 0))
        return own, past, meta, mown
    past = _bias_call(rel_bias, _bucket_ids([N_META + j * kb - geom.qpos0 for j in range(geom.npast)], tq, kb, kb))
    return own, past, None, None


def _split_tables(tables, lo, hi):
    return [None if t is None else t[:, lo:hi] for t in tables]


def _no_past(npast):
    return isinstance(npast, int) and npast == 0


STATIC_UNROLL = 4


def _aligned(x, m):
    return x if isinstance(x, int) else pl.multiple_of(x, m)


def _fold_blocks(count, fn, init):
    if isinstance(count, int) and count <= STATIC_UNROLL:
        for j in range(count):
            init = fn(j, init)
        return init
    carry = lax.fori_loop(0, count // 2, lambda t, c: fn(2 * t + 1, fn(2 * t, c)), init)
    return lax.cond(count % 2 == 1, lambda c: fn(count - 1, c), lambda c: c, carry)


def _past_loop(npast, body):
    _fold_blocks(npast, body, 0)


def _tile_walk(geom, frames_tile, meta_tile):
    i = pl.program_id(1)

    @pl.when(i < geom.ntile)
    def _():
        q0 = _aligned(geom.row0 + i * geom.tq, 16)
        npast = i if geom.npast is None else geom.npast
        frames_tile(q0, geom.tq, geom.qpos0 + i * geom.tq, npast)

    if geom.n_meta:
        @pl.when(i == geom.ntile)
        def _():
            meta_tile(0, geom.n_meta, 0)


def _own_mask(qpos, n, cols, causal):
    r = lax.broadcasted_iota(I32, (n, 1), 0)
    c = lax.broadcasted_iota(I32, (1, cols), 1)
    if causal:
        return c < r
    return (c < n) & (((qpos - N_META + c) >> 6) <= ((qpos - N_META + r) >> 6))


def _slab_spec(arr):
    return pl.BlockSpec((1,) + arr.shape[1:], lambda b, i: (b, 0, 0))


def _cache_spec(arr, layer):
    return pl.BlockSpec((1, 1) + arr.shape[2:], lambda b, i: (layer, b, 0, 0))


def _past_spec(arr, layer):
    return _cache_spec(arr, layer) if arr.ndim == 4 else _slab_spec(arr)


def _const_spec(arr):
    nd = arr.ndim
    return pl.BlockSpec(arr.shape, lambda b, i: (0,) * nd)


def _readers(geom, new_refs, past_refs):
    def new(k, r0, n):
        return new_refs[k][0, pl.ds(r0, n), :]

    def past(k, j):
        if geom.npast is None:
            return new_refs[k][0, pl.ds(_aligned(geom.row0 + j * KEY_BLOCK, 16), KEY_BLOCK), :]
        ref, r0 = past_refs[k], _aligned(j * KEY_BLOCK, KEY_BLOCK)
        return ref[0, 0, pl.ds(r0, KEY_BLOCK), :] if len(ref.shape) == 4 else ref[0, pl.ds(r0, KEY_BLOCK), :]
    return new, past


def _fill_narrow(dst_ref, rows):
    dst_ref[...] = jnp.zeros(dst_ref.shape, dst_ref.dtype)
    dst_ref[0:rows.shape[0], :] = rows.astype(dst_ref.dtype)


def _attn_grid(geom, bsz):
    return (bsz, geom.ntile + (1 if geom.n_meta else 0))


def _maps_dot(lhs, n, w, rhs, transpose_rhs=False):
    dot = _dot_nt if transpose_rhs else _dot
    if isinstance(lhs, (list, tuple)):
        return [dot(x, rhs) for x in lhs]
    nmaps = lhs.shape[0]
    if n % LANES == 0:
        out = dot(lhs[:, 0:n, 0:w].reshape(nmaps * n, w), rhs)
        return [out[p * n:(p + 1) * n] for p in range(nmaps)]
    return [dot(lhs[p, 0:n, 0:w], rhs) for p in range(nmaps)]


class Softmax:
    NREFS = 7

    def __init__(self, refs, nmaps, n, ones_lane=None):
        self.sw, self.sn, self.mx, self.m, self.ls, self.acc, self.e = refs
        self.nmaps, self.n, self.ones_lane = nmaps, n, ones_lane
        self.fresh_scores = self.fresh_values = True

    @staticmethod
    def scratch(geom, nmaps, vw):
        return [pltpu.VMEM((nmaps, geom.nslot, geom.tq, KEY_BLOCK), F32),
                pltpu.VMEM((nmaps, 2, geom.tq, NARROW), F32),
                pltpu.VMEM((nmaps, geom.tq, LANES), F32), pltpu.VMEM((nmaps, geom.tq, LANES), F32),
                pltpu.VMEM((nmaps, geom.tq, LANES), F32), pltpu.VMEM((nmaps, geom.tq, vw), F32),
                pltpu.VMEM((nmaps, geom.tq, KEY_BLOCK), BF)]

    def _store(self, p, slot, wide, s):
        if wide:
            self.sw[p, slot, 0:self.n, :] = s
        else:
            self.sn[p, slot, 0:self.n, :] = s

    def _load(self, p, slot, wide):
        return self.sw[p, slot, 0:self.n, :] if wide else self.sn[p, slot, 0:self.n, :]

    def scores(self, slot, wide, q_maps, k_b, scale, bias_of=None, extra=None):
        n = self.n
        d = q_maps[0].shape[-1] if isinstance(q_maps, (list, tuple)) else q_maps.shape[2]
        s_all = _maps_dot(q_maps, n, d, k_b, transpose_rhs=True)
        for p in range(self.nmaps):
            s = s_all[p]
            if scale != 1.0:
                s = s * scale
            if bias_of is not None:
                s = s + bias_of(p)
            if extra is not None:
                s = s + extra
            self._store(p, slot, wide, s)
            top = _halves(s, jnp.maximum)
            self.mx[p, 0:n, :] = top if self.fresh_scores else jnp.maximum(self.mx[p, 0:n, :], top)
        self.fresh_scores = False

    def finish_max(self):
        n = self.n
        for p in range(self.nmaps):
            m = jnp.max(self.mx[p, 0:n, :], axis=1, keepdims=True)
            self.m[p, 0:n, :] = jnp.broadcast_to(m, (n, LANES))

    def values(self, slot, wide, v_b):
        n, w = self.n, (KEY_BLOCK if wide else NARROW)
        staged = n % LANES == 0
        es = []
        for p in range(self.nmaps):
            s = self._load(p, slot, wide)
            e = jnp.exp(s - _rep(self.m[p, 0:n, :], w))
            if self.ones_lane is None:
                part = _halves(e, jnp.add)
                self.ls[p, 0:n, :] = part if self.fresh_values else self.ls[p, 0:n, :] + part
            if staged:
                self.e[p, 0:n, 0:w] = e.astype(BF)
            else:
                es.append(e)
        for p, pv in enumerate(_maps_dot(self.e if staged else es, n, w, v_b)):
            self.acc[p, 0:n, :] = pv if self.fresh_values else self.acc[p, 0:n, :] + pv
        self.fresh_values = False

    def result(self, p):
        n = self.n
        acc = self.acc[p, 0:n, :]
        if self.ones_lane is not None:
            return acc / acc[:, self.ones_lane:self.ones_lane + 1]
        return acc / jnp.sum(self.ls[p, 0:n, :], axis=1, keepdims=True)


def _mla_kernel(geom, *refs):
    n_past = 0 if geom.npast is None else 1
    (q_ref, kc_ref), refs = refs[:2], refs[2:]
    past_refs, refs = refs[:n_past], refs[n_past:]
    (wuv_ref, o_ref), refs = refs[:2], refs[2:]
    sm_refs, (qm_s, kn_s, km_s) = refs[:Softmax.NREFS], refs[Softmax.NREFS:]
    new, past = _readers(geom, (kc_ref,), past_refs)
    scale = 1.0

    def tile(q0, n, qpos, npast, with_meta, own_wide):
        sm = Softmax(sm_refs, MLA_HEADS, n, ones_lane=ONES_LANE)
        q_heads = lambda: [q_ref[0, pl.ds(q0, n), 2 * LANES * h:2 * LANES * (h + 1)] for h in range(MLA_HEADS)]
        if n % LANES == 0:
            for h, q_h in enumerate(q_heads()):
                qm_s[h, 0:n, :] = q_h
        q_src = lambda: qm_s if n % LANES == 0 else q_heads()
        own_slot = npast if own_wide else 0
        if own_wide:
            own_k = lambda: new(0, q0, n)
        else:
            _fill_narrow(kn_s, new(0, q0, n))
            own_k = lambda: kn_s[...]
        own_cols = n if own_wide else NARROW
        sm.scores(own_slot, own_wide, q_src(), own_k(), scale,
                  extra=jnp.where(_own_mask(qpos, n, own_cols, False), 0.0, NEG_INF))

        def p1(j, carry):
            sm.scores(j, True, q_src(), past(0, j), scale)
            return carry
        _past_loop(npast, p1)
        if with_meta:
            _fill_narrow(km_s, new(0, 0, geom.n_meta))
            pad = jnp.where(lax.broadcasted_iota(I32, (1, NARROW), 1) < geom.n_meta, 0.0, NEG_INF)
            sm.scores(1, False, q_src(), km_s[...], scale, extra=pad)
        sm.finish_max()

        sm.values(own_slot, own_wide, own_k())

        def p2(j, carry):
            sm.values(j, True, past(0, j))
            return carry
        _past_loop(npast, p2)
        if with_meta:
            sm.values(1, False, km_s[...])

        out = None
        ohead = _lane_group((n, BR_W), MLA_V)
        for h in range(MLA_HEADS):
            o_h = jnp.where(ohead == h, _dot(sm.result(h), wuv_ref[...]), 0.0)
            out = o_h if out is None else out + o_h
        o_ref[0, pl.ds(q0, n), :] = out

    _tile_walk(geom,
               lambda q0, n, qpos, npast: tile(q0, n, qpos, npast, bool(geom.n_meta), geom.own_wide),
               lambda q0, n, qpos: tile(q0, n, qpos, 0, False, False))


def _mla_call(geom, layer, qcat, kcat, past, wuv_pad):
    bsz = qcat.shape[0]
    ins = [qcat, kcat] + list(past) + [wuv_pad]
    specs = ([_slab_spec(a) for a in (qcat, kcat)] + [_past_spec(a, layer) for a in past] + [_const_spec(wuv_pad)])
    return pl.pallas_call(
        functools.partial(_mla_kernel, geom), grid=_attn_grid(geom, bsz),
        in_specs=specs, out_specs=pl.BlockSpec((1, geom.lq, BR_W), lambda b, i: (b, 0, 0)),
        out_shape=jax.ShapeDtypeStruct((bsz, geom.lq, BR_W), F32),
        scratch_shapes=(Softmax.scratch(geom, MLA_HEADS, 2 * LANES)
                        + [pltpu.VMEM((MLA_HEADS, geom.tq, 2 * LANES), BF)] + [pltpu.VMEM((NARROW, 2 * LANES), BF)] * 2),
        compiler_params=_cparams(("parallel", "arbitrary")), name="mla_attn")(*ins)


def _suffix_matrix(n):
    later = lax.broadcasted_iota(I32, (n, n), 0) > lax.broadcasted_iota(I32, (n, n), 1)
    return jnp.where(later, 1.0, 0.0).astype(BF)


def _sb_kernel(geom, *refs):
    n_past = 0 if geom.npast is None else 2
    (q_ref, k_ref, v_ref), refs = refs[:3], refs[3:]
    past_refs, (o_ref, acc_s, carry_s, qm_s, hl_s, lb_s, a_s, tri_s, kn_s, vn_s, km_s, vm_s) = \
        refs[:n_past], refs[n_past:]
    new, past = _readers(geom, (k_ref, v_ref), past_refs)

    def tile(q0, n, qpos, npast, with_meta, own_wide):
        q = q_ref[0, pl.ds(q0, n), :]
        head = _lane_group(q.shape, SB_DH)
        for h in range(SB_HEADS):
            qm_s[h, 0:n, :] = jnp.where(head == h, q, jnp.zeros_like(q))
        acc_s[:, 0:n, :] = jnp.zeros((SB_HEADS, n, BR_W), F32)
        carry_s[:, 0:n, :] = jnp.zeros((SB_HEADS, n, LANES), F32)

        def block(k_b, v_b, mask, tri):
            w = k_b.shape[0]
            staged = n % LANES == 0
            z_all = _maps_dot(qm_s, n, BR_W, k_b, transpose_rhs=True)
            halves, log_bs, firsts = [], [], []
            for h in range(SB_HEADS):
                z = z_all[h]
                soft = jnp.log(1.0 + jnp.exp(-jnp.abs(z)))
                log_b = jnp.minimum(z, 0.0) - soft
                log_1m = log_b - z
                if mask is not None:
                    log_1m = jnp.where(mask, log_1m, 0.0)
                firsts.append(log_1m[:, 0:1])
                hi = log_1m.astype(BF)
                lo = (log_1m - hi.astype(F32)).astype(BF)
                if staged:
                    hl_s[2 * h, 0:n, 0:w], hl_s[2 * h + 1, 0:n, 0:w], lb_s[h, 0:n, 0:w] = hi, lo, log_b
                else:
                    halves += [hi, lo]
                    log_bs.append(log_b)
            sums_all = _maps_dot(hl_s if staged else halves, n, w, tri)
            weights = []
            for h in range(SB_HEADS):
                later = sums_all[2 * h] + sums_all[2 * h + 1]
                carry = carry_s[h, 0:n, :]
                log_b = lb_s[h, 0:n, 0:w] if staged else log_bs[h]
                a = jnp.exp(log_b + later + _rep(carry, w))
                if mask is not None:
                    a = jnp.where(mask, a, 0.0)
                if staged:
                    a_s[h, 0:n, 0:w] = a.astype(BF)
                else:
                    weights.append(a)
                carry_s[h, 0:n, :] = carry + jnp.broadcast_to(later[:, 0:1] + firsts[h], (n, LANES))
            for h, pv in enumerate(_maps_dot(a_s if staged else weights, n, w, v_b)):
                acc_s[h, 0:n, :] += pv

        if own_wide:
            block(new(0, q0, n), new(1, q0, n), _own_mask(qpos, n, n, True), _suffix_matrix(n))
        else:
            _fill_narrow(kn_s, new(0, q0, n))
            _fill_narrow(vn_s, new(1, q0, n))
            block(kn_s[...], vn_s[...], _own_mask(qpos, n, NARROW, True), _suffix_matrix(NARROW))
        if _no_past(npast):
            tri_kb = None
        elif n % LANES == 0:
            tri_s[...] = _suffix_matrix(KEY_BLOCK)
            tri_kb = lambda: tri_s[...]
        else:
            tri_value = _suffix_matrix(KEY_BLOCK)
            tri_kb = lambda: tri_value

        def past_block(jj, carry):
            j = npast - 1 - jj
            block(past(0, j).astype(BF), past(1, j).astype(BF), None, tri_kb())
            return carry
        _past_loop(npast, past_block)
        if with_meta:
            m0 = 0
            _fill_narrow(km_s, new(0, m0, geom.n_meta))
            _fill_narrow(vm_s, new(1, m0, geom.n_meta))
            pad = jnp.broadcast_to(lax.broadcasted_iota(I32, (1, NARROW), 1) < geom.n_meta, (n, NARROW))
            block(km_s[...], vm_s[...], pad, _suffix_matrix(NARROW))

        out = None
        for h in range(SB_HEADS):
            o_h = jnp.where(head == h, acc_s[h, 0:n, :], 0.0)
            out = o_h if out is None else out + o_h
        o_ref[0, pl.ds(q0, n), :] = out

    _tile_walk(geom,
               lambda q0, n, qpos, npast: tile(q0, n, qpos, npast, bool(geom.n_meta), geom.own_wide),
               lambda q0, n, qpos: tile(q0, n, qpos, 0, False, False))


def _sb_call(geom, layer, q, k, v, past):
    bsz = q.shape[0]
    ins = [q, k, v] + list(past)
    specs = [_slab_spec(a) for a in (q, k, v)] + [_past_spec(a, layer) for a in past]
    return pl.pallas_call(
        functools.partial(_sb_kernel, geom), grid=_attn_grid(geom, bsz),
        in_specs=specs, out_specs=pl.BlockSpec((1, geom.lq, BR_W), lambda b, i: (b, 0, 0)),
        out_shape=jax.ShapeDtypeStruct((bsz, geom.lq, BR_W), F32),
        scratch_shapes=[pltpu.VMEM((SB_HEADS, geom.tq, BR_W), F32), pltpu.VMEM((SB_HEADS, geom.tq, LANES), F32),
                        pltpu.VMEM((SB_HEADS, geom.tq, BR_W), BF),
                        pltpu.VMEM((2 * SB_HEADS, geom.tq, KEY_BLOCK), BF), pltpu.VMEM((SB_HEADS, geom.tq, KEY_BLOCK), F32),
                        pltpu.VMEM((SB_HEADS, geom.tq, KEY_BLOCK), BF), pltpu.VMEM((KEY_BLOCK, KEY_BLOCK), BF)]
                       + [pltpu.VMEM((NARROW, BR_W), BF)] * 4,
        compiler_params=_cparams(("parallel", "arbitrary")), name="sb_attn")(*ins)


def _bias_readers(geom, own_ref, past_ref, meta_ref, mown_ref, tile_idx):
    def own(h, n):
        return own_ref[0, h, 0:n, :]

    def earlier(j, h):
        slot = j if geom.npast is not None else jnp.where(j == tile_idx - 1, 0, 1)
        return past_ref[slot, h]

    def meta(h):
        return meta_ref[jnp.minimum(tile_idx, 1), h]

    def meta_own(h):
        return mown_ref[0, h]
    return own, earlier, meta, meta_own


def _diff_kernel(geom, lam_init, *refs):
    n_past = 0 if geom.npast is None else 2
    n_bias = 4 if geom.n_meta else 2
    (q_ref, k_ref, v_ref), refs = refs[:3], refs[3:]
    past_refs, refs = refs[:n_past], refs[n_past:]
    bias_refs, refs = list(refs[:n_bias]) + [None] * (4 - n_bias), refs[n_bias:]
    (lam_ref, sg_ref, o_ref), refs = refs[:3], refs[3:]
    sm_refs, (qm_s, kn_s, vn_s, km_s, vm_s) = refs[:Softmax.NREFS], refs[Softmax.NREFS:]
    new, past = _readers(geom, (k_ref, v_ref), past_refs)
    b_own, b_earlier, b_meta, b_meta_own = _bias_readers(geom, *bias_refs, pl.program_id(1))
    scale = 1.0
    npair = 2 * DIFF_HEADS

    def tile(q0, n, qpos, npast, with_meta, own_wide, is_meta):
        sm = Softmax(sm_refs, npair, n)
        q = q_ref[0, pl.ds(q0, n), :]
        pair = _lane_group(q.shape, DIFF_DQK)
        for p in range(npair):
            qm_s[p, 0:n, :] = jnp.where(pair == p, q, jnp.zeros_like(q))
        q_of = qm_s
        own_slot = npast if own_wide else 0
        if own_wide:
            own_k, own_v = (lambda: new(0, q0, n)), (lambda: new(1, q0, n))
        else:
            _fill_narrow(kn_s, new(0, q0, n))
            _fill_narrow(vn_s, new(1, q0, n))
            own_k, own_v = (lambda: kn_s[...]), (lambda: vn_s[...])
        sm.scores(own_slot, own_wide, q_of, own_k(), scale,
                  bias_of=(lambda p: b_meta_own(p // 2)) if is_meta else (lambda p: b_own(p // 2, n)))

        def p1(j, carry):
            sm.scores(j, True, q_of, past(0, j), scale,
                      bias_of=lambda p: b_earlier(j, p // 2))
            return carry
        _past_loop(npast, p1)
        if with_meta:
            m0 = 0
            _fill_narrow(km_s, new(0, m0, geom.n_meta))
            _fill_narrow(vm_s, new(1, m0, geom.n_meta))
            sm.scores(1, False, q_of, km_s[...], scale, bias_of=lambda p: b_meta(p // 2))
        sm.finish_max()

        sm.values(own_slot, own_wide, own_v().astype(BF))

        def p2(j, carry):
            sm.values(j, True, past(1, j).astype(BF))
            return carry
        _past_loop(npast, p2)
        if with_meta:
            sm.values(1, False, vm_s[...])

        lp = lam_ref[...]
        lam = (jnp.exp(jnp.sum(lp[0:1] * lp[1:2], axis=1, keepdims=True))
               - jnp.exp(jnp.sum(lp[2:3] * lp[3:4], axis=1, keepdims=True)) + lam_init)
        head = _lane_group((n, BR_W), DIFF_DV)
        out = None
        for h in range(DIFF_HEADS):
            o_h = jnp.where(head == h, sm.result(2 * h) - lam * sm.result(2 * h + 1), 0.0)
            ms = jnp.sum(o_h * o_h, axis=1, keepdims=True) * (1.0 / DIFF_DV)
            o_h = o_h * lax.rsqrt(ms + RMS_EPS) * sg_ref[...] * (1.0 - lam_init)
            out = o_h if out is None else out + o_h
        o_ref[0, pl.ds(q0, n), :] = out

    _tile_walk(geom,
               lambda q0, n, qpos, npast: tile(q0, n, qpos, npast, bool(geom.n_meta), geom.own_wide, False),
               lambda q0, n, qpos: tile(q0, n, qpos, 0, False, False, True))


def _diff_call(geom, layer, q, k, v, past, tables, lam_p, sg_tiled):
    bsz = q.shape[0]
    lam_init = 0.8 - 0.6 * math.exp(-0.3 * layer)
    tables = [t for t in tables if t is not None]
    ins = [q, k, v] + list(past) + tables + [lam_p, sg_tiled]
    specs = ([_slab_spec(a) for a in (q, k, v)] + [_past_spec(a, layer) for a in past]
             + [_const_spec(a) for a in tables + [lam_p, sg_tiled]])
    npair = 2 * DIFF_HEADS
    return pl.pallas_call(
        functools.partial(_diff_kernel, geom, lam_init), grid=_attn_grid(geom, bsz),
        in_specs=specs, out_specs=pl.BlockSpec((1, geom.lq, BR_W), lambda b, i: (b, 0, 0)),
        out_shape=jax.ShapeDtypeStruct((bsz, geom.lq, BR_W), F32),
        scratch_shapes=(Softmax.scratch(geom, npair, BR_W) + [pltpu.VMEM((npair, geom.tq, BR_W), BF)]
                        + [pltpu.VMEM((NARROW, BR_W), BF)] * 4),
        compiler_params=_cparams(("parallel", "arbitrary")), name="diff_attn")(*ins)


def _sortable(x):
    b = lax.bitcast_convert_type(x + 0.0, I32)
    return b ^ ((b >> 31) & I32(0x7FFFFFFF))


def _dsa_kernel(geom, *refs):
    n_past = 0 if geom.npast is None else 3
    n_bias = 4 if geom.n_meta else 2
    (q_ref, qi_ref, wi_ref, k_ref, v_ref, kit_ref), refs = refs[:6], refs[6:]
    past_refs, refs = refs[:n_past], refs[n_past:]
    bias_refs, refs = list(refs[:n_bias]) + [None] * (4 - n_bias), refs[n_bias:]
    o_ref, refs = refs[0], refs[1:]
    sm_refs, (qm_s, qim_s, wib_s, kw_s, kn_s, last_s,
              kno_s, vno_s, kio_s, knm_s, vnm_s, kim_s) = refs[:Softmax.NREFS], refs[Softmax.NREFS:]
    new, past = _readers(geom, (k_ref, v_ref, kit_ref), past_refs)
    b_own, b_earlier, b_meta, b_meta_own = _bias_readers(geom, *bias_refs, pl.program_id(1))
    topk = float(geom.topk)
    past_pos0 = N_META
    index_bits = max(1, int(geom.lq + geom.lp + N_META).bit_length())
    ones_count = jnp.ones((LANES, LANES), BF)

    def tile(q0, n, qpos, npast, with_meta, own_wide, is_meta):
        nm = geom.n_meta
        m0 = 0
        own_slot = npast if own_wide else 0
        own_cols = n if own_wide else NARROW

        qi = qi_ref[0, pl.ds(q0, n), :]
        igrp = _lane_group(qi.shape, IDX_DIM)
        wi = wi_ref[0, pl.ds(q0, n), :] * (IDX_HEADS ** -0.5 * IDX_DIM ** -0.5)
        for h in range(IDX_HEADS):
            qim_s[h, 0:n, :] = jnp.where(igrp == h, qi, jnp.zeros_like(qi))
            wib_s[h, 0:n, :] = jnp.broadcast_to(wi[:, h:h + 1], (n, LANES))
        q = q_ref[0, pl.ds(q0, n), :]
        head = _lane_group(q.shape, DSA_DH)
        for h in range(DSA_HEADS):
            qm_s[h, 0:n, :] = jnp.where(head == h, q, jnp.zeros_like(q))
        if own_wide:
            own_k, own_v, own_ki = (lambda: new(0, q0, n)), (lambda: new(1, q0, n)), (lambda: new(2, q0, n))
        else:
            _fill_narrow(kno_s, new(0, q0, n))
            _fill_narrow(vno_s, new(1, q0, n))
            _fill_narrow(kio_s, new(2, q0, n))
            own_k, own_v, own_ki = (lambda: kno_s[...]), (lambda: vno_s[...]), (lambda: kio_s[...])
        if with_meta:
            _fill_narrow(knm_s, new(0, m0, nm))
            _fill_narrow(vnm_s, new(1, m0, nm))
            _fill_narrow(kim_s, new(2, m0, nm))

        def index_keys(kit_b):
            raw = _maps_dot(qim_s, n, BR_W, kit_b, transpose_rhs=True)
            score = None
            for h in range(IDX_HEADS):
                t = _rep(wib_s[h, 0:n, :], kit_b.shape[0]) * jnp.maximum(raw[h], 0.0)
                score = t if score is None else score + t
            return _sortable(score)

        own_keys = jnp.where(_own_mask(qpos, n, own_cols, False), index_keys(own_ki()), I32(KEY_NEG))
        if own_wide:
            kw_s[own_slot, 0:n, :] = own_keys
        else:
            pad = lax.broadcasted_iota(I32, (1, NARROW), 1) < n
            kn_s[0, 0:n, :] = jnp.where(pad, own_keys, I32(INT_MIN))

        def score_block(j, carry):
            kw_s[j, 0:n, :] = index_keys(past(2, j).astype(BF))
            return carry
        _past_loop(npast, score_block)
        if with_meta:
            pad = lax.broadcasted_iota(I32, (1, NARROW), 1) < nm
            kn_s[1, 0:n, :] = jnp.where(pad, index_keys(kim_s[...]), I32(INT_MIN))
        nwide = npast + 1 if own_wide else npast

        def wide_sweep(init, fn):
            return _fold_blocks(nwide, fn, init)

        def one(cond):
            return jnp.where(cond, 1.0, 0.0)

        def partial_counts(r0, nr, wide_fn, own_narrow_fn, meta_fn):
            part = wide_sweep(jnp.zeros((nr, LANES), F32), lambda j, a: a + wide_fn(kw_s[j, r0:r0 + nr, :], j))
            if not own_wide:
                part = part + own_narrow_fn(kn_s[0, r0:r0 + nr, :])
            if with_meta:
                part = part + meta_fn(kn_s[1, r0:r0 + nr, :])
            return part

        def lane_sums(part):
            return jnp.dot(part.astype(BF), ones_count, preferred_element_type=F32)

        def count(wide_fn, own_narrow_fn, meta_fn):
            return lane_sums(partial_counts(0, n, wide_fn, own_narrow_fn, meta_fn))

        def count_cmp(cmp):
            return count(lambda x, j: _halves(one(cmp(x, KEY_BLOCK)), jnp.add),
                         lambda x: one(cmp(x, NARROW)), lambda x: one(cmp(x, NARROW)))

        def ge_part(r0, nr, cand):
            return partial_counts(r0, nr, lambda x, j: _halves(one(x >= _rep(cand, KEY_BLOCK)), jnp.add),
                                  lambda x: one(x >= cand), lambda x: one(x >= cand))

        def sign_step(r0, nr):
            c0 = lane_sums(ge_part(r0, nr, jnp.zeros((nr, LANES), I32)))
            return jnp.where(c0 >= topk, I32(0), I32(INT_MIN)), jnp.where(c0 >= topk, c0, topk + 1.0)

        def accept(thr, cnt, cand, c):
            keep = c >= topk
            return jnp.where(keep, cand, thr), jnp.where(keep, c, cnt)

        if is_meta and geom.n_meta <= geom.topk:
            thr = jnp.full((n, LANES), INT_MIN, I32)
            cnt = jnp.zeros((n, LANES), F32)
        else:
            def bit_step(b, state):
                thr, cnt = state
                cand = thr | jnp.left_shift(I32(1), I32(30) - b)
                return accept(thr, cnt, cand, lane_sums(ge_part(0, n, cand)))
            thr, cnt = lax.fori_loop(0, 31, bit_step, sign_step(0, n))

        last_s[0:n, :] = jnp.full((n, LANES), INT_MAX, I32)
        has_ties = jnp.max(one((cnt > topk) & (thr > I32(KEY_NEG)))) > 0.0

        @pl.when(has_ties)
        def _():
            need = topk - count_cmp(lambda x, w: x > _rep(thr, w))
            colw = lax.broadcasted_iota(I32, (1, KEY_BLOCK), 1)
            coln = lax.broadcasted_iota(I32, (1, NARROW), 1)

            def wide_pos0(j):
                return past_pos0 + j * KEY_BLOCK

            def ties_before(x):
                return count(lambda kk, j: _halves(one((kk == _rep(thr, KEY_BLOCK))
                                                       & (colw < _rep(x, KEY_BLOCK) - wide_pos0(j))), jnp.add),
                             lambda kk: one((kk == thr) & (coln < x - qpos)),
                             lambda kk: one((kk == thr) & (coln < x)))

            def pos_step(b, last_pos):
                cand = last_pos + jnp.left_shift(I32(1), I32(index_bits - 1) - b)
                return jnp.where(ties_before(cand) < need, cand, last_pos)
            last_s[0:n, :] = lax.fori_loop(0, index_bits, pos_step, jnp.zeros((n, LANES), I32))

        last_pos = last_s[0:n, :]

        def sel_bias(kk, kpos0):
            w = kk.shape[1]
            cols = lax.broadcasted_iota(I32, (1, w), 1)
            t = _rep(thr, w)
            sel = (kk > t) | ((kk == t) & (cols <= _rep(last_pos, w) - kpos0))
            return jnp.where(sel, 0.0, NEG_INF)

        sm = Softmax(sm_refs, DSA_HEADS, n)
        q_of = qm_s
        own_kk = kw_s[own_slot, 0:n, :] if own_wide else kn_s[0, 0:n, :]
        sm.scores(own_slot, own_wide, q_of, own_k(), 1.0,
                  bias_of=(lambda h: b_meta_own(h)) if is_meta else (lambda h: b_own(h, n)),
                  extra=sel_bias(own_kk, qpos))

        def p1(j, carry):
            sm.scores(j, True, q_of, past(0, j).astype(BF), 1.0,
                      bias_of=lambda h: b_earlier(j, h),
                      extra=sel_bias(kw_s[j, 0:n, :], past_pos0 + j * KEY_BLOCK))
            return carry
        _past_loop(npast, p1)
        if with_meta:
            sm.scores(1, False, q_of, knm_s[...], 1.0, bias_of=lambda h: b_meta(h),
                      extra=sel_bias(kn_s[1, 0:n, :], 0))
        sm.finish_max()

        sm.values(own_slot, own_wide, own_v().astype(BF))

        def p2(j, carry):
            sm.values(j, True, past(1, j).astype(BF))
            return carry
        _past_loop(npast, p2)
        if with_meta:
            sm.values(1, False, vnm_s[...])

        out = None
        for h in range(DSA_HEADS):
            o_h = jnp.where(head == h, sm.result(h), 0.0)
            out = o_h if out is None else out + o_h
        o_ref[0, pl.ds(q0, n), :] = out

    _tile_walk(geom,
               lambda q0, n, qpos, npast: tile(q0, n, qpos, npast, bool(geom.n_meta), geom.own_wide, False),
               lambda q0, n, qpos: tile(q0, n, qpos, 0, False, False, True))


def _dsa_call(geom, layer, q, qi, wi, k, v, kit, past, tables):
    bsz = q.shape[0]
    tables = [t for t in tables if t is not None]
    ins = [q, qi, wi, k, v, kit] + list(past) + tables
    specs = ([_slab_spec(a) for a in (q, qi, wi, k, v, kit)] + [_past_spec(a, layer) for a in past]
             + [_const_spec(a) for a in tables])
    tq = geom.tq
    return pl.pallas_call(
        functools.partial(_dsa_kernel, geom), grid=_attn_grid(geom, bsz),
        in_specs=specs, out_specs=pl.BlockSpec((1, geom.lq, BR_W), lambda b, i: (b, 0, 0)),
        out_shape=jax.ShapeDtypeStruct((bsz, geom.lq, BR_W), F32),
        scratch_shapes=(Softmax.scratch(geom, DSA_HEADS, BR_W)
                        + [pltpu.VMEM((DSA_HEADS, tq, BR_W), BF), pltpu.VMEM((IDX_HEADS, tq, BR_W), BF),
                           pltpu.VMEM((IDX_HEADS, tq, LANES), F32),
                           pltpu.VMEM((geom.nslot, tq, KEY_BLOCK), I32), pltpu.VMEM((2, tq, NARROW), I32),
                           pltpu.VMEM((tq, LANES), I32)]
                        + [pltpu.VMEM((NARROW, BR_W), BF)] * 6),
        compiler_params=_cparams(("parallel", "arbitrary")), name="dsa_attn")(*ins)


def _row_tile(t):
    for tm in (768, 384, 256, 128, 64, 32, 16, 8):
        if t % tm == 0:
            return tm
    raise ValueError(f"token count {t} has no supported row tile")


def _rope_tables(pos):
    half = MLA_ROPE // 2
    inv_freq = ROPE_THETA ** (-jnp.arange(half, dtype=jnp.float32) / half)
    ang = pos.astype(jnp.float32)[:, None] * inv_freq[None, :]
    cos, sin = jnp.cos(ang), jnp.sin(ang)
    return jnp.concatenate([cos, cos], axis=1), jnp.concatenate([-sin, sin], axis=1)


def _run_group(geom, x, pos_rows, caches, weights):
    (ln_in_g, ln_in_b, w_in, mla_qnorm_g, mla_w_uq, mla_kvnorm_g, mla_w_uk, mla_w_uv, diff_lambda,
     diff_subln_g, rel_bias, w_br, w_out, ln1_g, ln1_b, w_ff1, b_ff1, w_ff2, b_ff2, ln2_g, ln2_b) = weights
    bsz, lq, d = x.shape
    t = bsz * lq
    tm = _row_tile(t)
    cos32, sin32 = _rope_tables(pos_rows)
    cos_k = jnp.tile(cos32, (bsz, MLA_HEADS))
    sin_k = jnp.tile(sin32, (bsz, MLA_HEADS))
    tables = _bias_tables(geom, rel_bias)
    diff_tables = _split_tables(tables, 0, DIFF_HEADS)
    dsa_tables = _split_tables(tables, DIFF_HEADS, DIFF_HEADS + DSA_HEADS)

    xf = _ln_call(x.reshape(t, d), ln_in_g, ln_in_b, tm)
    rows = []
    for l in range(DEPTH):
        w_uq = mla_w_uq[l].reshape(MLA_Q_LORA, MLA_HEADS, MLA_NOPE + MLA_ROPE)
        wn = w_uq[:, :, :MLA_NOPE].reshape(MLA_Q_LORA, MLA_HEADS * MLA_NOPE).astype(BF)
        wp = w_uq[:, :, MLA_NOPE:].reshape(MLA_Q_LORA, MLA_HEADS * MLA_ROPE).astype(BF)
        wuk = mla_w_uk[l].reshape(MLA_KV_LORA, MLA_HEADS * MLA_NOPE).T.astype(BF)
        wuv = jnp.pad(mla_w_uv[l].reshape(MLA_KV_LORA, MLA_HEADS * MLA_V),
                      ((0, 2 * LANES - MLA_KV_LORA), (0, 0))).astype(BF)
        (ckv, kpe, b_k, b_v, c_k, c_v, d_k, d_v, d_ki, d_wi,
         qcat, kcat, bq_b, bk_b, bv_b, cq_b, ck_b, cv_b, dq_b, dk_b, dv_b, dqi_b, kit_b) = _proj_call(
            xf, _mix_weight(w_in[l]), mla_kvnorm_g[l].reshape(1, -1), mla_qnorm_g[l].reshape(1, -1),
            wn, wp, wuk, cos_k, sin_k, tm)
        per = lambda a: a.reshape(bsz, lq, a.shape[-1])
        if caches is None:
            past = lambda *idx: []
            mla_past, kit_past = [], []
        else:
            past = lambda *idx: [caches[i] for i in idx]
            fill = jnp.zeros(caches[0].shape[1:3] + (2 * LANES - MLA_KV_LORA - MLA_ROPE,), F32).at[..., -1].set(1.0)
            mla_past = [jnp.concatenate([caches[0][l], caches[1][l], fill], axis=-1).astype(BF)]
            kit_past = [jnp.tile(caches[8][l], (1, 1, IDX_HEADS)).astype(BF)]

        o_a = _mla_call(geom, l, per(qcat), per(kcat), mla_past, wuv)
        o_b = _sb_call(geom, l, per(bq_b), per(bk_b), per(bv_b), past(2, 3))
        o_c = _diff_call(geom, l, per(cq_b), per(ck_b), per(cv_b), past(4, 5), diff_tables, diff_lambda[l],
                         jnp.tile(diff_subln_g[l], DIFF_HEADS).reshape(1, -1))
        o_d = _dsa_call(geom, l, per(dq_b), per(dqi_b), per(d_wi), per(dk_b), per(dv_b), per(kit_b),
                        past(6, 7) + kit_past, dsa_tables)

        flat = lambda a: a.reshape(t, BR_W)
        wg = w_in[l][:, IN_OFFS[15]:IN_OFFS[16]].astype(BF)
        x1 = _merge_call(xf, flat(o_a), flat(o_b), flat(o_c), flat(o_d), wg, w_br[l].astype(BF),
                         w_out[l].astype(BF), ln1_g[l].reshape(1, -1), ln1_b[l].reshape(1, -1), tm)
        xf = _ffn_call(x1, w_ff1[l].astype(BF), b_ff1[l].reshape(1, -1), w_ff2[l].astype(BF),
                       b_ff2[l].reshape(1, -1), ln2_g[l].reshape(1, -1), ln2_b[l].reshape(1, -1), tm)
        rows.append([per(a) for a in (ckv, kpe, b_k, b_v, c_k, c_v, d_k, d_v, d_ki)])
    return xf.reshape(bsz, lq, d), rows


_ROW_TRAILING = ((MLA_KV_LORA,), (MLA_ROPE,), (SB_HEADS, SB_DH), (SB_HEADS, SB_DH),
                 (DIFF_HEADS, 2, DIFF_DQK), (DIFF_HEADS, DIFF_DV), (DSA_HEADS, DSA_DH),
                 (DSA_HEADS, DSA_DH), (IDX_DIM,))


def kernel(x_prompt, x_sample, cache_mla_kv, cache_mla_pe, cache_sb_k, cache_sb_v, cache_diff_k, cache_diff_v, cache_dsa_k, cache_dsa_v, cache_dsa_kidx, meta, ln_in_g, ln_in_b, w_in, mla_qnorm_g, mla_w_uq, mla_kvnorm_g, mla_w_uk, mla_w_uv, diff_lambda, diff_subln_g, rel_bias, w_br, w_out, ln1_g, ln1_b, w_ff1, b_ff1, w_ff2, b_ff2, ln2_g, ln2_b):
    weights = (ln_in_g, ln_in_b, w_in, mla_qnorm_g, mla_w_uq, mla_kvnorm_g, mla_w_uk, mla_w_uv, diff_lambda,
               diff_subln_g, rel_bias, w_br, w_out, ln1_g, ln1_b, w_ff1, b_ff1, w_ff2, b_ff2, ln2_g, ln2_b)
    assert w_in.shape[0] == DEPTH and x_prompt.shape[2] == D_MODEL

    bsz_p, seq_p, _ = x_prompt.shape
    assert seq_p % KEY_BLOCK == 0
    meta_b = jnp.broadcast_to(meta[None].astype(x_prompt.dtype), (bsz_p, N_META, D_MODEL))
    xp = jnp.concatenate([meta_b, x_prompt], axis=1)
    pos_p = jnp.arange(N_META + seq_p, dtype=I32)
    geom_p = Geom(tq=KEY_BLOCK, ntile=seq_p // KEY_BLOCK, npast=None, n_meta=N_META, qpos0=N_META,
                  row0=N_META, lq=seq_p + N_META, lp=0, topk=min(DSA_TOPK, seq_p // 4))
    yp, rows_p = _run_group(geom_p, xp, pos_p, None, weights)
    y_prompt = yp[:, N_META:]
    p_rows = []
    for i, trailing in enumerate(_ROW_TRAILING):
        stacked = jnp.stack([r[i] for r in rows_p], axis=0)
        p_rows.append(stacked.reshape(stacked.shape[:3] + trailing))

    past_len = cache_mla_kv.shape[2]
    bsz_s, dec_seq, _ = x_sample.shape
    assert past_len % KEY_BLOCK == 0 and dec_seq % 16 == 0 and dec_seq <= NARROW
    assert past_len % CHUNK == 0 and dec_seq <= CHUNK, "new frames must share one chunk"
    caches = [c.reshape(c.shape[:3] + (-1,)) for c in
              (cache_mla_kv, cache_mla_pe, cache_sb_k, cache_sb_v, cache_diff_k, cache_diff_v,
               cache_dsa_k, cache_dsa_v, cache_dsa_kidx)]
    pos_s = N_META + past_len + jnp.arange(dec_seq, dtype=I32)
    geom_s = Geom(tq=dec_seq, ntile=1, npast=past_len // KEY_BLOCK, n_meta=0, qpos0=N_META + past_len,
                  row0=0, lq=dec_seq, lp=past_len, topk=min(DSA_TOPK, (past_len + dec_seq) // 4))
    y_sample, rows_s = _run_group(geom_s, x_sample, pos_s, caches, weights)
    s_rows = []
    for i, trailing in enumerate(_ROW_TRAILING):
        stacked = jnp.stack([r[i] for r in rows_s], axis=0)
        s_rows.append(stacked.reshape(stacked.shape[:3] + trailing))

    return (y_prompt, y_sample, *p_rows, *s_rows)
```

```python
import functools
import math
from typing import NamedTuple, Optional

import jax
import jax.numpy as jnp
import numpy as np
from jax import lax
from jax.experimental import pallas as pl
from jax.experimental.pallas import tpu as pltpu

D_MODEL = 1024
CHUNK = 64
N_META = 16
MLA_HEADS = 4
MLA_Q_LORA = 256
MLA_KV_LORA = 128
MLA_NOPE = 64
MLA_ROPE = 32
MLA_V = 64
ROPE_THETA = 10000.0
SB_HEADS = 4
SB_DH = 64
DIFF_HEADS = 4
DIFF_DQK = 32
DIFF_DV = 64
DSA_HEADS = 4
DSA_DH = 64
IDX_HEADS = 8
IDX_DIM = 32
DSA_TOPK = 256
N_BRANCH = 4
BR_W = 256
D_FF = 4 * D_MODEL
T5_BUCKETS = 32
T5_MAX_DIST = 128
LN_EPS = 1e-5
RMS_EPS = 1e-6
NEG_INF = -1e30
DEPTH = 2
DN_ALPHA = (2 * DEPTH) ** 0.25
IN_SIZES = (MLA_Q_LORA, MLA_KV_LORA, MLA_ROPE,
            SB_HEADS * SB_DH, SB_HEADS * SB_DH, SB_HEADS * SB_DH,
            DIFF_HEADS * 2 * DIFF_DQK, DIFF_HEADS * 2 * DIFF_DQK, DIFF_HEADS * DIFF_DV,
            DSA_HEADS * DSA_DH, DSA_HEADS * DSA_DH, DSA_HEADS * DSA_DH,
            IDX_HEADS * IDX_DIM, IDX_DIM, IDX_HEADS,
            N_BRANCH * D_MODEL)
IN_OFFS = tuple(int(s) for s in np.cumsum((0,) + IN_SIZES))

LANES = 128
KEY_BLOCK = 256
NARROW = LANES
VMEM_LIMIT_MB = 56

BF = jnp.bfloat16
F32 = jnp.float32
I32 = jnp.int32

_NEG_BITS = int(np.float32(NEG_INF).view(np.int32))
KEY_NEG = _NEG_BITS ^ ((_NEG_BITS >> 31) & 0x7FFFFFFF)
INT_MIN = -(2 ** 31)
INT_MAX = 2 ** 31 - 1


def _cparams(sem):
    return pltpu.CompilerParams(dimension_semantics=sem, vmem_limit_bytes=VMEM_LIMIT_MB * 1024 * 1024)


def _dot(a, b):
    return jnp.dot(a.astype(BF), b.astype(BF), preferred_element_type=F32)


def _dot_nt(a, b):
    return lax.dot_general(a.astype(BF), b.astype(BF), (((1,), (1,)), ((), ())),
                           preferred_element_type=F32)


def _layer_norm(x, g, b):
    mu = jnp.mean(x, axis=-1, keepdims=True)
    xc = x - mu
    var = jnp.mean(xc * xc, axis=-1, keepdims=True)
    return xc * lax.rsqrt(var + LN_EPS) * g + b


def _rms_norm(x, g):
    return x * lax.rsqrt(jnp.mean(x * x, axis=-1, keepdims=True) + RMS_EPS) * g


def _rope_lanes(x, cos, sin):
    lane = lax.broadcasted_iota(I32, x.shape, 1)
    swapped = jnp.where((lane & 31) < 16, pltpu.roll(x, LANES - 16, 1), pltpu.roll(x, 16, 1))
    return x * cos + swapped * sin


def _lane_group(shape, width):
    return lax.broadcasted_iota(I32, shape, 1) // width


def _halves(x, op):
    return op(x[:, :LANES], x[:, LANES:]) if x.shape[1] == 2 * LANES else x


def _rep(x, width):
    return jnp.concatenate([x, x], axis=1) if width == 2 * LANES else x


class Geom(NamedTuple):
    tq: int
    ntile: int
    npast: Optional[int]
    n_meta: int
    qpos0: int
    row0: int
    lq: int
    lp: int
    topk: int

    @property
    def own_wide(self):
        return self.tq == KEY_BLOCK

    @property
    def nslot(self):
        return max(self.ntile if self.npast is None else self.npast, 1)


def _ln_kernel(x_ref, g_ref, b_ref, o_ref):
    o_ref[...] = _layer_norm(x_ref[...], g_ref[...], b_ref[...])


def _ln_call(x, g, b, tm):
    t, d = x.shape
    return pl.pallas_call(
        _ln_kernel, grid=(t // tm,),
        in_specs=[pl.BlockSpec((tm, d), lambda i: (i, 0)),
                  pl.BlockSpec((1, d), lambda i: (0, 0)),
                  pl.BlockSpec((1, d), lambda i: (0, 0))],
        out_specs=pl.BlockSpec((tm, d), lambda i: (i, 0)),
        out_shape=jax.ShapeDtypeStruct((t, d), F32),
        compiler_params=_cparams(("parallel",)), name="ln_in")(x, g.reshape(1, d), b.reshape(1, d))


_WIDE = (3, 4, 5, 6, 7, 8, 9, 10, 11, 12)
_WIDE_F32 = (1, 2, 4, 5, 7, 8)
_MIX_COLS = 256 + 128 + 256 * len(_WIDE) + LANES + 256 + LANES
_Q_SCALE = {0: SB_DH ** -0.5, 3: DIFF_DQK ** -0.5, 6: DSA_DH ** -0.5}
MLA_SCALE = (MLA_NOPE + MLA_ROPE) ** -0.5
ONES_LANE = 2 * LANES - 1


def _mix_weight(w_in_l):
    def seg(i, pad_to=None):
        w = w_in_l[:, IN_OFFS[i]:IN_OFFS[i + 1]]
        if pad_to is not None:
            w = jnp.pad(w, ((0, 0), (0, pad_to - w.shape[1])))
        return w
    cols = ([seg(0), seg(1)] + [seg(i) for i in _WIDE]
            + [seg(2, LANES), jnp.tile(seg(13), (1, IDX_HEADS)), seg(14, LANES)])
    return jnp.concatenate(cols, axis=1).astype(BF)


def _proj_kernel(x_ref, w_ref, kvg_ref, qg_ref, wn_ref, wp_ref, wuk_ref, cos_ref, sin_ref, *outs):
    (ckv_ref, kpe_ref, bk_ref, bv_ref, ck_ref, cv_ref, dk_ref, dv_ref, dki_ref, dwi_ref,
     qcat_ref, kcat_ref, *wide_bf) = outs
    kit_ref = wide_bf[-1]
    wide_bf = wide_bf[:-1]
    wide_f32 = dict(zip(_WIDE_F32, (bk_ref, bv_ref, ck_ref, cv_ref, dk_ref, dv_ref)))
    xb = x_ref[...].astype(BF)
    cos, sin = cos_ref[...], sin_ref[...]

    def seg(off, width):
        return jnp.dot(xb, w_ref[:, off:off + width], preferred_element_type=F32)

    qn = _rms_norm(seg(0, 256), qg_ref[...])
    nope = _dot(qn, wn_ref[...])
    pe = _rope_lanes(_dot(qn, wp_ref[...]), cos, sin)
    head = _lane_group(nope.shape, MLA_NOPE)
    lane = lax.broadcasted_iota(I32, pe.shape, 1)
    for h in range(MLA_HEADS):
        q_lat = _dot(jnp.where(head == h, nope, 0.0), wuk_ref[...]) * MLA_SCALE
        pe_h = (pe if h == 0 else pltpu.roll(pe, LANES - h * MLA_ROPE, 1)) * MLA_SCALE
        qcat_ref[:, 2 * LANES * h:2 * LANES * h + LANES] = q_lat.astype(BF)
        qcat_ref[:, 2 * LANES * h + LANES:2 * LANES * (h + 1)] = jnp.where(lane < MLA_ROPE, pe_h, 0.0).astype(BF)

    ckv = _rms_norm(seg(256, 128), kvg_ref[...])
    ckv_ref[...] = ckv
    off = 384
    for n, r in enumerate(wide_bf):
        y = seg(off, 256)
        if n in wide_f32:
            wide_f32[n][...] = y
        r[...] = (y * _Q_SCALE[n]).astype(BF) if n in _Q_SCALE else y.astype(BF)
        off += 256
    kpe = _rope_lanes(seg(off, LANES), cos, sin)
    kpe_ref[...] = kpe[:, :MLA_ROPE]
    kcat = jnp.concatenate([ckv, kpe], axis=1)
    kcat_ref[...] = jnp.where(lax.broadcasted_iota(I32, kcat.shape, 1) == ONES_LANE, 1.0, kcat).astype(BF)
    kit = seg(off + LANES, 256)
    dki_ref[...] = kit[:, :IDX_DIM]
    kit_ref[...] = kit.astype(BF)
    dwi_ref[...] = seg(off + LANES + 256, LANES)[:, :IDX_HEADS]


def _proj_call(x, w_mix, kvg, qg, wn, wp, wuk, cos_k, sin_k, tm):
    t, d = x.shape
    f32_w = [128, MLA_ROPE, 256, 256, 256, 256, 256, 256, IDX_DIM, IDX_HEADS]
    bf_w = [4 * 256, 256] + [256] * len(_WIDE) + [256]
    row = lambda w: pl.BlockSpec((tm, w), lambda i: (i, 0))
    const = lambda a: pl.BlockSpec(a.shape, lambda i: (0, 0))
    return pl.pallas_call(
        _proj_kernel, grid=(t // tm,),
        in_specs=[row(d), const(w_mix), const(kvg), const(qg), const(wn), const(wp), const(wuk),
                  row(LANES), row(LANES)],
        out_specs=[row(w) for w in f32_w + bf_w],
        out_shape=([jax.ShapeDtypeStruct((t, w), F32) for w in f32_w]
                   + [jax.ShapeDtypeStruct((t, w), BF) for w in bf_w]),
        compiler_params=_cparams(("parallel",)), name="mix_proj")(x, w_mix, kvg, qg, wn, wp, wuk, cos_k, sin_k)


def _merge_kernel(x_ref, oa_ref, ob_ref, oc_ref, od_ref, wg_ref, wbr_ref, wout_ref, g_ref, b_ref, o_ref):
    x = x_ref[...]
    xb = x.astype(BF)
    acc = None
    for n, o_n in enumerate((oa_ref, ob_ref, oc_ref, od_ref)):
        gate = jax.nn.sigmoid(jnp.dot(xb, wg_ref[:, n * D_MODEL:(n + 1) * D_MODEL],
                                      preferred_element_type=F32))
        br = jnp.dot(o_n[...].astype(BF), wbr_ref[n], preferred_element_type=F32)
        acc = gate * br if acc is None else acc + gate * br
    mix = jnp.dot(acc.astype(BF), wout_ref[...], preferred_element_type=F32)
    o_ref[...] = _layer_norm(DN_ALPHA * x + mix, g_ref[...], b_ref[...])


def _merge_call(x, o_a, o_b, o_c, o_d, wg, wbr, wout, g, b, tm):
    t, d = x.shape
    row = lambda w: pl.BlockSpec((tm, w), lambda i: (i, 0))
    return pl.pallas_call(
        _merge_kernel, grid=(t // tm,),
        in_specs=[row(d), row(BR_W), row(BR_W), row(BR_W), row(BR_W),
                  pl.BlockSpec((d, N_BRANCH * d), lambda i: (0, 0)),
                  pl.BlockSpec((N_BRANCH, BR_W, d), lambda i: (0, 0, 0)),
                  pl.BlockSpec((d, d), lambda i: (0, 0)),
                  pl.BlockSpec((1, d), lambda i: (0, 0)), pl.BlockSpec((1, d), lambda i: (0, 0))],
        out_specs=row(d), out_shape=jax.ShapeDtypeStruct((t, d), F32),
        compiler_params=_cparams(("parallel",)), name="merge")(x, o_a, o_b, o_c, o_d, wg, wbr, wout, g, b)


def _ffn_kernel(x_ref, w1_ref, b1_ref, w2_ref, b2_ref, g_ref, b_ref, o_ref):
    x = x_ref[...]
    xb = x.astype(BF)
    acc = None
    for c in range(D_FF // D_MODEL):
        sl = slice(c * D_MODEL, (c + 1) * D_MODEL)
        h = jnp.dot(xb, w1_ref[:, sl], preferred_element_type=F32) + b1_ref[:, sl]
        h = jnp.square(jnp.maximum(h, 0.0))
        y = jnp.dot(h.astype(BF), w2_ref[sl, :], preferred_element_type=F32)
        acc = y if acc is None else acc + y
    o_ref[...] = _layer_norm(DN_ALPHA * x + acc + b2_ref[...], g_ref[...], b_ref[...])


def _ffn_call(x, w1, b1, w2, b2, g, b, tm):
    t, d = x.shape
    row = pl.BlockSpec((tm, d), lambda i: (i, 0))
    vec = lambda w: pl.BlockSpec((1, w), lambda i: (0, 0))
    return pl.pallas_call(
        _ffn_kernel, grid=(t // tm,),
        in_specs=[row, pl.BlockSpec((d, D_FF), lambda i: (0, 0)), vec(D_FF),
                  pl.BlockSpec((D_FF, d), lambda i: (0, 0)), vec(d), vec(d), vec(d)],
        out_specs=row, out_shape=jax.ShapeDtypeStruct((t, d), F32),
        compiler_params=_cparams(("parallel",)), name="ffn")(x, w1, b1, w2, b2, g, b)


def _t5_bucket(rel):
    nb = T5_BUCKETS // 2
    max_exact = nb // 2
    n = jnp.abs(rel)
    nf = jnp.maximum(n, 1).astype(jnp.float32)
    large = max_exact + (jnp.log(nf / max_exact) / math.log(T5_MAX_DIST / max_exact)
                         * (nb - max_exact)).astype(jnp.int32)
    large = jnp.minimum(large, nb - 1)
    return jnp.where(rel > 0, nb, 0) + jnp.where(n < max_exact, n, large)


def _bucket_ids(rel0s, rows, cols, valid_cols, qpos0_for_mask=None):
    r = jnp.arange(rows, dtype=I32)[:, None]
    c = jnp.arange(cols, dtype=I32)[None, :]
    ok = jnp.broadcast_to(c < valid_cols, (rows, cols))
    if qpos0_for_mask is not None:
        ok = ok & (((qpos0_for_mask - N_META + c) >> 6) <= ((qpos0_for_mask - N_META + r) >> 6))
    return jnp.stack([jnp.where(ok, _t5_bucket(I32(rel0) + c - r), -1) for rel0 in rel0s], axis=0)


def _bias_kernel(tab_ref, bk_ref, o_ref):
    bk = bk_ref[0]
    for h in range(DIFF_HEADS + DSA_HEADS):
        acc = jnp.where(bk < 0, NEG_INF, 0.0).astype(F32)
        for b in range(T5_BUCKETS):
            acc = jnp.where(bk == b, tab_ref[b, h], acc)
        o_ref[0, h] = acc


def _bias_call(rel_bias, bucket_ids):
    n, rows, cols = bucket_ids.shape
    nh = DIFF_HEADS + DSA_HEADS
    return pl.pallas_call(
        _bias_kernel, grid=(n,),
        in_specs=[pl.BlockSpec(memory_space=pltpu.SMEM),
                  pl.BlockSpec((1, rows, cols), lambda i: (i, 0, 0))],
        out_specs=pl.BlockSpec((1, nh, rows, cols), lambda i: (i, 0, 0, 0)),
        out_shape=jax.ShapeDtypeStruct((n, nh, rows, cols), F32),
        compiler_params=_cparams(("parallel",)), name="rel_bias_table")(rel_bias, bucket_ids)


def _bias_tables(geom, rel_bias):
    tq, kb = geom.tq, KEY_BLOCK
    own_cols = tq if geom.own_wide else NARROW
    own = _bias_call(rel_bias, _bucket_ids([0], tq, own_cols, tq, geom.qpos0))
    if geom.npast is None:
        assert tq == kb and kb + 1 >= T5_MAX_DIST
        assert geom.qpos0 + tq - (geom.n_meta - 1) >= T5_MAX_DIST
        past = _bias_call(rel_bias, _bucket_ids([-kb, -2 * kb], tq, kb, kb))
        meta = _bias_call(rel_bias, _bucket_ids([-geom.qpos0, -geom.qpos0 - tq], tq, NARROW, geom.n_meta))
        mown = _bias_call(rel_bias, _bucket_ids([0], geom.n_meta, NARROW, geom.n_meta, 0))
        return own, past, meta, mown
    past = _bias_call(rel_bias, _bucket_ids([N_META + j * kb - geom.qpos0 for j in range(geom.npast)], tq, kb, kb))
    return own, past, None, None


def _split_tables(tables, lo, hi):
    return [None if t is None else t[:, lo:hi] for t in tables]


def _no_past(npast):
    return isinstance(npast, int) and npast == 0


STATIC_UNROLL = 4
SEARCH_BITS_WHOLE_TILE = 1
SEARCH_BITS_SHORT_TILE = 3


def _aligned(x, m):
    return x if isinstance(x, int) else pl.multiple_of(x, m)


def _fold_blocks(count, fn, init):
    if isinstance(count, int) and count <= STATIC_UNROLL:
        for j in range(count):
            init = fn(j, init)
        return init
    carry = lax.fori_loop(0, count // 2, lambda t, c: fn(2 * t + 1, fn(2 * t, c)), init)
    return lax.cond(count % 2 == 1, lambda c: fn(count - 1, c), lambda c: c, carry)


def _past_loop(npast, body):
    _fold_blocks(npast, body, 0)


def _tile_walk(geom, frames_tile, meta_tile):
    i = pl.program_id(1)

    @pl.when(i < geom.ntile)
    def _():
        q0 = _aligned(geom.row0 + i * geom.tq, 16)
        npast = i if geom.npast is None else geom.npast
        frames_tile(q0, geom.tq, geom.qpos0 + i * geom.tq, npast)

    if geom.n_meta:
        @pl.when(i == geom.ntile)
        def _():
            meta_tile(0, geom.n_meta, 0)


def _own_mask(qpos, n, cols, causal):
    r = lax.broadcasted_iota(I32, (n, 1), 0)
    c = lax.broadcasted_iota(I32, (1, cols), 1)
    if causal:
        return c < r
    return (c < n) & (((qpos - N_META + c) >> 6) <= ((qpos - N_META + r) >> 6))


def _slab_spec(arr):
    return pl.BlockSpec((1,) + arr.shape[1:], lambda b, i: (b, 0, 0))


def _cache_spec(arr, layer):
    return pl.BlockSpec((1, 1) + arr.shape[2:], lambda b, i: (layer, b, 0, 0))


def _past_spec(arr, layer):
    return _cache_spec(arr, layer) if arr.ndim == 4 else _slab_spec(arr)


def _const_spec(arr):
    nd = arr.ndim
    return pl.BlockSpec(arr.shape, lambda b, i: (0,) * nd)


def _readers(geom, new_refs, past_refs):
    def new(k, r0, n):
        return new_refs[k][0, pl.ds(r0, n), :]

    def past(k, j):
        if geom.npast is None:
            return new_refs[k][0, pl.ds(_aligned(geom.row0 + j * KEY_BLOCK, 16), KEY_BLOCK), :]
        ref, r0 = past_refs[k], _aligned(j * KEY_BLOCK, KEY_BLOCK)
        return ref[0, 0, pl.ds(r0, KEY_BLOCK), :] if len(ref.shape) == 4 else ref[0, pl.ds(r0, KEY_BLOCK), :]
    return new, past


def _fill_narrow(dst_ref, rows):
    dst_ref[...] = jnp.zeros(dst_ref.shape, dst_ref.dtype)
    dst_ref[0:rows.shape[0], :] = rows.astype(dst_ref.dtype)


def _attn_grid(geom, bsz):
    return (bsz, geom.ntile + (1 if geom.n_meta else 0))


def _maps_dot(lhs, n, w, rhs, transpose_rhs=False):
    dot = _dot_nt if transpose_rhs else _dot
    if isinstance(lhs, (list, tuple)):
        return [dot(x, rhs) for x in lhs]
    nmaps = lhs.shape[0]
    if n % LANES == 0:
        out = dot(lhs[:, 0:n, 0:w].reshape(nmaps * n, w), rhs)
        return [out[p * n:(p + 1) * n] for p in range(nmaps)]
    return [dot(lhs[p, 0:n, 0:w], rhs) for p in range(nmaps)]


class Softmax:
    NREFS = 7

    def __init__(self, refs, nmaps, n, ones_lane=None):
        self.sw, self.sn, self.mx, self.m, self.ls, self.acc, self.e = refs
        self.nmaps, self.n, self.ones_lane = nmaps, n, ones_lane
        self.fresh_scores = self.fresh_values = True

    @staticmethod
    def scratch(geom, nmaps, vw):
        return [pltpu.VMEM((nmaps, geom.nslot, geom.tq, KEY_BLOCK), F32),
                pltpu.VMEM((nmaps, 2, geom.tq, NARROW), F32),
                pltpu.VMEM((nmaps, geom.tq, LANES), F32), pltpu.VMEM((nmaps, geom.tq, LANES), F32),
                pltpu.VMEM((nmaps, geom.tq, LANES), F32), pltpu.VMEM((nmaps, geom.tq, vw), F32),
                pltpu.VMEM((nmaps, geom.tq, KEY_BLOCK), BF)]

    def _store(self, p, slot, wide, s):
        if wide:
            self.sw[p, slot, 0:self.n, :] = s
        else:
            self.sn[p, slot, 0:self.n, :] = s

    def _load(self, p, slot, wide):
        return self.sw[p, slot, 0:self.n, :] if wide else self.sn[p, slot, 0:self.n, :]

    def scores(self, slot, wide, q_maps, k_b, scale, bias_of=None, extra=None):
        n = self.n
        d = q_maps[0].shape[-1] if isinstance(q_maps, (list, tuple)) else q_maps.shape[2]
        s_all = _maps_dot(q_maps, n, d, k_b, transpose_rhs=True)
        for p in range(self.nmaps):
            s = s_all[p]
            if scale != 1.0:
                s = s * scale
            if bias_of is not None:
                s = s + bias_of(p)
            if extra is not None:
                s = s + extra
            self._store(p, slot, wide, s)
            top = _halves(s, jnp.maximum)
            self.mx[p, 0:n, :] = top if self.fresh_scores else jnp.maximum(self.mx[p, 0:n, :], top)
        self.fresh_scores = False

    def finish_max(self):
        n = self.n
        for p in range(self.nmaps):
            m = jnp.max(self.mx[p, 0:n, :], axis=1, keepdims=True)
            self.m[p, 0:n, :] = jnp.broadcast_to(m, (n, LANES))

    def values(self, slot, wide, v_b):
        n, w = self.n, (KEY_BLOCK if wide else NARROW)
        staged = n % LANES == 0
        es = []
        for p in range(self.nmaps):
            s = self._load(p, slot, wide)
            e = jnp.exp(s - _rep(self.m[p, 0:n, :], w))
            if self.ones_lane is None:
                part = _halves(e, jnp.add)
                self.ls[p, 0:n, :] = part if self.fresh_values else self.ls[p, 0:n, :] + part
            if staged:
                self.e[p, 0:n, 0:w] = e.astype(BF)
            else:
                es.append(e)
        for p, pv in enumerate(_maps_dot(self.e if staged else es, n, w, v_b)):
            self.acc[p, 0:n, :] = pv if self.fresh_values else self.acc[p, 0:n, :] + pv
        self.fresh_values = False

    def result(self, p):
        n = self.n
        acc = self.acc[p, 0:n, :]
        if self.ones_lane is not None:
            return acc / acc[:, self.ones_lane:self.ones_lane + 1]
        return acc / jnp.sum(self.ls[p, 0:n, :], axis=1, keepdims=True)


def _mla_kernel(geom, *refs):
    n_past = 0 if geom.npast is None else 1
    (q_ref, kc_ref), refs = refs[:2], refs[2:]
    past_refs, refs = refs[:n_past], refs[n_past:]
    (wuv_ref, o_ref), refs = refs[:2], refs[2:]
    sm_refs, (qm_s, kn_s, km_s) = refs[:Softmax.NREFS], refs[Softmax.NREFS:]
    new, past = _readers(geom, (kc_ref,), past_refs)
    scale = 1.0

    def tile(q0, n, qpos, npast, with_meta, own_wide):
        sm = Softmax(sm_refs, MLA_HEADS, n, ones_lane=ONES_LANE)
        q_heads = lambda: [q_ref[0, pl.ds(q0, n), 2 * LANES * h:2 * LANES * (h + 1)] for h in range(MLA_HEADS)]
        if n % LANES == 0:
            for h, q_h in enumerate(q_heads()):
                qm_s[h, 0:n, :] = q_h
        q_src = lambda: qm_s if n % LANES == 0 else q_heads()
        own_slot = npast if own_wide else 0
        if own_wide:
            own_k = lambda: new(0, q0, n)
        else:
            _fill_narrow(kn_s, new(0, q0, n))
            own_k = lambda: kn_s[...]
        own_cols = n if own_wide else NARROW
        sm.scores(own_slot, own_wide, q_src(), own_k(), scale,
                  extra=jnp.where(_own_mask(qpos, n, own_cols, False), 0.0, NEG_INF))

        def p1(j, carry):
            sm.scores(j, True, q_src(), past(0, j), scale)
            return carry
        _past_loop(npast, p1)
        if with_meta:
            _fill_narrow(km_s, new(0, 0, geom.n_meta))
            pad = jnp.where(lax.broadcasted_iota(I32, (1, NARROW), 1) < geom.n_meta, 0.0, NEG_INF)
            sm.scores(1, False, q_src(), km_s[...], scale, extra=pad)
        sm.finish_max()

        sm.values(own_slot, own_wide, own_k())

        def p2(j, carry):
            sm.values(j, True, past(0, j))
            return carry
        _past_loop(npast, p2)
        if with_meta:
            sm.values(1, False, km_s[...])

        out = None
        ohead = _lane_group((n, BR_W), MLA_V)
        for h in range(MLA_HEADS):
            o_h = jnp.where(ohead == h, _dot(sm.result(h), wuv_ref[...]), 0.0)
            out = o_h if out is None else out + o_h
        o_ref[0, pl.ds(q0, n), :] = out

    _tile_walk(geom,
               lambda q0, n, qpos, npast: tile(q0, n, qpos, npast, bool(geom.n_meta), geom.own_wide),
               lambda q0, n, qpos: tile(q0, n, qpos, 0, False, False))


def _mla_call(geom, layer, qcat, kcat, past, wuv_pad):
    bsz = qcat.shape[0]
    ins = [qcat, kcat] + list(past) + [wuv_pad]
    specs = ([_slab_spec(a) for a in (qcat, kcat)] + [_past_spec(a, layer) for a in past] + [_const_spec(wuv_pad)])
    return pl.pallas_call(
        functools.partial(_mla_kernel, geom), grid=_attn_grid(geom, bsz),
        in_specs=specs, out_specs=pl.BlockSpec((1, geom.lq, BR_W), lambda b, i: (b, 0, 0)),
        out_shape=jax.ShapeDtypeStruct((bsz, geom.lq, BR_W), F32),
        scratch_shapes=(Softmax.scratch(geom, MLA_HEADS, 2 * LANES)
                        + [pltpu.VMEM((MLA_HEADS, geom.tq, 2 * LANES), BF)] + [pltpu.VMEM((NARROW, 2 * LANES), BF)] * 2),
        compiler_params=_cparams(("parallel", "arbitrary")), name="mla_attn")(*ins)


def _suffix_matrix(n):
    later = lax.broadcasted_iota(I32, (n, n), 0) > lax.broadcasted_iota(I32, (n, n), 1)
    return jnp.where(later, 1.0, 0.0).astype(BF)


def _sb_kernel(geom, *refs):
    n_past = 0 if geom.npast is None else 2
    (q_ref, k_ref, v_ref), refs = refs[:3], refs[3:]
    past_refs, (o_ref, acc_s, carry_s, qm_s, hl_s, lb_s, a_s, tri_s, kn_s, vn_s, km_s, vm_s) = \
        refs[:n_past], refs[n_past:]
    new, past = _readers(geom, (k_ref, v_ref), past_refs)

    def tile(q0, n, qpos, npast, with_meta, own_wide):
        q = q_ref[0, pl.ds(q0, n), :]
        head = _lane_group(q.shape, SB_DH)
        for h in range(SB_HEADS):
            qm_s[h, 0:n, :] = jnp.where(head == h, q, jnp.zeros_like(q))
        acc_s[:, 0:n, :] = jnp.zeros((SB_HEADS, n, BR_W), F32)
        carry_s[:, 0:n, :] = jnp.zeros((SB_HEADS, n, LANES), F32)

        def block(k_b, v_b, mask, tri):
            w = k_b.shape[0]
            staged = n % LANES == 0
            z_all = _maps_dot(qm_s, n, BR_W, k_b, transpose_rhs=True)
            halves, log_bs, firsts = [], [], []
            for h in range(SB_HEADS):
                z = z_all[h]
                soft = jnp.log(1.0 + jnp.exp(-jnp.abs(z)))
                log_b = jnp.minimum(z, 0.0) - soft
                log_1m = log_b - z
                if mask is not None:
                    log_1m = jnp.where(mask, log_1m, 0.0)
                firsts.append(log_1m[:, 0:1])
                hi = log_1m.astype(BF)
                lo = (log_1m - hi.astype(F32)).astype(BF)
                if staged:
                    hl_s[2 * h, 0:n, 0:w], hl_s[2 * h + 1, 0:n, 0:w], lb_s[h, 0:n, 0:w] = hi, lo, log_b
                else:
                    halves += [hi, lo]
                    log_bs.append(log_b)
            sums_all = _maps_dot(hl_s if staged else halves, n, w, tri)
            weights = []
            for h in range(SB_HEADS):
                later = sums_all[2 * h] + sums_all[2 * h + 1]
                carry = carry_s[h, 0:n, :]
                log_b = lb_s[h, 0:n, 0:w] if staged else log_bs[h]
                a = jnp.exp(log_b + later + _rep(carry, w))
                if mask is not None:
                    a = jnp.where(mask, a, 0.0)
                if staged:
                    a_s[h, 0:n, 0:w] = a.astype(BF)
                else:
                    weights.append(a)
                carry_s[h, 0:n, :] = carry + jnp.broadcast_to(later[:, 0:1] + firsts[h], (n, LANES))
            for h, pv in enumerate(_maps_dot(a_s if staged else weights, n, w, v_b)):
                acc_s[h, 0:n, :] += pv

        if own_wide:
            block(new(0, q0, n), new(1, q0, n), _own_mask(qpos, n, n, True), _suffix_matrix(n))
        else:
            _fill_narrow(kn_s, new(0, q0, n))
            _fill_narrow(vn_s, new(1, q0, n))
            block(kn_s[...], vn_s[...], _own_mask(qpos, n, NARROW, True), _suffix_matrix(NARROW))
        if _no_past(npast):
            tri_kb = None
        elif n % LANES == 0:
            tri_s[...] = _suffix_matrix(KEY_BLOCK)
            tri_kb = lambda: tri_s[...]
        else:
            tri_value = _suffix_matrix(KEY_BLOCK)
            tri_kb = lambda: tri_value

        def past_block(jj, carry):
            j = npast - 1 - jj
            block(past(0, j).astype(BF), past(1, j).astype(BF), None, tri_kb())
            return carry
        _past_loop(npast, past_block)
        if with_meta:
            m0 = 0
            _fill_narrow(km_s, new(0, m0, geom.n_meta))
            _fill_narrow(vm_s, new(1, m0, geom.n_meta))
            pad = jnp.broadcast_to(lax.broadcasted_iota(I32, (1, NARROW), 1) < geom.n_meta, (n, NARROW))
            block(km_s[...], vm_s[...], pad, _suffix_matrix(NARROW))

        out = None
        for h in range(SB_HEADS):
            o_h = jnp.where(head == h, acc_s[h, 0:n, :], 0.0)
            out = o_h if out is None else out + o_h
        o_ref[0, pl.ds(q0, n), :] = out

    _tile_walk(geom,
               lambda q0, n, qpos, npast: tile(q0, n, qpos, npast, bool(geom.n_meta), geom.own_wide),
               lambda q0, n, qpos: tile(q0, n, qpos, 0, False, False))


def _sb_call(geom, layer, q, k, v, past):
    bsz = q.shape[0]
    ins = [q, k, v] + list(past)
    specs = [_slab_spec(a) for a in (q, k, v)] + [_past_spec(a, layer) for a in past]
    return pl.pallas_call(
        functools.partial(_sb_kernel, geom), grid=_attn_grid(geom, bsz),
        in_specs=specs, out_specs=pl.BlockSpec((1, geom.lq, BR_W), lambda b, i: (b, 0, 0)),
        out_shape=jax.ShapeDtypeStruct((bsz, geom.lq, BR_W), F32),
        scratch_shapes=[pltpu.VMEM((SB_HEADS, geom.tq, BR_W), F32), pltpu.VMEM((SB_HEADS, geom.tq, LANES), F32),
                        pltpu.VMEM((SB_HEADS, geom.tq, BR_W), BF),
                        pltpu.VMEM((2 * SB_HEADS, geom.tq, KEY_BLOCK), BF), pltpu.VMEM((SB_HEADS, geom.tq, KEY_BLOCK), F32),
                        pltpu.VMEM((SB_HEADS, geom.tq, KEY_BLOCK), BF), pltpu.VMEM((KEY_BLOCK, KEY_BLOCK), BF)]
                       + [pltpu.VMEM((NARROW, BR_W), BF)] * 4,
        compiler_params=_cparams(("parallel", "arbitrary")), name="sb_attn")(*ins)


def _bias_readers(geom, own_ref, past_ref, meta_ref, mown_ref, tile_idx):
    def own(h, n):
        return own_ref[0, h, 0:n, :]

    def earlier(j, h):
        slot = j if geom.npast is not None else jnp.where(j == tile_idx - 1, 0, 1)
        return past_ref[slot, h]

    def meta(h):
        return meta_ref[jnp.minimum(tile_idx, 1), h]

    def meta_own(h):
        return mown_ref[0, h]
    return own, earlier, meta, meta_own


def _diff_kernel(geom, lam_init, *refs):
    n_past = 0 if geom.npast is None else 2
    n_bias = 4 if geom.n_meta else 2
    (q_ref, k_ref, v_ref), refs = refs[:3], refs[3:]
    past_refs, refs = refs[:n_past], refs[n_past:]
    bias_refs, refs = list(refs[:n_bias]) + [None] * (4 - n_bias), refs[n_bias:]
    (lam_ref, sg_ref, o_ref), refs = refs[:3], refs[3:]
    sm_refs, (qm_s, kn_s, vn_s, km_s, vm_s) = refs[:Softmax.NREFS], refs[Softmax.NREFS:]
    new, past = _readers(geom, (k_ref, v_ref), past_refs)
    b_own, b_earlier, b_meta, b_meta_own = _bias_readers(geom, *bias_refs, pl.program_id(1))
    scale = 1.0
    npair = 2 * DIFF_HEADS

    def tile(q0, n, qpos, npast, with_meta, own_wide, is_meta):
        sm = Softmax(sm_refs, npair, n)
        q = q_ref[0, pl.ds(q0, n), :]
        pair = _lane_group(q.shape, DIFF_DQK)
        for p in range(npair):
            qm_s[p, 0:n, :] = jnp.where(pair == p, q, jnp.zeros_like(q))
        q_of = qm_s
        own_slot = npast if own_wide else 0
        if own_wide:
            own_k, own_v = (lambda: new(0, q0, n)), (lambda: new(1, q0, n))
        else:
            _fill_narrow(kn_s, new(0, q0, n))
            _fill_narrow(vn_s, new(1, q0, n))
            own_k, own_v = (lambda: kn_s[...]), (lambda: vn_s[...])
        sm.scores(own_slot, own_wide, q_of, own_k(), scale,
                  bias_of=(lambda p: b_meta_own(p // 2)) if is_meta else (lambda p: b_own(p // 2, n)))

        def p1(j, carry):
            sm.scores(j, True, q_of, past(0, j), scale,
                      bias_of=lambda p: b_earlier(j, p // 2))
            return carry
        _past_loop(npast, p1)
        if with_meta:
            m0 = 0
            _fill_narrow(km_s, new(0, m0, geom.n_meta))
            _fill_narrow(vm_s, new(1, m0, geom.n_meta))
            sm.scores(1, False, q_of, km_s[...], scale, bias_of=lambda p: b_meta(p // 2))
        sm.finish_max()

        sm.values(own_slot, own_wide, own_v().astype(BF))

        def p2(j, carry):
            sm.values(j, True, past(1, j).astype(BF))
            return carry
        _past_loop(npast, p2)
        if with_meta:
            sm.values(1, False, vm_s[...])

        lp = lam_ref[...]
        lam = (jnp.exp(jnp.sum(lp[0:1] * lp[1:2], axis=1, keepdims=True))
               - jnp.exp(jnp.sum(lp[2:3] * lp[3:4], axis=1, keepdims=True)) + lam_init)
        head = _lane_group((n, BR_W), DIFF_DV)
        out = None
        for h in range(DIFF_HEADS):
            o_h = jnp.where(head == h, sm.result(2 * h) - lam * sm.result(2 * h + 1), 0.0)
            ms = jnp.sum(o_h * o_h, axis=1, keepdims=True) * (1.0 / DIFF_DV)
            o_h = o_h * lax.rsqrt(ms + RMS_EPS) * sg_ref[...] * (1.0 - lam_init)
            out = o_h if out is None else out + o_h
        o_ref[0, pl.ds(q0, n), :] = out

    _tile_walk(geom,
               lambda q0, n, qpos, npast: tile(q0, n, qpos, npast, bool(geom.n_meta), geom.own_wide, False),
               lambda q0, n, qpos: tile(q0, n, qpos, 0, False, False, True))


def _diff_call(geom, layer, q, k, v, past, tables, lam_p, sg_tiled):
    bsz = q.shape[0]
    lam_init = 0.8 - 0.6 * math.exp(-0.3 * layer)
    tables = [t for t in tables if t is not None]
    ins = [q, k, v] + list(past) + tables + [lam_p, sg_tiled]
    specs = ([_slab_spec(a) for a in (q, k, v)] + [_past_spec(a, layer) for a in past]
             + [_const_spec(a) for a in tables + [lam_p, sg_tiled]])
    npair = 2 * DIFF_HEADS
    return pl.pallas_call(
        functools.partial(_diff_kernel, geom, lam_init), grid=_attn_grid(geom, bsz),
        in_specs=specs, out_specs=pl.BlockSpec((1, geom.lq, BR_W), lambda b, i: (b, 0, 0)),
        out_shape=jax.ShapeDtypeStruct((bsz, geom.lq, BR_W), F32),
        scratch_shapes=(Softmax.scratch(geom, npair, BR_W) + [pltpu.VMEM((npair, geom.tq, BR_W), BF)]
                        + [pltpu.VMEM((NARROW, BR_W), BF)] * 4),
        compiler_params=_cparams(("parallel", "arbitrary")), name="diff_attn")(*ins)


def _sortable(x):
    b = lax.bitcast_convert_type(x + 0.0, I32)
    return b ^ ((b >> 31) & I32(0x7FFFFFFF))


def _dsa_kernel(geom, *refs):
    n_past = 0 if geom.npast is None else 3
    n_bias = 4 if geom.n_meta else 2
    (q_ref, qi_ref, wi_ref, k_ref, v_ref, kit_ref), refs = refs[:6], refs[6:]
    past_refs, refs = refs[:n_past], refs[n_past:]
    bias_refs, refs = list(refs[:n_bias]) + [None] * (4 - n_bias), refs[n_bias:]
    o_ref, refs = refs[0], refs[1:]
    sm_refs, (qm_s, qim_s, wib_s, kw_s, kn_s, last_s,
              kno_s, vno_s, kio_s, knm_s, vnm_s, kim_s) = refs[:Softmax.NREFS], refs[Softmax.NREFS:]
    new, past = _readers(geom, (k_ref, v_ref, kit_ref), past_refs)
    b_own, b_earlier, b_meta, b_meta_own = _bias_readers(geom, *bias_refs, pl.program_id(1))
    topk = float(geom.topk)
    past_pos0 = N_META
    index_bits = max(1, int(geom.lq + geom.lp + N_META).bit_length())
    ones_count = jnp.ones((LANES, LANES), BF)

    def tile(q0, n, qpos, npast, with_meta, own_wide, is_meta):
        nm = geom.n_meta
        m0 = 0
        own_slot = npast if own_wide else 0
        own_cols = n if own_wide else NARROW

        qi = qi_ref[0, pl.ds(q0, n), :]
        igrp = _lane_group(qi.shape, IDX_DIM)
        wi = wi_ref[0, pl.ds(q0, n), :] * (IDX_HEADS ** -0.5 * IDX_DIM ** -0.5)
        for h in range(IDX_HEADS):
            qim_s[h, 0:n, :] = jnp.where(igrp == h, qi, jnp.zeros_like(qi))
            wib_s[h, 0:n, :] = jnp.broadcast_to(wi[:, h:h + 1], (n, LANES))
        q = q_ref[0, pl.ds(q0, n), :]
        head = _lane_group(q.shape, DSA_DH)
        for h in range(DSA_HEADS):
            qm_s[h, 0:n, :] = jnp.where(head == h, q, jnp.zeros_like(q))
        if own_wide:
            own_k, own_v, own_ki = (lambda: new(0, q0, n)), (lambda: new(1, q0, n)), (lambda: new(2, q0, n))
        else:
            _fill_narrow(kno_s, new(0, q0, n))
            _fill_narrow(vno_s, new(1, q0, n))
            _fill_narrow(kio_s, new(2, q0, n))
            own_k, own_v, own_ki = (lambda: kno_s[...]), (lambda: vno_s[...]), (lambda: kio_s[...])
        if with_meta:
            _fill_narrow(knm_s, new(0, m0, nm))
            _fill_narrow(vnm_s, new(1, m0, nm))
            _fill_narrow(kim_s, new(2, m0, nm))

        def index_keys(kit_b):
            raw = _maps_dot(qim_s, n, BR_W, kit_b, transpose_rhs=True)
            score = None
            for h in range(IDX_HEADS):
                t = _rep(wib_s[h, 0:n, :], kit_b.shape[0]) * jnp.maximum(raw[h], 0.0)
                score = t if score is None else score + t
            return _sortable(score)

        own_keys = jnp.where(_own_mask(qpos, n, own_cols, False), index_keys(own_ki()), I32(KEY_NEG))
        if own_wide:
            kw_s[own_slot, 0:n, :] = own_keys
        else:
            pad = lax.broadcasted_iota(I32, (1, NARROW), 1) < n
            kn_s[0, 0:n, :] = jnp.where(pad, own_keys, I32(INT_MIN))

        def score_block(j, carry):
            kw_s[j, 0:n, :] = index_keys(past(2, j).astype(BF))
            return carry
        _past_loop(npast, score_block)
        if with_meta:
            pad = lax.broadcasted_iota(I32, (1, NARROW), 1) < nm
            kn_s[1, 0:n, :] = jnp.where(pad, index_keys(kim_s[...]), I32(INT_MIN))
        nwide = npast + 1 if own_wide else npast

        def wide_sweep(init, fn):
            return _fold_blocks(nwide, fn, init)

        def one(cond):
            return jnp.where(cond, 1.0, 0.0)

        def partial_counts(r0, nr, wide_fn, own_narrow_fn, meta_fn):
            part = wide_sweep(jnp.zeros((nr, LANES), F32), lambda j, a: a + wide_fn(kw_s[j, r0:r0 + nr, :], j))
            if not own_wide:
                part = part + own_narrow_fn(kn_s[0, r0:r0 + nr, :])
            if with_meta:
                part = part + meta_fn(kn_s[1, r0:r0 + nr, :])
            return part

        def lane_sums(part):
            return jnp.dot(part.astype(BF), ones_count, preferred_element_type=F32)

        def count(wide_fn, own_narrow_fn, meta_fn):
            return lane_sums(partial_counts(0, n, wide_fn, own_narrow_fn, meta_fn))

        def count_cmp(cmp):
            return count(lambda x, j: _halves(one(cmp(x, KEY_BLOCK)), jnp.add),
                         lambda x: one(cmp(x, NARROW)), lambda x: one(cmp(x, NARROW)))

        def ge_part(r0, nr, cand):
            return partial_counts(r0, nr, lambda x, j: _halves(one(x >= _rep(cand, KEY_BLOCK)), jnp.add),
                                  lambda x: one(x >= cand), lambda x: one(x >= cand))

        def sign_step(r0, nr):
            c0 = lane_sums(ge_part(r0, nr, jnp.zeros((nr, LANES), I32)))
            return jnp.where(c0 >= topk, I32(0), I32(INT_MIN)), jnp.where(c0 >= topk, c0, topk + 1.0)

        def accept(thr, cnt, cand, c):
            keep = c >= topk
            return jnp.where(keep, cand, thr), jnp.where(keep, c, cnt)

        if is_meta and geom.n_meta <= geom.topk:
            thr = jnp.full((n, LANES), INT_MIN, I32)
            cnt = jnp.zeros((n, LANES), F32)
        else:
            nb = SEARCH_BITS_WHOLE_TILE if n % LANES == 0 else SEARCH_BITS_SHORT_TILE

            def ge_counts(cands):
                zeros = tuple(jnp.zeros((n, LANES), F32) for _ in cands)

                def hits(x, parts):
                    w = x.shape[1]
                    return tuple(p + _halves(one(x >= _rep(c, w)), jnp.add) for p, c in zip(parts, cands))
                parts = wide_sweep(zeros, lambda j, parts: hits(kw_s[j, 0:n, :], parts))
                if not own_wide:
                    parts = hits(kn_s[0, 0:n, :], parts)
                if with_meta:
                    parts = hits(kn_s[1, 0:n, :], parts)
                sums = lane_sums(jnp.concatenate(parts, axis=0))
                return [sums[m * n:(m + 1) * n] for m in range(len(cands))]

            def search_pass(shift, bits, thr, cnt):
                cands = [thr | jnp.left_shift(I32(m), shift) for m in range(1, 2 ** bits)]
                for cand, c in zip(cands, ge_counts(cands)):
                    thr, cnt = accept(thr, cnt, cand, c)
                return thr, cnt
            thr, cnt = lax.fori_loop(0, 31 // nb, lambda t, st: search_pass(I32(31 - nb) - nb * t, nb, *st),
                                     sign_step(0, n))
            if 31 % nb:
                thr, cnt = search_pass(I32(0), 31 % nb, thr, cnt)

        last_s[0:n, :] = jnp.full((n, LANES), INT_MAX, I32)
        has_ties = jnp.max(one((cnt > topk) & (thr > I32(KEY_NEG)))) > 0.0

        @pl.when(has_ties)
        def _():
            need = topk - count_cmp(lambda x, w: x > _rep(thr, w))
            colw = lax.broadcasted_iota(I32, (1, KEY_BLOCK), 1)
            coln = lax.broadcasted_iota(I32, (1, NARROW), 1)

            def wide_pos0(j):
                return past_pos0 + j * KEY_BLOCK

            def ties_before(x):
                return count(lambda kk, j: _halves(one((kk == _rep(thr, KEY_BLOCK))
                                                       & (colw < _rep(x, KEY_BLOCK) - wide_pos0(j))), jnp.add),
                             lambda kk: one((kk == thr) & (coln < x - qpos)),
                             lambda kk: one((kk == thr) & (coln < x)))

            def pos_step(b, last_pos):
                cand = last_pos + jnp.left_shift(I32(1), I32(index_bits - 1) - b)
                return jnp.where(ties_before(cand) < need, cand, last_pos)
            last_s[0:n, :] = lax.fori_loop(0, index_bits, pos_step, jnp.zeros((n, LANES), I32))

        last_pos = last_s[0:n, :]

        def sel_bias(kk, kpos0):
            w = kk.shape[1]
            cols = lax.broadcasted_iota(I32, (1, w), 1)
            t = _rep(thr, w)
            sel = (kk > t) | ((kk == t) & (cols <= _rep(last_pos, w) - kpos0))
            return jnp.where(sel, 0.0, NEG_INF)

        sm = Softmax(sm_refs, DSA_HEADS, n)
        q_of = qm_s
        own_kk = kw_s[own_slot, 0:n, :] if own_wide else kn_s[0, 0:n, :]
        sm.scores(own_slot, own_wide, q_of, own_k(), 1.0,
                  bias_of=(lambda h: b_meta_own(h)) if is_meta else (lambda h: b_own(h, n)),
                  extra=sel_bias(own_kk, qpos))

        def p1(j, carry):
            sm.scores(j, True, q_of, past(0, j).astype(BF), 1.0,
                      bias_of=lambda h: b_earlier(j, h),
                      extra=sel_bias(kw_s[j, 0:n, :], past_pos0 + j * KEY_BLOCK))
            return carry
        _past_loop(npast, p1)
        if with_meta:
            sm.scores(1, False, q_of, knm_s[...], 1.0, bias_of=lambda h: b_meta(h),
                      extra=sel_bias(kn_s[1, 0:n, :], 0))
        sm.finish_max()

        sm.values(own_slot, own_wide, own_v().astype(BF))

        def p2(j, carry):
            sm.values(j, True, past(1, j).astype(BF))
            return carry
        _past_loop(npast, p2)
        if with_meta:
            sm.values(1, False, vnm_s[...])

        out = None
        for h in range(DSA_HEADS):
            o_h = jnp.where(head == h, sm.result(h), 0.0)
            out = o_h if out is None else out + o_h
        o_ref[0, pl.ds(q0, n), :] = out

    _tile_walk(geom,
               lambda q0, n, qpos, npast: tile(q0, n, qpos, npast, bool(geom.n_meta), geom.own_wide, False),
               lambda q0, n, qpos: tile(q0, n, qpos, 0, False, False, True))


def _dsa_call(geom, layer, q, qi, wi, k, v, kit, past, tables):
    bsz = q.shape[0]
    tables = [t for t in tables if t is not None]
    ins = [q, qi, wi, k, v, kit] + list(past) + tables
    specs = ([_slab_spec(a) for a in (q, qi, wi, k, v, kit)] + [_past_spec(a, layer) for a in past]
             + [_const_spec(a) for a in tables])
    tq = geom.tq
    return pl.pallas_call(
        functools.partial(_dsa_kernel, geom), grid=_attn_grid(geom, bsz),
        in_specs=specs, out_specs=pl.BlockSpec((1, geom.lq, BR_W), lambda b, i: (b, 0, 0)),
        out_shape=jax.ShapeDtypeStruct((bsz, geom.lq, BR_W), F32),
        scratch_shapes=(Softmax.scratch(geom, DSA_HEADS, BR_W)
                        + [pltpu.VMEM((DSA_HEADS, tq, BR_W), BF), pltpu.VMEM((IDX_HEADS, tq, BR_W), BF),
                           pltpu.VMEM((IDX_HEADS, tq, LANES), F32),
                           pltpu.VMEM((geom.nslot, tq, KEY_BLOCK), I32), pltpu.VMEM((2, tq, NARROW), I32),
                           pltpu.VMEM((tq, LANES), I32)]
                        + [pltpu.VMEM((NARROW, BR_W), BF)] * 6),
        compiler_params=_cparams(("parallel", "arbitrary")), name="dsa_attn")(*ins)


def _row_tile(t):
    for tm in (768, 384, 256, 128, 64, 32, 16, 8):
        if t % tm == 0:
            return tm
    raise ValueError(f"token count {t} has no supported row tile")


def _rope_tables(pos):
    half = MLA_ROPE // 2
    inv_freq = ROPE_THETA ** (-jnp.arange(half, dtype=jnp.float32) / half)
    ang = pos.astype(jnp.float32)[:, None] * inv_freq[None, :]
    cos, sin = jnp.cos(ang), jnp.sin(ang)
    return jnp.concatenate([cos, cos], axis=1), jnp.concatenate([-sin, sin], axis=1)


def _run_group(geom, x, pos_rows, caches, weights):
    (ln_in_g, ln_in_b, w_in, mla_qnorm_g, mla_w_uq, mla_kvnorm_g, mla_w_uk, mla_w_uv, diff_lambda,
     diff_subln_g, rel_bias, w_br, w_out, ln1_g, ln1_b, w_ff1, b_ff1, w_ff2, b_ff2, ln2_g, ln2_b) = weights
    bsz, lq, d = x.shape
    t = bsz * lq
    tm = _row_tile(t)
    cos32, sin32 = _rope_tables(pos_rows)
    cos_k = jnp.tile(cos32, (bsz, MLA_HEADS))
    sin_k = jnp.tile(sin32, (bsz, MLA_HEADS))
    tables = _bias_tables(geom, rel_bias)
    diff_tables = _split_tables(tables, 0, DIFF_HEADS)
    dsa_tables = _split_tables(tables, DIFF_HEADS, DIFF_HEADS + DSA_HEADS)

    xf = _ln_call(x.reshape(t, d), ln_in_g, ln_in_b, tm)
    rows = []
    for l in range(DEPTH):
        w_uq = mla_w_uq[l].reshape(MLA_Q_LORA, MLA_HEADS, MLA_NOPE + MLA_ROPE)
        wn = w_uq[:, :, :MLA_NOPE].reshape(MLA_Q_LORA, MLA_HEADS * MLA_NOPE).astype(BF)
        wp = w_uq[:, :, MLA_NOPE:].reshape(MLA_Q_LORA, MLA_HEADS * MLA_ROPE).astype(BF)
        wuk = mla_w_uk[l].reshape(MLA_KV_LORA, MLA_HEADS * MLA_NOPE).T.astype(BF)
        wuv = jnp.pad(mla_w_uv[l].reshape(MLA_KV_LORA, MLA_HEADS * MLA_V),
                      ((0, 2 * LANES - MLA_KV_LORA), (0, 0))).astype(BF)
        (ckv, kpe, b_k, b_v, c_k, c_v, d_k, d_v, d_ki, d_wi,
         qcat, kcat, bq_b, bk_b, bv_b, cq_b, ck_b, cv_b, dq_b, dk_b, dv_b, dqi_b, kit_b) = _proj_call(
            xf, _mix_weight(w_in[l]), mla_kvnorm_g[l].reshape(1, -1), mla_qnorm_g[l].reshape(1, -1),
            wn, wp, wuk, cos_k, sin_k, tm)
        per = lambda a: a.reshape(bsz, lq, a.shape[-1])
        if caches is None:
            past = lambda *idx: []
            mla_past, kit_past = [], []
        else:
            past = lambda *idx: [caches[i] for i in idx]
            fill = jnp.zeros(caches[0].shape[1:3] + (2 * LANES - MLA_KV_LORA - MLA_ROPE,), F32).at[..., -1].set(1.0)
            mla_past = [jnp.concatenate([caches[0][l], caches[1][l], fill], axis=-1).astype(BF)]
            kit_past = [jnp.tile(caches[8][l], (1, 1, IDX_HEADS)).astype(BF)]

        o_a = _mla_call(geom, l, per(qcat), per(kcat), mla_past, wuv)
        o_b = _sb_call(geom, l, per(bq_b), per(bk_b), per(bv_b), past(2, 3))
        o_c = _diff_call(geom, l, per(cq_b), per(ck_b), per(cv_b), past(4, 5), diff_tables, diff_lambda[l],
                         jnp.tile(diff_subln_g[l], DIFF_HEADS).reshape(1, -1))
        o_d = _dsa_call(geom, l, per(dq_b), per(dqi_b), per(d_wi), per(dk_b), per(dv_b), per(kit_b),
                        past(6, 7) + kit_past, dsa_tables)

        flat = lambda a: a.reshape(t, BR_W)
        wg = w_in[l][:, IN_OFFS[15]:IN_OFFS[16]].astype(BF)
        x1 = _merge_call(xf, flat(o_a), flat(o_b), flat(o_c), flat(o_d), wg, w_br[l].astype(BF),
                         w_out[l].astype(BF), ln1_g[l].reshape(1, -1), ln1_b[l].reshape(1, -1), tm)
        xf = _ffn_call(x1, w_ff1[l].astype(BF), b_ff1[l].reshape(1, -1), w_ff2[l].astype(BF),
                       b_ff2[l].reshape(1, -1), ln2_g[l].reshape(1, -1), ln2_b[l].reshape(1, -1), tm)
        rows.append([per(a) for a in (ckv, kpe, b_k, b_v, c_k, c_v, d_k, d_v, d_ki)])
    return xf.reshape(bsz, lq, d), rows


_ROW_TRAILING = ((MLA_KV_LORA,), (MLA_ROPE,), (SB_HEADS, SB_DH), (SB_HEADS, SB_DH),
                 (DIFF_HEADS, 2, DIFF_DQK), (DIFF_HEADS, DIFF_DV), (DSA_HEADS, DSA_DH),
                 (DSA_HEADS, DSA_DH), (IDX_DIM,))


def kernel(x_prompt, x_sample, cache_mla_kv, cache_mla_pe, cache_sb_k, cache_sb_v, cache_diff_k, cache_diff_v, cache_dsa_k, cache_dsa_v, cache_dsa_kidx, meta, ln_in_g, ln_in_b, w_in, mla_qnorm_g, mla_w_uq, mla_kvnorm_g, mla_w_uk, mla_w_uv, diff_lambda, diff_subln_g, rel_bias, w_br, w_out, ln1_g, ln1_b, w_ff1, b_ff1, w_ff2, b_ff2, ln2_g, ln2_b):
    weights = (ln_in_g, ln_in_b, w_in, mla_qnorm_g, mla_w_uq, mla_kvnorm_g, mla_w_uk, mla_w_uv, diff_lambda,
               diff_subln_g, rel_bias, w_br, w_out, ln1_g, ln1_b, w_ff1, b_ff1, w_ff2, b_ff2, ln2_g, ln2_b)
    assert w_in.shape[0] == DEPTH and x_prompt.shape[2] == D_MODEL

    bsz_p, seq_p, _ = x_prompt.shape
    assert seq_p % KEY_BLOCK == 0
    meta_b = jnp.broadcast_to(meta[None].astype(x_prompt.dtype), (bsz_p, N_META, D_MODEL))
    xp = jnp.concatenate([meta_b, x_prompt], axis=1)
    pos_p = jnp.arange(N_META + seq_p, dtype=I32)
    geom_p = Geom(tq=KEY_BLOCK, ntile=seq_p // KEY_BLOCK, npast=None, n_meta=N_META, qpos0=N_META,
                  row0=N_META, lq=seq_p + N_META, lp=0, topk=min(DSA_TOPK, seq_p // 4))
    yp, rows_p = _run_group(geom_p, xp, pos_p, None, weights)
    y_prompt = yp[:, N_META:]
    p_rows = []
    for i, trailing in enumerate(_ROW_TRAILING):
        stacked = jnp.stack([r[i] for r in rows_p], axis=0)
        p_rows.append(stacked.reshape(stacked.shape[:3] + trailing))

    past_len = cache_mla_kv.shape[2]
    bsz_s, dec_seq, _ = x_sample.shape
    assert past_len % KEY_BLOCK == 0 and dec_seq % 16 == 0 and dec_seq <= NARROW
    assert past_len % CHUNK == 0 and dec_seq <= CHUNK, "new frames must share one chunk"
    caches = [c.reshape(c.shape[:3] + (-1,)) for c in
              (cache_mla_kv, cache_mla_pe, cache_sb_k, cache_sb_v, cache_diff_k, cache_diff_v,
               cache_dsa_k, cache_dsa_v, cache_dsa_kidx)]
    pos_s = N_META + past_len + jnp.arange(dec_seq, dtype=I32)
    geom_s = Geom(tq=dec_seq, ntile=1, npast=past_len // KEY_BLOCK, n_meta=0, qpos0=N_META + past_len,
                  row0=0, lq=dec_seq, lp=past_len, topk=min(DSA_TOPK, (past_len + dec_seq) // 4))
    y_sample, rows_s = _run_group(geom_s, x_sample, pos_s, caches, weights)
    s_rows = []
    for i, trailing in enumerate(_ROW_TRAILING):
        stacked = jnp.stack([r[i] for r in rows_s], axis=0)
        s_rows.append(stacked.reshape(stacked.shape[:3] + trailing))

    return (y_prompt, y_sample, *p_rows, *s_rows)
```

```python
import functools
import math
from typing import NamedTuple, Optional

import jax
import jax.numpy as jnp
import numpy as np
from jax import lax
from jax.experimental import pallas as pl
from jax.experimental.pallas import tpu as pltpu

D_MODEL = 1024
CHUNK = 64
N_META = 16
MLA_HEADS = 4
MLA_Q_LORA = 256
MLA_KV_LORA = 128
MLA_NOPE = 64
MLA_ROPE = 32
MLA_V = 64
ROPE_THETA = 10000.0
SB_HEADS = 4
SB_DH = 64
DIFF_HEADS = 4
DIFF_DQK = 32
DIFF_DV = 64
DSA_HEADS = 4
DSA_DH = 64
IDX_HEADS = 8
IDX_DIM = 32
DSA_TOPK = 256
N_BRANCH = 4
BR_W = 256
D_FF = 4 * D_MODEL
T5_BUCKETS = 32
T5_MAX_DIST = 128
LN_EPS = 1e-5
RMS_EPS = 1e-6
NEG_INF = -1e30
DEPTH = 2
DN_ALPHA = (2 * DEPTH) ** 0.25
IN_SIZES = (MLA_Q_LORA, MLA_KV_LORA, MLA_ROPE,
            SB_HEADS * SB_DH, SB_HEADS * SB_DH, SB_HEADS * SB_DH,
            DIFF_HEADS * 2 * DIFF_DQK, DIFF_HEADS * 2 * DIFF_DQK, DIFF_HEADS * DIFF_DV,
            DSA_HEADS * DSA_DH, DSA_HEADS * DSA_DH, DSA_HEADS * DSA_DH,
            IDX_HEADS * IDX_DIM, IDX_DIM, IDX_HEADS,
            N_BRANCH * D_MODEL)
IN_OFFS = tuple(int(s) for s in np.cumsum((0,) + IN_SIZES))

LANES = 128
KEY_BLOCK = 256
NARROW = LANES
VMEM_LIMIT_MB = 56

BF = jnp.bfloat16
F32 = jnp.float32
I32 = jnp.int32

_NEG_BITS = int(np.float32(NEG_INF).view(np.int32))
KEY_NEG = _NEG_BITS ^ ((_NEG_BITS >> 31) & 0x7FFFFFFF)
INT_MIN = -(2 ** 31)
INT_MAX = 2 ** 31 - 1


def _cparams(sem):
    return pltpu.CompilerParams(dimension_semantics=sem, vmem_limit_bytes=VMEM_LIMIT_MB * 1024 * 1024)


def _dot(a, b):
    return jnp.dot(a.astype(BF), b.astype(BF), preferred_element_type=F32)


def _dot_nt(a, b):
    return lax.dot_general(a.astype(BF), b.astype(BF), (((1,), (1,)), ((), ())),
                           preferred_element_type=F32)


def _layer_norm(x, g, b):
    mu = jnp.mean(x, axis=-1, keepdims=True)
    xc = x - mu
    var = jnp.mean(xc * xc, axis=-1, keepdims=True)
    return xc * lax.rsqrt(var + LN_EPS) * g + b


def _rms_norm(x, g):
    return x * lax.rsqrt(jnp.mean(x * x, axis=-1, keepdims=True) + RMS_EPS) * g


def _rope_lanes(x, cos, sin):
    lane = lax.broadcasted_iota(I32, x.shape, 1)
    swapped = jnp.where((lane & 31) < 16, pltpu.roll(x, LANES - 16, 1), pltpu.roll(x, 16, 1))
    return x * cos + swapped * sin


def _lane_group(shape, width):
    return lax.broadcasted_iota(I32, shape, 1) // width


def _halves(x, op):
    return op(x[:, :LANES], x[:, LANES:]) if x.shape[1] == 2 * LANES else x


def _rep(x, width):
    return jnp.concatenate([x, x], axis=1) if width == 2 * LANES else x


class Geom(NamedTuple):
    tq: int
    ntile: int
    npast: Optional[int]
    n_meta: int
    qpos0: int
    row0: int
    lq: int
    lp: int
    topk: int

    @property
    def own_wide(self):
        return self.tq == KEY_BLOCK

    @property
    def nslot(self):
        return max(self.ntile if self.npast is None else self.npast, 1)


def _ln_kernel(x_ref, g_ref, b_ref, o_ref):
    o_ref[...] = _layer_norm(x_ref[...], g_ref[...], b_ref[...])


def _ln_call(x, g, b, tm):
    t, d = x.shape
    return pl.pallas_call(
        _ln_kernel, grid=(t // tm,),
        in_specs=[pl.BlockSpec((tm, d), lambda i: (i, 0)),
                  pl.BlockSpec((1, d), lambda i: (0, 0)),
                  pl.BlockSpec((1, d), lambda i: (0, 0))],
        out_specs=pl.BlockSpec((tm, d), lambda i: (i, 0)),
        out_shape=jax.ShapeDtypeStruct((t, d), F32),
        compiler_params=_cparams(("parallel",)), name="ln_in")(x, g.reshape(1, d), b.reshape(1, d))


_WIDE = (3, 4, 5, 6, 7, 8, 9, 10, 11, 12)
_WIDE_F32 = (1, 2, 4, 5, 7, 8)
_MIX_COLS = 256 + 128 + 256 * len(_WIDE) + LANES + 256 + LANES
_Q_SCALE = {0: SB_DH ** -0.5, 3: DIFF_DQK ** -0.5, 6: DSA_DH ** -0.5}
MLA_SCALE = (MLA_NOPE + MLA_ROPE) ** -0.5
ONES_LANE = 2 * LANES - 1


def _mix_weight(w_in_l):
    def seg(i, pad_to=None):
        w = w_in_l[:, IN_OFFS[i]:IN_OFFS[i + 1]]
        if pad_to is not None:
            w = jnp.pad(w, ((0, 0), (0, pad_to - w.shape[1])))
        return w
    cols = ([seg(0), seg(1)] + [seg(i) for i in _WIDE]
            + [seg(2, LANES), jnp.tile(seg(13), (1, IDX_HEADS)), seg(14, LANES)])
    return jnp.concatenate(cols, axis=1).astype(BF)


N_ROW_SETS = 9


def _proj_kernel(n_prev, x_ref, w_ref, kvg_ref, qg_ref, wn_ref, wp_ref, wuk_ref, cos_ref, sin_ref, *refs):
    prev, outs = (refs[:N_ROW_SETS], refs[N_ROW_SETS:]) if n_prev else ((None,) * N_ROW_SETS, refs)
    row_refs, (dwi_ref, qcat_ref, kcat_ref, *wide_bf) = outs[:N_ROW_SETS], outs[N_ROW_SETS:]
    kit_ref = wide_bf[-1]
    wide_bf = wide_bf[:-1]
    wide_f32 = dict(zip(_WIDE_F32, range(2, 8)))

    def put_rows(k, val):
        if n_prev:
            row_refs[k][0:n_prev] = prev[k][...]
            row_refs[k][n_prev] = val
        else:
            row_refs[k][...] = val
    xb = x_ref[...].astype(BF)
    cos, sin = cos_ref[...], sin_ref[...]

    def seg(off, width):
        return jnp.dot(xb, w_ref[:, off:off + width], preferred_element_type=F32)

    qn = _rms_norm(seg(0, 256), qg_ref[...])
    nope = _dot(qn, wn_ref[...])
    pe = _rope_lanes(_dot(qn, wp_ref[...]), cos, sin)
    head = _lane_group(nope.shape, MLA_NOPE)
    lane = lax.broadcasted_iota(I32, pe.shape, 1)
    for h in range(MLA_HEADS):
        q_lat = _dot(jnp.where(head == h, nope, 0.0), wuk_ref[...]) * MLA_SCALE
        pe_h = (pe if h == 0 else pltpu.roll(pe, LANES - h * MLA_ROPE, 1)) * MLA_SCALE
        qcat_ref[:, 2 * LANES * h:2 * LANES * h + LANES] = q_lat.astype(BF)
        qcat_ref[:, 2 * LANES * h + LANES:2 * LANES * (h + 1)] = jnp.where(lane < MLA_ROPE, pe_h, 0.0).astype(BF)

    ckv = _rms_norm(seg(256, 128), kvg_ref[...])
    put_rows(0, ckv)
    off = 384
    for n, r in enumerate(wide_bf):
        y = seg(off, 256)
        if n in wide_f32:
            put_rows(wide_f32[n], y)
        r[...] = (y * _Q_SCALE[n]).astype(BF) if n in _Q_SCALE else y.astype(BF)
        off += 256
    kpe = _rope_lanes(seg(off, LANES), cos, sin)
    put_rows(1, kpe[:, :MLA_ROPE])
    kcat = jnp.concatenate([ckv, kpe], axis=1)
    kcat_ref[...] = jnp.where(lax.broadcasted_iota(I32, kcat.shape, 1) == ONES_LANE, 1.0, kcat).astype(BF)
    kit = seg(off + LANES, 256)
    put_rows(8, kit[:, :IDX_DIM])
    kit_ref[...] = kit.astype(BF)
    dwi_ref[...] = seg(off + LANES + 256, LANES)[:, :IDX_HEADS]


def _proj_call(x, w_mix, kvg, qg, wn, wp, wuk, cos_k, sin_k, prev_rows, tm):
    t, d = x.shape
    n_prev = 0 if prev_rows is None else prev_rows[0].shape[0]
    row_w = [128, MLA_ROPE, 256, 256, 256, 256, 256, 256, IDX_DIM]
    bf_w = [4 * 256, 256] + [256] * len(_WIDE) + [256]
    row = lambda w: pl.BlockSpec((tm, w), lambda i: (i, 0))
    deep = lambda n, w: pl.BlockSpec((n, tm, w), lambda i: (0, i, 0))
    const = lambda a: pl.BlockSpec(a.shape, lambda i: (0, 0))
    if n_prev:
        row_specs = [deep(n_prev + 1, w) for w in row_w]
        row_shapes = [jax.ShapeDtypeStruct((n_prev + 1, t, w), F32) for w in row_w]
    else:
        row_specs = [row(w) for w in row_w]
        row_shapes = [jax.ShapeDtypeStruct((t, w), F32) for w in row_w]
    return pl.pallas_call(
        functools.partial(_proj_kernel, n_prev), grid=(t // tm,),
        in_specs=([row(d), const(w_mix), const(kvg), const(qg), const(wn), const(wp), const(wuk),
                   row(LANES), row(LANES)] + ([deep(n_prev, w) for w in row_w] if n_prev else [])),
        out_specs=row_specs + [row(IDX_HEADS)] + [row(w) for w in bf_w],
        out_shape=(row_shapes + [jax.ShapeDtypeStruct((t, IDX_HEADS), F32)]
                   + [jax.ShapeDtypeStruct((t, w), BF) for w in bf_w]),
        compiler_params=_cparams(("parallel",)), name="mix_proj")(
            x, w_mix, kvg, qg, wn, wp, wuk, cos_k, sin_k, *(prev_rows or []))


def _merge_kernel(x_ref, oa_ref, ob_ref, oc_ref, od_ref, wg_ref, wbr_ref, wout_ref, g_ref, b_ref, o_ref):
    x = x_ref[...]
    xb = x.astype(BF)
    acc = None
    for n, o_n in enumerate((oa_ref, ob_ref, oc_ref, od_ref)):
        gate = jax.nn.sigmoid(jnp.dot(xb, wg_ref[:, n * D_MODEL:(n + 1) * D_MODEL],
                                      preferred_element_type=F32))
        br = jnp.dot(o_n[...].astype(BF), wbr_ref[n], preferred_element_type=F32)
        acc = gate * br if acc is None else acc + gate * br
    mix = jnp.dot(acc.astype(BF), wout_ref[...], preferred_element_type=F32)
    o_ref[...] = _layer_norm(DN_ALPHA * x + mix, g_ref[...], b_ref[...])


def _merge_call(x, o_a, o_b, o_c, o_d, wg, wbr, wout, g, b, tm):
    t, d = x.shape
    row = lambda w: pl.BlockSpec((tm, w), lambda i: (i, 0))
    return pl.pallas_call(
        _merge_kernel, grid=(t // tm,),
        in_specs=[row(d), row(BR_W), row(BR_W), row(BR_W), row(BR_W),
                  pl.BlockSpec((d, N_BRANCH * d), lambda i: (0, 0)),
                  pl.BlockSpec((N_BRANCH, BR_W, d), lambda i: (0, 0, 0)),
                  pl.BlockSpec((d, d), lambda i: (0, 0)),
                  pl.BlockSpec((1, d), lambda i: (0, 0)), pl.BlockSpec((1, d), lambda i: (0, 0))],
        out_specs=row(d), out_shape=jax.ShapeDtypeStruct((t, d), F32),
        compiler_params=_cparams(("parallel",)), name="merge")(x, o_a, o_b, o_c, o_d, wg, wbr, wout, g, b)


def _ffn_kernel(x_ref, w1_ref, b1_ref, w2_ref, b2_ref, g_ref, b_ref, o_ref):
    x = x_ref[...]
    xb = x.astype(BF)
    acc = None
    for c in range(D_FF // D_MODEL):
        sl = slice(c * D_MODEL, (c + 1) * D_MODEL)
        h = jnp.dot(xb, w1_ref[:, sl], preferred_element_type=F32) + b1_ref[:, sl]
        h = jnp.square(jnp.maximum(h, 0.0))
        y = jnp.dot(h.astype(BF), w2_ref[sl, :], preferred_element_type=F32)
        acc = y if acc is None else acc + y
    o_ref[...] = _layer_norm(DN_ALPHA * x + acc + b2_ref[...], g_ref[...], b_ref[...])


def _ffn_call(x, w1, b1, w2, b2, g, b, tm):
    t, d = x.shape
    row = pl.BlockSpec((tm, d), lambda i: (i, 0))
    vec = lambda w: pl.BlockSpec((1, w), lambda i: (0, 0))
    return pl.pallas_call(
        _ffn_kernel, grid=(t // tm,),
        in_specs=[row, pl.BlockSpec((d, D_FF), lambda i: (0, 0)), vec(D_FF),
                  pl.BlockSpec((D_FF, d), lambda i: (0, 0)), vec(d), vec(d), vec(d)],
        out_specs=row, out_shape=jax.ShapeDtypeStruct((t, d), F32),
        compiler_params=_cparams(("parallel",)), name="ffn")(x, w1, b1, w2, b2, g, b)


def _t5_bucket(rel):
    nb = T5_BUCKETS // 2
    max_exact = nb // 2
    n = jnp.abs(rel)
    nf = jnp.maximum(n, 1).astype(jnp.float32)
    large = max_exact + (jnp.log(nf / max_exact) / math.log(T5_MAX_DIST / max_exact)
                         * (nb - max_exact)).astype(jnp.int32)
    large = jnp.minimum(large, nb - 1)
    return jnp.where(rel > 0, nb, 0) + jnp.where(n < max_exact, n, large)


def _bucket_ids(rel0s, rows, cols, valid_cols, qpos0_for_mask=None):
    r = jnp.arange(rows, dtype=I32)[:, None]
    c = jnp.arange(cols, dtype=I32)[None, :]
    ok = jnp.broadcast_to(c < valid_cols, (rows, cols))
    if qpos0_for_mask is not None:
        ok = ok & (((qpos0_for_mask - N_META + c) >> 6) <= ((qpos0_for_mask - N_META + r) >> 6))
    return jnp.stack([jnp.where(ok, _t5_bucket(I32(rel0) + c - r), -1) for rel0 in rel0s], axis=0)


def _bias_kernel(tab_ref, bk_ref, o_ref):
    bk = bk_ref[0]
    for h in range(DIFF_HEADS + DSA_HEADS):
        acc = jnp.where(bk < 0, NEG_INF, 0.0).astype(F32)
        for b in range(T5_BUCKETS):
            acc = jnp.where(bk == b, tab_ref[b, h], acc)
        o_ref[0, h] = acc


def _bias_call(rel_bias, bucket_ids):
    n, rows, cols = bucket_ids.shape
    nh = DIFF_HEADS + DSA_HEADS
    return pl.pallas_call(
        _bias_kernel, grid=(n,),
        in_specs=[pl.BlockSpec(memory_space=pltpu.SMEM),
                  pl.BlockSpec((1, rows, cols), lambda i: (i, 0, 0))],
        out_specs=pl.BlockSpec((1, nh, rows, cols), lambda i: (i, 0, 0, 0)),
        out_shape=jax.ShapeDtypeStruct((n, nh, rows, cols), F32),
        compiler_params=_cparams(("parallel",)), name="rel_bias_table")(rel_bias, bucket_ids)


def _bias_tables(geom, rel_bias):
    tq, kb = geom.tq, KEY_BLOCK
    own_cols = tq if geom.own_wide else NARROW
    own = _bias_call(rel_bias, _bucket_ids([0], tq, own_cols, tq, geom.qpos0))
    if geom.npast is None:
        assert tq == kb and kb + 1 >= T5_MAX_DIST
        assert geom.qpos0 + tq - (geom.n_meta - 1) >= T5_MAX_DIST
        past = _bias_call(rel_bias, _bucket_ids([-kb, -2 * kb], tq, kb, kb))
        meta = _bias_call(rel_bias, _bucket_ids([-geom.qpos0, -geom.qpos0 - tq], tq, NARROW, geom.n_meta))
        mown = _bias_call(rel_bias, _bucket_ids([0], geom.n_meta, NARROW, geom.n_meta, 0))
        return own, past, meta, mown
    past = _bias_call(rel_bias, _bucket_ids([N_META + j * kb - geom.qpos0 for j in range(geom.npast)], tq, kb, kb))
    return own, past, None, None


def _split_tables(tables, lo, hi):
    return [None if t is None else t[:, lo:hi] for t in tables]


def _no_past(npast):
    return isinstance(npast, int) and npast == 0


STATIC_UNROLL = 4
SEARCH_BITS_WHOLE_TILE = 1
SEARCH_BITS_SHORT_TILE = 3
PASS_THROUGH_ROW_TILE = 384


def _aligned(x, m):
    return x if isinstance(x, int) else pl.multiple_of(x, m)


def _fold_blocks(count, fn, init):
    if isinstance(count, int) and count <= STATIC_UNROLL:
        for j in range(count):
            init = fn(j, init)
        return init
    carry = lax.fori_loop(0, count // 2, lambda t, c: fn(2 * t + 1, fn(2 * t, c)), init)
    return lax.cond(count % 2 == 1, lambda c: fn(count - 1, c), lambda c: c, carry)


def _past_loop(npast, body):
    _fold_blocks(npast, body, 0)


def _tile_walk(geom, frames_tile, meta_tile):
    i = pl.program_id(1)

    @pl.when(i < geom.ntile)
    def _():
        q0 = _aligned(geom.row0 + i * geom.tq, 16)
        npast = i if geom.npast is None else geom.npast
        frames_tile(q0, geom.tq, geom.qpos0 + i * geom.tq, npast)

    if geom.n_meta:
        @pl.when(i == geom.ntile)
        def _():
            meta_tile(0, geom.n_meta, 0)


def _own_mask(qpos, n, cols, causal):
    r = lax.broadcasted_iota(I32, (n, 1), 0)
    c = lax.broadcasted_iota(I32, (1, cols), 1)
    if causal:
        return c < r
    return (c < n) & (((qpos - N_META + c) >> 6) <= ((qpos - N_META + r) >> 6))


def _slab_spec(arr):
    return pl.BlockSpec((1,) + arr.shape[1:], lambda b, i: (b, 0, 0))


def _cache_spec(arr, layer):
    return pl.BlockSpec((1, 1) + arr.shape[2:], lambda b, i: (layer, b, 0, 0))


def _past_spec(arr, layer):
    return _cache_spec(arr, layer) if arr.ndim == 4 else _slab_spec(arr)


def _const_spec(arr):
    nd = arr.ndim
    return pl.BlockSpec(arr.shape, lambda b, i: (0,) * nd)


def _readers(geom, new_refs, past_refs):
    def new(k, r0, n):
        return new_refs[k][0, pl.ds(r0, n), :]

    def past(k, j):
        if geom.npast is None:
            return new_refs[k][0, pl.ds(_aligned(geom.row0 + j * KEY_BLOCK, 16), KEY_BLOCK), :]
        ref, r0 = past_refs[k], _aligned(j * KEY_BLOCK, KEY_BLOCK)
        return ref[0, 0, pl.ds(r0, KEY_BLOCK), :] if len(ref.shape) == 4 else ref[0, pl.ds(r0, KEY_BLOCK), :]
    return new, past


def _fill_narrow(dst_ref, rows):
    dst_ref[...] = jnp.zeros(dst_ref.shape, dst_ref.dtype)
    dst_ref[0:rows.shape[0], :] = rows.astype(dst_ref.dtype)


def _attn_grid(geom, bsz):
    return (bsz, geom.ntile + (1 if geom.n_meta else 0))


def _maps_dot(lhs, n, w, rhs, transpose_rhs=False):
    dot = _dot_nt if transpose_rhs else _dot
    if isinstance(lhs, (list, tuple)):
        return [dot(x, rhs) for x in lhs]
    nmaps = lhs.shape[0]
    if n % LANES == 0:
        out = dot(lhs[:, 0:n, 0:w].reshape(nmaps * n, w), rhs)
        return [out[p * n:(p + 1) * n] for p in range(nmaps)]
    return [dot(lhs[p, 0:n, 0:w], rhs) for p in range(nmaps)]


class Softmax:
    NREFS = 7

    def __init__(self, refs, nmaps, n, ones_lane=None):
        self.sw, self.sn, self.mx, self.m, self.ls, self.acc, self.e = refs
        self.nmaps, self.n, self.ones_lane = nmaps, n, ones_lane
        self.fresh_scores = self.fresh_values = True

    @staticmethod
    def scratch(geom, nmaps, vw):
        return [pltpu.VMEM((nmaps, geom.nslot, geom.tq, KEY_BLOCK), F32),
                pltpu.VMEM((nmaps, 2, geom.tq, NARROW), F32),
                pltpu.VMEM((nmaps, geom.tq, LANES), F32), pltpu.VMEM((nmaps, geom.tq, LANES), F32),
                pltpu.VMEM((nmaps, geom.tq, LANES), F32), pltpu.VMEM((nmaps, geom.tq, vw), F32),
                pltpu.VMEM((nmaps, geom.tq, KEY_BLOCK), BF)]

    def _store(self, p, slot, wide, s):
        if wide:
            self.sw[p, slot, 0:self.n, :] = s
        else:
            self.sn[p, slot, 0:self.n, :] = s

    def _load(self, p, slot, wide):
        return self.sw[p, slot, 0:self.n, :] if wide else self.sn[p, slot, 0:self.n, :]

    def scores(self, slot, wide, q_maps, k_b, scale, bias_of=None, extra=None):
        n = self.n
        d = q_maps[0].shape[-1] if isinstance(q_maps, (list, tuple)) else q_maps.shape[2]
        s_all = _maps_dot(q_maps, n, d, k_b, transpose_rhs=True)
        for p in range(self.nmaps):
            s = s_all[p]
            if scale != 1.0:
                s = s * scale
            if bias_of is not None:
                s = s + bias_of(p)
            if extra is not None:
                s = s + extra
            self._store(p, slot, wide, s)
            top = _halves(s, jnp.maximum)
            self.mx[p, 0:n, :] = top if self.fresh_scores else jnp.maximum(self.mx[p, 0:n, :], top)
        self.fresh_scores = False

    def finish_max(self):
        n = self.n
        for p in range(self.nmaps):
            m = jnp.max(self.mx[p, 0:n, :], axis=1, keepdims=True)
            self.m[p, 0:n, :] = jnp.broadcast_to(m, (n, LANES))

    def values(self, slot, wide, v_b):
        n, w = self.n, (KEY_BLOCK if wide else NARROW)
        staged = n % LANES == 0
        es = []
        for p in range(self.nmaps):
            s = self._load(p, slot, wide)
            e = jnp.exp(s - _rep(self.m[p, 0:n, :], w))
            if self.ones_lane is None:
                part = _halves(e, jnp.add)
                self.ls[p, 0:n, :] = part if self.fresh_values else self.ls[p, 0:n, :] + part
            if staged:
                self.e[p, 0:n, 0:w] = e.astype(BF)
            else:
                es.append(e)
        for p, pv in enumerate(_maps_dot(self.e if staged else es, n, w, v_b)):
            self.acc[p, 0:n, :] = pv if self.fresh_values else self.acc[p, 0:n, :] + pv
        self.fresh_values = False

    def result(self, p):
        n = self.n
        acc = self.acc[p, 0:n, :]
        if self.ones_lane is not None:
            return acc / acc[:, self.ones_lane:self.ones_lane + 1]
        return acc / jnp.sum(self.ls[p, 0:n, :], axis=1, keepdims=True)


def _mla_kernel(geom, *refs):
    n_past = 0 if geom.npast is None else 1
    (q_ref, kc_ref), refs = refs[:2], refs[2:]
    past_refs, refs = refs[:n_past], refs[n_past:]
    (wuv_ref, o_ref), refs = refs[:2], refs[2:]
    sm_refs, (qm_s, kn_s, km_s) = refs[:Softmax.NREFS], refs[Softmax.NREFS:]
    new, past = _readers(geom, (kc_ref,), past_refs)
    scale = 1.0

    def tile(q0, n, qpos, npast, with_meta, own_wide):
        sm = Softmax(sm_refs, MLA_HEADS, n, ones_lane=ONES_LANE)
        q_heads = lambda: [q_ref[0, pl.ds(q0, n), 2 * LANES * h:2 * LANES * (h + 1)] for h in range(MLA_HEADS)]
        if n % LANES == 0:
            for h, q_h in enumerate(q_heads()):
                qm_s[h, 0:n, :] = q_h
        q_src = lambda: qm_s if n % LANES == 0 else q_heads()
        own_slot = npast if own_wide else 0
        if own_wide:
            own_k = lambda: new(0, q0, n)
        else:
            _fill_narrow(kn_s, new(0, q0, n))
            own_k = lambda: kn_s[...]
        own_cols = n if own_wide else NARROW
        sm.scores(own_slot, own_wide, q_src(), own_k(), scale,
                  extra=jnp.where(_own_mask(qpos, n, own_cols, False), 0.0, NEG_INF))

        def p1(j, carry):
            sm.scores(j, True, q_src(), past(0, j), scale)
            return carry
        _past_loop(npast, p1)
        if with_meta:
            _fill_narrow(km_s, new(0, 0, geom.n_meta))
            pad = jnp.where(lax.broadcasted_iota(I32, (1, NARROW), 1) < geom.n_meta, 0.0, NEG_INF)
            sm.scores(1, False, q_src(), km_s[...], scale, extra=pad)
        sm.finish_max()

        sm.values(own_slot, own_wide, own_k())

        def p2(j, carry):
            sm.values(j, True, past(0, j))
            return carry
        _past_loop(npast, p2)
        if with_meta:
            sm.values(1, False, km_s[...])

        out = None
        ohead = _lane_group((n, BR_W), MLA_V)
        for h in range(MLA_HEADS):
            o_h = jnp.where(ohead == h, _dot(sm.result(h), wuv_ref[...]), 0.0)
            out = o_h if out is None else out + o_h
        o_ref[0, pl.ds(q0, n), :] = out

    _tile_walk(geom,
               lambda q0, n, qpos, npast: tile(q0, n, qpos, npast, bool(geom.n_meta), geom.own_wide),
               lambda q0, n, qpos: tile(q0, n, qpos, 0, False, False))


def _mla_call(geom, layer, qcat, kcat, past, wuv_pad):
    bsz = qcat.shape[0]
    ins = [qcat, kcat] + list(past) + [wuv_pad]
    specs = ([_slab_spec(a) for a in (qcat, kcat)] + [_past_spec(a, layer) for a in past] + [_const_spec(wuv_pad)])
    return pl.pallas_call(
        functools.partial(_mla_kernel, geom), grid=_attn_grid(geom, bsz),
        in_specs=specs, out_specs=pl.BlockSpec((1, geom.lq, BR_W), lambda b, i: (b, 0, 0)),
        out_shape=jax.ShapeDtypeStruct((bsz, geom.lq, BR_W), F32),
        scratch_shapes=(Softmax.scratch(geom, MLA_HEADS, 2 * LANES)
                        + [pltpu.VMEM((MLA_HEADS, geom.tq, 2 * LANES), BF)] + [pltpu.VMEM((NARROW, 2 * LANES), BF)] * 2),
        compiler_params=_cparams(("parallel", "arbitrary")), name="mla_attn")(*ins)


def _suffix_matrix(n):
    later = lax.broadcasted_iota(I32, (n, n), 0) > lax.broadcasted_iota(I32, (n, n), 1)
    return jnp.where(later, 1.0, 0.0).astype(BF)


def _sb_kernel(geom, *refs):
    n_past = 0 if geom.npast is None else 2
    (q_ref, k_ref, v_ref), refs = refs[:3], refs[3:]
    past_refs, (o_ref, acc_s, carry_s, qm_s, hl_s, lb_s, a_s, tri_s, kn_s, vn_s, km_s, vm_s) = \
        refs[:n_past], refs[n_past:]
    new, past = _readers(geom, (k_ref, v_ref), past_refs)

    def tile(q0, n, qpos, npast, with_meta, own_wide):
        q = q_ref[0, pl.ds(q0, n), :]
        head = _lane_group(q.shape, SB_DH)
        for h in range(SB_HEADS):
            qm_s[h, 0:n, :] = jnp.where(head == h, q, jnp.zeros_like(q))
        acc_s[:, 0:n, :] = jnp.zeros((SB_HEADS, n, BR_W), F32)
        carry_s[:, 0:n, :] = jnp.zeros((SB_HEADS, n, LANES), F32)

        def block(k_b, v_b, mask, tri):
            w = k_b.shape[0]
            staged = n % LANES == 0
            z_all = _maps_dot(qm_s, n, BR_W, k_b, transpose_rhs=True)
            halves, log_bs, firsts = [], [], []
            for h in range(SB_HEADS):
                z = z_all[h]
                soft = jnp.log(1.0 + jnp.exp(-jnp.abs(z)))
                log_b = jnp.minimum(z, 0.0) - soft
                log_1m = log_b - z
                if mask is not None:
                    log_1m = jnp.where(mask, log_1m, 0.0)
                firsts.append(log_1m[:, 0:1])
                hi = log_1m.astype(BF)
                lo = (log_1m - hi.astype(F32)).astype(BF)
                if staged:
                    hl_s[2 * h, 0:n, 0:w], hl_s[2 * h + 1, 0:n, 0:w], lb_s[h, 0:n, 0:w] = hi, lo, log_b
                else:
                    halves += [hi, lo]
                    log_bs.append(log_b)
            sums_all = _maps_dot(hl_s if staged else halves, n, w, tri)
            weights = []
            for h in range(SB_HEADS):
                later = sums_all[2 * h] + sums_all[2 * h + 1]
                carry = carry_s[h, 0:n, :]
                log_b = lb_s[h, 0:n, 0:w] if staged else log_bs[h]
                a = jnp.exp(log_b + later + _rep(carry, w))
                if mask is not None:
                    a = jnp.where(mask, a, 0.0)
                if staged:
                    a_s[h, 0:n, 0:w] = a.astype(BF)
                else:
                    weights.append(a)
                carry_s[h, 0:n, :] = carry + jnp.broadcast_to(later[:, 0:1] + firsts[h], (n, LANES))
            for h, pv in enumerate(_maps_dot(a_s if staged else weights, n, w, v_b)):
                acc_s[h, 0:n, :] += pv

        if own_wide:
            block(new(0, q0, n), new(1, q0, n), _own_mask(qpos, n, n, True), _suffix_matrix(n))
        else:
            _fill_narrow(kn_s, new(0, q0, n))
            _fill_narrow(vn_s, new(1, q0, n))
            block(kn_s[...], vn_s[...], _own_mask(qpos, n, NARROW, True), _suffix_matrix(NARROW))
        if _no_past(npast):
            tri_kb = None
        elif n % LANES == 0:
            tri_s[...] = _suffix_matrix(KEY_BLOCK)
            tri_kb = lambda: tri_s[...]
        else:
            tri_value = _suffix_matrix(KEY_BLOCK)
            tri_kb = lambda: tri_value

        def past_block(jj, carry):
            j = npast - 1 - jj
            block(past(0, j).astype(BF), past(1, j).astype(BF), None, tri_kb())
            return carry
        _past_loop(npast, past_block)
        if with_meta:
            m0 = 0
            _fill_narrow(km_s, new(0, m0, geom.n_meta))
            _fill_narrow(vm_s, new(1, m0, geom.n_meta))
            pad = jnp.broadcast_to(lax.broadcasted_iota(I32, (1, NARROW), 1) < geom.n_meta, (n, NARROW))
            block(km_s[...], vm_s[...], pad, _suffix_matrix(NARROW))

        out = None
        for h in range(SB_HEADS):
            o_h = jnp.where(head == h, acc_s[h, 0:n, :], 0.0)
            out = o_h if out is None else out + o_h
        o_ref[0, pl.ds(q0, n), :] = out

    _tile_walk(geom,
               lambda q0, n, qpos, npast: tile(q0, n, qpos, npast, bool(geom.n_meta), geom.own_wide),
               lambda q0, n, qpos: tile(q0, n, qpos, 0, False, False))


def _sb_call(geom, layer, q, k, v, past):
    bsz = q.shape[0]
    ins = [q, k, v] + list(past)
    specs = [_slab_spec(a) for a in (q, k, v)] + [_past_spec(a, layer) for a in past]
    return pl.pallas_call(
        functools.partial(_sb_kernel, geom), grid=_attn_grid(geom, bsz),
        in_specs=specs, out_specs=pl.BlockSpec((1, geom.lq, BR_W), lambda b, i: (b, 0, 0)),
        out_shape=jax.ShapeDtypeStruct((bsz, geom.lq, BR_W), F32),
        scratch_shapes=[pltpu.VMEM((SB_HEADS, geom.tq, BR_W), F32), pltpu.VMEM((SB_HEADS, geom.tq, LANES), F32),
                        pltpu.VMEM((SB_HEADS, geom.tq, BR_W), BF),
                        pltpu.VMEM((2 * SB_HEADS, geom.tq, KEY_BLOCK), BF), pltpu.VMEM((SB_HEADS, geom.tq, KEY_BLOCK), F32),
                        pltpu.VMEM((SB_HEADS, geom.tq, KEY_BLOCK), BF), pltpu.VMEM((KEY_BLOCK, KEY_BLOCK), BF)]
                       + [pltpu.VMEM((NARROW, BR_W), BF)] * 4,
        compiler_params=_cparams(("parallel", "arbitrary")), name="sb_attn")(*ins)


def _bias_readers(geom, own_ref, past_ref, meta_ref, mown_ref, tile_idx):
    def own(h, n):
        return own_ref[0, h, 0:n, :]

    def earlier(j, h):
        slot = j if geom.npast is not None else jnp.where(j == tile_idx - 1, 0, 1)
        return past_ref[slot, h]

    def meta(h):
        return meta_ref[jnp.minimum(tile_idx, 1), h]

    def meta_own(h):
        return mown_ref[0, h]
    return own, earlier, meta, meta_own


def _diff_kernel(geom, lam_init, *refs):
    n_past = 0 if geom.npast is None else 2
    n_bias = 4 if geom.n_meta else 2
    (q_ref, k_ref, v_ref), refs = refs[:3], refs[3:]
    past_refs, refs = refs[:n_past], refs[n_past:]
    bias_refs, refs = list(refs[:n_bias]) + [None] * (4 - n_bias), refs[n_bias:]
    (lam_ref, sg_ref, o_ref), refs = refs[:3], refs[3:]
    sm_refs, (qm_s, kn_s, vn_s, km_s, vm_s) = refs[:Softmax.NREFS], refs[Softmax.NREFS:]
    new, past = _readers(geom, (k_ref, v_ref), past_refs)
    b_own, b_earlier, b_meta, b_meta_own = _bias_readers(geom, *bias_refs, pl.program_id(1))
    scale = 1.0
    npair = 2 * DIFF_HEADS

    def tile(q0, n, qpos, npast, with_meta, own_wide, is_meta):
        sm = Softmax(sm_refs, npair, n)
        q = q_ref[0, pl.ds(q0, n), :]
        pair = _lane_group(q.shape, DIFF_DQK)
        for p in range(npair):
            qm_s[p, 0:n, :] = jnp.where(pair == p, q, jnp.zeros_like(q))
        q_of = qm_s
        own_slot = npast if own_wide else 0
        if own_wide:
            own_k, own_v = (lambda: new(0, q0, n)), (lambda: new(1, q0, n))
        else:
            _fill_narrow(kn_s, new(0, q0, n))
            _fill_narrow(vn_s, new(1, q0, n))
            own_k, own_v = (lambda: kn_s[...]), (lambda: vn_s[...])
        sm.scores(own_slot, own_wide, q_of, own_k(), scale,
                  bias_of=(lambda p: b_meta_own(p // 2)) if is_meta else (lambda p: b_own(p // 2, n)))

        def p1(j, carry):
            sm.scores(j, True, q_of, past(0, j), scale,
                      bias_of=lambda p: b_earlier(j, p // 2))
            return carry
        _past_loop(npast, p1)
        if with_meta:
            m0 = 0
            _fill_narrow(km_s, new(0, m0, geom.n_meta))
            _fill_narrow(vm_s, new(1, m0, geom.n_meta))
            sm.scores(1, False, q_of, km_s[...], scale, bias_of=lambda p: b_meta(p // 2))
        sm.finish_max()

        sm.values(own_slot, own_wide, own_v().astype(BF))

        def p2(j, carry):
            sm.values(j, True, past(1, j).astype(BF))
            return carry
        _past_loop(npast, p2)
        if with_meta:
            sm.values(1, False, vm_s[...])

        lp = lam_ref[...]
        lam = (jnp.exp(jnp.sum(lp[0:1] * lp[1:2], axis=1, keepdims=True))
               - jnp.exp(jnp.sum(lp[2:3] * lp[3:4], axis=1, keepdims=True)) + lam_init)
        head = _lane_group((n, BR_W), DIFF_DV)
        out = None
        for h in range(DIFF_HEADS):
            o_h = jnp.where(head == h, sm.result(2 * h) - lam * sm.result(2 * h + 1), 0.0)
            ms = jnp.sum(o_h * o_h, axis=1, keepdims=True) * (1.0 / DIFF_DV)
            o_h = o_h * lax.rsqrt(ms + RMS_EPS) * sg_ref[...] * (1.0 - lam_init)
            out = o_h if out is None else out + o_h
        o_ref[0, pl.ds(q0, n), :] = out

    _tile_walk(geom,
               lambda q0, n, qpos, npast: tile(q0, n, qpos, npast, bool(geom.n_meta), geom.own_wide, False),
               lambda q0, n, qpos: tile(q0, n, qpos, 0, False, False, True))


def _diff_call(geom, layer, q, k, v, past, tables, lam_p, sg_tiled):
    bsz = q.shape[0]
    lam_init = 0.8 - 0.6 * math.exp(-0.3 * layer)
    tables = [t for t in tables if t is not None]
    ins = [q, k, v] + list(past) + tables + [lam_p, sg_tiled]
    specs = ([_slab_spec(a) for a in (q, k, v)] + [_past_spec(a, layer) for a in past]
             + [_const_spec(a) for a in tables + [lam_p, sg_tiled]])
    npair = 2 * DIFF_HEADS
    return pl.pallas_call(
        functools.partial(_diff_kernel, geom, lam_init), grid=_attn_grid(geom, bsz),
        in_specs=specs, out_specs=pl.BlockSpec((1, geom.lq, BR_W), lambda b, i: (b, 0, 0)),
        out_shape=jax.ShapeDtypeStruct((bsz, geom.lq, BR_W), F32),
        scratch_shapes=(Softmax.scratch(geom, npair, BR_W) + [pltpu.VMEM((npair, geom.tq, BR_W), BF)]
                        + [pltpu.VMEM((NARROW, BR_W), BF)] * 4),
        compiler_params=_cparams(("parallel", "arbitrary")), name="diff_attn")(*ins)


def _sortable(x):
    b = lax.bitcast_convert_type(x + 0.0, I32)
    return b ^ ((b >> 31) & I32(0x7FFFFFFF))


def _dsa_kernel(geom, *refs):
    n_past = 0 if geom.npast is None else 3
    n_bias = 4 if geom.n_meta else 2
    (q_ref, qi_ref, wi_ref, k_ref, v_ref, kit_ref), refs = refs[:6], refs[6:]
    past_refs, refs = refs[:n_past], refs[n_past:]
    bias_refs, refs = list(refs[:n_bias]) + [None] * (4 - n_bias), refs[n_bias:]
    o_ref, refs = refs[0], refs[1:]
    sm_refs, (qm_s, qim_s, wib_s, kw_s, kn_s, last_s,
              kno_s, vno_s, kio_s, knm_s, vnm_s, kim_s) = refs[:Softmax.NREFS], refs[Softmax.NREFS:]
    new, past = _readers(geom, (k_ref, v_ref, kit_ref), past_refs)
    b_own, b_earlier, b_meta, b_meta_own = _bias_readers(geom, *bias_refs, pl.program_id(1))
    topk = float(geom.topk)
    past_pos0 = N_META
    index_bits = max(1, int(geom.lq + geom.lp + N_META).bit_length())
    ones_count = jnp.ones((LANES, LANES), BF)

    def tile(q0, n, qpos, npast, with_meta, own_wide, is_meta):
        nm = geom.n_meta
        m0 = 0
        own_slot = npast if own_wide else 0
        own_cols = n if own_wide else NARROW

        qi = qi_ref[0, pl.ds(q0, n), :]
        igrp = _lane_group(qi.shape, IDX_DIM)
        wi = wi_ref[0, pl.ds(q0, n), :] * (IDX_HEADS ** -0.5 * IDX_DIM ** -0.5)
        for h in range(IDX_HEADS):
            qim_s[h, 0:n, :] = jnp.where(igrp == h, qi, jnp.zeros_like(qi))
            wib_s[h, 0:n, :] = jnp.broadcast_to(wi[:, h:h + 1], (n, LANES))
        q = q_ref[0, pl.ds(q0, n), :]
        head = _lane_group(q.shape, DSA_DH)
        for h in range(DSA_HEADS):
            qm_s[h, 0:n, :] = jnp.where(head == h, q, jnp.zeros_like(q))
        if own_wide:
            own_k, own_v, own_ki = (lambda: new(0, q0, n)), (lambda: new(1, q0, n)), (lambda: new(2, q0, n))
        else:
            _fill_narrow(kno_s, new(0, q0, n))
            _fill_narrow(vno_s, new(1, q0, n))
            _fill_narrow(kio_s, new(2, q0, n))
            own_k, own_v, own_ki = (lambda: kno_s[...]), (lambda: vno_s[...]), (lambda: kio_s[...])
        if with_meta:
            _fill_narrow(knm_s, new(0, m0, nm))
            _fill_narrow(vnm_s, new(1, m0, nm))
            _fill_narrow(kim_s, new(2, m0, nm))

        def index_keys(kit_b):
            raw = _maps_dot(qim_s, n, BR_W, kit_b, transpose_rhs=True)
            score = None
            for h in range(IDX_HEADS):
                t = _rep(wib_s[h, 0:n, :], kit_b.shape[0]) * jnp.maximum(raw[h], 0.0)
                score = t if score is None else score + t
            return _sortable(score)

        own_keys = jnp.where(_own_mask(qpos, n, own_cols, False), index_keys(own_ki()), I32(KEY_NEG))
        if own_wide:
            kw_s[own_slot, 0:n, :] = own_keys
        else:
            pad = lax.broadcasted_iota(I32, (1, NARROW), 1) < n
            kn_s[0, 0:n, :] = jnp.where(pad, own_keys, I32(INT_MIN))

        def score_block(j, carry):
            kw_s[j, 0:n, :] = index_keys(past(2, j).astype(BF))
            return carry
        _past_loop(npast, score_block)
        if with_meta:
            pad = lax.broadcasted_iota(I32, (1, NARROW), 1) < nm
            kn_s[1, 0:n, :] = jnp.where(pad, index_keys(kim_s[...]), I32(INT_MIN))
        nwide = npast + 1 if own_wide else npast

        def wide_sweep(init, fn):
            return _fold_blocks(nwide, fn, init)

        def one(cond):
            return jnp.where(cond, 1.0, 0.0)

        def partial_counts(r0, nr, wide_fn, own_narrow_fn, meta_fn):
            part = wide_sweep(jnp.zeros((nr, LANES), F32), lambda j, a: a + wide_fn(kw_s[j, r0:r0 + nr, :], j))
            if not own_wide:
                part = part + own_narrow_fn(kn_s[0, r0:r0 + nr, :])
            if with_meta:
                part = part + meta_fn(kn_s[1, r0:r0 + nr, :])
            return part

        def lane_sums(part):
            return jnp.dot(part.astype(BF), ones_count, preferred_element_type=F32)

        def count(wide_fn, own_narrow_fn, meta_fn):
            return lane_sums(partial_counts(0, n, wide_fn, own_narrow_fn, meta_fn))

        def count_cmp(cmp):
            return count(lambda x, j: _halves(one(cmp(x, KEY_BLOCK)), jnp.add),
                         lambda x: one(cmp(x, NARROW)), lambda x: one(cmp(x, NARROW)))

        def ge_part(r0, nr, cand):
            return partial_counts(r0, nr, lambda x, j: _halves(one(x >= _rep(cand, KEY_BLOCK)), jnp.add),
                                  lambda x: one(x >= cand), lambda x: one(x >= cand))

        def sign_step(r0, nr):
            c0 = lane_sums(ge_part(r0, nr, jnp.zeros((nr, LANES), I32)))
            return jnp.where(c0 >= topk, I32(0), I32(INT_MIN)), jnp.where(c0 >= topk, c0, topk + 1.0)

        def accept(thr, cnt, cand, c):
            keep = c >= topk
            return jnp.where(keep, cand, thr), jnp.where(keep, c, cnt)

        if is_meta and geom.n_meta <= geom.topk:
            thr = jnp.full((n, LANES), INT_MIN, I32)
            cnt = jnp.zeros((n, LANES), F32)
        else:
            nb = SEARCH_BITS_WHOLE_TILE if n % LANES == 0 else SEARCH_BITS_SHORT_TILE

            def ge_counts(cands):
                zeros = tuple(jnp.zeros((n, LANES), F32) for _ in cands)

                def hits(x, parts):
                    w = x.shape[1]
                    return tuple(p + _halves(one(x >= _rep(c, w)), jnp.add) for p, c in zip(parts, cands))
                parts = wide_sweep(zeros, lambda j, parts: hits(kw_s[j, 0:n, :], parts))
                if not own_wide:
                    parts = hits(kn_s[0, 0:n, :], parts)
                if with_meta:
                    parts = hits(kn_s[1, 0:n, :], parts)
                sums = lane_sums(jnp.concatenate(parts, axis=0))
                return [sums[m * n:(m + 1) * n] for m in range(len(cands))]

            def search_pass(shift, bits, thr, cnt):
                cands = [thr | jnp.left_shift(I32(m), shift) for m in range(1, 2 ** bits)]
                for cand, c in zip(cands, ge_counts(cands)):
                    thr, cnt = accept(thr, cnt, cand, c)
                return thr, cnt
            thr, cnt = lax.fori_loop(0, 31 // nb, lambda t, st: search_pass(I32(31 - nb) - nb * t, nb, *st),
                                     sign_step(0, n))
            if 31 % nb:
                thr, cnt = search_pass(I32(0), 31 % nb, thr, cnt)

        last_s[0:n, :] = jnp.full((n, LANES), INT_MAX, I32)
        has_ties = jnp.max(one((cnt > topk) & (thr > I32(KEY_NEG)))) > 0.0

        @pl.when(has_ties)
        def _():
            need = topk - count_cmp(lambda x, w: x > _rep(thr, w))
            colw = lax.broadcasted_iota(I32, (1, KEY_BLOCK), 1)
            coln = lax.broadcasted_iota(I32, (1, NARROW), 1)

            def wide_pos0(j):
                return past_pos0 + j * KEY_BLOCK

            def ties_before(x):
                return count(lambda kk, j: _halves(one((kk == _rep(thr, KEY_BLOCK))
                                                       & (colw < _rep(x, KEY_BLOCK) - wide_pos0(j))), jnp.add),
                             lambda kk: one((kk == thr) & (coln < x - qpos)),
                             lambda kk: one((kk == thr) & (coln < x)))

            def pos_step(b, last_pos):
                cand = last_pos + jnp.left_shift(I32(1), I32(index_bits - 1) - b)
                return jnp.where(ties_before(cand) < need, cand, last_pos)
            last_s[0:n, :] = lax.fori_loop(0, index_bits, pos_step, jnp.zeros((n, LANES), I32))

        last_pos = last_s[0:n, :]

        def sel_bias(kk, kpos0):
            w = kk.shape[1]
            cols = lax.broadcasted_iota(I32, (1, w), 1)
            t = _rep(thr, w)
            sel = (kk > t) | ((kk == t) & (cols <= _rep(last_pos, w) - kpos0))
            return jnp.where(sel, 0.0, NEG_INF)

        sm = Softmax(sm_refs, DSA_HEADS, n)
        q_of = qm_s
        own_kk = kw_s[own_slot, 0:n, :] if own_wide else kn_s[0, 0:n, :]
        sm.scores(own_slot, own_wide, q_of, own_k(), 1.0,
                  bias_of=(lambda h: b_meta_own(h)) if is_meta else (lambda h: b_own(h, n)),
                  extra=sel_bias(own_kk, qpos))

        def p1(j, carry):
            sm.scores(j, True, q_of, past(0, j).astype(BF), 1.0,
                      bias_of=lambda h: b_earlier(j, h),
                      extra=sel_bias(kw_s[j, 0:n, :], past_pos0 + j * KEY_BLOCK))
            return carry
        _past_loop(npast, p1)
        if with_meta:
            sm.scores(1, False, q_of, knm_s[...], 1.0, bias_of=lambda h: b_meta(h),
                      extra=sel_bias(kn_s[1, 0:n, :], 0))
        sm.finish_max()

        sm.values(own_slot, own_wide, own_v().astype(BF))

        def p2(j, carry):
            sm.values(j, True, past(1, j).astype(BF))
            return carry
        _past_loop(npast, p2)
        if with_meta:
            sm.values(1, False, vnm_s[...])

        out = None
        for h in range(DSA_HEADS):
            o_h = jnp.where(head == h, sm.result(h), 0.0)
            out = o_h if out is None else out + o_h
        o_ref[0, pl.ds(q0, n), :] = out

    _tile_walk(geom,
               lambda q0, n, qpos, npast: tile(q0, n, qpos, npast, bool(geom.n_meta), geom.own_wide, False),
               lambda q0, n, qpos: tile(q0, n, qpos, 0, False, False, True))


def _dsa_call(geom, layer, q, qi, wi, k, v, kit, past, tables):
    bsz = q.shape[0]
    tables = [t for t in tables if t is not None]
    ins = [q, qi, wi, k, v, kit] + list(past) + tables
    specs = ([_slab_spec(a) for a in (q, qi, wi, k, v, kit)] + [_past_spec(a, layer) for a in past]
             + [_const_spec(a) for a in tables])
    tq = geom.tq
    return pl.pallas_call(
        functools.partial(_dsa_kernel, geom), grid=_attn_grid(geom, bsz),
        in_specs=specs, out_specs=pl.BlockSpec((1, geom.lq, BR_W), lambda b, i: (b, 0, 0)),
        out_shape=jax.ShapeDtypeStruct((bsz, geom.lq, BR_W), F32),
        scratch_shapes=(Softmax.scratch(geom, DSA_HEADS, BR_W)
                        + [pltpu.VMEM((DSA_HEADS, tq, BR_W), BF), pltpu.VMEM((IDX_HEADS, tq, BR_W), BF),
                           pltpu.VMEM((IDX_HEADS, tq, LANES), F32),
                           pltpu.VMEM((geom.nslot, tq, KEY_BLOCK), I32), pltpu.VMEM((2, tq, NARROW), I32),
                           pltpu.VMEM((tq, LANES), I32)]
                        + [pltpu.VMEM((NARROW, BR_W), BF)] * 6),
        compiler_params=_cparams(("parallel", "arbitrary")), name="dsa_attn")(*ins)


def _row_tile(t):
    for tm in (768, 384, 256, 128, 64, 32, 16, 8):
        if t % tm == 0:
            return tm
    raise ValueError(f"token count {t} has no supported row tile")


def _rope_tables(pos):
    half = MLA_ROPE // 2
    inv_freq = ROPE_THETA ** (-jnp.arange(half, dtype=jnp.float32) / half)
    ang = pos.astype(jnp.float32)[:, None] * inv_freq[None, :]
    cos, sin = jnp.cos(ang), jnp.sin(ang)
    return jnp.concatenate([cos, cos], axis=1), jnp.concatenate([-sin, sin], axis=1)


def _run_group(geom, x, pos_rows, caches, weights):
    (ln_in_g, ln_in_b, w_in, mla_qnorm_g, mla_w_uq, mla_kvnorm_g, mla_w_uk, mla_w_uv, diff_lambda,
     diff_subln_g, rel_bias, w_br, w_out, ln1_g, ln1_b, w_ff1, b_ff1, w_ff2, b_ff2, ln2_g, ln2_b) = weights
    bsz, lq, d = x.shape
    t = bsz * lq
    tm = _row_tile(t)
    cos32, sin32 = _rope_tables(pos_rows)
    cos_k = jnp.tile(cos32, (bsz, MLA_HEADS))
    sin_k = jnp.tile(sin32, (bsz, MLA_HEADS))
    tables = _bias_tables(geom, rel_bias)
    diff_tables = _split_tables(tables, 0, DIFF_HEADS)
    dsa_tables = _split_tables(tables, DIFF_HEADS, DIFF_HEADS + DSA_HEADS)

    xf = _ln_call(x.reshape(t, d), ln_in_g, ln_in_b, tm)
    rows = None
    for l in range(DEPTH):
        w_uq = mla_w_uq[l].reshape(MLA_Q_LORA, MLA_HEADS, MLA_NOPE + MLA_ROPE)
        wn = w_uq[:, :, :MLA_NOPE].reshape(MLA_Q_LORA, MLA_HEADS * MLA_NOPE).astype(BF)
        wp = w_uq[:, :, MLA_NOPE:].reshape(MLA_Q_LORA, MLA_HEADS * MLA_ROPE).astype(BF)
        wuk = mla_w_uk[l].reshape(MLA_KV_LORA, MLA_HEADS * MLA_NOPE).T.astype(BF)
        wuv = jnp.pad(mla_w_uv[l].reshape(MLA_KV_LORA, MLA_HEADS * MLA_V),
                      ((0, 2 * LANES - MLA_KV_LORA), (0, 0))).astype(BF)
        tm_proj = tm if rows is None else min(tm, PASS_THROUGH_ROW_TILE)
        (*rows, d_wi, qcat, kcat, bq_b, bk_b, bv_b, cq_b, ck_b, cv_b, dq_b, dk_b, dv_b, dqi_b, kit_b) = _proj_call(
            xf, _mix_weight(w_in[l]), mla_kvnorm_g[l].reshape(1, -1), mla_qnorm_g[l].reshape(1, -1),
            wn, wp, wuk, cos_k, sin_k, rows, tm_proj)
        if l == 0:
            rows = [a.reshape((1,) + a.shape) for a in rows]
        per = lambda a: a.reshape(bsz, lq, a.shape[-1])
        if caches is None:
            past = lambda *idx: []
            mla_past, kit_past = [], []
        else:
            past = lambda *idx: [caches[i] for i in idx]
            fill = jnp.zeros(caches[0].shape[1:3] + (2 * LANES - MLA_KV_LORA - MLA_ROPE,), F32).at[..., -1].set(1.0)
            mla_past = [jnp.concatenate([caches[0][l], caches[1][l], fill], axis=-1).astype(BF)]
            kit_past = [jnp.tile(caches[8][l], (1, 1, IDX_HEADS)).astype(BF)]

        o_a = _mla_call(geom, l, per(qcat), per(kcat), mla_past, wuv)
        o_b = _sb_call(geom, l, per(bq_b), per(bk_b), per(bv_b), past(2, 3))
        o_c = _diff_call(geom, l, per(cq_b), per(ck_b), per(cv_b), past(4, 5), diff_tables, diff_lambda[l],
                         jnp.tile(diff_subln_g[l], DIFF_HEADS).reshape(1, -1))
        o_d = _dsa_call(geom, l, per(dq_b), per(dqi_b), per(d_wi), per(dk_b), per(dv_b), per(kit_b),
                        past(6, 7) + kit_past, dsa_tables)

        flat = lambda a: a.reshape(t, BR_W)
        wg = w_in[l][:, IN_OFFS[15]:IN_OFFS[16]].astype(BF)
        x1 = _merge_call(xf, flat(o_a), flat(o_b), flat(o_c), flat(o_d), wg, w_br[l].astype(BF),
                         w_out[l].astype(BF), ln1_g[l].reshape(1, -1), ln1_b[l].reshape(1, -1), tm)
        xf = _ffn_call(x1, w_ff1[l].astype(BF), b_ff1[l].reshape(1, -1), w_ff2[l].astype(BF),
                       b_ff2[l].reshape(1, -1), ln2_g[l].reshape(1, -1), ln2_b[l].reshape(1, -1), tm)
    return xf.reshape(bsz, lq, d), [a.reshape(DEPTH, bsz, lq, a.shape[-1]) for a in rows]


_ROW_TRAILING = ((MLA_KV_LORA,), (MLA_ROPE,), (SB_HEADS, SB_DH), (SB_HEADS, SB_DH),
                 (DIFF_HEADS, 2, DIFF_DQK), (DIFF_HEADS, DIFF_DV), (DSA_HEADS, DSA_DH),
                 (DSA_HEADS, DSA_DH), (IDX_DIM,))


def kernel(x_prompt, x_sample, cache_mla_kv, cache_mla_pe, cache_sb_k, cache_sb_v, cache_diff_k, cache_diff_v, cache_dsa_k, cache_dsa_v, cache_dsa_kidx, meta, ln_in_g, ln_in_b, w_in, mla_qnorm_g, mla_w_uq, mla_kvnorm_g, mla_w_uk, mla_w_uv, diff_lambda, diff_subln_g, rel_bias, w_br, w_out, ln1_g, ln1_b, w_ff1, b_ff1, w_ff2, b_ff2, ln2_g, ln2_b):
    weights = (ln_in_g, ln_in_b, w_in, mla_qnorm_g, mla_w_uq, mla_kvnorm_g, mla_w_uk, mla_w_uv, diff_lambda,
               diff_subln_g, rel_bias, w_br, w_out, ln1_g, ln1_b, w_ff1, b_ff1, w_ff2, b_ff2, ln2_g, ln2_b)
    assert w_in.shape[0] == DEPTH and x_prompt.shape[2] == D_MODEL

    bsz_p, seq_p, _ = x_prompt.shape
    assert seq_p % KEY_BLOCK == 0
    meta_b = jnp.broadcast_to(meta[None].astype(x_prompt.dtype), (bsz_p, N_META, D_MODEL))
    xp = jnp.concatenate([meta_b, x_prompt], axis=1)
    pos_p = jnp.arange(N_META + seq_p, dtype=I32)
    geom_p = Geom(tq=KEY_BLOCK, ntile=seq_p // KEY_BLOCK, npast=None, n_meta=N_META, qpos0=N_META,
                  row0=N_META, lq=seq_p + N_META, lp=0, topk=min(DSA_TOPK, seq_p // 4))
    yp, rows_p = _run_group(geom_p, xp, pos_p, None, weights)
    y_prompt = yp[:, N_META:]
    p_rows = []
    for i, trailing in enumerate(_ROW_TRAILING):
        p_rows.append(rows_p[i].reshape(rows_p[i].shape[:3] + trailing))

    past_len = cache_mla_kv.shape[2]
    bsz_s, dec_seq, _ = x_sample.shape
    assert past_len % KEY_BLOCK == 0 and dec_seq % 16 == 0 and dec_seq <= NARROW
    assert past_len % CHUNK == 0 and dec_seq <= CHUNK, "new frames must share one chunk"
    caches = [c.reshape(c.shape[:3] + (-1,)) for c in
              (cache_mla_kv, cache_mla_pe, cache_sb_k, cache_sb_v, cache_diff_k, cache_diff_v,
               cache_dsa_k, cache_dsa_v, cache_dsa_kidx)]
    pos_s = N_META + past_len + jnp.arange(dec_seq, dtype=I32)
    geom_s = Geom(tq=dec_seq, ntile=1, npast=past_len // KEY_BLOCK, n_meta=0, qpos0=N_META + past_len,
                  row0=0, lq=dec_seq, lp=past_len, topk=min(DSA_TOPK, (past_len + dec_seq) // 4))
    y_sample, rows_s = _run_group(geom_s, x_sample, pos_s, caches, weights)
    s_rows = []
    for i, trailing in enumerate(_ROW_TRAILING):
        s_rows.append(rows_s[i].reshape(rows_s[i].shape[:3] + trailing))

    return (y_prompt, y_sample, *p_rows, *s_rows)
```

```python
import functools
import math
from typing import NamedTuple, Optional

import jax
import jax.numpy as jnp
import numpy as np
from jax import lax
from jax.experimental import pallas as pl
from jax.experimental.pallas import tpu as pltpu

D_MODEL = 1024
CHUNK = 64
N_META = 16
MLA_HEADS = 4
MLA_Q_LORA = 256
MLA_KV_LORA = 128
MLA_NOPE = 64
MLA_ROPE = 32
MLA_V = 64
ROPE_THETA = 10000.0
SB_HEADS = 4
SB_DH = 64
DIFF_HEADS = 4
DIFF_DQK = 32
DIFF_DV = 64
DSA_HEADS = 4
DSA_DH = 64
IDX_HEADS = 8
IDX_DIM = 32
DSA_TOPK = 256
N_BRANCH = 4
BR_W = 256
D_FF = 4 * D_MODEL
T5_BUCKETS = 32
T5_MAX_DIST = 128
LN_EPS = 1e-5
RMS_EPS = 1e-6
NEG_INF = -1e30
DEPTH = 2
DN_ALPHA = (2 * DEPTH) ** 0.25
IN_SIZES = (MLA_Q_LORA, MLA_KV_LORA, MLA_ROPE,
            SB_HEADS * SB_DH, SB_HEADS * SB_DH, SB_HEADS * SB_DH,
            DIFF_HEADS * 2 * DIFF_DQK, DIFF_HEADS * 2 * DIFF_DQK, DIFF_HEADS * DIFF_DV,
            DSA_HEADS * DSA_DH, DSA_HEADS * DSA_DH, DSA_HEADS * DSA_DH,
            IDX_HEADS * IDX_DIM, IDX_DIM, IDX_HEADS,
            N_BRANCH * D_MODEL)
IN_OFFS = tuple(int(s) for s in np.cumsum((0,) + IN_SIZES))

LANES = 128
KEY_BLOCK = 256
NARROW = LANES
VMEM_LIMIT_MB = 56

BF = jnp.bfloat16
F32 = jnp.float32
I32 = jnp.int32

_NEG_BITS = int(np.float32(NEG_INF).view(np.int32))
KEY_NEG = _NEG_BITS ^ ((_NEG_BITS >> 31) & 0x7FFFFFFF)
INT_MIN = -(2 ** 31)
INT_MAX = 2 ** 31 - 1


def _cparams(sem):
    return pltpu.CompilerParams(dimension_semantics=sem, vmem_limit_bytes=VMEM_LIMIT_MB * 1024 * 1024)


def _dot(a, b):
    return jnp.dot(a.astype(BF), b.astype(BF), preferred_element_type=F32)


def _dot_nt(a, b):
    return lax.dot_general(a.astype(BF), b.astype(BF), (((1,), (1,)), ((), ())),
                           preferred_element_type=F32)


def _layer_norm(x, g, b):
    mu = jnp.mean(x, axis=-1, keepdims=True)
    xc = x - mu
    var = jnp.mean(xc * xc, axis=-1, keepdims=True)
    return xc * lax.rsqrt(var + LN_EPS) * g + b


def _rms_norm(x, g):
    return x * lax.rsqrt(jnp.mean(x * x, axis=-1, keepdims=True) + RMS_EPS) * g


def _rope_lanes(x, cos, sin):
    lane = lax.broadcasted_iota(I32, x.shape, 1)
    swapped = jnp.where((lane & 31) < 16, pltpu.roll(x, LANES - 16, 1), pltpu.roll(x, 16, 1))
    return x * cos + swapped * sin


def _lane_group(shape, width):
    return lax.broadcasted_iota(I32, shape, 1) // width


def _halves(x, op):
    return op(x[:, :LANES], x[:, LANES:]) if x.shape[1] == 2 * LANES else x


def _rep(x, width):
    return jnp.concatenate([x, x], axis=1) if width == 2 * LANES else x


class Geom(NamedTuple):
    tq: int
    ntile: int
    npast: Optional[int]
    n_meta: int
    qpos0: int
    row0: int
    lq: int
    lp: int
    topk: int

    @property
    def own_wide(self):
        return self.tq == KEY_BLOCK

    @property
    def nslot(self):
        return max(self.ntile if self.npast is None else self.npast, 1)


def _ln_kernel(x_ref, g_ref, b_ref, o_ref):
    o_ref[...] = _layer_norm(x_ref[...], g_ref[...], b_ref[...])


def _ln_call(x, g, b, tm):
    t, d = x.shape
    return pl.pallas_call(
        _ln_kernel, grid=(t // tm,),
        in_specs=[pl.BlockSpec((tm, d), lambda i: (i, 0)),
                  pl.BlockSpec((1, d), lambda i: (0, 0)),
                  pl.BlockSpec((1, d), lambda i: (0, 0))],
        out_specs=pl.BlockSpec((tm, d), lambda i: (i, 0)),
        out_shape=jax.ShapeDtypeStruct((t, d), F32),
        compiler_params=_cparams(("parallel",)), name="ln_in")(x, g.reshape(1, d), b.reshape(1, d))


_WIDE = (3, 4, 5, 6, 7, 8, 9, 10, 11, 12)
_WIDE_F32 = (1, 2, 4, 5, 7, 8)
_MIX_COLS = 256 + 128 + 256 * len(_WIDE) + LANES + 256 + LANES
_Q_SCALE = {0: SB_DH ** -0.5, 3: DIFF_DQK ** -0.5, 6: DSA_DH ** -0.5}
MLA_SCALE = (MLA_NOPE + MLA_ROPE) ** -0.5
ONES_LANE = 2 * LANES - 1


def _mix_weight(w_in_l):
    def seg(i, pad_to=None):
        w = w_in_l[:, IN_OFFS[i]:IN_OFFS[i + 1]]
        if pad_to is not None:
            w = jnp.pad(w, ((0, 0), (0, pad_to - w.shape[1])))
        return w
    cols = ([seg(0), seg(1)] + [seg(i) for i in _WIDE]
            + [seg(2, LANES), jnp.tile(seg(13), (1, IDX_HEADS)), seg(14, LANES)])
    return jnp.concatenate(cols, axis=1).astype(BF)


N_ROW_SETS = 9


def _proj_kernel(n_prev, x_ref, w_ref, kvg_ref, qg_ref, wn_ref, wp_ref, wuk_ref, cos_ref, sin_ref, *refs):
    prev, outs = (refs[:N_ROW_SETS], refs[N_ROW_SETS:]) if n_prev else ((None,) * N_ROW_SETS, refs)
    row_refs, (dwi_ref, qcat_ref, kcat_ref, *wide_bf) = outs[:N_ROW_SETS], outs[N_ROW_SETS:]
    kit_ref = wide_bf[-1]
    wide_bf = wide_bf[:-1]
    wide_f32 = dict(zip(_WIDE_F32, range(2, 8)))

    def put_rows(k, val):
        if n_prev:
            row_refs[k][0:n_prev] = prev[k][...]
            row_refs[k][n_prev] = val
        else:
            row_refs[k][...] = val
    xb = x_ref[...].astype(BF)
    cos, sin = cos_ref[...], sin_ref[...]

    def seg(off, width):
        return jnp.dot(xb, w_ref[:, off:off + width], preferred_element_type=F32)

    qn = _rms_norm(seg(0, 256), qg_ref[...])
    nope = _dot(qn, wn_ref[...])
    pe = _rope_lanes(_dot(qn, wp_ref[...]), cos, sin)
    head = _lane_group(nope.shape, MLA_NOPE)
    lane = lax.broadcasted_iota(I32, pe.shape, 1)
    for h in range(MLA_HEADS):
        q_lat = _dot(jnp.where(head == h, nope, 0.0), wuk_ref[...]) * MLA_SCALE
        pe_h = (pe if h == 0 else pltpu.roll(pe, LANES - h * MLA_ROPE, 1)) * MLA_SCALE
        qcat_ref[:, 2 * LANES * h:2 * LANES * h + LANES] = q_lat.astype(BF)
        qcat_ref[:, 2 * LANES * h + LANES:2 * LANES * (h + 1)] = jnp.where(lane < MLA_ROPE, pe_h, 0.0).astype(BF)

    ckv = _rms_norm(seg(256, 128), kvg_ref[...])
    put_rows(0, ckv)
    off = 384
    for n, r in enumerate(wide_bf):
        y = seg(off, 256)
        if n in wide_f32:
            put_rows(wide_f32[n], y)
        r[...] = (y * _Q_SCALE[n]).astype(BF) if n in _Q_SCALE else y.astype(BF)
        off += 256
    kpe = _rope_lanes(seg(off, LANES), cos, sin)
    put_rows(1, kpe[:, :MLA_ROPE])
    kcat = jnp.concatenate([ckv, kpe], axis=1)
    kcat_ref[...] = jnp.where(lax.broadcasted_iota(I32, kcat.shape, 1) == ONES_LANE, 1.0, kcat).astype(BF)
    kit = seg(off + LANES, 256)
    put_rows(8, kit[:, :IDX_DIM])
    kit_ref[...] = kit.astype(BF)
    dwi_ref[...] = seg(off + LANES + 256, LANES)[:, :IDX_HEADS]


def _proj_call(x, w_mix, kvg, qg, wn, wp, wuk, cos_k, sin_k, prev_rows, tm):
    t, d = x.shape
    n_prev = 0 if prev_rows is None else prev_rows[0].shape[0]
    row_w = [128, MLA_ROPE, 256, 256, 256, 256, 256, 256, IDX_DIM]
    bf_w = [4 * 256, 256] + [256] * len(_WIDE) + [256]
    row = lambda w: pl.BlockSpec((tm, w), lambda i: (i, 0))
    deep = lambda n, w: pl.BlockSpec((n, tm, w), lambda i: (0, i, 0))
    const = lambda a: pl.BlockSpec(a.shape, lambda i: (0, 0))
    if n_prev:
        row_specs = [deep(n_prev + 1, w) for w in row_w]
        row_shapes = [jax.ShapeDtypeStruct((n_prev + 1, t, w), F32) for w in row_w]
    else:
        row_specs = [row(w) for w in row_w]
        row_shapes = [jax.ShapeDtypeStruct((t, w), F32) for w in row_w]
    return pl.pallas_call(
        functools.partial(_proj_kernel, n_prev), grid=(t // tm,),
        in_specs=([row(d), const(w_mix), const(kvg), const(qg), const(wn), const(wp), const(wuk),
                   row(LANES), row(LANES)] + ([deep(n_prev, w) for w in row_w] if n_prev else [])),
        out_specs=row_specs + [row(IDX_HEADS)] + [row(w) for w in bf_w],
        out_shape=(row_shapes + [jax.ShapeDtypeStruct((t, IDX_HEADS), F32)]
                   + [jax.ShapeDtypeStruct((t, w), BF) for w in bf_w]),
        compiler_params=_cparams(("parallel",)), name="mix_proj")(
            x, w_mix, kvg, qg, wn, wp, wuk, cos_k, sin_k, *(prev_rows or []))


def _merge_kernel(x_ref, oa_ref, ob_ref, oc_ref, od_ref, wg_ref, wbr_ref, wout_ref, g_ref, b_ref, o_ref):
    x = x_ref[...]
    xb = x.astype(BF)
    acc = None
    for n, o_n in enumerate((oa_ref, ob_ref, oc_ref, od_ref)):
        gate = jax.nn.sigmoid(jnp.dot(xb, wg_ref[:, n * D_MODEL:(n + 1) * D_MODEL],
                                      preferred_element_type=F32))
        br = jnp.dot(o_n[...].astype(BF), wbr_ref[n], preferred_element_type=F32)
        acc = gate * br if acc is None else acc + gate * br
    mix = jnp.dot(acc.astype(BF), wout_ref[...], preferred_element_type=F32)
    o_ref[...] = _layer_norm(DN_ALPHA * x + mix, g_ref[...], b_ref[...])


def _merge_call(x, o_a, o_b, o_c, o_d, wg, wbr, wout, g, b, tm):
    t, d = x.shape
    row = lambda w: pl.BlockSpec((tm, w), lambda i: (i, 0))
    return pl.pallas_call(
        _merge_kernel, grid=(t // tm,),
        in_specs=[row(d), row(BR_W), row(BR_W), row(BR_W), row(BR_W),
                  pl.BlockSpec((d, N_BRANCH * d), lambda i: (0, 0)),
                  pl.BlockSpec((N_BRANCH, BR_W, d), lambda i: (0, 0, 0)),
                  pl.BlockSpec((d, d), lambda i: (0, 0)),
                  pl.BlockSpec((1, d), lambda i: (0, 0)), pl.BlockSpec((1, d), lambda i: (0, 0))],
        out_specs=row(d), out_shape=jax.ShapeDtypeStruct((t, d), F32),
        compiler_params=_cparams(("parallel",)), name="merge")(x, o_a, o_b, o_c, o_d, wg, wbr, wout, g, b)


def _ffn_kernel(x_ref, w1_ref, b1_ref, w2_ref, b2_ref, g_ref, b_ref, o_ref):
    x = x_ref[...]
    xb = x.astype(BF)
    acc = None
    for c in range(D_FF // D_MODEL):
        sl = slice(c * D_MODEL, (c + 1) * D_MODEL)
        h = jnp.dot(xb, w1_ref[:, sl], preferred_element_type=F32) + b1_ref[:, sl]
        h = jnp.square(jnp.maximum(h, 0.0))
        y = jnp.dot(h.astype(BF), w2_ref[sl, :], preferred_element_type=F32)
        acc = y if acc is None else acc + y
    o_ref[...] = _layer_norm(DN_ALPHA * x + acc + b2_ref[...], g_ref[...], b_ref[...])


def _ffn_call(x, w1, b1, w2, b2, g, b, tm):
    t, d = x.shape
    row = pl.BlockSpec((tm, d), lambda i: (i, 0))
    vec = lambda w: pl.BlockSpec((1, w), lambda i: (0, 0))
    return pl.pallas_call(
        _ffn_kernel, grid=(t // tm,),
        in_specs=[row, pl.BlockSpec((d, D_FF), lambda i: (0, 0)), vec(D_FF),
                  pl.BlockSpec((D_FF, d), lambda i: (0, 0)), vec(d), vec(d), vec(d)],
        out_specs=row, out_shape=jax.ShapeDtypeStruct((t, d), F32),
        compiler_params=_cparams(("parallel",)), name="ffn")(x, w1, b1, w2, b2, g, b)


def _t5_bucket(rel):
    nb = T5_BUCKETS // 2
    max_exact = nb // 2
    n = jnp.abs(rel)
    nf = jnp.maximum(n, 1).astype(jnp.float32)
    large = max_exact + (jnp.log(nf / max_exact) / math.log(T5_MAX_DIST / max_exact)
                         * (nb - max_exact)).astype(jnp.int32)
    large = jnp.minimum(large, nb - 1)
    return jnp.where(rel > 0, nb, 0) + jnp.where(n < max_exact, n, large)


def _bucket_ids(rel0s, rows, cols, valid_cols, qpos0_for_mask=None):
    r = jnp.arange(rows, dtype=I32)[:, None]
    c = jnp.arange(cols, dtype=I32)[None, :]
    ok = jnp.broadcast_to(c < valid_cols, (rows, cols))
    if qpos0_for_mask is not None:
        ok = ok & (((qpos0_for_mask - N_META + c) >> 6) <= ((qpos0_for_mask - N_META + r) >> 6))
    return jnp.stack([jnp.where(ok, _t5_bucket(I32(rel0) + c - r), -1) for rel0 in rel0s], axis=0)


def _bias_kernel(tab_ref, bk_ref, o_ref):
    bk = bk_ref[0]
    for h in range(DIFF_HEADS + DSA_HEADS):
        acc = jnp.where(bk < 0, NEG_INF, 0.0).astype(F32)
        for b in range(T5_BUCKETS):
            acc = jnp.where(bk == b, tab_ref[b, h], acc)
        o_ref[0, h] = acc


def _bias_call(rel_bias, bucket_ids):
    n, rows, cols = bucket_ids.shape
    nh = DIFF_HEADS + DSA_HEADS
    return pl.pallas_call(
        _bias_kernel, grid=(n,),
        in_specs=[pl.BlockSpec(memory_space=pltpu.SMEM),
                  pl.BlockSpec((1, rows, cols), lambda i: (i, 0, 0))],
        out_specs=pl.BlockSpec((1, nh, rows, cols), lambda i: (i, 0, 0, 0)),
        out_shape=jax.ShapeDtypeStruct((n, nh, rows, cols), F32),
        compiler_params=_cparams(("parallel",)), name="rel_bias_table")(rel_bias, bucket_ids)


def _bias_tables(geom, rel_bias):
    tq, kb = geom.tq, KEY_BLOCK
    own_cols = tq if geom.own_wide else NARROW
    own = _bias_call(rel_bias, _bucket_ids([0], tq, own_cols, tq, geom.qpos0))
    if geom.npast is None:
        assert tq == kb and kb + 1 >= T5_MAX_DIST
        assert geom.qpos0 + tq - (geom.n_meta - 1) >= T5_MAX_DIST
        past = _bias_call(rel_bias, _bucket_ids([-kb, -2 * kb], tq, kb, kb))
        meta = _bias_call(rel_bias, _bucket_ids([-geom.qpos0, -geom.qpos0 - tq], tq, NARROW, geom.n_meta))
        mown = _bias_call(rel_bias, _bucket_ids([0], geom.n_meta, NARROW, geom.n_meta, 0))
        return own, past, meta, mown
    past = _bias_call(rel_bias, _bucket_ids([N_META + j * kb - geom.qpos0 for j in range(geom.npast)], tq, kb, kb))
    return own, past, None, None


def _split_tables(tables, lo, hi):
    return [None if t is None else t[:, lo:hi] for t in tables]


def _no_past(npast):
    return isinstance(npast, int) and npast == 0


STATIC_UNROLL = 4
SEARCH_BITS_WHOLE_TILE = 1
SEARCH_BITS_SHORT_TILE = 3
PASS_THROUGH_ROW_TILE = 384


def _aligned(x, m):
    return x if isinstance(x, int) else pl.multiple_of(x, m)


def _fold_blocks(count, fn, init):
    if isinstance(count, int) and count <= STATIC_UNROLL:
        for j in range(count):
            init = fn(j, init)
        return init
    carry = lax.fori_loop(0, count // 2, lambda t, c: fn(2 * t + 1, fn(2 * t, c)), init)
    return lax.cond(count % 2 == 1, lambda c: fn(count - 1, c), lambda c: c, carry)


def _past_loop(npast, body):
    _fold_blocks(npast, body, 0)


def _tile_walk(geom, frames_tile, meta_tile):
    i = pl.program_id(1)

    @pl.when(i < geom.ntile)
    def _():
        q0 = _aligned(geom.row0 + i * geom.tq, 16)
        npast = i if geom.npast is None else geom.npast
        frames_tile(q0, geom.tq, geom.qpos0 + i * geom.tq, npast)

    if geom.n_meta:
        @pl.when(i == geom.ntile)
        def _():
            meta_tile(0, geom.n_meta, 0)


def _own_mask(qpos, n, cols, causal):
    r = lax.broadcasted_iota(I32, (n, 1), 0)
    c = lax.broadcasted_iota(I32, (1, cols), 1)
    if causal:
        return c < r
    return (c < n) & (((qpos - N_META + c) >> 6) <= ((qpos - N_META + r) >> 6))


def _slab_spec(arr):
    return pl.BlockSpec((1,) + arr.shape[1:], lambda b, i: (b, 0, 0))


def _cache_spec(arr, layer):
    return pl.BlockSpec((1, 1) + arr.shape[2:], lambda b, i: (layer, b, 0, 0))


def _past_spec(arr, layer):
    return _cache_spec(arr, layer) if arr.ndim == 4 else _slab_spec(arr)


def _const_spec(arr):
    nd = arr.ndim
    return pl.BlockSpec(arr.shape, lambda b, i: (0,) * nd)


def _readers(geom, new_refs, past_refs):
    def new(k, r0, n):
        return new_refs[k][0, pl.ds(r0, n), :]

    def past(k, j):
        if geom.npast is None:
            return new_refs[k][0, pl.ds(_aligned(geom.row0 + j * KEY_BLOCK, 16), KEY_BLOCK), :]
        ref, r0 = past_refs[k], _aligned(j * KEY_BLOCK, KEY_BLOCK)
        return ref[0, 0, pl.ds(r0, KEY_BLOCK), :] if len(ref.shape) == 4 else ref[0, pl.ds(r0, KEY_BLOCK), :]
    return new, past


def _fill_narrow(dst_ref, rows):
    dst_ref[...] = jnp.zeros(dst_ref.shape, dst_ref.dtype)
    dst_ref[0:rows.shape[0], :] = rows.astype(dst_ref.dtype)


def _attn_grid(geom, bsz):
    return (bsz, geom.ntile + (1 if geom.n_meta else 0))


def _maps_dot(lhs, n, w, rhs, transpose_rhs=False):
    dot = _dot_nt if transpose_rhs else _dot
    if isinstance(lhs, (list, tuple)):
        return [dot(x, rhs) for x in lhs]
    nmaps = lhs.shape[0]
    if n % LANES == 0:
        out = dot(lhs[:, 0:n, 0:w].reshape(nmaps * n, w), rhs)
        return [out[p * n:(p + 1) * n] for p in range(nmaps)]
    return [dot(lhs[p, 0:n, 0:w], rhs) for p in range(nmaps)]


class Softmax:
    NREFS = 7

    def __init__(self, refs, nmaps, n, ones_lane=None):
        self.sw, self.sn, self.mx, self.m, self.ls, self.acc, self.e = refs
        self.nmaps, self.n, self.ones_lane = nmaps, n, ones_lane
        self.fresh_scores = self.fresh_values = True

    @staticmethod
    def scratch(geom, nmaps, vw):
        return [pltpu.VMEM((nmaps, geom.nslot, geom.tq, KEY_BLOCK), F32),
                pltpu.VMEM((nmaps, 2, geom.tq, NARROW), F32),
                pltpu.VMEM((nmaps, geom.tq, LANES), F32), pltpu.VMEM((nmaps, geom.tq, LANES), F32),
                pltpu.VMEM((nmaps, geom.tq, LANES), F32), pltpu.VMEM((nmaps, geom.tq, vw), F32),
                pltpu.VMEM((nmaps, geom.tq, KEY_BLOCK), BF)]

    def _store(self, p, slot, wide, s):
        if wide:
            self.sw[p, slot, 0:self.n, :] = s
        else:
            self.sn[p, slot, 0:self.n, :] = s

    def _load(self, p, slot, wide):
        return self.sw[p, slot, 0:self.n, :] if wide else self.sn[p, slot, 0:self.n, :]

    def scores(self, slot, wide, q_maps, k_b, scale, bias_of=None, extra=None):
        n = self.n
        d = q_maps[0].shape[-1] if isinstance(q_maps, (list, tuple)) else q_maps.shape[2]
        s_all = _maps_dot(q_maps, n, d, k_b, transpose_rhs=True)
        for p in range(self.nmaps):
            s = s_all[p]
            if scale != 1.0:
                s = s * scale
            if bias_of is not None:
                s = s + bias_of(p)
            if extra is not None:
                s = s + extra
            self._store(p, slot, wide, s)
            top = _halves(s, jnp.maximum)
            self.mx[p, 0:n, :] = top if self.fresh_scores else jnp.maximum(self.mx[p, 0:n, :], top)
        self.fresh_scores = False

    def finish_max(self):
        n = self.n
        for p in range(self.nmaps):
            m = jnp.max(self.mx[p, 0:n, :], axis=1, keepdims=True)
            self.m[p, 0:n, :] = jnp.broadcast_to(m, (n, LANES))

    def values(self, slot, wide, v_b):
        n, w = self.n, (KEY_BLOCK if wide else NARROW)
        staged = n % LANES == 0
        es = []
        for p in range(self.nmaps):
            s = self._load(p, slot, wide)
            e = jnp.exp(s - _rep(self.m[p, 0:n, :], w))
            if self.ones_lane is None:
                part = _halves(e, jnp.add)
                self.ls[p, 0:n, :] = part if self.fresh_values else self.ls[p, 0:n, :] + part
            if staged:
                self.e[p, 0:n, 0:w] = e.astype(BF)
            else:
                es.append(e)
        for p, pv in enumerate(_maps_dot(self.e if staged else es, n, w, v_b)):
            self.acc[p, 0:n, :] = pv if self.fresh_values else self.acc[p, 0:n, :] + pv
        self.fresh_values = False

    def result(self, p):
        n = self.n
        acc = self.acc[p, 0:n, :]
        if self.ones_lane is not None:
            return acc / acc[:, self.ones_lane:self.ones_lane + 1]
        return acc / jnp.sum(self.ls[p, 0:n, :], axis=1, keepdims=True)


def _mla_kernel(geom, *refs):
    n_past = 0 if geom.npast is None else 1
    (q_ref, kc_ref), refs = refs[:2], refs[2:]
    past_refs, refs = refs[:n_past], refs[n_past:]
    (wuv_ref, o_ref), refs = refs[:2], refs[2:]
    sm_refs, (qm_s, kn_s, km_s) = refs[:Softmax.NREFS], refs[Softmax.NREFS:]
    new, past = _readers(geom, (kc_ref,), past_refs)
    scale = 1.0

    def tile(q0, n, qpos, npast, with_meta, own_wide):
        sm = Softmax(sm_refs, MLA_HEADS, n, ones_lane=ONES_LANE)
        q_heads = lambda: [q_ref[0, pl.ds(q0, n), 2 * LANES * h:2 * LANES * (h + 1)] for h in range(MLA_HEADS)]
        if n % LANES == 0:
            for h, q_h in enumerate(q_heads()):
                qm_s[h, 0:n, :] = q_h
        q_src = lambda: qm_s if n % LANES == 0 else q_heads()
        own_slot = npast if own_wide else 0
        if own_wide:
            own_k = lambda: new(0, q0, n)
        else:
            _fill_narrow(kn_s, new(0, q0, n))
            own_k = lambda: kn_s[...]
        own_cols = n if own_wide else NARROW
        sm.scores(own_slot, own_wide, q_src(), own_k(), scale,
                  extra=jnp.where(_own_mask(qpos, n, own_cols, False), 0.0, NEG_INF))

        def p1(j, carry):
            sm.scores(j, True, q_src(), past(0, j), scale)
            return carry
        _past_loop(npast, p1)
        if with_meta:
            _fill_narrow(km_s, new(0, 0, geom.n_meta))
            pad = jnp.where(lax.broadcasted_iota(I32, (1, NARROW), 1) < geom.n_meta, 0.0, NEG_INF)
            sm.scores(1, False, q_src(), km_s[...], scale, extra=pad)
        sm.finish_max()

        sm.values(own_slot, own_wide, own_k())

        def p2(j, carry):
            sm.values(j, True, past(0, j))
            return carry
        _past_loop(npast, p2)
        if with_meta:
            sm.values(1, False, km_s[...])

        out = None
        ohead = _lane_group((n, BR_W), MLA_V)
        for h in range(MLA_HEADS):
            o_h = jnp.where(ohead == h, _dot(sm.result(h), wuv_ref[...]), 0.0)
            out = o_h if out is None else out + o_h
        o_ref[0, pl.ds(q0, n), :] = out.astype(o_ref.dtype)

    _tile_walk(geom,
               lambda q0, n, qpos, npast: tile(q0, n, qpos, npast, bool(geom.n_meta), geom.own_wide),
               lambda q0, n, qpos: tile(q0, n, qpos, 0, False, False))


def _mla_call(geom, layer, qcat, kcat, past, wuv_pad):
    bsz = qcat.shape[0]
    ins = [qcat, kcat] + list(past) + [wuv_pad]
    specs = ([_slab_spec(a) for a in (qcat, kcat)] + [_past_spec(a, layer) for a in past] + [_const_spec(wuv_pad)])
    return pl.pallas_call(
        functools.partial(_mla_kernel, geom), grid=_attn_grid(geom, bsz),
        in_specs=specs, out_specs=pl.BlockSpec((1, geom.lq, BR_W), lambda b, i: (b, 0, 0)),
        out_shape=jax.ShapeDtypeStruct((bsz, geom.lq, BR_W), BF),
        scratch_shapes=(Softmax.scratch(geom, MLA_HEADS, 2 * LANES)
                        + [pltpu.VMEM((MLA_HEADS, geom.tq, 2 * LANES), BF)] + [pltpu.VMEM((NARROW, 2 * LANES), BF)] * 2),
        compiler_params=_cparams(("parallel", "arbitrary")), name="mla_attn")(*ins)


def _suffix_matrix(n):
    later = lax.broadcasted_iota(I32, (n, n), 0) > lax.broadcasted_iota(I32, (n, n), 1)
    return jnp.where(later, 1.0, 0.0).astype(BF)


def _sb_kernel(geom, *refs):
    n_past = 0 if geom.npast is None else 2
    (q_ref, k_ref, v_ref), refs = refs[:3], refs[3:]
    past_refs, (o_ref, acc_s, carry_s, qm_s, hl_s, lb_s, a_s, tri_s, kn_s, vn_s, km_s, vm_s) = \
        refs[:n_past], refs[n_past:]
    new, past = _readers(geom, (k_ref, v_ref), past_refs)

    def tile(q0, n, qpos, npast, with_meta, own_wide):
        q = q_ref[0, pl.ds(q0, n), :]
        head = _lane_group(q.shape, SB_DH)
        for h in range(SB_HEADS):
            qm_s[h, 0:n, :] = jnp.where(head == h, q, jnp.zeros_like(q))
        acc_s[:, 0:n, :] = jnp.zeros((SB_HEADS, n, BR_W), F32)
        carry_s[:, 0:n, :] = jnp.zeros((SB_HEADS, n, LANES), F32)

        def block(k_b, v_b, mask, tri):
            w = k_b.shape[0]
            staged = n % LANES == 0
            z_all = _maps_dot(qm_s, n, BR_W, k_b, transpose_rhs=True)
            halves, log_bs, firsts = [], [], []
            for h in range(SB_HEADS):
                z = z_all[h]
                soft = jnp.log(1.0 + jnp.exp(-jnp.abs(z)))
                log_b = jnp.minimum(z, 0.0) - soft
                log_1m = log_b - z
                if mask is not None:
                    log_1m = jnp.where(mask, log_1m, 0.0)
                firsts.append(log_1m[:, 0:1])
                hi = log_1m.astype(BF)
                lo = (log_1m - hi.astype(F32)).astype(BF)
                if staged:
                    hl_s[2 * h, 0:n, 0:w], hl_s[2 * h + 1, 0:n, 0:w], lb_s[h, 0:n, 0:w] = hi, lo, log_b
                else:
                    halves += [hi, lo]
                    log_bs.append(log_b)
            sums_all = _maps_dot(hl_s if staged else halves, n, w, tri)
            weights = []
            for h in range(SB_HEADS):
                later = sums_all[2 * h] + sums_all[2 * h + 1]
                carry = carry_s[h, 0:n, :]
                log_b = lb_s[h, 0:n, 0:w] if staged else log_bs[h]
                a = jnp.exp(log_b + later + _rep(carry, w))
                if mask is not None:
                    a = jnp.where(mask, a, 0.0)
                if staged:
                    a_s[h, 0:n, 0:w] = a.astype(BF)
                else:
                    weights.append(a)
                carry_s[h, 0:n, :] = carry + jnp.broadcast_to(later[:, 0:1] + firsts[h], (n, LANES))
            for h, pv in enumerate(_maps_dot(a_s if staged else weights, n, w, v_b)):
                acc_s[h, 0:n, :] += pv

        if own_wide:
            block(new(0, q0, n), new(1, q0, n), _own_mask(qpos, n, n, True), _suffix_matrix(n))
        else:
            _fill_narrow(kn_s, new(0, q0, n))
            _fill_narrow(vn_s, new(1, q0, n))
            block(kn_s[...], vn_s[...], _own_mask(qpos, n, NARROW, True), _suffix_matrix(NARROW))
        if _no_past(npast):
            tri_kb = None
        elif n % LANES == 0:
            tri_s[...] = _suffix_matrix(KEY_BLOCK)
            tri_kb = lambda: tri_s[...]
        else:
            tri_value = _suffix_matrix(KEY_BLOCK)
            tri_kb = lambda: tri_value

        def past_block(jj, carry):
            j = npast - 1 - jj
            block(past(0, j).astype(BF), past(1, j).astype(BF), None, tri_kb())
            return carry
        _past_loop(npast, past_block)
        if with_meta:
            m0 = 0
            _fill_narrow(km_s, new(0, m0, geom.n_meta))
            _fill_narrow(vm_s, new(1, m0, geom.n_meta))
            pad = jnp.broadcast_to(lax.broadcasted_iota(I32, (1, NARROW), 1) < geom.n_meta, (n, NARROW))
            block(km_s[...], vm_s[...], pad, _suffix_matrix(NARROW))

        out = None
        for h in range(SB_HEADS):
            o_h = jnp.where(head == h, acc_s[h, 0:n, :], 0.0)
            out = o_h if out is None else out + o_h
        o_ref[0, pl.ds(q0, n), :] = out.astype(o_ref.dtype)

    _tile_walk(geom,
               lambda q0, n, qpos, npast: tile(q0, n, qpos, npast, bool(geom.n_meta), geom.own_wide),
               lambda q0, n, qpos: tile(q0, n, qpos, 0, False, False))


def _sb_call(geom, layer, q, k, v, past):
    bsz = q.shape[0]
    ins = [q, k, v] + list(past)
    specs = [_slab_spec(a) for a in (q, k, v)] + [_past_spec(a, layer) for a in past]
    return pl.pallas_call(
        functools.partial(_sb_kernel, geom), grid=_attn_grid(geom, bsz),
        in_specs=specs, out_specs=pl.BlockSpec((1, geom.lq, BR_W), lambda b, i: (b, 0, 0)),
        out_shape=jax.ShapeDtypeStruct((bsz, geom.lq, BR_W), BF),
        scratch_shapes=[pltpu.VMEM((SB_HEADS, geom.tq, BR_W), F32), pltpu.VMEM((SB_HEADS, geom.tq, LANES), F32),
                        pltpu.VMEM((SB_HEADS, geom.tq, BR_W), BF),
                        pltpu.VMEM((2 * SB_HEADS, geom.tq, KEY_BLOCK), BF), pltpu.VMEM((SB_HEADS, geom.tq, KEY_BLOCK), F32),
                        pltpu.VMEM((SB_HEADS, geom.tq, KEY_BLOCK), BF), pltpu.VMEM((KEY_BLOCK, KEY_BLOCK), BF)]
                       + [pltpu.VMEM((NARROW, BR_W), BF)] * 4,
        compiler_params=_cparams(("parallel", "arbitrary")), name="sb_attn")(*ins)


def _bias_readers(geom, own_ref, past_ref, meta_ref, mown_ref, tile_idx):
    def own(h, n):
        return own_ref[0, h, 0:n, :]

    def earlier(j, h):
        slot = j if geom.npast is not None else jnp.where(j == tile_idx - 1, 0, 1)
        return past_ref[slot, h]

    def meta(h):
        return meta_ref[jnp.minimum(tile_idx, 1), h]

    def meta_own(h):
        return mown_ref[0, h]
    return own, earlier, meta, meta_own


def _diff_kernel(geom, lam_init, *refs):
    n_past = 0 if geom.npast is None else 2
    n_bias = 4 if geom.n_meta else 2
    (q_ref, k_ref, v_ref), refs = refs[:3], refs[3:]
    past_refs, refs = refs[:n_past], refs[n_past:]
    bias_refs, refs = list(refs[:n_bias]) + [None] * (4 - n_bias), refs[n_bias:]
    (lam_ref, sg_ref, o_ref), refs = refs[:3], refs[3:]
    sm_refs, (qm_s, kn_s, vn_s, km_s, vm_s) = refs[:Softmax.NREFS], refs[Softmax.NREFS:]
    new, past = _readers(geom, (k_ref, v_ref), past_refs)
    b_own, b_earlier, b_meta, b_meta_own = _bias_readers(geom, *bias_refs, pl.program_id(1))
    scale = 1.0
    npair = 2 * DIFF_HEADS

    def tile(q0, n, qpos, npast, with_meta, own_wide, is_meta):
        sm = Softmax(sm_refs, npair, n)
        q = q_ref[0, pl.ds(q0, n), :]
        pair = _lane_group(q.shape, DIFF_DQK)
        for p in range(npair):
            qm_s[p, 0:n, :] = jnp.where(pair == p, q, jnp.zeros_like(q))
        q_of = qm_s
        own_slot = npast if own_wide else 0
        if own_wide:
            own_k, own_v = (lambda: new(0, q0, n)), (lambda: new(1, q0, n))
        else:
            _fill_narrow(kn_s, new(0, q0, n))
            _fill_narrow(vn_s, new(1, q0, n))
            own_k, own_v = (lambda: kn_s[...]), (lambda: vn_s[...])
        sm.scores(own_slot, own_wide, q_of, own_k(), scale,
                  bias_of=(lambda p: b_meta_own(p // 2)) if is_meta else (lambda p: b_own(p // 2, n)))

        def p1(j, carry):
            sm.scores(j, True, q_of, past(0, j), scale,
                      bias_of=lambda p: b_earlier(j, p // 2))
            return carry
        _past_loop(npast, p1)
        if with_meta:
            m0 = 0
            _fill_narrow(km_s, new(0, m0, geom.n_meta))
            _fill_narrow(vm_s, new(1, m0, geom.n_meta))
            sm.scores(1, False, q_of, km_s[...], scale, bias_of=lambda p: b_meta(p // 2))
        sm.finish_max()

        sm.values(own_slot, own_wide, own_v().astype(BF))

        def p2(j, carry):
            sm.values(j, True, past(1, j).astype(BF))
            return carry
        _past_loop(npast, p2)
        if with_meta:
            sm.values(1, False, vm_s[...])

        lp = lam_ref[...]
        lam = (jnp.exp(jnp.sum(lp[0:1] * lp[1:2], axis=1, keepdims=True))
               - jnp.exp(jnp.sum(lp[2:3] * lp[3:4], axis=1, keepdims=True)) + lam_init)
        head = _lane_group((n, BR_W), DIFF_DV)
        out = None
        for h in range(DIFF_HEADS):
            o_h = jnp.where(head == h, sm.result(2 * h) - lam * sm.result(2 * h + 1), 0.0)
            ms = jnp.sum(o_h * o_h, axis=1, keepdims=True) * (1.0 / DIFF_DV)
            o_h = o_h * lax.rsqrt(ms + RMS_EPS) * sg_ref[...] * (1.0 - lam_init)
            out = o_h if out is None else out + o_h
        o_ref[0, pl.ds(q0, n), :] = out.astype(o_ref.dtype)

    _tile_walk(geom,
               lambda q0, n, qpos, npast: tile(q0, n, qpos, npast, bool(geom.n_meta), geom.own_wide, False),
               lambda q0, n, qpos: tile(q0, n, qpos, 0, False, False, True))


def _diff_call(geom, layer, q, k, v, past, tables, lam_p, sg_tiled):
    bsz = q.shape[0]
    lam_init = 0.8 - 0.6 * math.exp(-0.3 * layer)
    tables = [t for t in tables if t is not None]
    ins = [q, k, v] + list(past) + tables + [lam_p, sg_tiled]
    specs = ([_slab_spec(a) for a in (q, k, v)] + [_past_spec(a, layer) for a in past]
             + [_const_spec(a) for a in tables + [lam_p, sg_tiled]])
    npair = 2 * DIFF_HEADS
    return pl.pallas_call(
        functools.partial(_diff_kernel, geom, lam_init), grid=_attn_grid(geom, bsz),
        in_specs=specs, out_specs=pl.BlockSpec((1, geom.lq, BR_W), lambda b, i: (b, 0, 0)),
        out_shape=jax.ShapeDtypeStruct((bsz, geom.lq, BR_W), BF),
        scratch_shapes=(Softmax.scratch(geom, npair, BR_W) + [pltpu.VMEM((npair, geom.tq, BR_W), BF)]
                        + [pltpu.VMEM((NARROW, BR_W), BF)] * 4),
        compiler_params=_cparams(("parallel", "arbitrary")), name="diff_attn")(*ins)


def _sortable(x):
    b = lax.bitcast_convert_type(x + 0.0, I32)
    return b ^ ((b >> 31) & I32(0x7FFFFFFF))


def _dsa_kernel(geom, *refs):
    n_past = 0 if geom.npast is None else 3
    n_bias = 4 if geom.n_meta else 2
    (q_ref, qi_ref, wi_ref, k_ref, v_ref, kit_ref), refs = refs[:6], refs[6:]
    past_refs, refs = refs[:n_past], refs[n_past:]
    bias_refs, refs = list(refs[:n_bias]) + [None] * (4 - n_bias), refs[n_bias:]
    o_ref, refs = refs[0], refs[1:]
    sm_refs, (qm_s, qim_s, wib_s, kw_s, kn_s, last_s,
              kno_s, vno_s, kio_s, knm_s, vnm_s, kim_s) = refs[:Softmax.NREFS], refs[Softmax.NREFS:]
    new, past = _readers(geom, (k_ref, v_ref, kit_ref), past_refs)
    b_own, b_earlier, b_meta, b_meta_own = _bias_readers(geom, *bias_refs, pl.program_id(1))
    topk = float(geom.topk)
    past_pos0 = N_META
    index_bits = max(1, int(geom.lq + geom.lp + N_META).bit_length())
    ones_count = jnp.ones((LANES, LANES), BF)

    def tile(q0, n, qpos, npast, with_meta, own_wide, is_meta):
        nm = geom.n_meta
        m0 = 0
        own_slot = npast if own_wide else 0
        own_cols = n if own_wide else NARROW

        qi = qi_ref[0, pl.ds(q0, n), :]
        igrp = _lane_group(qi.shape, IDX_DIM)
        wi = wi_ref[0, pl.ds(q0, n), :] * (IDX_HEADS ** -0.5 * IDX_DIM ** -0.5)
        for h in range(IDX_HEADS):
            qim_s[h, 0:n, :] = jnp.where(igrp == h, qi, jnp.zeros_like(qi))
            wib_s[h, 0:n, :] = jnp.broadcast_to(wi[:, h:h + 1], (n, LANES))
        q = q_ref[0, pl.ds(q0, n), :]
        head = _lane_group(q.shape, DSA_DH)
        for h in range(DSA_HEADS):
            qm_s[h, 0:n, :] = jnp.where(head == h, q, jnp.zeros_like(q))
        if own_wide:
            own_k, own_v, own_ki = (lambda: new(0, q0, n)), (lambda: new(1, q0, n)), (lambda: new(2, q0, n))
        else:
            _fill_narrow(kno_s, new(0, q0, n))
            _fill_narrow(vno_s, new(1, q0, n))
            _fill_narrow(kio_s, new(2, q0, n))
            own_k, own_v, own_ki = (lambda: kno_s[...]), (lambda: vno_s[...]), (lambda: kio_s[...])
        if with_meta:
            _fill_narrow(knm_s, new(0, m0, nm))
            _fill_narrow(vnm_s, new(1, m0, nm))
            _fill_narrow(kim_s, new(2, m0, nm))

        def index_keys(kit_b):
            raw = _maps_dot(qim_s, n, BR_W, kit_b, transpose_rhs=True)
            score = None
            for h in range(IDX_HEADS):
                t = _rep(wib_s[h, 0:n, :], kit_b.shape[0]) * jnp.maximum(raw[h], 0.0)
                score = t if score is None else score + t
            return _sortable(score)

        own_keys = jnp.where(_own_mask(qpos, n, own_cols, False), index_keys(own_ki()), I32(KEY_NEG))
        if own_wide:
            kw_s[own_slot, 0:n, :] = own_keys
        else:
            pad = lax.broadcasted_iota(I32, (1, NARROW), 1) < n
            kn_s[0, 0:n, :] = jnp.where(pad, own_keys, I32(INT_MIN))

        def score_block(j, carry):
            kw_s[j, 0:n, :] = index_keys(past(2, j).astype(BF))
            return carry
        _past_loop(npast, score_block)
        if with_meta:
            pad = lax.broadcasted_iota(I32, (1, NARROW), 1) < nm
            kn_s[1, 0:n, :] = jnp.where(pad, index_keys(kim_s[...]), I32(INT_MIN))
        nwide = npast + 1 if own_wide else npast

        def wide_sweep(init, fn):
            return _fold_blocks(nwide, fn, init)

        def one(cond):
            return jnp.where(cond, 1.0, 0.0)

        def partial_counts(r0, nr, wide_fn, own_narrow_fn, meta_fn):
            part = wide_sweep(jnp.zeros((nr, LANES), F32), lambda j, a: a + wide_fn(kw_s[j, r0:r0 + nr, :], j))
            if not own_wide:
                part = part + own_narrow_fn(kn_s[0, r0:r0 + nr, :])
            if with_meta:
                part = part + meta_fn(kn_s[1, r0:r0 + nr, :])
            return part

        def lane_sums(part):
            return jnp.dot(part.astype(BF), ones_count, preferred_element_type=F32)

        def count(wide_fn, own_narrow_fn, meta_fn):
            return lane_sums(partial_counts(0, n, wide_fn, own_narrow_fn, meta_fn))

        def count_cmp(cmp):
            return count(lambda x, j: _halves(one(cmp(x, KEY_BLOCK)), jnp.add),
                         lambda x: one(cmp(x, NARROW)), lambda x: one(cmp(x, NARROW)))

        def ge_part(r0, nr, cand):
            return partial_counts(r0, nr, lambda x, j: _halves(one(x >= _rep(cand, KEY_BLOCK)), jnp.add),
                                  lambda x: one(x >= cand), lambda x: one(x >= cand))

        def sign_step(r0, nr):
            c0 = lane_sums(ge_part(r0, nr, jnp.zeros((nr, LANES), I32)))
            return jnp.where(c0 >= topk, I32(0), I32(INT_MIN)), jnp.where(c0 >= topk, c0, topk + 1.0)

        def accept(thr, cnt, cand, c):
            keep = c >= topk
            return jnp.where(keep, cand, thr), jnp.where(keep, c, cnt)

        if is_meta and geom.n_meta <= geom.topk:
            thr = jnp.full((n, LANES), INT_MIN, I32)
            cnt = jnp.zeros((n, LANES), F32)
        else:
            nb = SEARCH_BITS_WHOLE_TILE if n % LANES == 0 else SEARCH_BITS_SHORT_TILE

            def ge_counts(cands):
                zeros = tuple(jnp.zeros((n, LANES), F32) for _ in cands)

                def hits(x, parts):
                    w = x.shape[1]
                    return tuple(p + _halves(one(x >= _rep(c, w)), jnp.add) for p, c in zip(parts, cands))
                parts = wide_sweep(zeros, lambda j, parts: hits(kw_s[j, 0:n, :], parts))
                if not own_wide:
                    parts = hits(kn_s[0, 0:n, :], parts)
                if with_meta:
                    parts = hits(kn_s[1, 0:n, :], parts)
                sums = lane_sums(jnp.concatenate(parts, axis=0))
                return [sums[m * n:(m + 1) * n] for m in range(len(cands))]

            def search_pass(shift, bits, thr, cnt):
                cands = [thr | jnp.left_shift(I32(m), shift) for m in range(1, 2 ** bits)]
                for cand, c in zip(cands, ge_counts(cands)):
                    thr, cnt = accept(thr, cnt, cand, c)
                return thr, cnt
            thr, cnt = lax.fori_loop(0, 31 // nb, lambda t, st: search_pass(I32(31 - nb) - nb * t, nb, *st),
                                     sign_step(0, n))
            if 31 % nb:
                thr, cnt = search_pass(I32(0), 31 % nb, thr, cnt)

        last_s[0:n, :] = jnp.full((n, LANES), INT_MAX, I32)
        has_ties = jnp.max(one((cnt > topk) & (thr > I32(KEY_NEG)))) > 0.0

        @pl.when(has_ties)
        def _():
            need = topk - count_cmp(lambda x, w: x > _rep(thr, w))
            colw = lax.broadcasted_iota(I32, (1, KEY_BLOCK), 1)
            coln = lax.broadcasted_iota(I32, (1, NARROW), 1)

            def wide_pos0(j):
                return past_pos0 + j * KEY_BLOCK

            def ties_before(x):
                return count(lambda kk, j: _halves(one((kk == _rep(thr, KEY_BLOCK))
                                                       & (colw < _rep(x, KEY_BLOCK) - wide_pos0(j))), jnp.add),
                             lambda kk: one((kk == thr) & (coln < x - qpos)),
                             lambda kk: one((kk == thr) & (coln < x)))

            def pos_step(b, last_pos):
                cand = last_pos + jnp.left_shift(I32(1), I32(index_bits - 1) - b)
                return jnp.where(ties_before(cand) < need, cand, last_pos)
            last_s[0:n, :] = lax.fori_loop(0, index_bits, pos_step, jnp.zeros((n, LANES), I32))

        last_pos = last_s[0:n, :]

        def sel_bias(kk, kpos0):
            w = kk.shape[1]
            cols = lax.broadcasted_iota(I32, (1, w), 1)
            t = _rep(thr, w)
            sel = (kk > t) | ((kk == t) & (cols <= _rep(last_pos, w) - kpos0))
            return jnp.where(sel, 0.0, NEG_INF)

        sm = Softmax(sm_refs, DSA_HEADS, n)
        q_of = qm_s
        own_kk = kw_s[own_slot, 0:n, :] if own_wide else kn_s[0, 0:n, :]
        sm.scores(own_slot, own_wide, q_of, own_k(), 1.0,
                  bias_of=(lambda h: b_meta_own(h)) if is_meta else (lambda h: b_own(h, n)),
                  extra=sel_bias(own_kk, qpos))

        def p1(j, carry):
            sm.scores(j, True, q_of, past(0, j).astype(BF), 1.0,
                      bias_of=lambda h: b_earlier(j, h),
                      extra=sel_bias(kw_s[j, 0:n, :], past_pos0 + j * KEY_BLOCK))
            return carry
        _past_loop(npast, p1)
        if with_meta:
            sm.scores(1, False, q_of, knm_s[...], 1.0, bias_of=lambda h: b_meta(h),
                      extra=sel_bias(kn_s[1, 0:n, :], 0))
        sm.finish_max()

        sm.values(own_slot, own_wide, own_v().astype(BF))

        def p2(j, carry):
            sm.values(j, True, past(1, j).astype(BF))
            return carry
        _past_loop(npast, p2)
        if with_meta:
            sm.values(1, False, vnm_s[...])

        out = None
        for h in range(DSA_HEADS):
            o_h = jnp.where(head == h, sm.result(h), 0.0)
            out = o_h if out is None else out + o_h
        o_ref[0, pl.ds(q0, n), :] = out.astype(o_ref.dtype)

    _tile_walk(geom,
               lambda q0, n, qpos, npast: tile(q0, n, qpos, npast, bool(geom.n_meta), geom.own_wide, False),
               lambda q0, n, qpos: tile(q0, n, qpos, 0, False, False, True))


def _dsa_call(geom, layer, q, qi, wi, k, v, kit, past, tables):
    bsz = q.shape[0]
    tables = [t for t in tables if t is not None]
    ins = [q, qi, wi, k, v, kit] + list(past) + tables
    specs = ([_slab_spec(a) for a in (q, qi, wi, k, v, kit)] + [_past_spec(a, layer) for a in past]
             + [_const_spec(a) for a in tables])
    tq = geom.tq
    return pl.pallas_call(
        functools.partial(_dsa_kernel, geom), grid=_attn_grid(geom, bsz),
        in_specs=specs, out_specs=pl.BlockSpec((1, geom.lq, BR_W), lambda b, i: (b, 0, 0)),
        out_shape=jax.ShapeDtypeStruct((bsz, geom.lq, BR_W), BF),
        scratch_shapes=(Softmax.scratch(geom, DSA_HEADS, BR_W)
                        + [pltpu.VMEM((DSA_HEADS, tq, BR_W), BF), pltpu.VMEM((IDX_HEADS, tq, BR_W), BF),
                           pltpu.VMEM((IDX_HEADS, tq, LANES), F32),
                           pltpu.VMEM((geom.nslot, tq, KEY_BLOCK), I32), pltpu.VMEM((2, tq, NARROW), I32),
                           pltpu.VMEM((tq, LANES), I32)]
                        + [pltpu.VMEM((NARROW, BR_W), BF)] * 6),
        compiler_params=_cparams(("parallel", "arbitrary")), name="dsa_attn")(*ins)


def _row_tile(t):
    for tm in (768, 384, 256, 128, 64, 32, 16, 8):
        if t % tm == 0:
            return tm
    raise ValueError(f"token count {t} has no supported row tile")


def _rope_tables(pos):
    half = MLA_ROPE // 2
    inv_freq = ROPE_THETA ** (-jnp.arange(half, dtype=jnp.float32) / half)
    ang = pos.astype(jnp.float32)[:, None] * inv_freq[None, :]
    cos, sin = jnp.cos(ang), jnp.sin(ang)
    return jnp.concatenate([cos, cos], axis=1), jnp.concatenate([-sin, sin], axis=1)


def _run_group(geom, x, pos_rows, caches, weights):
    (ln_in_g, ln_in_b, w_in, mla_qnorm_g, mla_w_uq, mla_kvnorm_g, mla_w_uk, mla_w_uv, diff_lambda,
     diff_subln_g, rel_bias, w_br, w_out, ln1_g, ln1_b, w_ff1, b_ff1, w_ff2, b_ff2, ln2_g, ln2_b) = weights
    bsz, lq, d = x.shape
    t = bsz * lq
    tm = _row_tile(t)
    cos32, sin32 = _rope_tables(pos_rows)
    cos_k = jnp.tile(cos32, (bsz, MLA_HEADS))
    sin_k = jnp.tile(sin32, (bsz, MLA_HEADS))
    tables = _bias_tables(geom, rel_bias)
    diff_tables = _split_tables(tables, 0, DIFF_HEADS)
    dsa_tables = _split_tables(tables, DIFF_HEADS, DIFF_HEADS + DSA_HEADS)

    xf = _ln_call(x.reshape(t, d), ln_in_g, ln_in_b, tm)
    rows = None
    for l in range(DEPTH):
        w_uq = mla_w_uq[l].reshape(MLA_Q_LORA, MLA_HEADS, MLA_NOPE + MLA_ROPE)
        wn = w_uq[:, :, :MLA_NOPE].reshape(MLA_Q_LORA, MLA_HEADS * MLA_NOPE).astype(BF)
        wp = w_uq[:, :, MLA_NOPE:].reshape(MLA_Q_LORA, MLA_HEADS * MLA_ROPE).astype(BF)
        wuk = mla_w_uk[l].reshape(MLA_KV_LORA, MLA_HEADS * MLA_NOPE).T.astype(BF)
        wuv = jnp.pad(mla_w_uv[l].reshape(MLA_KV_LORA, MLA_HEADS * MLA_V),
                      ((0, 2 * LANES - MLA_KV_LORA), (0, 0))).astype(BF)
        tm_proj = tm if rows is None else min(tm, PASS_THROUGH_ROW_TILE)
        (*rows, d_wi, qcat, kcat, bq_b, bk_b, bv_b, cq_b, ck_b, cv_b, dq_b, dk_b, dv_b, dqi_b, kit_b) = _proj_call(
            xf, _mix_weight(w_in[l]), mla_kvnorm_g[l].reshape(1, -1), mla_qnorm_g[l].reshape(1, -1),
            wn, wp, wuk, cos_k, sin_k, rows, tm_proj)
        if l == 0:
            rows = [a.reshape((1,) + a.shape) for a in rows]
        per = lambda a: a.reshape(bsz, lq, a.shape[-1])
        if caches is None:
            past = lambda *idx: []
            mla_past, kit_past = [], []
        else:
            past = lambda *idx: [caches[i] for i in idx]
            fill = jnp.zeros(caches[0].shape[1:3] + (2 * LANES - MLA_KV_LORA - MLA_ROPE,), F32).at[..., -1].set(1.0)
            mla_past = [jnp.concatenate([caches[0][l], caches[1][l], fill], axis=-1).astype(BF)]
            kit_past = [jnp.tile(caches[8][l], (1, 1, IDX_HEADS)).astype(BF)]

        o_a = _mla_call(geom, l, per(qcat), per(kcat), mla_past, wuv)
        o_b = _sb_call(geom, l, per(bq_b), per(bk_b), per(bv_b), past(2, 3))
        o_c = _diff_call(geom, l, per(cq_b), per(ck_b), per(cv_b), past(4, 5), diff_tables, diff_lambda[l],
                         jnp.tile(diff_subln_g[l], DIFF_HEADS).reshape(1, -1))
        o_d = _dsa_call(geom, l, per(dq_b), per(dqi_b), per(d_wi), per(dk_b), per(dv_b), per(kit_b),
                        past(6, 7) + kit_past, dsa_tables)

        flat = lambda a: a.reshape(t, BR_W)
        wg = w_in[l][:, IN_OFFS[15]:IN_OFFS[16]].astype(BF)
        x1 = _merge_call(xf, flat(o_a), flat(o_b), flat(o_c), flat(o_d), wg, w_br[l].astype(BF),
                         w_out[l].astype(BF), ln1_g[l].reshape(1, -1), ln1_b[l].reshape(1, -1), tm)
        xf = _ffn_call(x1, w_ff1[l].astype(BF), b_ff1[l].reshape(1, -1), w_ff2[l].astype(BF),
                       b_ff2[l].reshape(1, -1), ln2_g[l].reshape(1, -1), ln2_b[l].reshape(1, -1), tm)
    return xf.reshape(bsz, lq, d), [a.reshape(DEPTH, bsz, lq, a.shape[-1]) for a in rows]


_ROW_TRAILING = ((MLA_KV_LORA,), (MLA_ROPE,), (SB_HEADS, SB_DH), (SB_HEADS, SB_DH),
                 (DIFF_HEADS, 2, DIFF_DQK), (DIFF_HEADS, DIFF_DV), (DSA_HEADS, DSA_DH),
                 (DSA_HEADS, DSA_DH), (IDX_DIM,))


def kernel(x_prompt, x_sample, cache_mla_kv, cache_mla_pe, cache_sb_k, cache_sb_v, cache_diff_k, cache_diff_v, cache_dsa_k, cache_dsa_v, cache_dsa_kidx, meta, ln_in_g, ln_in_b, w_in, mla_qnorm_g, mla_w_uq, mla_kvnorm_g, mla_w_uk, mla_w_uv, diff_lambda, diff_subln_g, rel_bias, w_br, w_out, ln1_g, ln1_b, w_ff1, b_ff1, w_ff2, b_ff2, ln2_g, ln2_b):
    weights = (ln_in_g, ln_in_b, w_in, mla_qnorm_g, mla_w_uq, mla_kvnorm_g, mla_w_uk, mla_w_uv, diff_lambda,
               diff_subln_g, rel_bias, w_br, w_out, ln1_g, ln1_b, w_ff1, b_ff1, w_ff2, b_ff2, ln2_g, ln2_b)
    assert w_in.shape[0] == DEPTH and x_prompt.shape[2] == D_MODEL

    bsz_p, seq_p, _ = x_prompt.shape
    assert seq_p % KEY_BLOCK == 0
    meta_b = jnp.broadcast_to(meta[None].astype(x_prompt.dtype), (bsz_p, N_META, D_MODEL))
    xp = jnp.concatenate([meta_b, x_prompt], axis=1)
    pos_p = jnp.arange(N_META + seq_p, dtype=I32)
    geom_p = Geom(tq=KEY_BLOCK, ntile=seq_p // KEY_BLOCK, npast=None, n_meta=N_META, qpos0=N_META,
                  row0=N_META, lq=seq_p + N_META, lp=0, topk=min(DSA_TOPK, seq_p // 4))
    yp, rows_p = _run_group(geom_p, xp, pos_p, None, weights)
    y_prompt = yp[:, N_META:]
    p_rows = []
    for i, trailing in enumerate(_ROW_TRAILING):
        p_rows.append(rows_p[i].reshape(rows_p[i].shape[:3] + trailing))

    past_len = cache_mla_kv.shape[2]
    bsz_s, dec_seq, _ = x_sample.shape
    assert past_len % KEY_BLOCK == 0 and dec_seq % 16 == 0 and dec_seq <= NARROW
    assert past_len % CHUNK == 0 and dec_seq <= CHUNK, "new frames must share one chunk"
    caches = [c.reshape(c.shape[:3] + (-1,)) for c in
              (cache_mla_kv, cache_mla_pe, cache_sb_k, cache_sb_v, cache_diff_k, cache_diff_v,
               cache_dsa_k, cache_dsa_v, cache_dsa_kidx)]
    pos_s = N_META + past_len + jnp.arange(dec_seq, dtype=I32)
    geom_s = Geom(tq=dec_seq, ntile=1, npast=past_len // KEY_BLOCK, n_meta=0, qpos0=N_META + past_len,
                  row0=0, lq=dec_seq, lp=past_len, topk=min(DSA_TOPK, (past_len + dec_seq) // 4))
    y_sample, rows_s = _run_group(geom_s, x_sample, pos_s, caches, weights)
    s_rows = []
    for i, trailing in enumerate(_ROW_TRAILING):
        s_rows.append(rows_s[i].reshape(rows_s[i].shape[:3] + trailing))

    return (y_prompt, y_sample, *p_rows, *s_rows)
```
